```python
import math
import jax, jax.numpy as jnp
from jax import lax
import numpy as np

D_MODEL = 1024
BATCH = 2
SEQ = 8192
DEPTH = 2
DEC_BATCH = 32
DEC_SEQ = 64
PAST_LEN = 2048

CHUNK = 64
SSM_GROUP = 16
N_SSM_GROUPS = 16
SSM_WIDTH = SSM_GROUP * N_SSM_GROUPS
SSM_STATE = 64
CONV_WIDTH = 256
CONV_KERNEL = 31
N_ATTN_HEADS = 4
ATTN_HEAD_DIM = 64
ATTN_V_DIM = 2 * ATTN_HEAD_DIM
QK_WIDTH = N_ATTN_HEADS * ATTN_HEAD_DIM
ATTN_WIDTH = N_ATTN_HEADS * ATTN_V_DIM
MIX_WIDTH = SSM_WIDTH + CONV_WIDTH + ATTN_WIDTH
IN_WIDTH = SSM_WIDTH + 2 * CONV_WIDTH + 4 * QK_WIDTH + ATTN_WIDTH
SPLIT_POINTS = (SSM_WIDTH,
                SSM_WIDTH + CONV_WIDTH,
                SSM_WIDTH + 2 * CONV_WIDTH,
                SSM_WIDTH + 2 * CONV_WIDTH + QK_WIDTH,
                SSM_WIDTH + 2 * CONV_WIDTH + 2 * QK_WIDTH,
                SSM_WIDTH + 2 * CONV_WIDTH + 3 * QK_WIDTH,
                SSM_WIDTH + 2 * CONV_WIDTH + 4 * QK_WIDTH)
ATTN_SCALE = 1.0 / math.sqrt(ATTN_HEAD_DIM)
Q_BLOCK = 128
NEG_INF = -1e30
D_FF = 2816
N_EXPERTS = 8
TOP_K = 2
N_DENSE = (DEPTH + 1) // 2
N_MOE = DEPTH // 2
EPS = 1e-6

kernel_name = "hymba_s5_conformer_diffattn_streaming_step"


def _rmsnorm(x, g):
    x32 = x.astype(jnp.float32)
    y = x32 * lax.rsqrt(jnp.mean(x32 * x32, axis=-1, keepdims=True) + EPS)
    return (y * g.astype(jnp.float32)).astype(x.dtype)


def _layernorm(x, g, b):
    x32 = x.astype(jnp.float32)
    xc = x32 - jnp.mean(x32, axis=-1, keepdims=True)
    var = jnp.mean(xc * xc, axis=-1, keepdims=True)
    return (xc * lax.rsqrt(var + EPS) * g.astype(jnp.float32) + b.astype(jnp.float32)).astype(x.dtype)


def _linear_recurrence(e1, e2):
    a1, b1 = e1
    a2, b2 = e2
    return a2 * a1, a2 * b1 + b2


def _s5_mixer(u, s0, a_re, a_im, log_dt, b_re, b_im, c_re, c_im, d_skip, w_glu):
    bsz, seq_len, _ = u.shape
    f32 = jnp.float32
    ug = u.astype(f32).reshape(bsz, seq_len, N_SSM_GROUPS, SSM_GROUP)
    lam = lax.complex(a_re.astype(f32), a_im.astype(f32))
    dt = jnp.exp(log_dt.astype(f32))[:, None]
    a_bar = jnp.exp(lam * dt)
    b = lax.complex(b_re.astype(f32), b_im.astype(f32))
    b_bar = ((a_bar - 1.0) / lam)[..., None] * b
    c = lax.complex(c_re.astype(f32), c_im.astype(f32))
    bu = jnp.einsum('gph,blgh->blgp', b_bar, ug.astype(jnp.complex64))
    bu = bu.at[:, 0].add(a_bar[None] * s0)
    a_seq = jnp.broadcast_to(a_bar, bu.shape)
    _, states = lax.associative_scan(_linear_recurrence, (a_seq, bu), axis=1)
    y = jnp.real(jnp.einsum('ghp,blgp->blgh', c, states)) + d_skip.astype(f32).reshape(N_SSM_GROUPS, SSM_GROUP) * ug
    z = jax.nn.gelu(y.reshape(bsz, seq_len, SSM_WIDTH))
    out = z * jax.nn.sigmoid(z @ w_glu.astype(f32))
    return out.astype(u.dtype), states[:, -1]


def _conv_module(c_val, c_gate, conv_buf, conv_w, ln_g, ln_b):
    h = c_val * jax.nn.sigmoid(c_gate)
    hp = jnp.concatenate([conv_buf.astype(h.dtype), h], axis=1)
    y = lax.conv_general_dilated(hp, conv_w.astype(hp.dtype)[:, None, :], window_strides=(1,), padding='VALID',
                                 dimension_numbers=('NWC', 'WIO', 'NWC'), feature_group_count=CONV_WIDTH)
    y = jax.nn.silu(_layernorm(y, ln_g, ln_b))
    return y, hp[:, -(CONV_KERNEL - 1):]


def _diff_attend(q, k, v, lam, mask):
    s = jnp.einsum('bqthd,bkthd->bthqk', q.astype(jnp.float32), k.astype(jnp.float32)) * ATTN_SCALE
    if mask is not None:
        s = jnp.where(mask, s, NEG_INF)
    p = jax.nn.softmax(s, axis=-1)
    a = p[:, 0] - lam * p[:, 1]
    return jnp.einsum('bhqk,bkhe->bqhe', a, v.astype(jnp.float32))


def _diff_attention_prompt(q, k, v, lam):
    bsz, seq_len = q.shape[:2]
    n_blocks = seq_len // Q_BLOCK
    q_blocks = jnp.moveaxis(q.reshape(bsz, n_blocks, Q_BLOCK, 2, N_ATTN_HEADS, ATTN_HEAD_DIM), 1, 0)
    key_chunk = jnp.arange(seq_len) // CHUNK

    def one_block(args):
        blk, q_blk = args
        q_chunk = (blk * Q_BLOCK + jnp.arange(Q_BLOCK)) // CHUNK
        mask = q_chunk[:, None] >= key_chunk[None, :]
        return _diff_attend(q_blk, k, v, lam, mask)

    o = lax.map(one_block, (jnp.arange(n_blocks), q_blocks))
    return jnp.moveaxis(o, 0, 1).reshape(bsz, seq_len, N_ATTN_HEADS, ATTN_V_DIM)


def _token_mixers(h, s0, conv_buf, past_k, past_v, lam_init,
                  w_in, a_re, a_im, log_dt, b_re, b_im, c_re, c_im, d_skip, w_glu,
                  conv_w, conv_ln_g, conv_ln_b, lq1, lk1, lq2, lk2, sub_g, w_out):
    bsz, seq_len, _ = h.shape
    z = jnp.einsum('bld,de->ble', h, w_in)
    u_ssm, c_val, c_gate, q1, q2, k1, k2, v = jnp.split(z, SPLIT_POINTS, axis=-1)
    ssm_out, s_new = _s5_mixer(u_ssm, s0, a_re, a_im, log_dt, b_re, b_im, c_re, c_im, d_skip, w_glu)
    conv_out, conv_new = _conv_module(c_val, c_gate, conv_buf, conv_w, conv_ln_g, conv_ln_b)
    hd = (bsz, seq_len, N_ATTN_HEADS, ATTN_HEAD_DIM)
    q = jnp.stack([q1.reshape(hd), q2.reshape(hd)], axis=2)
    k = jnp.stack([k1.reshape(hd), k2.reshape(hd)], axis=2)
    v = v.reshape(bsz, seq_len, N_ATTN_HEADS, ATTN_V_DIM)
    f32 = jnp.float32
    lam = (jnp.exp(jnp.sum(lq1.astype(f32) * lk1.astype(f32)))
           - jnp.exp(jnp.sum(lq2.astype(f32) * lk2.astype(f32))) + lam_init)
    if past_k is None:
        o = _diff_attention_prompt(q, k, v, lam)
    else:
        k_all = jnp.concatenate([past_k.astype(k.dtype), k], axis=1)
        v_all = jnp.concatenate([past_v.astype(v.dtype), v], axis=1)
        o = _diff_attend(q, k_all, v_all, lam, None)
    o = _rmsnorm(o, sub_g) * (1.0 - lam_init)
    attn_out = o.reshape(bsz, seq_len, ATTN_WIDTH).astype(h.dtype)
    mixed = jnp.concatenate([ssm_out, conv_out, attn_out], axis=-1)
    out = jnp.einsum('ble,ed->bld', mixed, w_out)
    return out, k, v, s_new, conv_new


def _swiglu(h, wg, wu, wd):
    return (jax.nn.silu(h @ wg) * (h @ wu)) @ wd


def _moe_ffn(h, router, wg, wu, wd):
    logits = jnp.einsum('bld,de->ble', h.astype(jnp.float32), router.astype(jnp.float32))
    top_v, top_i = lax.top_k(logits, TOP_K)
    w = jax.nn.softmax(top_v, axis=-1)
    gates = jnp.sum(jax.nn.one_hot(top_i, N_EXPERTS, dtype=jnp.float32) * w[..., None], axis=-2)
    out = jnp.zeros(h.shape, jnp.float32)
    for e in range(N_EXPERTS):
        out = out + gates[..., e:e + 1] * _swiglu(h, wg[e], wu[e], wd[e]).astype(jnp.float32)
    return out.astype(h.dtype)


def setup_inputs(seed: int = 0) -> dict:
    key = jax.random.key(seed)
    ks = jax.random.split(key, 40)
    f32 = jnp.float32

    def nrm(i, shape, scale):
        return scale * jax.random.normal(ks[i], shape, f32)

    G, P, H = N_SSM_GROUPS, SSM_STATE, SSM_GROUP
    return {
        "x_prompt": nrm(0, (BATCH, SEQ, D_MODEL), 1.0),
        "x_sample": nrm(1, (DEC_BATCH, DEC_SEQ, D_MODEL), 1.0),
        "cache_k": nrm(2, (DEPTH, DEC_BATCH, PAST_LEN, 2, N_ATTN_HEADS, ATTN_HEAD_DIM), 1.0),
        "cache_v": nrm(3, (DEPTH, DEC_BATCH, PAST_LEN, N_ATTN_HEADS, ATTN_V_DIM), 1.0),
        "state_ssm_re": nrm(4, (DEPTH, DEC_BATCH, G, P), 0.1),
        "state_ssm_im": nrm(5, (DEPTH, DEC_BATCH, G, P), 0.1),
        "state_conv": nrm(6, (DEPTH, DEC_BATCH, CONV_KERNEL - 1, CONV_WIDTH), 0.5),
        "norm_mix_g": 1.0 + nrm(7, (DEPTH, D_MODEL), 0.02),
        "w_in": nrm(8, (DEPTH, D_MODEL, IN_WIDTH), D_MODEL ** -0.5),
        "ssm_a_re": -0.5 + nrm(9, (DEPTH, G, P), 0.01),
        "ssm_a_im": jnp.pi * jnp.arange(P, dtype=f32) + nrm(10, (DEPTH, G, P), 0.01),
        "ssm_log_dt": jax.random.uniform(ks[11], (DEPTH, G), f32, math.log(1e-3), math.log(1e-1)),
        "ssm_b_re": nrm(12, (DEPTH, G, P, H), (2 * H) ** -0.5),
        "ssm_b_im": nrm(13, (DEPTH, G, P, H), (2 * H) ** -0.5),
        "ssm_c_re": nrm(14, (DEPTH, G, H, P), P ** -0.5),
        "ssm_c_im": nrm(15, (DEPTH, G, H, P), P ** -0.5),
        "ssm_d": nrm(16, (DEPTH, SSM_WIDTH), 1.0),
        "ssm_w_glu": nrm(17, (DEPTH, SSM_WIDTH, SSM_WIDTH), SSM_WIDTH ** -0.5),
        "conv_w": nrm(18, (DEPTH, CONV_KERNEL, CONV_WIDTH), CONV_KERNEL ** -0.5),
        "conv_ln_g": 1.0 + nrm(19, (DEPTH, CONV_WIDTH), 0.02),
        "conv_ln_b": nrm(20, (DEPTH, CONV_WIDTH), 0.02),
        "attn_lq1": nrm(21, (DEPTH, ATTN_HEAD_DIM), 0.1),
        "attn_lk1": nrm(22, (DEPTH, ATTN_HEAD_DIM), 0.1),
        "attn_lq2": nrm(23, (DEPTH, ATTN_HEAD_DIM), 0.1),
        "attn_lk2": nrm(24, (DEPTH, ATTN_HEAD_DIM), 0.1),
        "attn_sub_g": 1.0 + nrm(25, (DEPTH, ATTN_V_DIM), 0.02),
        "w_out": nrm(26, (DEPTH, MIX_WIDTH, D_MODEL), MIX_WIDTH ** -0.5),
        "norm_ffn_g": 1.0 + nrm(27, (DEPTH, D_MODEL), 0.02),
        "ffn_w_gate": nrm(28, (N_DENSE, D_MODEL, D_FF), D_MODEL ** -0.5),
        "ffn_w_up": nrm(29, (N_DENSE, D_MODEL, D_FF), D_MODEL ** -0.5),
        "ffn_w_down": nrm(30, (N_DENSE, D_FF, D_MODEL), D_FF ** -0.5),
        "moe_router": nrm(31, (N_MOE, D_MODEL, N_EXPERTS), D_MODEL ** -0.5),
        "moe_w_gate": nrm(32, (N_MOE, N_EXPERTS, D_MODEL, D_FF), D_MODEL ** -0.5),
        "moe_w_up": nrm(33, (N_MOE, N_EXPERTS, D_MODEL, D_FF), D_MODEL ** -0.5),
        "moe_w_down": nrm(34, (N_MOE, N_EXPERTS, D_FF, D_MODEL), D_FF ** -0.5),
        "final_norm_g": 1.0 + nrm(35, (D_MODEL,), 0.02),
    }


def reference(x_prompt, x_sample, cache_k, cache_v, state_ssm_re, state_ssm_im, state_conv,
              norm_mix_g, w_in, ssm_a_re, ssm_a_im, ssm_log_dt, ssm_b_re, ssm_b_im, ssm_c_re, ssm_c_im,
              ssm_d, ssm_w_glu, conv_w, conv_ln_g, conv_ln_b, attn_lq1, attn_lk1, attn_lq2, attn_lk2,
              attn_sub_g, w_out, norm_ffn_g, ffn_w_gate, ffn_w_up, ffn_w_down,
              moe_router, moe_w_gate, moe_w_up, moe_w_down, final_norm_g):
    xp, xs = x_prompt, x_sample
    kp_l, vp_l, srp_l, sip_l, cp_l = [], [], [], [], []
    ks_l, vs_l, srs_l, sis_l, cs_l = [], [], [], [], []
    s0_prompt = jnp.zeros((xp.shape[0], N_SSM_GROUPS, SSM_STATE), jnp.complex64)
    buf_prompt = jnp.zeros((xp.shape[0], CONV_KERNEL - 1, CONV_WIDTH), xp.dtype)
    for l in range(DEPTH):
        lam_init = 0.8 - 0.6 * math.exp(-0.3 * l)
        mix_p = (w_in[l], ssm_a_re[l], ssm_a_im[l], ssm_log_dt[l], ssm_b_re[l], ssm_b_im[l],
                 ssm_c_re[l], ssm_c_im[l], ssm_d[l], ssm_w_glu[l], conv_w[l], conv_ln_g[l], conv_ln_b[l],
                 attn_lq1[l], attn_lk1[l], attn_lq2[l], attn_lk2[l], attn_sub_g[l], w_out[l])
        mp, kp, vp, sp, cp = _token_mixers(_rmsnorm(xp, norm_mix_g[l]), s0_prompt, buf_prompt,
                                           None, None, lam_init, *mix_p)
        s0_sample = lax.complex(state_ssm_re[l].astype(jnp.float32), state_ssm_im[l].astype(jnp.float32))
        ms, kss, vss, ss, cs = _token_mixers(_rmsnorm(xs, norm_mix_g[l]), s0_sample, state_conv[l],
                                             cache_k[l], cache_v[l], lam_init, *mix_p)
        xp = xp + mp
        xs = xs + ms
        hp = _rmsnorm(xp, norm_ffn_g[l])
        hs = _rmsnorm(xs, norm_ffn_g[l])
        if l % 2 == 0:
            j = l // 2
            xp = xp + _swiglu(hp, ffn_w_gate[j], ffn_w_up[j], ffn_w_down[j])
            xs = xs + _swiglu(hs, ffn_w_gate[j], ffn_w_up[j], ffn_w_down[j])
        else:
            j = l // 2
            xp = xp + _moe_ffn(hp, moe_router[j], moe_w_gate[j], moe_w_up[j], moe_w_down[j])
            xs = xs + _moe_ffn(hs, moe_router[j], moe_w_gate[j], moe_w_up[j], moe_w_down[j])
        kp_l.append(kp); vp_l.append(vp); srp_l.append(jnp.real(sp)); sip_l.append(jnp.imag(sp)); cp_l.append(cp)
        ks_l.append(kss); vs_l.append(vss); srs_l.append(jnp.real(ss)); sis_l.append(jnp.imag(ss)); cs_l.append(cs)
    y_prompt = _rmsnorm(xp, final_norm_g)
    y_sample = _rmsnorm(xs, final_norm_g)
    return (y_prompt, y_sample,
            jnp.stack(kp_l), jnp.stack(vp_l), jnp.stack(srp_l), jnp.stack(sip_l), jnp.stack(cp_l),
            jnp.stack(ks_l), jnp.stack(vs_l), jnp.stack(srs_l), jnp.stack(sis_l), jnp.stack(cs_l))
```

```python
import functools
import math

import jax
import jax.numpy as jnp
from jax import lax
from jax.experimental import pallas as pl
from jax.experimental.pallas import tpu as pltpu

F32 = jnp.float32
BF16 = jnp.bfloat16

D_MODEL = 1024
CHUNK = 64
SSM_GROUP = 16
N_SSM_GROUPS = 16
SSM_WIDTH = 256
SSM_STATE = 64
STATE_W = N_SSM_GROUPS * SSM_STATE
CONV_WIDTH = 256
CONV_KERNEL = 31
N_HEADS = 4
HEAD_DIM = 64
V_DIM = 128
QK_WIDTH = 256
ATTN_WIDTH = 512
IN_WIDTH = 2304
ATTN_SCALE = 1.0 / math.sqrt(HEAD_DIM)
NEG_INF = -1e30
D_FF = 2816
N_EXPERTS = 8
EPS = 1e-6

S5_CHUNK = 8
LANES = 128
VMEM_LIMIT = 48 * 1024 * 1024


def _cparams(sem):
    return pltpu.CompilerParams(dimension_semantics=sem, vmem_limit_bytes=VMEM_LIMIT)


def _rms(x, g):
    return x * lax.rsqrt(jnp.mean(x * x, axis=-1, keepdims=True) + EPS) * g


def _in_proj_kernel(x_ref, g_ref, w_ref, u_ref, hc_ref, q_ref, k_ref, v_ref, kb_ref, vb_ref):
    h = _rms(x_ref[...], g_ref[...]).astype(BF16)

    def proj(lo, hi):
        return jnp.dot(h, w_ref[:, lo:hi], preferred_element_type=F32)

    u_ref[...] = proj(0, 256)
    c_val = proj(256, 512)
    c_gate = proj(512, 768)
    hc_ref[...] = c_val * jax.nn.sigmoid(c_gate)
    q_ref[...] = (proj(768, 1280) * ATTN_SCALE).astype(BF16)
    k = proj(1280, 1792)
    k_ref[...] = k
    kb_ref[...] = k.astype(BF16)
    v = proj(1792, 2304)
    v_ref[...] = v
    vb_ref[...] = v.astype(BF16)


def _in_proj(x, g, w_bf16, tm=512):
    t = x.shape[0]
    row = lambda width: pl.BlockSpec((tm, width), lambda i: (i, 0))
    return pl.pallas_call(
        _in_proj_kernel,
        grid=(t // tm,),
        in_specs=[row(D_MODEL), pl.BlockSpec((1, D_MODEL), lambda i: (0, 0)),
                  pl.BlockSpec((D_MODEL, IN_WIDTH), lambda i: (0, 0))],
        out_specs=[row(256), row(256), row(512), row(512), row(512), row(512), row(512)],
        out_shape=[jax.ShapeDtypeStruct((t, 256), F32), jax.ShapeDtypeStruct((t, 256), F32),
                   jax.ShapeDtypeStruct((t, 512), BF16), jax.ShapeDtypeStruct((t, 512), F32),
                   jax.ShapeDtypeStruct((t, 512), F32), jax.ShapeDtypeStruct((t, 512), BF16),
                   jax.ShapeDtypeStruct((t, 512), BF16)],
        compiler_params=_cparams(("parallel",)),
        name="in_proj",
    )(x, g.reshape(1, D_MODEL), w_bf16)


def _mm_kernel(x_ref, w_ref, o_ref):
    o_ref[...] = jnp.dot(x_ref[...].astype(BF16), w_ref[...], preferred_element_type=F32)


def _mm(x, w_bf16, tm, tn, name):
    m, k = x.shape
    n = w_bf16.shape[1]
    return pl.pallas_call(
        _mm_kernel,
        grid=(n // tn, m // tm),
        in_specs=[pl.BlockSpec((tm, k), lambda j, i: (i, 0)), pl.BlockSpec((k, tn), lambda j, i: (0, j))],
        out_specs=pl.BlockSpec((tm, tn), lambda j, i: (i, j)),
        out_shape=jax.ShapeDtypeStruct((m, n), F32),
        compiler_params=_cparams(("parallel", "parallel")),
        name=name,
    )(x, w_bf16)


def _s5_scan_kernel(d_ref, s0_ref, a_ref, ss_ref, sf_ref, st_ref, *, rows):
    t = pl.program_id(1)

    @pl.when(t == 0)
    def _():
        st_ref[...] = s0_ref[...]

    a_re = a_ref[:, :STATE_W]
    a_im = a_ref[:, STATE_W:]

    def step(c, carry):
        s_re, s_im = carry
        ss_ref[pl.ds(c, 1), :STATE_W] = s_re
        ss_ref[pl.ds(c, 1), STATE_W:] = s_im
        d = d_ref[pl.ds(c, 1), :]
        n_re = a_re * s_re - a_im * s_im + d[:, :STATE_W]
        n_im = a_re * s_im + a_im * s_re + d[:, STATE_W:]
        return n_re, n_im

    s_re, s_im = lax.fori_loop(0, rows, step, (st_ref[:, :STATE_W], st_ref[:, STATE_W:]))
    st_ref[:, :STATE_W] = s_re
    st_ref[:, STATE_W:] = s_im
    sf_ref[...] = st_ref[...]


def _s5_scan(d, row0, n_seq, n_chunks, s0, a_c):
    rows = min(n_chunks, 256)
    nt = n_chunks // rows
    blk0 = row0 // rows
    w = 2 * STATE_W
    return pl.pallas_call(
        functools.partial(_s5_scan_kernel, rows=rows),
        grid=(n_seq, nt),
        in_specs=[pl.BlockSpec((rows, w), lambda s, t: (blk0 + s * nt + t, 0)),
                  pl.BlockSpec((None, 1, w), lambda s, t: (s, 0, 0)),
                  pl.BlockSpec((1, w), lambda s, t: (0, 0))],
        out_specs=[pl.BlockSpec((rows, w), lambda s, t: (s * nt + t, 0)),
                   pl.BlockSpec((None, 1, w), lambda s, t: (s, 0, 0))],
        out_shape=[jax.ShapeDtypeStruct((n_seq * n_chunks, w), F32), jax.ShapeDtypeStruct((n_seq, 1, w), F32)],
        scratch_shapes=[pltpu.VMEM((1, w), F32)],
        compiler_params=_cparams(("parallel", "arbitrary")),
        name="s5_scan",
    )(d, s0, a_c)


def _s5_out_kernel(yi_ref, ss_ref, w_ref, u_ref, dsk_ref, y_ref):
    y_ref[...] = (yi_ref[...] + jnp.dot(ss_ref[...].astype(BF16), w_ref[...], preferred_element_type=F32)
                  + dsk_ref[...] * u_ref[...])


def _s5_out(y_intra, s_start, w_inter, u2, dsk, tm=256, tn=512):
    m, n = y_intra.shape
    k = s_start.shape[1]
    tile = pl.BlockSpec((tm, tn), lambda j, i: (i, j))
    return pl.pallas_call(
        _s5_out_kernel,
        grid=(n // tn, m // tm),
        in_specs=[tile, pl.BlockSpec((tm, k), lambda j, i: (i, 0)), pl.BlockSpec((k, tn), lambda j, i: (0, j)),
                  tile, pl.BlockSpec((1, tn), lambda j, i: (0, j))],
        out_specs=tile,
        out_shape=jax.ShapeDtypeStruct((m, n), F32),
        compiler_params=_cparams(("parallel", "parallel")),
        name="s5_out",
    )(y_intra, s_start, w_inter, u2, dsk)


def _s5_operators(a_re, a_im, log_dt, b_re, b_im, c_re, c_im, d_skip, n_c):
    g_n, p_n, h_n = N_SSM_GROUPS, SSM_STATE, SSM_GROUP
    lam = lax.complex(a_re.astype(F32), a_im.astype(F32))
    dt = jnp.exp(log_dt.astype(F32))[:, None]
    a_bar = jnp.exp(lam * dt)
    b_bar = ((a_bar - 1.0) / lam)[..., None] * lax.complex(b_re.astype(F32), b_im.astype(F32))
    c = lax.complex(c_re.astype(F32), c_im.astype(F32))
    pows = [jnp.ones_like(a_bar)]
    for _ in range(n_c):
        pows.append(pows[-1] * a_bar)
    a_pow = jnp.stack(pows)
    eye_g = jnp.eye(g_n, dtype=F32)

    ws = a_pow[:n_c][::-1][:, :, :, None] * b_bar[None]
    ws = jnp.transpose(ws, (0, 1, 3, 2))

    def to_state(x):
        return (x[:, :, :, None, :] * eye_g[None, :, None, :, None]).reshape(n_c * g_n * h_n, g_n * p_n)

    w_state = jnp.concatenate([to_state(jnp.real(ws)), to_state(jnp.imag(ws))], axis=1)

    ker = jnp.real(jnp.einsum('ghp,kgp,gpx->kghx', c, a_pow[:n_c], b_bar, precision=lax.Precision.HIGHEST))
    lag = jnp.arange(n_c)[None, :] - jnp.arange(n_c)[:, None]
    toe = jnp.where((lag >= 0)[:, :, None, None, None], ker[jnp.clip(lag, 0, n_c - 1)], 0.0)
    toe = jnp.transpose(toe, (0, 2, 4, 1, 3))
    w_intra = (toe[:, :, :, :, None, :] * eye_g[None, :, None, None, :, None]).reshape(n_c * 256, n_c * 256)

    ca = c[None] * jnp.transpose(a_pow[1:], (0, 1, 2))[:, :, None, :]
    ca = jnp.transpose(ca, (1, 3, 0, 2))

    def from_state(x):
        return (x[:, :, :, None, :] * eye_g[:, None, None, :, None]).reshape(g_n * p_n, n_c * 256)

    w_inter = jnp.concatenate([from_state(jnp.real(ca)), from_state(-jnp.imag(ca))], axis=0)
    a_c = jnp.concatenate([jnp.real(a_pow[n_c]).reshape(1, -1), jnp.imag(a_pow[n_c]).reshape(1, -1)], axis=1)
    dsk = jnp.tile(d_skip.astype(F32).reshape(1, SSM_WIDTH), (1, n_c))
    return w_state.astype(BF16), w_intra.astype(BF16), w_inter.astype(BF16), a_c, dsk


_CONV_PAD = 32
_CONV_RB = 64


def _conv_kernel(h_ref, buf_ref, w_ref, g_ref, b_ref, y_ref, nb_ref, xp_ref, *, tl):
    t = pl.program_id(1)
    keep = CONV_KERNEL - 1
    lo = _CONV_PAD - keep

    @pl.when(t == 0)
    def _():
        xp_ref[lo:_CONV_PAD, :] = buf_ref[...]

    @pl.when(t > 0)
    def _():
        xp_ref[lo:_CONV_PAD, :] = xp_ref[tl + lo:tl + _CONV_PAD, :]

    xp_ref[_CONV_PAD:_CONV_PAD + tl, :] = h_ref[...]
    for r in range(tl // _CONV_RB):
        base = r * _CONV_RB
        acc = jnp.zeros((_CONV_RB, CONV_WIDTH), F32)
        for k in range(CONV_KERNEL):
            acc = acc + w_ref[k:k + 1, :] * xp_ref[base + lo + k:base + lo + k + _CONV_RB, :]
        xc = acc - jnp.mean(acc, axis=-1, keepdims=True)
        var = jnp.mean(xc * xc, axis=-1, keepdims=True)
        y = xc * lax.rsqrt(var + EPS) * g_ref[...] + b_ref[...]
        y_ref[base:base + _CONV_RB, :] = y * jax.nn.sigmoid(y)
    nb_ref[...] = xp_ref[tl + lo:tl + _CONV_PAD, :]


def _conv(h, buf, w, ln_g, ln_b, tl):
    s, l, c = h.shape
    keep = CONV_KERNEL - 1
    return pl.pallas_call(
        functools.partial(_conv_kernel, tl=tl),
        grid=(s, l // tl),
        in_specs=[pl.BlockSpec((None, tl, c), lambda i, t: (i, t, 0)),
                  pl.BlockSpec((None, keep, c), lambda i, t: (i, 0, 0)),
                  pl.BlockSpec((CONV_KERNEL, c), lambda i, t: (0, 0)),
                  pl.BlockSpec((1, c), lambda i, t: (0, 0)), pl.BlockSpec((1, c), lambda i, t: (0, 0))],
        out_specs=[pl.BlockSpec((None, tl, c), lambda i, t: (i, t, 0)),
                   pl.BlockSpec((None, keep, c), lambda i, t: (i, 0, 0))],
        out_shape=[jax.ShapeDtypeStruct((s, l, c), F32), jax.ShapeDtypeStruct((s, keep, c), F32)],
        scratch_shapes=[pltpu.VMEM((tl + _CONV_PAD, c), F32)],
        compiler_params=_cparams(("parallel", "arbitrary")),
        name="conv_module",
    )(h, buf, w, ln_g.reshape(1, c), ln_b.reshape(1, c))


def _stack_q(q_ref, qs_ref):
    tq = q_ref.shape[0]
    lane = lax.broadcasted_iota(jnp.int32, (tq, LANES), 1)
    for b in range(4):
        qb = q_ref[:, b * LANES:(b + 1) * LANES]
        qs_ref[b, :tq, :] = jnp.where(lane < HEAD_DIM, qb, jnp.zeros_like(qb))
        qs_ref[b, tq:, :] = jnp.where(lane >= HEAD_DIM, qb, jnp.zeros_like(qb))


def _attn_update(qs_ref, k, v, m_ref, l_ref, acc_ref, tq, mask):
    for b in range(4):
        s = lax.dot_general(qs_ref[b], k[:, b * LANES:(b + 1) * LANES], (((1,), (1,)), ((), ())),
                            preferred_element_type=F32)
        if mask is not None:
            s = jnp.where(mask, s, NEG_INF)
        m_old = m_ref[b]
        m_new = jnp.maximum(m_old, jnp.max(s, axis=-1, keepdims=True))
        alpha = jnp.exp(m_old - m_new)
        p = jnp.exp(s - m_new)
        l_ref[b] = alpha * l_ref[b] + jnp.sum(p, axis=-1, keepdims=True)
        m_ref[b] = m_new
        pb = p.astype(BF16)
        h0 = 2 * (b % 2)
        pv0 = jnp.dot(pb[:tq], v[:, h0 * V_DIM:(h0 + 1) * V_DIM], preferred_element_type=F32)
        pv1 = jnp.dot(pb[tq:], v[:, (h0 + 1) * V_DIM:(h0 + 2) * V_DIM], preferred_element_type=F32)
        acc_ref[b, :tq, :] = alpha[:tq] * acc_ref[b, :tq, :] + pv0
        acc_ref[b, tq:, :] = alpha[tq:] * acc_ref[b, tq:, :] + pv1


def _attn_finish(lp_ref, sg_ref, l_ref, acc_ref, o_ref, tq, lam_init):
    lp = lp_ref[...]
    lam = (jnp.exp(jnp.sum(lp[0:1] * lp[1:2], axis=-1, keepdims=True))
           - jnp.exp(jnp.sum(lp[2:3] * lp[3:4], axis=-1, keepdims=True)) + lam_init)
    for h in range(N_HEADS):
        b1, half = h // 2, h % 2
        rows = slice(half * tq, (half + 1) * tq)
        o1 = acc_ref[b1, rows, :] / l_ref[b1, rows, :]
        o2 = acc_ref[b1 + 2, rows, :] / l_ref[b1 + 2, rows, :]
        o = o1 - lam * o2
        o_ref[:, h * V_DIM:(h + 1) * V_DIM] = (_rms(o, sg_ref[...]) * (1.0 - lam_init)).astype(o_ref.dtype)


def _attn_init(m_ref, l_ref, acc_ref):
    m_ref[...] = jnp.full(m_ref.shape, NEG_INF, F32)
    l_ref[...] = jnp.zeros(l_ref.shape, F32)
    acc_ref[...] = jnp.zeros(acc_ref.shape, F32)


def _attn_prompt_kernel(lp_ref, sg_ref, q_ref, k_ref, v_ref, o_ref, qs_ref, m_ref, l_ref, acc_ref, *, tq, lam_init):
    i = pl.program_id(1)
    j = pl.program_id(2)

    @pl.when(j == 0)
    def _():
        _attn_init(m_ref, l_ref, acc_ref)
        _stack_q(q_ref, qs_ref)

    @pl.when(j < i)
    def _():
        _attn_update(qs_ref, k_ref[...], v_ref[...], m_ref, l_ref, acc_ref, tq, None)

    @pl.when(j == i)
    def _():
        tk = k_ref.shape[0]
        row_chunk = (lax.broadcasted_iota(jnp.int32, (2 * tq, tk), 0) % tq) // CHUNK
        col_chunk = lax.broadcasted_iota(jnp.int32, (2 * tq, tk), 1) // CHUNK
        _attn_update(qs_ref, k_ref[...], v_ref[...], m_ref, l_ref, acc_ref, tq, row_chunk >= col_chunk)
        _attn_finish(lp_ref, sg_ref, l_ref, acc_ref, o_ref, tq, lam_init)


def _attn_prompt(q, kb, vb, lam_params, sub_g, n_seq, seq_len, lam_init, tq=256):
    nq = seq_len // tq
    return pl.pallas_call(
        functools.partial(_attn_prompt_kernel, tq=tq, lam_init=lam_init),
        grid=(n_seq, nq, nq),
        in_specs=[pl.BlockSpec((4, HEAD_DIM), lambda b, i, j: (0, 0)),
                  pl.BlockSpec((1, V_DIM), lambda b, i, j: (0, 0)),
                  pl.BlockSpec((tq, 512), lambda b, i, j: (b * nq + i, 0)),
                  pl.BlockSpec((tq, 512), lambda b, i, j: (b * nq + jnp.minimum(j, i), 0)),
                  pl.BlockSpec((tq, 512), lambda b, i, j: (b * nq + jnp.minimum(j, i), 0))],
        out_specs=pl.BlockSpec((tq, 512), lambda b, i, j: (b * nq + i, 0)),
        out_shape=jax.ShapeDtypeStruct((n_seq * seq_len, 512), BF16),
        scratch_shapes=[pltpu.VMEM((4, 2 * tq, LANES), BF16), pltpu.VMEM((4, 2 * tq, 1), F32),
                        pltpu.VMEM((4, 2 * tq, 1), F32), pltpu.VMEM((4, 2 * tq, V_DIM), F32)],
        compiler_params=_cparams(("parallel", "parallel", "arbitrary")),
        name="attn_prompt",
    )(lam_params, sub_g.reshape(1, V_DIM), q, kb, vb)


def _attn_sample_kernel(lp_ref, sg_ref, q_ref, ck_ref, cv_ref, kn_ref, vn_ref, o_ref, qs_ref, m_ref, l_ref, acc_ref,
                        *, tq, lam_init):
    j = pl.program_id(1)

    @pl.when(j == 0)
    def _():
        _attn_init(m_ref, l_ref, acc_ref)
        _stack_q(q_ref, qs_ref)

    _attn_update(qs_ref, ck_ref[...].astype(BF16), cv_ref[...].astype(BF16), m_ref, l_ref, acc_ref, tq, None)

    @pl.when(j == pl.num_programs(1) - 1)
    def _():
        _attn_update(qs_ref, kn_ref[...], vn_ref[...], m_ref, l_ref, acc_ref, tq, None)
        _attn_finish(lp_ref, sg_ref, l_ref, acc_ref, o_ref, tq, lam_init)


def _attn_sample(q, kb, vb, cache_k, cache_v, layer, row0, lam_params, sub_g, lam_init, tk=512):
    _, n_seq, past, _ = cache_k.shape
    tq = (q.shape[0] - row0) // n_seq
    blk0 = row0 // tq
    new = pl.BlockSpec((tq, 512), lambda b, j: (blk0 + b, 0))
    cache = pl.BlockSpec((None, None, tk, 512), lambda b, j: (layer, b, j, 0))
    return pl.pallas_call(
        functools.partial(_attn_sample_kernel, tq=tq, lam_init=lam_init),
        grid=(n_seq, past // tk),
        in_specs=[pl.BlockSpec((4, HEAD_DIM), lambda b, j: (0, 0)), pl.BlockSpec((1, V_DIM), lambda b, j: (0, 0)),
                  new, cache, cache, new, new],
        out_specs=pl.BlockSpec((tq, 512), lambda b, j: (b, 0)),
        out_shape=jax.ShapeDtypeStruct((n_seq * tq, 512), BF16),
        scratch_shapes=[pltpu.VMEM((4, 2 * tq, LANES), BF16), pltpu.VMEM((4, 2 * tq, 1), F32),
                        pltpu.VMEM((4, 2 * tq, 1), F32), pltpu.VMEM((4, 2 * tq, V_DIM), F32)],
        compiler_params=_cparams(("parallel", "arbitrary")),
        name="attn_sample",
    )(lam_params, sub_g.reshape(1, V_DIM), q, cache_k, cache_v, kb, vb)


def _gelu_tanh(x):
    return 0.5 * x * (1.0 + jnp.tanh(math.sqrt(2.0 / math.pi) * (x + 0.044715 * (x * x * x))))


def _out_proj_kernel(x_ref, ys_ref, cv_ref, at_ref, wglu_ref, wo_ref, o_ref):
    z = _gelu_tanh(ys_ref[...])
    gate = jax.nn.sigmoid(jnp.dot(z.astype(BF16), wglu_ref[...], preferred_element_type=F32))
    ssm = (z * gate).astype(BF16)
    acc = jnp.dot(ssm, wo_ref[0:256, :], preferred_element_type=F32)
    acc = acc + jnp.dot(cv_ref[...].astype(BF16), wo_ref[256:512, :], preferred_element_type=F32)
    acc = acc + jnp.dot(at_ref[...], wo_ref[512:1024, :], preferred_element_type=F32)
    o_ref[...] = x_ref[...] + acc


def _out_proj(x, y_ssm, conv_out, attn_out, w_glu, w_out, tm=512):
    t = x.shape[0]
    row = lambda width: pl.BlockSpec((tm, width), lambda i: (i, 0))
    return pl.pallas_call(
        _out_proj_kernel,
        grid=(t // tm,),
        in_specs=[row(D_MODEL), row(256), row(256), row(512),
                  pl.BlockSpec((256, 256), lambda i: (0, 0)), pl.BlockSpec((D_MODEL, D_MODEL), lambda i: (0, 0))],
        out_specs=row(D_MODEL),
        out_shape=jax.ShapeDtypeStruct((t, D_MODEL), F32),
        compiler_params=_cparams(("parallel",)),
        name="out_proj",
    )(x, y_ssm, conv_out, attn_out, w_glu, w_out)


def _swiglu_step(h, wg, wu, wd):
    g = jnp.dot(h, wg, preferred_element_type=F32)
    u = jnp.dot(h, wu, preferred_element_type=F32)
    a = (g * jax.nn.sigmoid(g) * u).astype(BF16)
    return jnp.dot(a, wd, preferred_element_type=F32)


def _ffn_kernel(x_ref, g_ref, wg_ref, wu_ref, wd_ref, o_ref, h_ref, acc_ref):
    j = pl.program_id(1)

    @pl.when(j == 0)
    def _():
        h_ref[...] = _rms(x_ref[...], g_ref[...]).astype(BF16)
        acc_ref[...] = jnp.zeros(acc_ref.shape, F32)

    acc_ref[...] += _swiglu_step(h_ref[...], wg_ref[...], wu_ref[...], wd_ref[...])

    @pl.when(j == pl.num_programs(1) - 1)
    def _():
        o_ref[...] = x_ref[...] + acc_ref[...]


def _ffn(x, g, wg, wu, wd, tm=1024, tf=256):
    t = x.shape[0]
    return pl.pallas_call(
        _ffn_kernel,
        grid=(t // tm, D_FF // tf),
        in_specs=[pl.BlockSpec((tm, D_MODEL), lambda i, j: (i, 0)), pl.BlockSpec((1, D_MODEL), lambda i, j: (0, 0)),
                  pl.BlockSpec((D_MODEL, tf), lambda i, j: (0, j)), pl.BlockSpec((D_MODEL, tf), lambda i, j: (0, j)),
                  pl.BlockSpec((tf, D_MODEL), lambda i, j: (j, 0))],
        out_specs=pl.BlockSpec((tm, D_MODEL), lambda i, j: (i, 0)),
        out_shape=jax.ShapeDtypeStruct((t, D_MODEL), F32),
        scratch_shapes=[pltpu.VMEM((tm, D_MODEL), BF16), pltpu.VMEM((tm, D_MODEL), F32)],
        compiler_params=_cparams(("parallel", "arbitrary")),
        name="ffn_dense",
    )(x, g.reshape(1, D_MODEL), wg, wu, wd)


def _top2_gates(logits):
    lane = lax.broadcasted_iota(jnp.int32, logits.shape, 1)
    lg = jnp.where(lane < N_EXPERTS, logits, -jnp.inf)
    m1 = jnp.max(lg, axis=-1, keepdims=True)
    i1 = jnp.min(jnp.where(lg == m1, lane, LANES), axis=-1, keepdims=True)
    lg2 = jnp.where(lane == i1, -jnp.inf, lg)
    m2 = jnp.max(lg2, axis=-1, keepdims=True)
    i2 = jnp.min(jnp.where(lg2 == m2, lane, LANES), axis=-1, keepdims=True)
    e2 = jnp.exp(m2 - m1)
    den = 1.0 + e2
    return jnp.where(lane == i1, 1.0 / den, 0.0) + jnp.where(lane == i2, e2 / den, 0.0)


def _moe_kernel(x_ref, g_ref, r_ref, wg_ref, wu_ref, wd_ref, o_ref, h_ref, acc_ref, gate_ref):
    e = pl.program_id(1)
    j = pl.program_id(2)

    @pl.when((e == 0) & (j == 0))
    def _():
        h32 = _rms(x_ref[...], g_ref[...])
        h_ref[...] = h32.astype(BF16)
        acc_ref[...] = jnp.zeros(acc_ref.shape, F32)
        gates = _top2_gates(jnp.dot(h32, r_ref[...], preferred_element_type=F32, precision=lax.Precision.HIGHEST))
        lane = lax.broadcasted_iota(jnp.int32, gates.shape, 1)
        for k in range(N_EXPERTS):
            gate_ref[k] = jnp.sum(jnp.where(lane == k, gates, 0.0), axis=-1, keepdims=True)

    acc_ref[...] += gate_ref[e] * _swiglu_step(h_ref[...], wg_ref[...], wu_ref[...], wd_ref[...])

    @pl.when((e == pl.num_programs(1) - 1) & (j == pl.num_programs(2) - 1))
    def _():
        o_ref[...] = x_ref[...] + acc_ref[...]


def _moe(x, g, router_pad, wg, wu, wd, tm=1024, tf=256):
    t = x.shape[0]
    return pl.pallas_call(
        _moe_kernel,
        grid=(t // tm, N_EXPERTS, D_FF // tf),
        in_specs=[pl.BlockSpec((tm, D_MODEL), lambda i, e, j: (i, 0)),
                  pl.BlockSpec((1, D_MODEL), lambda i, e, j: (0, 0)),
                  pl.BlockSpec((D_MODEL, LANES), lambda i, e, j: (0, 0)),
                  pl.BlockSpec((None, D_MODEL, tf), lambda i, e, j: (e, 0, j)),
                  pl.BlockSpec((None, D_MODEL, tf), lambda i, e, j: (e, 0, j)),
                  pl.BlockSpec((None, tf, D_MODEL), lambda i, e, j: (e, j, 0))],
        out_specs=pl.BlockSpec((tm, D_MODEL), lambda i, e, j: (i, 0)),
        out_shape=jax.ShapeDtypeStruct((t, D_MODEL), F32),
        scratch_shapes=[pltpu.VMEM((tm, D_MODEL), BF16), pltpu.VMEM((tm, D_MODEL), F32),
                        pltpu.VMEM((N_EXPERTS, tm, 1), F32)],
        compiler_params=_cparams(("parallel", "arbitrary", "arbitrary")),
        name="ffn_moe",
    )(x, g.reshape(1, D_MODEL), router_pad, wg, wu, wd)


def _final_norm_kernel(x_ref, g_ref, o_ref):
    o_ref[...] = _rms(x_ref[...], g_ref[...])


def _final_norm(x, g, tm=1024):
    t = x.shape[0]
    return pl.pallas_call(
        _final_norm_kernel,
        grid=(t // tm,),
        in_specs=[pl.BlockSpec((tm, D_MODEL), lambda i: (i, 0)), pl.BlockSpec((1, D_MODEL), lambda i: (0, 0))],
        out_specs=pl.BlockSpec((tm, D_MODEL), lambda i: (i, 0)),
        out_shape=jax.ShapeDtypeStruct((t, D_MODEL), F32),
        compiler_params=_cparams(("parallel",)),
        name="final_norm",
    )(x, g.reshape(1, D_MODEL))


def kernel(x_prompt, x_sample, cache_k, cache_v, state_ssm_re, state_ssm_im, state_conv, norm_mix_g, w_in, ssm_a_re, ssm_a_im, ssm_log_dt, ssm_b_re, ssm_b_im, ssm_c_re, ssm_c_im, ssm_d, ssm_w_glu, conv_w, conv_ln_g, conv_ln_b, attn_lq1, attn_lk1, attn_lq2, attn_lk2, attn_sub_g, w_out, norm_ffn_g, ffn_w_gate, ffn_w_up, ffn_w_down, moe_router, moe_w_gate, moe_w_up, moe_w_down, final_norm_g):
    depth = w_in.shape[0]
    n_p, l_p, _ = x_prompt.shape
    n_s, l_s, _ = x_sample.shape
    t_p, t_s = n_p * l_p, n_s * l_s
    t_all = t_p + t_s
    past = cache_k.shape[2]
    nc_p, nc_s = l_p // S5_CHUNK, l_s // S5_CHUNK
    keep = CONV_KERNEL - 1

    x = jnp.concatenate([x_prompt.reshape(t_p, D_MODEL), x_sample.reshape(t_s, D_MODEL)], axis=0)
    cache_k = cache_k.reshape(depth, n_s, past, 512)
    cache_v = cache_v.reshape(depth, n_s, past, 512)

    outs = {name: [] for name in ("kp", "vp", "srp", "sip", "cp", "ks", "vs", "srs", "sis", "cs")}
    for l in range(depth):
        lam_init = 0.8 - 0.6 * math.exp(-0.3 * l)
        u, hc, q, k, v, kb, vb = _in_proj(x, norm_mix_g[l], w_in[l].astype(BF16))

        w_state, w_intra, w_inter, a_c, dsk = _s5_operators(
            ssm_a_re[l], ssm_a_im[l], ssm_log_dt[l], ssm_b_re[l], ssm_b_im[l], ssm_c_re[l], ssm_c_im[l],
            ssm_d[l], S5_CHUNK)
        u2 = u.reshape(t_all // S5_CHUNK, S5_CHUNK * SSM_WIDTH)
        d_state = _mm(u2, w_state, 256, 1024, "s5_state")
        y_intra = _mm(u2, w_intra, 256, 1024, "s5_intra")
        s0_p = jnp.zeros((n_p, 1, 2 * STATE_W), F32)
        s0_s = jnp.concatenate([state_ssm_re[l].reshape(n_s, 1, STATE_W), state_ssm_im[l].reshape(n_s, 1, STATE_W)],
                               axis=-1).astype(F32)
        ss_p, sf_p = _s5_scan(d_state, 0, n_p, nc_p, s0_p, a_c)
        ss_s, sf_s = _s5_scan(d_state, n_p * nc_p, n_s, nc_s, s0_s, a_c)
        s_start = jnp.concatenate([ss_p, ss_s], axis=0)
        y_ssm = _s5_out(y_intra, s_start, w_inter, u2, dsk).reshape(t_all, SSM_WIDTH)

        cv_p, cb_p = _conv(hc[:t_p].reshape(n_p, l_p, CONV_WIDTH), jnp.zeros((n_p, keep, CONV_WIDTH), F32),
                           conv_w[l], conv_ln_g[l], conv_ln_b[l], tl=256)
        cv_s, cb_s = _conv(hc[t_p:].reshape(n_s, l_s, CONV_WIDTH), state_conv[l].astype(F32),
                           conv_w[l], conv_ln_g[l], conv_ln_b[l], tl=l_s)
        conv_out = jnp.concatenate([cv_p.reshape(t_p, CONV_WIDTH), cv_s.reshape(t_s, CONV_WIDTH)], axis=0)

        lam_params = jnp.stack([attn_lq1[l], attn_lk1[l], attn_lq2[l], attn_lk2[l]]).astype(F32)
        at_p = _attn_prompt(q, kb, vb, lam_params, attn_sub_g[l], n_p, l_p, lam_init)
        at_s = _attn_sample(q, kb, vb, cache_k, cache_v, l, t_p, lam_params, attn_sub_g[l], lam_init)
        attn_out = jnp.concatenate([at_p, at_s], axis=0)

        x = _out_proj(x, y_ssm, conv_out, attn_out, ssm_w_glu[l].astype(BF16), w_out[l].astype(BF16))

        j = l // 2
        if l % 2 == 0:
            x = _ffn(x, norm_ffn_g[l], ffn_w_gate[j].astype(BF16), ffn_w_up[j].astype(BF16),
                     ffn_w_down[j].astype(BF16))
        else:
            router_pad = jnp.pad(moe_router[j].astype(F32), ((0, 0), (0, LANES - N_EXPERTS)))
            x = _moe(x, norm_ffn_g[l], router_pad, moe_w_gate[j].astype(BF16), moe_w_up[j].astype(BF16),
                     moe_w_down[j].astype(BF16))

        outs["kp"].append(k[:t_p].reshape(n_p, l_p, 2, N_HEADS, HEAD_DIM))
        outs["vp"].append(v[:t_p].reshape(n_p, l_p, N_HEADS, V_DIM))
        outs["srp"].append(sf_p[:, 0, :STATE_W].reshape(n_p, N_SSM_GROUPS, SSM_STATE))
        outs["sip"].append(sf_p[:, 0, STATE_W:].reshape(n_p, N_SSM_GROUPS, SSM_STATE))
        outs["cp"].append(cb_p)
        outs["ks"].append(k[t_p:].reshape(n_s, l_s, 2, N_HEADS, HEAD_DIM))
        outs["vs"].append(v[t_p:].reshape(n_s, l_s, N_HEADS, V_DIM))
        outs["srs"].append(sf_s[:, 0, :STATE_W].reshape(n_s, N_SSM_GROUPS, SSM_STATE))
        outs["sis"].append(sf_s[:, 0, STATE_W:].reshape(n_s, N_SSM_GROUPS, SSM_STATE))
        outs["cs"].append(cb_s)

    y = _final_norm(x, final_norm_g)
    st = {name: jnp.stack(vals) for name, vals in outs.items()}
    return (y[:t_p].reshape(n_p, l_p, D_MODEL), y[t_p:].reshape(n_s, l_s, D_MODEL),
            st["kp"], st["vp"], st["srp"], st["sip"], st["cp"],
            st["ks"], st["vs"], st["srs"], st["sis"], st["cs"])
```

```python
import functools
import math

import jax
import jax.numpy as jnp
from jax import lax
from jax.experimental import pallas as pl
from jax.experimental.pallas import tpu as pltpu

F32 = jnp.float32
BF16 = jnp.bfloat16

D_MODEL = 1024
CHUNK = 64
SSM_GROUP = 16
N_SSM_GROUPS = 16
SSM_WIDTH = 256
SSM_STATE = 64
STATE_W = N_SSM_GROUPS * SSM_STATE
CONV_WIDTH = 256
CONV_KERNEL = 31
N_HEADS = 4
HEAD_DIM = 64
V_DIM = 128
QK_WIDTH = 256
ATTN_WIDTH = 512
IN_WIDTH = 2304
ATTN_SCALE = 1.0 / math.sqrt(HEAD_DIM)
LOG2E = math.log2(math.e)
NEG_INF = -1e30
D_FF = 2816
N_EXPERTS = 8
EPS = 1e-6

S5_CHUNK = 8
LANES = 128
VMEM_LIMIT = 48 * 1024 * 1024


def _cparams(sem):
    return pltpu.CompilerParams(dimension_semantics=sem, vmem_limit_bytes=VMEM_LIMIT)


def _rms(x, g):
    return x * lax.rsqrt(jnp.mean(x * x, axis=-1, keepdims=True) + EPS) * g


def _in_proj_kernel(x_ref, g_ref, w_ref, wvt_ref, u_ref, hc_ref, q_ref, k_ref, v_ref, kb_ref, vb_ref, vt_ref):
    h = _rms(x_ref[...], g_ref[...]).astype(BF16)

    def proj(lo, hi):
        return jnp.dot(h, w_ref[:, lo:hi], preferred_element_type=F32)

    u_ref[...] = proj(0, 256)
    c_val = proj(256, 512)
    c_gate = proj(512, 768)
    hc_ref[...] = c_val * jax.nn.sigmoid(c_gate)
    q_ref[...] = (proj(768, 1280) * (ATTN_SCALE * LOG2E)).astype(BF16)
    k = proj(1280, 1792)
    k_ref[...] = k
    kb_ref[...] = k.astype(BF16)
    v = proj(1792, 2304)
    v_ref[...] = v
    vb_ref[...] = v.astype(BF16)
    vt_ref[...] = lax.dot_general(wvt_ref[...], h, (((1,), (1,)), ((), ())),
                                  preferred_element_type=F32).astype(BF16)


def _in_proj(x, g, w_bf16, wvt_bf16, tm=512):
    t = x.shape[0]
    row = lambda width: pl.BlockSpec((tm, width), lambda i: (i, 0))
    return pl.pallas_call(
        _in_proj_kernel,
        grid=(t // tm,),
        in_specs=[row(D_MODEL), pl.BlockSpec((1, D_MODEL), lambda i: (0, 0)),
                  pl.BlockSpec((D_MODEL, IN_WIDTH), lambda i: (0, 0)),
                  pl.BlockSpec((ATTN_WIDTH, D_MODEL), lambda i: (0, 0))],
        out_specs=[row(256), row(256), row(512), row(512), row(512), row(512), row(512),
                   pl.BlockSpec((ATTN_WIDTH, tm), lambda i: (0, i))],
        out_shape=[jax.ShapeDtypeStruct((t, 256), F32), jax.ShapeDtypeStruct((t, 256), F32),
                   jax.ShapeDtypeStruct((t, 512), BF16), jax.ShapeDtypeStruct((t, 512), F32),
                   jax.ShapeDtypeStruct((t, 512), F32), jax.ShapeDtypeStruct((t, 512), BF16),
                   jax.ShapeDtypeStruct((t, 512), BF16), jax.ShapeDtypeStruct((ATTN_WIDTH, t), BF16)],
        compiler_params=_cparams(("parallel",)),
        name="in_proj",
    )(x, g.reshape(1, D_MODEL), w_bf16, wvt_bf16)


def _mm_kernel(x_ref, w_ref, o_ref):
    o_ref[...] = jnp.dot(x_ref[...].astype(BF16), w_ref[...], preferred_element_type=F32)


def _mm(x, w_bf16, tm, tn, name):
    m, k = x.shape
    n = w_bf16.shape[1]
    return pl.pallas_call(
        _mm_kernel,
        grid=(n // tn, m // tm),
        in_specs=[pl.BlockSpec((tm, k), lambda j, i: (i, 0)), pl.BlockSpec((k, tn), lambda j, i: (0, j))],
        out_specs=pl.BlockSpec((tm, tn), lambda j, i: (i, j)),
        out_shape=jax.ShapeDtypeStruct((m, n), F32),
        compiler_params=_cparams(("parallel", "parallel")),
        name=name,
    )(x, w_bf16)


def _s5_scan_kernel(d_ref, s0_ref, a_ref, ss_ref, sf_ref, st_ref, *, rows):
    t = pl.program_id(1)

    @pl.when(t == 0)
    def _():
        st_ref[...] = s0_ref[...]

    a_re = a_ref[:, :STATE_W]
    a_im = a_ref[:, STATE_W:]

    def step(c, carry):
        s_re, s_im = carry
        ss_ref[pl.ds(c, 1), :STATE_W] = s_re
        ss_ref[pl.ds(c, 1), STATE_W:] = s_im
        d = d_ref[pl.ds(c, 1), :]
        n_re = a_re * s_re - a_im * s_im + d[:, :STATE_W]
        n_im = a_re * s_im + a_im * s_re + d[:, STATE_W:]
        return n_re, n_im

    s_re, s_im = lax.fori_loop(0, rows, step, (st_ref[:, :STATE_W], st_ref[:, STATE_W:]))
    st_ref[:, :STATE_W] = s_re
    st_ref[:, STATE_W:] = s_im
    sf_ref[...] = st_ref[...]


def _s5_scan(d, row0, n_seq, n_chunks, s0, a_c):
    rows = min(n_chunks, 256)
    nt = n_chunks // rows
    blk0 = row0 // rows
    w = 2 * STATE_W
    return pl.pallas_call(
        functools.partial(_s5_scan_kernel, rows=rows),
        grid=(n_seq, nt),
        in_specs=[pl.BlockSpec((rows, w), lambda s, t: (blk0 + s * nt + t, 0)),
                  pl.BlockSpec((None, 1, w), lambda s, t: (s, 0, 0)),
                  pl.BlockSpec((1, w), lambda s, t: (0, 0))],
        out_specs=[pl.BlockSpec((rows, w), lambda s, t: (s * nt + t, 0)),
                   pl.BlockSpec((None, 1, w), lambda s, t: (s, 0, 0))],
        out_shape=[jax.ShapeDtypeStruct((n_seq * n_chunks, w), F32), jax.ShapeDtypeStruct((n_seq, 1, w), F32)],
        scratch_shapes=[pltpu.VMEM((1, w), F32)],
        compiler_params=_cparams(("parallel", "arbitrary")),
        name="s5_scan",
    )(d, s0, a_c)


def _s5_out_kernel(yi_ref, ss_ref, w_ref, u_ref, dsk_ref, y_ref):
    y_ref[...] = (yi_ref[...] + jnp.dot(ss_ref[...].astype(BF16), w_ref[...], preferred_element_type=F32)
                  + dsk_ref[...] * u_ref[...])


def _s5_out(y_intra, s_start, w_inter, u2, dsk, tm=256, tn=512):
    m, n = y_intra.shape
    k = s_start.shape[1]
    tile = pl.BlockSpec((tm, tn), lambda j, i: (i, j))
    return pl.pallas_call(
        _s5_out_kernel,
        grid=(n // tn, m // tm),
        in_specs=[tile, pl.BlockSpec((tm, k), lambda j, i: (i, 0)), pl.BlockSpec((k, tn), lambda j, i: (0, j)),
                  tile, pl.BlockSpec((1, tn), lambda j, i: (0, j))],
        out_specs=tile,
        out_shape=jax.ShapeDtypeStruct((m, n), F32),
        compiler_params=_cparams(("parallel", "parallel")),
        name="s5_out",
    )(y_intra, s_start, w_inter, u2, dsk)


def _s5_operators(a_re, a_im, log_dt, b_re, b_im, c_re, c_im, d_skip, n_c):
    g_n, p_n, h_n = N_SSM_GROUPS, SSM_STATE, SSM_GROUP
    lam = lax.complex(a_re.astype(F32), a_im.astype(F32))
    dt = jnp.exp(log_dt.astype(F32))[:, None]
    a_bar = jnp.exp(lam * dt)
    b_bar = ((a_bar - 1.0) / lam)[..., None] * lax.complex(b_re.astype(F32), b_im.astype(F32))
    c = lax.complex(c_re.astype(F32), c_im.astype(F32))
    pows = [jnp.ones_like(a_bar)]
    for _ in range(n_c):
        pows.append(pows[-1] * a_bar)
    a_pow = jnp.stack(pows)
    eye_g = jnp.eye(g_n, dtype=F32)

    ws = a_pow[:n_c][::-1][:, :, :, None] * b_bar[None]
    ws = jnp.transpose(ws, (0, 1, 3, 2))

    def to_state(x):
        return (x[:, :, :, None, :] * eye_g[None, :, None, :, None]).reshape(n_c * g_n * h_n, g_n * p_n)

    w_state = jnp.concatenate([to_state(jnp.real(ws)), to_state(jnp.imag(ws))], axis=1)

    ker = jnp.real(jnp.einsum('ghp,kgp,gpx->kghx', c, a_pow[:n_c], b_bar, precision=lax.Precision.HIGHEST))
    lag = jnp.arange(n_c)[None, :] - jnp.arange(n_c)[:, None]
    toe = jnp.where((lag >= 0)[:, :, None, None, None], ker[jnp.clip(lag, 0, n_c - 1)], 0.0)
    toe = jnp.transpose(toe, (0, 2, 4, 1, 3))
    w_intra = (toe[:, :, :, :, None, :] * eye_g[None, :, None, None, :, None]).reshape(n_c * 256, n_c * 256)

    ca = c[None] * jnp.transpose(a_pow[1:], (0, 1, 2))[:, :, None, :]
    ca = jnp.transpose(ca, (1, 3, 0, 2))

    def from_state(x):
        return (x[:, :, :, None, :] * eye_g[:, None, None, :, None]).reshape(g_n * p_n, n_c * 256)

    w_inter = jnp.concatenate([from_state(jnp.real(ca)), from_state(-jnp.imag(ca))], axis=0)
    a_c = jnp.concatenate([jnp.real(a_pow[n_c]).reshape(1, -1), jnp.imag(a_pow[n_c]).reshape(1, -1)], axis=1)
    dsk = jnp.tile(d_skip.astype(F32).reshape(1, SSM_WIDTH), (1, n_c))
    return w_state.astype(BF16), w_intra.astype(BF16), w_inter.astype(BF16), a_c, dsk


_CONV_PAD = 32
_CONV_RB = 64


def _conv_kernel(h_ref, buf_ref, w_ref, g_ref, b_ref, y_ref, nb_ref, xp_ref, *, tl):
    t = pl.program_id(1)
    keep = CONV_KERNEL - 1
    lo = _CONV_PAD - keep

    @pl.when(t == 0)
    def _():
        xp_ref[lo:_CONV_PAD, :] = buf_ref[...]

    @pl.when(t > 0)
    def _():
        xp_ref[lo:_CONV_PAD, :] = xp_ref[tl + lo:tl + _CONV_PAD, :]

    xp_ref[_CONV_PAD:_CONV_PAD + tl, :] = h_ref[...]
    for r in range(tl // _CONV_RB):
        base = r * _CONV_RB
        acc = jnp.zeros((_CONV_RB, CONV_WIDTH), F32)
        for k in range(CONV_KERNEL):
            acc = acc + w_ref[k:k + 1, :] * xp_ref[base + lo + k:base + lo + k + _CONV_RB, :]
        xc = acc - jnp.mean(acc, axis=-1, keepdims=True)
        var = jnp.mean(xc * xc, axis=-1, keepdims=True)
        y = xc * lax.rsqrt(var + EPS) * g_ref[...] + b_ref[...]
        y_ref[base:base + _CONV_RB, :] = y * jax.nn.sigmoid(y)
    nb_ref[...] = xp_ref[tl + lo:tl + _CONV_PAD, :]


def _conv(h, buf, w, ln_g, ln_b, tl):
    s, l, c = h.shape
    keep = CONV_KERNEL - 1
    return pl.pallas_call(
        functools.partial(_conv_kernel, tl=tl),
        grid=(s, l // tl),
        in_specs=[pl.BlockSpec((None, tl, c), lambda i, t: (i, t, 0)),
                  pl.BlockSpec((None, keep, c), lambda i, t: (i, 0, 0)),
                  pl.BlockSpec((CONV_KERNEL, c), lambda i, t: (0, 0)),
                  pl.BlockSpec((1, c), lambda i, t: (0, 0)), pl.BlockSpec((1, c), lambda i, t: (0, 0))],
        out_specs=[pl.BlockSpec((None, tl, c), lambda i, t: (i, t, 0)),
                   pl.BlockSpec((None, keep, c), lambda i, t: (i, 0, 0))],
        out_shape=[jax.ShapeDtypeStruct((s, l, c), F32), jax.ShapeDtypeStruct((s, keep, c), F32)],
        scratch_shapes=[pltpu.VMEM((tl + _CONV_PAD, c), F32)],
        compiler_params=_cparams(("parallel", "arbitrary")),
        name="conv_module",
    )(h, buf, w, ln_g.reshape(1, c), ln_b.reshape(1, c))


def _stack_q(q_ref, qs_ref):
    tq = q_ref.shape[0]
    lane = lax.broadcasted_iota(jnp.int32, (tq, LANES), 1)
    for b in range(4):
        qb = q_ref[:, b * LANES:(b + 1) * LANES]
        qs_ref[b, :tq, :] = jnp.where(lane < HEAD_DIM, qb, jnp.zeros_like(qb))
        qs_ref[b, tq:, :] = jnp.where(lane >= HEAD_DIM, qb, jnp.zeros_like(qb))


def _attn_update(qs_ref, k, v, m_ref, l_ref, acc_ref, tq, mask):
    for b in range(4):
        s = lax.dot_general(qs_ref[b], k[:, b * LANES:(b + 1) * LANES], (((1,), (1,)), ((), ())),
                            preferred_element_type=F32)
        if mask is not None:
            s = jnp.where(mask, s, NEG_INF)
        m_old = m_ref[b]
        m_new = jnp.maximum(m_old, jnp.max(s, axis=-1, keepdims=True))
        alpha = jnp.exp2(m_old - m_new)
        p = jnp.exp2(s - m_new)
        l_ref[b] = alpha * l_ref[b] + jnp.sum(p, axis=-1, keepdims=True)
        m_ref[b] = m_new
        pb = p.astype(BF16)
        h0 = 2 * (b % 2)
        pv0 = jnp.dot(pb[:tq], v[:, h0 * V_DIM:(h0 + 1) * V_DIM], preferred_element_type=F32)
        pv1 = jnp.dot(pb[tq:], v[:, (h0 + 1) * V_DIM:(h0 + 2) * V_DIM], preferred_element_type=F32)
        acc_ref[b, :tq, :] = alpha[:tq] * acc_ref[b, :tq, :] + pv0
        acc_ref[b, tq:, :] = alpha[tq:] * acc_ref[b, tq:, :] + pv1


def _attn_finish(lp_ref, sg_ref, l_ref, acc_ref, o_ref, tq, lam_init):
    lam = _lambda(lp_ref, lam_init)
    for h in range(N_HEADS):
        b1, half = h // 2, h % 2
        rows = slice(half * tq, (half + 1) * tq)
        o1 = acc_ref[b1, rows, :] / l_ref[b1, rows, :]
        o2 = acc_ref[b1 + 2, rows, :] / l_ref[b1 + 2, rows, :]
        o = o1 - lam * o2
        o_ref[:, h * V_DIM:(h + 1) * V_DIM] = (_rms(o, sg_ref[...]) * (1.0 - lam_init)).astype(o_ref.dtype)


def _attn_init(m_ref, l_ref, acc_ref):
    m_ref[...] = jnp.full(m_ref.shape, NEG_INF, F32)
    l_ref[...] = jnp.zeros(l_ref.shape, F32)
    acc_ref[...] = jnp.zeros(acc_ref.shape, F32)


def _attn_prompt_update(qs_ref, k_ref, vt_ref, m_ref, l_ref, acc_ref, tq, mask):
    for b in range(4):
        s = lax.dot_general(k_ref[:, b * LANES:(b + 1) * LANES], qs_ref[b], (((1,), (1,)), ((), ())),
                            preferred_element_type=F32)
        if mask is not None:
            s = jnp.where(mask, s, NEG_INF)
        m_old = m_ref[b]
        m_new = jnp.maximum(m_old, jnp.max(s, axis=0, keepdims=True))
        alpha = jnp.exp2(m_old - m_new)
        p = jnp.exp2(s - m_new)
        l_ref[b] = alpha * l_ref[b] + jnp.sum(p, axis=0, keepdims=True)
        m_ref[b] = m_new
        pb = p.astype(BF16)
        for r in range(2):
            h = 2 * (b % 2) + r
            cols = slice(r * tq, (r + 1) * tq)
            pv = jnp.dot(vt_ref[h * V_DIM:(h + 1) * V_DIM, :], pb[:, cols], preferred_element_type=F32)
            acc_ref[2 * b + r] = alpha[:, cols] * acc_ref[2 * b + r] + pv


def _lambda(lp_ref, lam_init):
    lp = lp_ref[...]
    return (jnp.exp(jnp.sum(lp[0:1] * lp[1:2], axis=-1, keepdims=True))
            - jnp.exp(jnp.sum(lp[2:3] * lp[3:4], axis=-1, keepdims=True)) + lam_init)


def _attn_prompt_finish(lp_ref, sgt_ref, l_ref, acc_ref, o_ref, tq, lam_init):
    lam = _lambda(lp_ref, lam_init)
    for h in range(N_HEADS):
        b1, r = h // 2, h % 2
        cols = slice(r * tq, (r + 1) * tq)
        o1 = acc_ref[2 * b1 + r] / l_ref[b1][:, cols]
        o2 = acc_ref[2 * (b1 + 2) + r] / l_ref[b1 + 2][:, cols]
        o = o1 - lam * o2
        on = o * lax.rsqrt(jnp.mean(o * o, axis=0, keepdims=True) + EPS) * sgt_ref[...] * (1.0 - lam_init)
        o_ref[:, h * V_DIM:(h + 1) * V_DIM] = on.T.astype(o_ref.dtype)


def _attn_prompt_kernel(it_ref, jt_ref, lp_ref, sgt_ref, q_ref, k_ref, vt_ref, o_ref, qs_ref, m_ref, l_ref, acc_ref,
                        *, tq, lam_init):
    step = pl.program_id(1)
    i = it_ref[step]
    j = jt_ref[step]

    @pl.when(j == 0)
    def _():
        _attn_init(m_ref, l_ref, acc_ref)
        _stack_q(q_ref, qs_ref)

    @pl.when(j < i)
    def _():
        _attn_prompt_update(qs_ref, k_ref, vt_ref, m_ref, l_ref, acc_ref, tq, None)

    @pl.when(j == i)
    def _():
        tk = k_ref.shape[0]
        key_chunk = lax.broadcasted_iota(jnp.int32, (tk, 2 * tq), 0) // CHUNK
        qry_chunk = (lax.broadcasted_iota(jnp.int32, (tk, 2 * tq), 1) % tq) // CHUNK
        _attn_prompt_update(qs_ref, k_ref, vt_ref, m_ref, l_ref, acc_ref, tq, qry_chunk >= key_chunk)
        _attn_prompt_finish(lp_ref, sgt_ref, l_ref, acc_ref, o_ref, tq, lam_init)


def _attn_prompt(q, kb, vt, lam_params, sub_g, n_seq, seq_len, lam_init, tq=256):
    nq = seq_len // tq
    pairs = [(i, j) for i in range(nq) for j in range(i + 1)]
    i_tab = jnp.asarray([p[0] for p in pairs], jnp.int32)
    j_tab = jnp.asarray([p[1] for p in pairs], jnp.int32)
    grid_spec = pltpu.PrefetchScalarGridSpec(
        num_scalar_prefetch=2,
        grid=(n_seq, len(pairs)),
        in_specs=[pl.BlockSpec((4, HEAD_DIM), lambda b, s, it, jt: (0, 0)),
                  pl.BlockSpec((V_DIM, 1), lambda b, s, it, jt: (0, 0)),
                  pl.BlockSpec((tq, 512), lambda b, s, it, jt: (b * nq + it[s], 0)),
                  pl.BlockSpec((tq, 512), lambda b, s, it, jt: (b * nq + jt[s], 0)),
                  pl.BlockSpec((512, tq), lambda b, s, it, jt: (0, b * nq + jt[s]))],
        out_specs=pl.BlockSpec((tq, 512), lambda b, s, it, jt: (b * nq + it[s], 0)),
        scratch_shapes=[pltpu.VMEM((4, 2 * tq, LANES), BF16), pltpu.VMEM((4, 1, 2 * tq), F32),
                        pltpu.VMEM((4, 1, 2 * tq), F32), pltpu.VMEM((8, V_DIM, tq), F32)],
    )
    return pl.pallas_call(
        functools.partial(_attn_prompt_kernel, tq=tq, lam_init=lam_init),
        grid_spec=grid_spec,
        out_shape=jax.ShapeDtypeStruct((n_seq * seq_len, 512), BF16),
        compiler_params=_cparams(("parallel", "arbitrary")),
        name="attn_prompt",
    )(i_tab, j_tab, lam_params, sub_g.reshape(V_DIM, 1), q, kb, vt)


def _attn_sample_kernel(lp_ref, sg_ref, q_ref, ck_ref, cv_ref, kn_ref, vn_ref, o_ref, qs_ref, m_ref, l_ref, acc_ref,
                        *, tq, lam_init):
    j = pl.program_id(1)

    @pl.when(j == 0)
    def _():
        _attn_init(m_ref, l_ref, acc_ref)
        _stack_q(q_ref, qs_ref)

    _attn_update(qs_ref, ck_ref[...].astype(BF16), cv_ref[...].astype(BF16), m_ref, l_ref, acc_ref, tq, None)

    @pl.when(j == pl.num_programs(1) - 1)
    def _():
        _attn_update(qs_ref, kn_ref[...], vn_ref[...], m_ref, l_ref, acc_ref, tq, None)
        _attn_finish(lp_ref, sg_ref, l_ref, acc_ref, o_ref, tq, lam_init)


def _attn_sample(q, kb, vb, cache_k, cache_v, layer, row0, lam_params, sub_g, lam_init, tk=512):
    _, n_seq, past, _ = cache_k.shape
    tq = (q.shape[0] - row0) // n_seq
    blk0 = row0 // tq
    new = pl.BlockSpec((tq, 512), lambda b, j: (blk0 + b, 0))
    cache = pl.BlockSpec((None, None, tk, 512), lambda b, j: (layer, b, j, 0))
    return pl.pallas_call(
        functools.partial(_attn_sample_kernel, tq=tq, lam_init=lam_init),
        grid=(n_seq, past // tk),
        in_specs=[pl.BlockSpec((4, HEAD_DIM), lambda b, j: (0, 0)), pl.BlockSpec((1, V_DIM), lambda b, j: (0, 0)),
                  new, cache, cache, new, new],
        out_specs=pl.BlockSpec((tq, 512), lambda b, j: (b, 0)),
        out_shape=jax.ShapeDtypeStruct((n_seq * tq, 512), BF16),
        scratch_shapes=[pltpu.VMEM((4, 2 * tq, LANES), BF16), pltpu.VMEM((4, 2 * tq, 1), F32),
                        pltpu.VMEM((4, 2 * tq, 1), F32), pltpu.VMEM((4, 2 * tq, V_DIM), F32)],
        compiler_params=_cparams(("parallel", "arbitrary")),
        name="attn_sample",
    )(lam_params, sub_g.reshape(1, V_DIM), q, cache_k, cache_v, kb, vb)


def _gelu_tanh(x):
    return 0.5 * x * (1.0 + jnp.tanh(math.sqrt(2.0 / math.pi) * (x + 0.044715 * (x * x * x))))


def _out_proj_kernel(x_ref, ys_ref, cv_ref, at_ref, wglu_ref, wo_ref, o_ref):
    z = _gelu_tanh(ys_ref[...])
    gate = jax.nn.sigmoid(jnp.dot(z.astype(BF16), wglu_ref[...], preferred_element_type=F32))
    ssm = (z * gate).astype(BF16)
    acc = jnp.dot(ssm, wo_ref[0:256, :], preferred_element_type=F32)
    acc = acc + jnp.dot(cv_ref[...].astype(BF16), wo_ref[256:512, :], preferred_element_type=F32)
    acc = acc + jnp.dot(at_ref[...], wo_ref[512:1024, :], preferred_element_type=F32)
    o_ref[...] = x_ref[...] + acc


def _out_proj(x, y_ssm, conv_out, attn_out, w_glu, w_out, tm=512):
    t = x.shape[0]
    row = lambda width: pl.BlockSpec((tm, width), lambda i: (i, 0))
    return pl.pallas_call(
        _out_proj_kernel,
        grid=(t // tm,),
        in_specs=[row(D_MODEL), row(256), row(256), row(512),
                  pl.BlockSpec((256, 256), lambda i: (0, 0)), pl.BlockSpec((D_MODEL, D_MODEL), lambda i: (0, 0))],
        out_specs=row(D_MODEL),
        out_shape=jax.ShapeDtypeStruct((t, D_MODEL), F32),
        compiler_params=_cparams(("parallel",)),
        name="out_proj",
    )(x, y_ssm, conv_out, attn_out, w_glu, w_out)


def _swiglu_step(h, wg, wu, wd):
    g = jnp.dot(h, wg, preferred_element_type=F32)
    u = jnp.dot(h, wu, preferred_element_type=F32)
    a = (g * jax.nn.sigmoid(g) * u).astype(BF16)
    return jnp.dot(a, wd, preferred_element_type=F32)


def _ffn_kernel(x_ref, g_ref, wg_ref, wu_ref, wd_ref, o_ref, h_ref, acc_ref):
    j = pl.program_id(1)

    @pl.when(j == 0)
    def _():
        h_ref[...] = _rms(x_ref[...], g_ref[...]).astype(BF16)
        acc_ref[...] = jnp.zeros(acc_ref.shape, F32)

    acc_ref[...] += _swiglu_step(h_ref[...], wg_ref[...], wu_ref[...], wd_ref[...])

    @pl.when(j == pl.num_programs(1) - 1)
    def _():
        o_ref[...] = x_ref[...] + acc_ref[...]


def _ffn(x, g, wg, wu, wd, tm=1024, tf=256):
    t = x.shape[0]
    return pl.pallas_call(
        _ffn_kernel,
        grid=(t // tm, D_FF // tf),
        in_specs=[pl.BlockSpec((tm, D_MODEL), lambda i, j: (i, 0)), pl.BlockSpec((1, D_MODEL), lambda i, j: (0, 0)),
                  pl.BlockSpec((D_MODEL, tf), lambda i, j: (0, j)), pl.BlockSpec((D_MODEL, tf), lambda i, j: (0, j)),
                  pl.BlockSpec((tf, D_MODEL), lambda i, j: (j, 0))],
        out_specs=pl.BlockSpec((tm, D_MODEL), lambda i, j: (i, 0)),
        out_shape=jax.ShapeDtypeStruct((t, D_MODEL), F32),
        scratch_shapes=[pltpu.VMEM((tm, D_MODEL), BF16), pltpu.VMEM((tm, D_MODEL), F32)],
        compiler_params=_cparams(("parallel", "arbitrary")),
        name="ffn_dense",
    )(x, g.reshape(1, D_MODEL), wg, wu, wd)


def _top2_gates(logits):
    lane = lax.broadcasted_iota(jnp.int32, logits.shape, 1)
    lg = jnp.where(lane < N_EXPERTS, logits, -jnp.inf)
    m1 = jnp.max(lg, axis=-1, keepdims=True)
    i1 = jnp.min(jnp.where(lg == m1, lane, LANES), axis=-1, keepdims=True)
    lg2 = jnp.where(lane == i1, -jnp.inf, lg)
    m2 = jnp.max(lg2, axis=-1, keepdims=True)
    i2 = jnp.min(jnp.where(lg2 == m2, lane, LANES), axis=-1, keepdims=True)
    e2 = jnp.exp(m2 - m1)
    den = 1.0 + e2
    return jnp.where(lane == i1, 1.0 / den, 0.0) + jnp.where(lane == i2, e2 / den, 0.0)


def _moe_kernel(x_ref, g_ref, r_ref, wg_ref, wu_ref, wd_ref, o_ref, h_ref, acc_ref, gate_ref):
    e = pl.program_id(1)
    j = pl.program_id(2)

    @pl.when((e == 0) & (j == 0))
    def _():
        h32 = _rms(x_ref[...], g_ref[...])
        h_ref[...] = h32.astype(BF16)
        acc_ref[...] = jnp.zeros(acc_ref.shape, F32)
        gates = _top2_gates(jnp.dot(h32, r_ref[...], preferred_element_type=F32, precision=lax.Precision.HIGHEST))
        lane = lax.broadcasted_iota(jnp.int32, gates.shape, 1)
        for k in range(N_EXPERTS):
            gate_ref[k] = jnp.sum(jnp.where(lane == k, gates, 0.0), axis=-1, keepdims=True)

    acc_ref[...] += gate_ref[e] * _swiglu_step(h_ref[...], wg_ref[...], wu_ref[...], wd_ref[...])

    @pl.when((e == pl.num_programs(1) - 1) & (j == pl.num_programs(2) - 1))
    def _():
        o_ref[...] = x_ref[...] + acc_ref[...]


def _moe(x, g, router_pad, wg, wu, wd, tm=1024, tf=256):
    t = x.shape[0]
    return pl.pallas_call(
        _moe_kernel,
        grid=(t // tm, N_EXPERTS, D_FF // tf),
        in_specs=[pl.BlockSpec((tm, D_MODEL), lambda i, e, j: (i, 0)),
                  pl.BlockSpec((1, D_MODEL), lambda i, e, j: (0, 0)),
                  pl.BlockSpec((D_MODEL, LANES), lambda i, e, j: (0, 0)),
                  pl.BlockSpec((None, D_MODEL, tf), lambda i, e, j: (e, 0, j)),
                  pl.BlockSpec((None, D_MODEL, tf), lambda i, e, j: (e, 0, j)),
                  pl.BlockSpec((None, tf, D_MODEL), lambda i, e, j: (e, j, 0))],
        out_specs=pl.BlockSpec((tm, D_MODEL), lambda i, e, j: (i, 0)),
        out_shape=jax.ShapeDtypeStruct((t, D_MODEL), F32),
        scratch_shapes=[pltpu.VMEM((tm, D_MODEL), BF16), pltpu.VMEM((tm, D_MODEL), F32),
                        pltpu.VMEM((N_EXPERTS, tm, 1), F32)],
        compiler_params=_cparams(("parallel", "arbitrary", "arbitrary")),
        name="ffn_moe",
    )(x, g.reshape(1, D_MODEL), router_pad, wg, wu, wd)


def _final_norm_kernel(x_ref, g_ref, o_ref):
    o_ref[...] = _rms(x_ref[...], g_ref[...])


def _final_norm(x, g, tm=1024):
    t = x.shape[0]
    return pl.pallas_call(
        _final_norm_kernel,
        grid=(t // tm,),
        in_specs=[pl.BlockSpec((tm, D_MODEL), lambda i: (i, 0)), pl.BlockSpec((1, D_MODEL), lambda i: (0, 0))],
        out_specs=pl.BlockSpec((tm, D_MODEL), lambda i: (i, 0)),
        out_shape=jax.ShapeDtypeStruct((t, D_MODEL), F32),
        compiler_params=_cparams(("parallel",)),
        name="final_norm",
    )(x, g.reshape(1, D_MODEL))


def kernel(x_prompt, x_sample, cache_k, cache_v, state_ssm_re, state_ssm_im, state_conv, norm_mix_g, w_in, ssm_a_re, ssm_a_im, ssm_log_dt, ssm_b_re, ssm_b_im, ssm_c_re, ssm_c_im, ssm_d, ssm_w_glu, conv_w, conv_ln_g, conv_ln_b, attn_lq1, attn_lk1, attn_lq2, attn_lk2, attn_sub_g, w_out, norm_ffn_g, ffn_w_gate, ffn_w_up, ffn_w_down, moe_router, moe_w_gate, moe_w_up, moe_w_down, final_norm_g):
    depth = w_in.shape[0]
    n_p, l_p, _ = x_prompt.shape
    n_s, l_s, _ = x_sample.shape
    t_p, t_s = n_p * l_p, n_s * l_s
    t_all = t_p + t_s
    past = cache_k.shape[2]
    nc_p, nc_s = l_p // S5_CHUNK, l_s // S5_CHUNK
    keep = CONV_KERNEL - 1

    x = jnp.concatenate([x_prompt.reshape(t_p, D_MODEL), x_sample.reshape(t_s, D_MODEL)], axis=0)
    cache_k = cache_k.reshape(depth, n_s, past, 512)
    cache_v = cache_v.reshape(depth, n_s, past, 512)

    outs = {name: [] for name in ("kp", "vp", "srp", "sip", "cp", "ks", "vs", "srs", "sis", "cs")}
    for l in range(depth):
        lam_init = 0.8 - 0.6 * math.exp(-0.3 * l)
        w_in_b = w_in[l].astype(BF16)
        u, hc, q, k, v, kb, vb, vt = _in_proj(x, norm_mix_g[l], w_in_b, w_in_b[:, IN_WIDTH - ATTN_WIDTH:].T)

        w_state, w_intra, w_inter, a_c, dsk = _s5_operators(
            ssm_a_re[l], ssm_a_im[l], ssm_log_dt[l], ssm_b_re[l], ssm_b_im[l], ssm_c_re[l], ssm_c_im[l],
            ssm_d[l], S5_CHUNK)
        u2 = u.reshape(t_all // S5_CHUNK, S5_CHUNK * SSM_WIDTH)
        d_state = _mm(u2, w_state, 256, 1024, "s5_state")
        y_intra = _mm(u2, w_intra, 256, 1024, "s5_intra")
        s0_p = jnp.zeros((n_p, 1, 2 * STATE_W), F32)
        s0_s = jnp.concatenate([state_ssm_re[l].reshape(n_s, 1, STATE_W), state_ssm_im[l].reshape(n_s, 1, STATE_W)],
                               axis=-1).astype(F32)
        ss_p, sf_p = _s5_scan(d_state, 0, n_p, nc_p, s0_p, a_c)
        ss_s, sf_s = _s5_scan(d_state, n_p * nc_p, n_s, nc_s, s0_s, a_c)
        s_start = jnp.concatenate([ss_p, ss_s], axis=0)
        y_ssm = _s5_out(y_intra, s_start, w_inter, u2, dsk).reshape(t_all, SSM_WIDTH)

        cv_p, cb_p = _conv(hc[:t_p].reshape(n_p, l_p, CONV_WIDTH), jnp.zeros((n_p, keep, CONV_WIDTH), F32),
                           conv_w[l], conv_ln_g[l], conv_ln_b[l], tl=256)
        cv_s, cb_s = _conv(hc[t_p:].reshape(n_s, l_s, CONV_WIDTH), state_conv[l].astype(F32),
                           conv_w[l], conv_ln_g[l], conv_ln_b[l], tl=l_s)
        conv_out = jnp.concatenate([cv_p.reshape(t_p, CONV_WIDTH), cv_s.reshape(t_s, CONV_WIDTH)], axis=0)

        lam_params = jnp.stack([attn_lq1[l], attn_lk1[l], attn_lq2[l], attn_lk2[l]]).astype(F32)
        at_p = _attn_prompt(q, kb, vt, lam_params, attn_sub_g[l], n_p, l_p, lam_init)
        at_s = _attn_sample(q, kb, vb, cache_k, cache_v, l, t_p, lam_params, attn_sub_g[l], lam_init)
        attn_out = jnp.concatenate([at_p, at_s], axis=0)

        x = _out_proj(x, y_ssm, conv_out, attn_out, ssm_w_glu[l].astype(BF16), w_out[l].astype(BF16))

        j = l // 2
        if l % 2 == 0:
            x = _ffn(x, norm_ffn_g[l], ffn_w_gate[j].astype(BF16), ffn_w_up[j].astype(BF16),
                     ffn_w_down[j].astype(BF16))
        else:
            router_pad = jnp.pad(moe_router[j].astype(F32), ((0, 0), (0, LANES - N_EXPERTS)))
            x = _moe(x, norm_ffn_g[l], router_pad, moe_w_gate[j].astype(BF16), moe_w_up[j].astype(BF16),
                     moe_w_down[j].astype(BF16))

        outs["kp"].append(k[:t_p].reshape(n_p, l_p, 2, N_HEADS, HEAD_DIM))
        outs["vp"].append(v[:t_p].reshape(n_p, l_p, N_HEADS, V_DIM))
        outs["srp"].append(sf_p[:, 0, :STATE_W].reshape(n_p, N_SSM_GROUPS, SSM_STATE))
        outs["sip"].append(sf_p[:, 0, STATE_W:].reshape(n_p, N_SSM_GROUPS, SSM_STATE))
        outs["cp"].append(cb_p)
        outs["ks"].append(k[t_p:].reshape(n_s, l_s, 2, N_HEADS, HEAD_DIM))
        outs["vs"].append(v[t_p:].reshape(n_s, l_s, N_HEADS, V_DIM))
        outs["srs"].append(sf_s[:, 0, :STATE_W].reshape(n_s, N_SSM_GROUPS, SSM_STATE))
        outs["sis"].append(sf_s[:, 0, STATE_W:].reshape(n_s, N_SSM_GROUPS, SSM_STATE))
        outs["cs"].append(cb_s)

    y = _final_norm(x, final_norm_g)
    st = {name: jnp.stack(vals) for name, vals in outs.items()}
    return (y[:t_p].reshape(n_p, l_p, D_MODEL), y[t_p:].reshape(n_s, l_s, D_MODEL),
            st["kp"], st["vp"], st["srp"], st["sip"], st["cp"],
            st["ks"], st["vs"], st["srs"], st["sis"], st["cs"])
```

```python
import functools
import math

import jax
import jax.numpy as jnp
from jax import lax
from jax.experimental import pallas as pl
from jax.experimental.pallas import tpu as pltpu

F32 = jnp.float32
BF16 = jnp.bfloat16

D_MODEL = 1024
CHUNK = 64
SSM_GROUP = 16
N_SSM_GROUPS = 16
SSM_WIDTH = 256
SSM_STATE = 64
STATE_W = N_SSM_GROUPS * SSM_STATE
CONV_WIDTH = 256
CONV_KERNEL = 31
N_HEADS = 4
HEAD_DIM = 64
V_DIM = 128
QK_WIDTH = 256
ATTN_WIDTH = 512
IN_WIDTH = 2304
ATTN_SCALE = 1.0 / math.sqrt(HEAD_DIM)
LOG2E = math.log2(math.e)
NEG_INF = -1e30
D_FF = 2816
N_EXPERTS = 8
EPS = 1e-6

S5_CHUNK = 8
LANES = 128
VMEM_LIMIT = 48 * 1024 * 1024
MOE_VMEM_LIMIT = 58 * 1024 * 1024


def _cparams(sem):
    return pltpu.CompilerParams(dimension_semantics=sem, vmem_limit_bytes=VMEM_LIMIT)


def _rms(x, g):
    return x * lax.rsqrt(jnp.mean(x * x, axis=-1, keepdims=True) + EPS) * g


def _mixer_inputs(x_ref, g_ref, w_ref, u_ref, hc_ref, q_ref):
    h = _rms(x_ref[...], g_ref[...]).astype(BF16)

    def proj(lo, hi):
        return jnp.dot(h, w_ref[:, lo:hi], preferred_element_type=F32)

    u_ref[...] = proj(0, 256)
    c_val = proj(256, 512)
    c_gate = proj(512, 768)
    hc_ref[...] = c_val * jax.nn.sigmoid(c_gate)
    q_ref[...] = (proj(768, 1280) * (ATTN_SCALE * LOG2E)).astype(BF16)
    return h, proj(1280, 1792), proj(1792, 2304)


def _in_proj_prompt_kernel(*refs, aliased):
    x_ref, g_ref, w_ref, wkvt_ref = refs[:4]
    u_ref, hc_ref, q_ref, kb_ref, v_ref, vt_ref, kt_ref = refs[5 if aliased else 4:]
    h, k, v = _mixer_inputs(x_ref, g_ref, w_ref, u_ref, hc_ref, q_ref)
    kb_ref[...] = k.astype(BF16)
    v_ref[...] = v
    nt = (((1,), (1,)), ((), ()))
    kt_ref[...] = lax.dot_general(wkvt_ref[:ATTN_WIDTH, :], h, nt, preferred_element_type=F32)
    vt_ref[...] = lax.dot_general(wkvt_ref[ATTN_WIDTH:, :], h, nt, preferred_element_type=F32).astype(BF16)


def _in_proj_prompt(x, g, w_bf16, wkvt_bf16, kt_buf, layer, depth, n_seq, tm=512):
    t = x.shape[0]
    seq_len = t // n_seq
    per_seq = seq_len // tm
    row = lambda width: pl.BlockSpec((tm, width), lambda i: (i, 0))
    in_specs = [row(D_MODEL), pl.BlockSpec((1, D_MODEL), lambda i: (0, 0)),
                pl.BlockSpec((D_MODEL, IN_WIDTH), lambda i: (0, 0)),
                pl.BlockSpec((2 * ATTN_WIDTH, D_MODEL), lambda i: (0, 0))]
    args = [x, g.reshape(1, D_MODEL), w_bf16, wkvt_bf16]
    aliases = {}
    if kt_buf is not None:
        in_specs.append(pl.BlockSpec(memory_space=pl.ANY))
        args.append(kt_buf)
        aliases = {4: 6}
    return pl.pallas_call(
        functools.partial(_in_proj_prompt_kernel, aliased=kt_buf is not None),
        grid=(t // tm,),
        in_specs=in_specs,
        out_specs=[row(256), row(256), row(512), row(512), row(512),
                   pl.BlockSpec((ATTN_WIDTH, tm), lambda i: (0, i)),
                   pl.BlockSpec((None, None, ATTN_WIDTH, tm), lambda i: (layer, i // per_seq, 0, i % per_seq))],
        out_shape=[jax.ShapeDtypeStruct((t, 256), F32), jax.ShapeDtypeStruct((t, 256), F32),
                   jax.ShapeDtypeStruct((t, 512), BF16), jax.ShapeDtypeStruct((t, 512), BF16),
                   jax.ShapeDtypeStruct((t, 512), F32), jax.ShapeDtypeStruct((ATTN_WIDTH, t), BF16),
                   jax.ShapeDtypeStruct((depth, n_seq, ATTN_WIDTH, seq_len), F32)],
        input_output_aliases=aliases,
        compiler_params=_cparams(("parallel",)),
        name="in_proj_prompt",
    )(*args)


def _in_proj_sample_kernel(x_ref, g_ref, w_ref, u_ref, hc_ref, q_ref, kb_ref, vb_ref, k_ref, v_ref):
    _, k, v = _mixer_inputs(x_ref, g_ref, w_ref, u_ref, hc_ref, q_ref)
    k_ref[...] = k
    kb_ref[...] = k.astype(BF16)
    v_ref[...] = v
    vb_ref[...] = v.astype(BF16)


def _in_proj_sample(x, g, w_bf16, tm=512):
    t = x.shape[0]
    row = lambda width: pl.BlockSpec((tm, width), lambda i: (i, 0))
    return pl.pallas_call(
        _in_proj_sample_kernel,
        grid=(t // tm,),
        in_specs=[row(D_MODEL), pl.BlockSpec((1, D_MODEL), lambda i: (0, 0)),
                  pl.BlockSpec((D_MODEL, IN_WIDTH), lambda i: (0, 0))],
        out_specs=[row(256), row(256), row(512), row(512), row(512), row(512), row(512)],
        out_shape=[jax.ShapeDtypeStruct((t, 256), F32), jax.ShapeDtypeStruct((t, 256), F32),
                   jax.ShapeDtypeStruct((t, 512), BF16), jax.ShapeDtypeStruct((t, 512), BF16),
                   jax.ShapeDtypeStruct((t, 512), BF16), jax.ShapeDtypeStruct((t, 512), F32),
                   jax.ShapeDtypeStruct((t, 512), F32)],
        compiler_params=_cparams(("parallel",)),
        name="in_proj_sample",
    )(x, g.reshape(1, D_MODEL), w_bf16)


def _mm_kernel(x_ref, w_ref, o_ref):
    o_ref[...] = jnp.dot(x_ref[...].astype(BF16), w_ref[...], preferred_element_type=F32)


def _mm(x, w_bf16, tm, tn, name):
    m, k = x.shape
    n = w_bf16.shape[1]
    return pl.pallas_call(
        _mm_kernel,
        grid=(n // tn, m // tm),
        in_specs=[pl.BlockSpec((tm, k), lambda j, i: (i, 0)), pl.BlockSpec((k, tn), lambda j, i: (0, j))],
        out_specs=pl.BlockSpec((tm, tn), lambda j, i: (i, j)),
        out_shape=jax.ShapeDtypeStruct((m, n), F32),
        compiler_params=_cparams(("parallel", "parallel")),
        name=name,
    )(x, w_bf16)


def _s5_scan_kernel(d_ref, s0_ref, a_ref, ss_ref, sf_ref, st_ref, *, rows):
    t = pl.program_id(1)

    @pl.when(t == 0)
    def _():
        st_ref[...] = s0_ref[...]

    a_re = a_ref[:, :STATE_W]
    a_im = a_ref[:, STATE_W:]

    def step(c, carry):
        s_re, s_im = carry
        ss_ref[pl.ds(c, 1), :STATE_W] = s_re
        ss_ref[pl.ds(c, 1), STATE_W:] = s_im
        d = d_ref[pl.ds(c, 1), :]
        n_re = a_re * s_re - a_im * s_im + d[:, :STATE_W]
        n_im = a_re * s_im + a_im * s_re + d[:, STATE_W:]
        return n_re, n_im

    s_re, s_im = lax.fori_loop(0, rows, step, (st_ref[:, :STATE_W], st_ref[:, STATE_W:]))
    st_ref[:, :STATE_W] = s_re
    st_ref[:, STATE_W:] = s_im
    sf_ref[...] = st_ref[...]


def _s5_scan(d, row0, n_seq, n_chunks, s0, a_c):
    rows = min(n_chunks, 256)
    nt = n_chunks // rows
    blk0 = row0 // rows
    w = 2 * STATE_W
    return pl.pallas_call(
        functools.partial(_s5_scan_kernel, rows=rows),
        grid=(n_seq, nt),
        in_specs=[pl.BlockSpec((rows, w), lambda s, t: (blk0 + s * nt + t, 0)),
                  pl.BlockSpec((None, 1, w), lambda s, t: (s, 0, 0)),
                  pl.BlockSpec((1, w), lambda s, t: (0, 0))],
        out_specs=[pl.BlockSpec((rows, w), lambda s, t: (s * nt + t, 0)),
                   pl.BlockSpec((None, 1, w), lambda s, t: (s, 0, 0))],
        out_shape=[jax.ShapeDtypeStruct((n_seq * n_chunks, w), F32), jax.ShapeDtypeStruct((n_seq, 1, w), F32)],
        scratch_shapes=[pltpu.VMEM((1, w), F32)],
        compiler_params=_cparams(("parallel", "arbitrary")),
        name="s5_scan",
    )(d, s0, a_c)


def _s5_out_kernel(yi_ref, ss_ref, w_ref, u_ref, dsk_ref, y_ref):
    y_ref[...] = (yi_ref[...] + jnp.dot(ss_ref[...].astype(BF16), w_ref[...], preferred_element_type=F32)
                  + dsk_ref[...] * u_ref[...])


def _s5_out(y_intra, s_start, w_inter, u2, dsk, tm=256, tn=512):
    m, n = y_intra.shape
    k = s_start.shape[1]
    tile = pl.BlockSpec((tm, tn), lambda j, i: (i, j))
    return pl.pallas_call(
        _s5_out_kernel,
        grid=(n // tn, m // tm),
        in_specs=[tile, pl.BlockSpec((tm, k), lambda j, i: (i, 0)), pl.BlockSpec((k, tn), lambda j, i: (0, j)),
                  tile, pl.BlockSpec((1, tn), lambda j, i: (0, j))],
        out_specs=tile,
        out_shape=jax.ShapeDtypeStruct((m, n), F32),
        compiler_params=_cparams(("parallel", "parallel")),
        name="s5_out",
    )(y_intra, s_start, w_inter, u2, dsk)


def _s5_operators(a_re, a_im, log_dt, b_re, b_im, c_re, c_im, d_skip, n_c):
    g_n, p_n, h_n = N_SSM_GROUPS, SSM_STATE, SSM_GROUP
    lam = lax.complex(a_re.astype(F32), a_im.astype(F32))
    dt = jnp.exp(log_dt.astype(F32))[:, None]
    a_bar = jnp.exp(lam * dt)
    b_bar = ((a_bar - 1.0) / lam)[..., None] * lax.complex(b_re.astype(F32), b_im.astype(F32))
    c = lax.complex(c_re.astype(F32), c_im.astype(F32))
    pows = [jnp.ones_like(a_bar)]
    for _ in range(n_c):
        pows.append(pows[-1] * a_bar)
    a_pow = jnp.stack(pows)
    eye_g = jnp.eye(g_n, dtype=F32)

    ws = a_pow[:n_c][::-1][:, :, :, None] * b_bar[None]
    ws = jnp.transpose(ws, (0, 1, 3, 2))

    def to_state(x):
        return (x[:, :, :, None, :] * eye_g[None, :, None, :, None]).reshape(n_c * g_n * h_n, g_n * p_n)

    w_state = jnp.concatenate([to_state(jnp.real(ws)), to_state(jnp.imag(ws))], axis=1)

    ker = jnp.real(jnp.einsum('ghp,kgp,gpx->kghx', c, a_pow[:n_c], b_bar, precision=lax.Precision.HIGHEST))
    blk = jnp.transpose(ker, (0, 1, 3, 2))
    blk = (blk[:, :, :, None, :] * eye_g[None, :, None, :, None]).reshape(n_c, 256, 256).astype(BF16)
    zero = jnp.zeros((256, 256), BF16)
    w_intra = jnp.concatenate(
        [jnp.concatenate([blk[t - j] if t >= j else zero for t in range(n_c)], axis=1) for j in range(n_c)], axis=0)

    ca = c[None] * jnp.transpose(a_pow[1:], (0, 1, 2))[:, :, None, :]
    ca = jnp.transpose(ca, (1, 3, 0, 2))

    def from_state(x):
        return (x[:, :, :, None, :] * eye_g[:, None, None, :, None]).reshape(g_n * p_n, n_c * 256)

    w_inter = jnp.concatenate([from_state(jnp.real(ca)), from_state(-jnp.imag(ca))], axis=0)
    a_c = jnp.concatenate([jnp.real(a_pow[n_c]).reshape(1, -1), jnp.imag(a_pow[n_c]).reshape(1, -1)], axis=1)
    dsk = jnp.tile(d_skip.astype(F32).reshape(1, SSM_WIDTH), (1, n_c))
    return w_state.astype(BF16), w_intra, w_inter.astype(BF16), a_c, dsk


def _s5_group(u, n_seq, seq_len, s0, ops):
    w_state, w_intra, w_inter, a_c, dsk = ops
    n_chunks = seq_len // S5_CHUNK
    u2 = u.reshape(n_seq * n_chunks, S5_CHUNK * SSM_WIDTH)
    d_state = _mm(u2, w_state, 256, 1024, "s5_state")
    y_intra = _mm(u2, w_intra, 256, 1024, "s5_intra")
    s_start, s_final = _s5_scan(d_state, 0, n_seq, n_chunks, s0, a_c)
    y = _s5_out(y_intra, s_start, w_inter, u2, dsk)
    return y.reshape(n_seq * seq_len, SSM_WIDTH), s_final


_CONV_PAD = 32
_CONV_RB = 64


def _conv_kernel(h_ref, buf_ref, w_ref, g_ref, b_ref, y_ref, nb_ref, xp_ref, *, tl):
    t = pl.program_id(1)
    keep = CONV_KERNEL - 1
    lo = _CONV_PAD - keep

    @pl.when(t == 0)
    def _():
        xp_ref[lo:_CONV_PAD, :] = buf_ref[...]

    @pl.when(t > 0)
    def _():
        xp_ref[lo:_CONV_PAD, :] = xp_ref[tl + lo:tl + _CONV_PAD, :]

    xp_ref[_CONV_PAD:_CONV_PAD + tl, :] = h_ref[...]
    for r in range(tl // _CONV_RB):
        base = r * _CONV_RB
        acc = jnp.zeros((_CONV_RB, CONV_WIDTH), F32)
        for k in range(CONV_KERNEL):
            acc = acc + w_ref[k:k + 1, :] * xp_ref[base + lo + k:base + lo + k + _CONV_RB, :]
        xc = acc - jnp.mean(acc, axis=-1, keepdims=True)
        var = jnp.mean(xc * xc, axis=-1, keepdims=True)
        y = xc * lax.rsqrt(var + EPS) * g_ref[...] + b_ref[...]
        y_ref[base:base + _CONV_RB, :] = y * jax.nn.sigmoid(y)
    nb_ref[...] = xp_ref[tl + lo:tl + _CONV_PAD, :]


def _conv(h, buf, w, ln_g, ln_b, tl):
    s, l, c = h.shape
    keep = CONV_KERNEL - 1
    return pl.pallas_call(
        functools.partial(_conv_kernel, tl=tl),
        grid=(s, l // tl),
        in_specs=[pl.BlockSpec((None, tl, c), lambda i, t: (i, t, 0)),
                  pl.BlockSpec((None, keep, c), lambda i, t: (i, 0, 0)),
                  pl.BlockSpec((CONV_KERNEL, c), lambda i, t: (0, 0)),
                  pl.BlockSpec((1, c), lambda i, t: (0, 0)), pl.BlockSpec((1, c), lambda i, t: (0, 0))],
        out_specs=[pl.BlockSpec((None, tl, c), lambda i, t: (i, t, 0)),
                   pl.BlockSpec((None, keep, c), lambda i, t: (i, 0, 0))],
        out_shape=[jax.ShapeDtypeStruct((s, l, c), F32), jax.ShapeDtypeStruct((s, keep, c), F32)],
        scratch_shapes=[pltpu.VMEM((tl + _CONV_PAD, c), F32)],
        compiler_params=_cparams(("parallel", "arbitrary")),
        name="conv_module",
    )(h, buf, w, ln_g.reshape(1, c), ln_b.reshape(1, c))


def _stack_q(q_ref, qs_ref):
    tq = q_ref.shape[0]
    lane = lax.broadcasted_iota(jnp.int32, (tq, LANES), 1)
    for b in range(4):
        qb = q_ref[:, b * LANES:(b + 1) * LANES]
        qs_ref[b, :tq, :] = jnp.where(lane < HEAD_DIM, qb, jnp.zeros_like(qb))
        qs_ref[b, tq:, :] = jnp.where(lane >= HEAD_DIM, qb, jnp.zeros_like(qb))


def _attn_update(score, value, m_ref, l_ref, acc_ref, tq):
    for b in range(4):
        s = score(b)
        m_old = m_ref[b]
        m_new = jnp.maximum(m_old, jnp.max(s, axis=-1, keepdims=True))
        alpha = jnp.exp2(m_old - m_new)
        p = jnp.exp2(s - m_new)
        l_ref[b] = alpha * l_ref[b] + jnp.sum(p, axis=-1, keepdims=True)
        m_ref[b] = m_new
        pb = p.astype(BF16)
        h0 = 2 * (b % 2)
        pv0 = jnp.dot(pb[:tq], value(h0), preferred_element_type=F32)
        pv1 = jnp.dot(pb[tq:], value(h0 + 1), preferred_element_type=F32)
        acc_ref[b, :tq, :] = alpha[:tq] * acc_ref[b, :tq, :] + pv0
        acc_ref[b, tq:, :] = alpha[tq:] * acc_ref[b, tq:, :] + pv1


def _attn_finish(lp_ref, sg_ref, l_ref, acc_ref, o_ref, tq, lam_init):
    lam = _lambda(lp_ref, lam_init)
    for h in range(N_HEADS):
        b1, half = h // 2, h % 2
        rows = slice(half * tq, (half + 1) * tq)
        o1 = acc_ref[b1, rows, :] / l_ref[b1, rows, :]
        o2 = acc_ref[b1 + 2, rows, :] / l_ref[b1 + 2, rows, :]
        o = o1 - lam * o2
        o_ref[:, h * V_DIM:(h + 1) * V_DIM] = (_rms(o, sg_ref[...]) * (1.0 - lam_init)).astype(o_ref.dtype)


def _attn_init(m_ref, l_ref, acc_ref):
    m_ref[...] = jnp.full(m_ref.shape, NEG_INF, F32)
    l_ref[...] = jnp.zeros(l_ref.shape, F32)
    acc_ref[...] = jnp.zeros(acc_ref.shape, F32)


def _attn_prompt_update(qs_ref, k_ref, vt_ref, m_ref, l_ref, acc_ref, tq, mask):
    for b in range(4):
        s = lax.dot_general(k_ref[:, b * LANES:(b + 1) * LANES], qs_ref[b], (((1,), (1,)), ((), ())),
                            preferred_element_type=F32)
        if mask is not None:
            s = jnp.where(mask, s, NEG_INF)
        m_old = m_ref[b]
        m_new = jnp.maximum(m_old, jnp.max(s, axis=0, keepdims=True))
        alpha = jnp.exp2(m_old - m_new)
        p = jnp.exp2(s - m_new)
        l_ref[b] = alpha * l_ref[b] + jnp.sum(p, axis=0, keepdims=True)
        m_ref[b] = m_new
        pb = p.astype(BF16)
        for r in range(2):
            h = 2 * (b % 2) + r
            cols = slice(r * tq, (r + 1) * tq)
            pv = jnp.dot(vt_ref[h * V_DIM:(h + 1) * V_DIM, :], pb[:, cols], preferred_element_type=F32)
            acc_ref[2 * b + r] = alpha[:, cols] * acc_ref[2 * b + r] + pv


def _lambda(lp_ref, lam_init):
    lp = lp_ref[...]
    return (jnp.exp(jnp.sum(lp[0:1] * lp[1:2], axis=-1, keepdims=True))
            - jnp.exp(jnp.sum(lp[2:3] * lp[3:4], axis=-1, keepdims=True)) + lam_init)


def _attn_prompt_finish(lp_ref, sgt_ref, l_ref, acc_ref, o_ref, tq, lam_init):
    lam = _lambda(lp_ref, lam_init)
    for h in range(N_HEADS):
        b1, r = h // 2, h % 2
        cols = slice(r * tq, (r + 1) * tq)
        o1 = acc_ref[2 * b1 + r] / l_ref[b1][:, cols]
        o2 = acc_ref[2 * (b1 + 2) + r] / l_ref[b1 + 2][:, cols]
        o = o1 - lam * o2
        on = o * lax.rsqrt(jnp.mean(o * o, axis=0, keepdims=True) + EPS) * sgt_ref[...] * (1.0 - lam_init)
        o_ref[:, h * V_DIM:(h + 1) * V_DIM] = on.T.astype(o_ref.dtype)


def _attn_prompt_kernel(it_ref, jt_ref, lp_ref, sgt_ref, q_ref, k_ref, vt_ref, o_ref, qs_ref, m_ref, l_ref, acc_ref,
                        *, tq, lam_init):
    step = pl.program_id(1)
    i = it_ref[step]
    j = jt_ref[step]

    @pl.when(j == 0)
    def _():
        _attn_init(m_ref, l_ref, acc_ref)
        _stack_q(q_ref, qs_ref)

    @pl.when(j < i)
    def _():
        _attn_prompt_update(qs_ref, k_ref, vt_ref, m_ref, l_ref, acc_ref, tq, None)

    @pl.when(j == i)
    def _():
        tk = k_ref.shape[0]
        key_chunk = lax.broadcasted_iota(jnp.int32, (tk, 2 * tq), 0) // CHUNK
        qry_chunk = (lax.broadcasted_iota(jnp.int32, (tk, 2 * tq), 1) % tq) // CHUNK
        _attn_prompt_update(qs_ref, k_ref, vt_ref, m_ref, l_ref, acc_ref, tq, qry_chunk >= key_chunk)
        _attn_prompt_finish(lp_ref, sgt_ref, l_ref, acc_ref, o_ref, tq, lam_init)


def _attn_prompt(q, kb, vt, lam_params, sub_g, n_seq, seq_len, lam_init, tq=256):
    nq = seq_len // tq
    pairs = [(i, j) for i in range(nq) for j in range(i + 1)]
    i_tab = jnp.asarray([p[0] for p in pairs], jnp.int32)
    j_tab = jnp.asarray([p[1] for p in pairs], jnp.int32)
    grid_spec = pltpu.PrefetchScalarGridSpec(
        num_scalar_prefetch=2,
        grid=(n_seq, len(pairs)),
        in_specs=[pl.BlockSpec((4, HEAD_DIM), lambda b, s, it, jt: (0, 0)),
                  pl.BlockSpec((V_DIM, 1), lambda b, s, it, jt: (0, 0)),
                  pl.BlockSpec((tq, 512), lambda b, s, it, jt: (b * nq + it[s], 0)),
                  pl.BlockSpec((tq, 512), lambda b, s, it, jt: (b * nq + jt[s], 0)),
                  pl.BlockSpec((512, tq), lambda b, s, it, jt: (0, b * nq + jt[s]))],
        out_specs=pl.BlockSpec((tq, 512), lambda b, s, it, jt: (b * nq + it[s], 0)),
        scratch_shapes=[pltpu.VMEM((4, 2 * tq, LANES), BF16), pltpu.VMEM((4, 1, 2 * tq), F32),
                        pltpu.VMEM((4, 1, 2 * tq), F32), pltpu.VMEM((8, V_DIM, tq), F32)],
    )
    return pl.pallas_call(
        functools.partial(_attn_prompt_kernel, tq=tq, lam_init=lam_init),
        grid_spec=grid_spec,
        out_shape=jax.ShapeDtypeStruct((n_seq * seq_len, 512), BF16),
        compiler_params=_cparams(("parallel", "arbitrary")),
        name="attn_prompt",
    )(i_tab, j_tab, lam_params, sub_g.reshape(V_DIM, 1), q, kb, vt)


def _attn_sample_kernel(lp_ref, sg_ref, q_ref, ckt_ref, cv_ref, kn_ref, vn_ref, o_ref, qs_ref, m_ref, l_ref, acc_ref,
                        *, tq, tk, lam_init):
    j = pl.program_id(1)

    @pl.when(j == 0)
    def _():
        _attn_init(m_ref, l_ref, acc_ref)
        _stack_q(q_ref, qs_ref)

    def cache_score(b):
        return jnp.dot(qs_ref[b], ckt_ref[b * LANES:(b + 1) * LANES, :].astype(BF16), preferred_element_type=F32)

    def cache_value(h):
        return cv_ref[pl.ds(h, tk, stride=N_HEADS), :].astype(BF16)

    _attn_update(cache_score, cache_value, m_ref, l_ref, acc_ref, tq)

    @pl.when(j == pl.num_programs(1) - 1)
    def _():
        def new_score(b):
            return lax.dot_general(qs_ref[b], kn_ref[:, b * LANES:(b + 1) * LANES], (((1,), (1,)), ((), ())),
                                   preferred_element_type=F32)

        def new_value(h):
            return vn_ref[:, h * V_DIM:(h + 1) * V_DIM]

        _attn_update(new_score, new_value, m_ref, l_ref, acc_ref, tq)
        _attn_finish(lp_ref, sg_ref, l_ref, acc_ref, o_ref, tq, lam_init)


def _attn_sample(q, kb, vb, cache_kt, cache_v, layer, lam_params, sub_g, lam_init, tk=512):
    _, n_seq, _, past = cache_kt.shape
    tq = q.shape[0] // n_seq
    new = pl.BlockSpec((tq, 512), lambda b, j: (b, 0))
    return pl.pallas_call(
        functools.partial(_attn_sample_kernel, tq=tq, tk=tk, lam_init=lam_init),
        grid=(n_seq, past // tk),
        in_specs=[pl.BlockSpec((4, HEAD_DIM), lambda b, j: (0, 0)), pl.BlockSpec((1, V_DIM), lambda b, j: (0, 0)),
                  new,
                  pl.BlockSpec((None, None, 512, tk), lambda b, j: (layer, b, 0, j)),
                  pl.BlockSpec((None, None, tk * N_HEADS, V_DIM), lambda b, j: (layer, b, j, 0)),
                  new, new],
        out_specs=new,
        out_shape=jax.ShapeDtypeStruct((n_seq * tq, 512), BF16),
        scratch_shapes=[pltpu.VMEM((4, 2 * tq, LANES), BF16), pltpu.VMEM((4, 2 * tq, 1), F32),
                        pltpu.VMEM((4, 2 * tq, 1), F32), pltpu.VMEM((4, 2 * tq, V_DIM), F32)],
        compiler_params=_cparams(("parallel", "arbitrary")),
        name="attn_sample",
    )(lam_params, sub_g.reshape(1, V_DIM), q, cache_kt, cache_v, kb, vb)


def _gelu_tanh(x):
    return 0.5 * x * (1.0 + jnp.tanh(math.sqrt(2.0 / math.pi) * (x + 0.044715 * (x * x * x))))


def _out_proj_kernel(x_ref, ys_ref, cv_ref, at_ref, wglu_ref, wo_ref, o_ref):
    z = _gelu_tanh(ys_ref[...])
    gate = jax.nn.sigmoid(jnp.dot(z.astype(BF16), wglu_ref[...], preferred_element_type=F32))
    ssm = (z * gate).astype(BF16)
    acc = jnp.dot(ssm, wo_ref[0:256, :], preferred_element_type=F32)
    acc = acc + jnp.dot(cv_ref[...].astype(BF16), wo_ref[256:512, :], preferred_element_type=F32)
    acc = acc + jnp.dot(at_ref[...], wo_ref[512:1024, :], preferred_element_type=F32)
    o_ref[...] = x_ref[...] + acc


def _out_proj(x, y_ssm, conv_out, attn_out, w_glu, w_out, tm=512):
    t = x.shape[0]
    row = lambda width: pl.BlockSpec((tm, width), lambda i: (i, 0))
    return pl.pallas_call(
        _out_proj_kernel,
        grid=(t // tm,),
        in_specs=[row(D_MODEL), row(256), row(256), row(512),
                  pl.BlockSpec((256, 256), lambda i: (0, 0)), pl.BlockSpec((D_MODEL, D_MODEL), lambda i: (0, 0))],
        out_specs=row(D_MODEL),
        out_shape=jax.ShapeDtypeStruct((t, D_MODEL), F32),
        compiler_params=_cparams(("parallel",)),
        name="out_proj",
    )(x, y_ssm, conv_out, attn_out, w_glu, w_out)


def _swiglu_step(h, wg, wu, wd):
    g = jnp.dot(h, wg, preferred_element_type=F32)
    u = jnp.dot(h, wu, preferred_element_type=F32)
    a = (g * jax.nn.sigmoid(g) * u).astype(BF16)
    return jnp.dot(a, wd, preferred_element_type=F32)


def _ffn_kernel(x_ref, g_ref, wg_ref, wu_ref, wd_ref, o_ref, h_ref, acc_ref):
    j = pl.program_id(1)

    @pl.when(j == 0)
    def _():
        h_ref[...] = _rms(x_ref[...], g_ref[...]).astype(BF16)
        acc_ref[...] = jnp.zeros(acc_ref.shape, F32)

    acc_ref[...] += _swiglu_step(h_ref[...], wg_ref[...], wu_ref[...], wd_ref[...])

    @pl.when(j == pl.num_programs(1) - 1)
    def _():
        o_ref[...] = x_ref[...] + acc_ref[...]


def _ffn(x, g, wg, wu, wd, tm=1024, tf=256):
    t = x.shape[0]
    return pl.pallas_call(
        _ffn_kernel,
        grid=(t // tm, D_FF // tf),
        in_specs=[pl.BlockSpec((tm, D_MODEL), lambda i, j: (i, 0)), pl.BlockSpec((1, D_MODEL), lambda i, j: (0, 0)),
                  pl.BlockSpec((D_MODEL, tf), lambda i, j: (0, j)), pl.BlockSpec((D_MODEL, tf), lambda i, j: (0, j)),
                  pl.BlockSpec((tf, D_MODEL), lambda i, j: (j, 0))],
        out_specs=pl.BlockSpec((tm, D_MODEL), lambda i, j: (i, 0)),
        out_shape=jax.ShapeDtypeStruct((t, D_MODEL), F32),
        scratch_shapes=[pltpu.VMEM((tm, D_MODEL), BF16), pltpu.VMEM((tm, D_MODEL), F32)],
        compiler_params=_cparams(("parallel", "arbitrary")),
        name="ffn_dense",
    )(x, g.reshape(1, D_MODEL), wg, wu, wd)


_MOE_PREFIX = 256


def _moe_route(x_ref, g_ref, r_ref, o_ref, hb_ref, rank_ref, sel_ref, gate_ref, cnt_ref):
    n_blk = x_ref.shape[0] // _MOE_PREFIX

    for blk in range(n_blk):
        rows = slice(blk * _MOE_PREFIX, (blk + 1) * _MOE_PREFIX)
        x = x_ref[rows, :]
        h32 = _rms(x, g_ref[...])
        hb_ref[rows, :] = h32.astype(BF16)
        o_ref[rows, :] = x
        logits = jnp.dot(h32, r_ref[...], preferred_element_type=F32, precision=lax.Precision.HIGHEST)
        rank_ref[:, rows] = logits.T[:N_EXPERTS, :]
    lt = rank_ref[...]
    row = lax.broadcasted_iota(jnp.int32, lt.shape, 0)
    m1 = jnp.max(lt, axis=0, keepdims=True)
    i1 = jnp.min(jnp.where(lt == m1, row, N_EXPERTS), axis=0, keepdims=True)
    lt2 = jnp.where(row == i1, -jnp.inf, lt)
    m2 = jnp.max(lt2, axis=0, keepdims=True)
    i2 = jnp.min(jnp.where(lt2 == m2, row, N_EXPERTS), axis=0, keepdims=True)
    e2 = jnp.exp(m2 - m1)
    den = 1.0 + e2
    gate_ref[...] = jnp.where(row == i1, 1.0 / den, 0.0) + jnp.where(row == i2, e2 / den, 0.0)
    sel = jnp.where(row == i1, 1.0, 0.0) + jnp.where(row == i2, 1.0, 0.0)
    sel_ref[...] = sel
    before = (lax.broadcasted_iota(jnp.int32, (_MOE_PREFIX, _MOE_PREFIX), 0)
              < lax.broadcasted_iota(jnp.int32, (_MOE_PREFIX, _MOE_PREFIX), 1))
    tri = jnp.where(before, 1.0, 0.0).astype(BF16)
    carry = jnp.zeros((N_EXPERTS, 1), F32)
    for blk in range(lt.shape[1] // _MOE_PREFIX):
        cols = slice(blk * _MOE_PREFIX, (blk + 1) * _MOE_PREFIX)
        rank_ref[:, cols] = jnp.dot(sel[:, cols].astype(BF16), tri, preferred_element_type=F32) + carry
        carry = carry + jnp.sum(sel[:, cols], axis=1, keepdims=True)
    for k in range(N_EXPERTS):
        cnt_ref[k] = carry[k, 0].astype(jnp.int32)


def _moe_kernel(x_ref, g_ref, r_ref, wg_ref, wu_ref, wd_ref, o_ref, hb_ref, rank_ref, sel_ref, gate_ref,
                xe_ref, ye_ref, cnt_ref, *, tb, g_rows, f_rows, s_rows):
    e = pl.program_id(1)
    f = pl.program_id(2)
    last_f = pl.num_programs(2) - 1

    @pl.when((e == 0) & (f == 0))
    def _():
        _moe_route(x_ref, g_ref, r_ref, o_ref, hb_ref, rank_ref, sel_ref, gate_ref, cnt_ref)

    cnt = cnt_ref[e]
    rank_e = rank_ref[pl.ds(e, 1), :]
    sel_e = sel_ref[pl.ds(e, 1), :]

    def groups(rows):
        return (cnt + rows - 1) // rows

    def onehot(first_row, rows):
        tgt = (lax.broadcasted_iota(jnp.int32, (rows, tb), 0) + first_row).astype(F32)
        return jnp.where((rank_e == tgt) & (sel_e > 0.0), 1.0, 0.0)

    @pl.when(f == 0)
    def _():
        def gather(s, c):
            off = pl.multiple_of(s * g_rows, g_rows)
            xe_ref[pl.ds(off, g_rows), :] = jnp.dot(onehot(off, g_rows).astype(BF16), hb_ref[...],
                                                    preferred_element_type=F32).astype(BF16)
            return c
        lax.fori_loop(0, groups(g_rows), gather, 0)

    def ffn(first):
        def body(s, c):
            off = pl.multiple_of(s * f_rows, f_rows)
            y = _swiglu_step(xe_ref[pl.ds(off, f_rows), :], wg_ref[...], wu_ref[...], wd_ref[...])
            if first:
                ye_ref[pl.ds(off, f_rows), :] = y
            else:
                ye_ref[pl.ds(off, f_rows), :] += y
            return c
        lax.fori_loop(0, groups(f_rows), body, 0)

    @pl.when(f == 0)
    def _():
        ffn(True)
        done = groups(f_rows) * f_rows
        @pl.when(done < groups(s_rows) * s_rows)
        def _():
            ye_ref[pl.ds(pl.multiple_of(done, f_rows), f_rows), :] = jnp.zeros((f_rows, D_MODEL), F32)

    @pl.when(f > 0)
    def _():
        ffn(False)

    @pl.when(f == last_f)
    def _():
        gate_e = gate_ref[pl.ds(e, 1), :]

        def scatter(s, c):
            off = pl.multiple_of(s * s_rows, s_rows)
            p = onehot(off, s_rows)
            gate_rows = jnp.sum(p * gate_e, axis=1, keepdims=True)
            valid = (lax.broadcasted_iota(jnp.int32, (s_rows, 1), 0) + off) < cnt
            yg = jnp.where(valid, gate_rows * ye_ref[pl.ds(off, s_rows), :], 0.0).astype(BF16)
            o_ref[...] += jnp.dot(p.T.astype(BF16), yg, preferred_element_type=F32)
            return c
        lax.fori_loop(0, groups(s_rows), scatter, 0)


def _moe(x, g, router_pad, wg, wu, wd, tb=2048, fc=1408, g_rows=256, f_rows=128, s_rows=256):
    t = x.shape[0]
    assert s_rows in (f_rows, 2 * f_rows) and g_rows % f_rows == 0 and tb % g_rows == 0 and tb % s_rows == 0
    once = pl.Buffered(1)
    return pl.pallas_call(
        functools.partial(_moe_kernel, tb=tb, g_rows=g_rows, f_rows=f_rows, s_rows=s_rows),
        grid=(t // tb, N_EXPERTS, D_FF // fc),
        in_specs=[pl.BlockSpec((tb, D_MODEL), lambda i, e, j: (i, 0), pipeline_mode=once),
                  pl.BlockSpec((1, D_MODEL), lambda i, e, j: (0, 0)),
                  pl.BlockSpec((D_MODEL, LANES), lambda i, e, j: (0, 0)),
                  pl.BlockSpec((None, D_MODEL, fc), lambda i, e, j: (e, 0, j)),
                  pl.BlockSpec((None, D_MODEL, fc), lambda i, e, j: (e, 0, j)),
                  pl.BlockSpec((None, fc, D_MODEL), lambda i, e, j: (e, j, 0))],
        out_specs=pl.BlockSpec((tb, D_MODEL), lambda i, e, j: (i, 0), pipeline_mode=once),
        out_shape=jax.ShapeDtypeStruct((t, D_MODEL), F32),
        scratch_shapes=[pltpu.VMEM((tb, D_MODEL), BF16), pltpu.VMEM((N_EXPERTS, tb), F32),
                        pltpu.VMEM((N_EXPERTS, tb), F32), pltpu.VMEM((N_EXPERTS, tb), F32),
                        pltpu.VMEM((tb, D_MODEL), BF16), pltpu.VMEM((tb, D_MODEL), F32),
                        pltpu.SMEM((N_EXPERTS,), jnp.int32)],
        compiler_params=pltpu.CompilerParams(dimension_semantics=("parallel", "arbitrary", "arbitrary"),
                                             vmem_limit_bytes=MOE_VMEM_LIMIT),
        name="ffn_moe",
    )(x, g.reshape(1, D_MODEL), router_pad, wg, wu, wd)


def _final_norm_kernel(x_ref, g_ref, o_ref):
    o_ref[...] = _rms(x_ref[...], g_ref[...])


def _final_norm(x, g, tm=1024):
    t = x.shape[0]
    return pl.pallas_call(
        _final_norm_kernel,
        grid=(t // tm,),
        in_specs=[pl.BlockSpec((tm, D_MODEL), lambda i: (i, 0)), pl.BlockSpec((1, D_MODEL), lambda i: (0, 0))],
        out_specs=pl.BlockSpec((tm, D_MODEL), lambda i: (i, 0)),
        out_shape=jax.ShapeDtypeStruct((t, D_MODEL), F32),
        compiler_params=_cparams(("parallel",)),
        name="final_norm",
    )(x, g.reshape(1, D_MODEL))


def kernel(x_prompt, x_sample, cache_k, cache_v, state_ssm_re, state_ssm_im, state_conv, norm_mix_g, w_in, ssm_a_re, ssm_a_im, ssm_log_dt, ssm_b_re, ssm_b_im, ssm_c_re, ssm_c_im, ssm_d, ssm_w_glu, conv_w, conv_ln_g, conv_ln_b, attn_lq1, attn_lk1, attn_lq2, attn_lk2, attn_sub_g, w_out, norm_ffn_g, ffn_w_gate, ffn_w_up, ffn_w_down, moe_router, moe_w_gate, moe_w_up, moe_w_down, final_norm_g):
    depth = w_in.shape[0]
    n_p, l_p, _ = x_prompt.shape
    n_s, l_s, _ = x_sample.shape
    past = cache_k.shape[2]
    keep = CONV_KERNEL - 1

    xp = x_prompt.reshape(n_p * l_p, D_MODEL)
    xs = x_sample.reshape(n_s * l_s, D_MODEL)
    cache_kt = jnp.transpose(cache_k, (0, 1, 3, 4, 5, 2)).reshape(depth, n_s, 512, past)
    cache_vr = cache_v.reshape(depth, n_s, past * N_HEADS, V_DIM)
    s0_p = jnp.zeros((n_p, 1, 2 * STATE_W), F32)
    buf_p = jnp.zeros((n_p, keep, CONV_WIDTH), F32)

    kt_buf = None
    outs = {name: [] for name in ("vp", "srp", "sip", "cp", "ks", "vs", "srs", "sis", "cs")}
    for l in range(depth):
        lam_init = 0.8 - 0.6 * math.exp(-0.3 * l)
        w_in_b = w_in[l].astype(BF16)
        wkvt = w_in_b[:, IN_WIDTH - 2 * ATTN_WIDTH:].T
        ops = _s5_operators(ssm_a_re[l], ssm_a_im[l], ssm_log_dt[l], ssm_b_re[l], ssm_b_im[l], ssm_c_re[l],
                            ssm_c_im[l], ssm_d[l], S5_CHUNK)
        lam_params = jnp.stack([attn_lq1[l], attn_lk1[l], attn_lq2[l], attn_lk2[l]]).astype(F32)
        w_glu_b = ssm_w_glu[l].astype(BF16)
        w_out_b = w_out[l].astype(BF16)
        conv_p = (conv_w[l], conv_ln_g[l], conv_ln_b[l])

        u, hc, q, kb, v_p, vt, kt_buf = _in_proj_prompt(xp, norm_mix_g[l], w_in_b, wkvt, kt_buf, l, depth, n_p)
        y_ssm, sf_p = _s5_group(u, n_p, l_p, s0_p, ops)
        cv, cb_p = _conv(hc.reshape(n_p, l_p, CONV_WIDTH), buf_p, *conv_p, tl=256)
        at = _attn_prompt(q, kb, vt, lam_params, attn_sub_g[l], n_p, l_p, lam_init)
        xp = _out_proj(xp, y_ssm, cv.reshape(n_p * l_p, CONV_WIDTH), at, w_glu_b, w_out_b)

        s0_s = jnp.concatenate([state_ssm_re[l].reshape(n_s, 1, STATE_W), state_ssm_im[l].reshape(n_s, 1, STATE_W)],
                               axis=-1).astype(F32)
        u, hc, q, kb, vb, k_s, v_s = _in_proj_sample(xs, norm_mix_g[l], w_in_b)
        y_ssm, sf_s = _s5_group(u, n_s, l_s, s0_s, ops)
        cv, cb_s = _conv(hc.reshape(n_s, l_s, CONV_WIDTH), state_conv[l].astype(F32), *conv_p, tl=l_s)
        at = _attn_sample(q, kb, vb, cache_kt, cache_vr, l, lam_params, attn_sub_g[l], lam_init)
        xs = _out_proj(xs, y_ssm, cv.reshape(n_s * l_s, CONV_WIDTH), at, w_glu_b, w_out_b)

        j = l // 2
        if l % 2 == 0:
            wg, wu, wd = ffn_w_gate[j].astype(BF16), ffn_w_up[j].astype(BF16), ffn_w_down[j].astype(BF16)
            xp = _ffn(xp, norm_ffn_g[l], wg, wu, wd)
            xs = _ffn(xs, norm_ffn_g[l], wg, wu, wd)
        else:
            router_pad = jnp.pad(moe_router[j].astype(F32), ((0, 0), (0, LANES - N_EXPERTS)))
            wg, wu, wd = moe_w_gate[j].astype(BF16), moe_w_up[j].astype(BF16), moe_w_down[j].astype(BF16)
            xp = _moe(xp, norm_ffn_g[l], router_pad, wg, wu, wd)
            xs = _moe(xs, norm_ffn_g[l], router_pad, wg, wu, wd)

        outs["vp"].append(v_p.reshape(n_p, l_p, N_HEADS, V_DIM))
        outs["srp"].append(sf_p[:, 0, :STATE_W].reshape(n_p, N_SSM_GROUPS, SSM_STATE))
        outs["sip"].append(sf_p[:, 0, STATE_W:].reshape(n_p, N_SSM_GROUPS, SSM_STATE))
        outs["cp"].append(cb_p)
        outs["ks"].append(k_s.reshape(n_s, l_s, 2, N_HEADS, HEAD_DIM))
        outs["vs"].append(v_s.reshape(n_s, l_s, N_HEADS, V_DIM))
        outs["srs"].append(sf_s[:, 0, :STATE_W].reshape(n_s, N_SSM_GROUPS, SSM_STATE))
        outs["sis"].append(sf_s[:, 0, STATE_W:].reshape(n_s, N_SSM_GROUPS, SSM_STATE))
        outs["cs"].append(cb_s)

    yp = _final_norm(xp, final_norm_g)
    ys = _final_norm(xs, final_norm_g)
    st = {name: jnp.stack(vals) for name, vals in outs.items()}
    k_prompt = jnp.transpose(kt_buf.reshape(depth, n_p, 2, N_HEADS, HEAD_DIM, l_p), (0, 1, 5, 2, 3, 4))
    return (yp.reshape(n_p, l_p, D_MODEL), ys.reshape(n_s, l_s, D_MODEL),
            k_prompt, st["vp"], st["srp"], st["sip"], st["cp"],
            st["ks"], st["vs"], st["srs"], st["sis"], st["cs"])
```

```python
import functools
import math

import jax
import jax.numpy as jnp
from jax import lax
from jax.experimental import pallas as pl
from jax.experimental.pallas import tpu as pltpu

F32 = jnp.float32
BF16 = jnp.bfloat16

D_MODEL = 1024
CHUNK = 64
SSM_GROUP = 16
N_SSM_GROUPS = 16
SSM_WIDTH = 256
SSM_STATE = 64
STATE_W = N_SSM_GROUPS * SSM_STATE
CONV_WIDTH = 256
CONV_KERNEL = 31
N_HEADS = 4
HEAD_DIM = 64
V_DIM = 128
QK_WIDTH = 256
ATTN_WIDTH = 512
IN_WIDTH = 2304
ATTN_SCALE = 1.0 / math.sqrt(HEAD_DIM)
LOG2E = math.log2(math.e)
NEG_INF = -1e30
D_FF = 2816
N_EXPERTS = 8
EPS = 1e-6

S5_CHUNK = 8
LANES = 128
VMEM_LIMIT = 48 * 1024 * 1024
MOE_VMEM_LIMIT = 58 * 1024 * 1024


def _cparams(sem):
    return pltpu.CompilerParams(dimension_semantics=sem, vmem_limit_bytes=VMEM_LIMIT)


def _rms(x, g):
    return x * lax.rsqrt(jnp.mean(x * x, axis=-1, keepdims=True) + EPS) * g


def _mixer_inputs(x_ref, g_ref, w_ref, u_ref, hc_ref, q_ref):
    h = _rms(x_ref[...], g_ref[...]).astype(BF16)

    def proj(lo, hi):
        return jnp.dot(h, w_ref[:, lo:hi], preferred_element_type=F32)

    u_ref[...] = proj(0, 256)
    c_val = proj(256, 512)
    c_gate = proj(512, 768)
    hc_ref[...] = c_val * jax.nn.sigmoid(c_gate)
    q_ref[...] = (proj(768, 1280) * (ATTN_SCALE * LOG2E)).astype(BF16)
    return h, proj(1280, 1792), proj(1792, 2304)


def _in_proj_prompt_kernel(*refs, aliased):
    x_ref, g_ref, w_ref, wkvt_ref = refs[:4]
    u_ref, hc_ref, q_ref, kb_ref, v_ref, vt_ref, kt_ref = refs[5 if aliased else 4:]
    h, k, v = _mixer_inputs(x_ref, g_ref, w_ref, u_ref, hc_ref, q_ref)
    kb_ref[...] = k.astype(BF16)
    v_ref[...] = v
    nt = (((1,), (1,)), ((), ()))
    kt_ref[...] = lax.dot_general(wkvt_ref[:ATTN_WIDTH, :], h, nt, preferred_element_type=F32)
    vt_ref[...] = lax.dot_general(wkvt_ref[ATTN_WIDTH:, :], h, nt, preferred_element_type=F32).astype(BF16)


def _in_proj_prompt(x, g, w_bf16, wkvt_bf16, kt_buf, layer, depth, n_seq, tm=512):
    t = x.shape[0]
    seq_len = t // n_seq
    per_seq = seq_len // tm
    row = lambda width: pl.BlockSpec((tm, width), lambda i: (i, 0))
    in_specs = [row(D_MODEL), pl.BlockSpec((1, D_MODEL), lambda i: (0, 0)),
                pl.BlockSpec((D_MODEL, IN_WIDTH), lambda i: (0, 0)),
                pl.BlockSpec((2 * ATTN_WIDTH, D_MODEL), lambda i: (0, 0))]
    args = [x, g.reshape(1, D_MODEL), w_bf16, wkvt_bf16]
    aliases = {}
    if kt_buf is not None:
        in_specs.append(pl.BlockSpec(memory_space=pl.ANY))
        args.append(kt_buf)
        aliases = {4: 6}
    return pl.pallas_call(
        functools.partial(_in_proj_prompt_kernel, aliased=kt_buf is not None),
        grid=(t // tm,),
        in_specs=in_specs,
        out_specs=[row(256), row(256), row(512), row(512), row(512),
                   pl.BlockSpec((ATTN_WIDTH, tm), lambda i: (0, i)),
                   pl.BlockSpec((None, None, ATTN_WIDTH, tm), lambda i: (layer, i // per_seq, 0, i % per_seq))],
        out_shape=[jax.ShapeDtypeStruct((t, 256), F32), jax.ShapeDtypeStruct((t, 256), F32),
                   jax.ShapeDtypeStruct((t, 512), BF16), jax.ShapeDtypeStruct((t, 512), BF16),
                   jax.ShapeDtypeStruct((t, 512), F32), jax.ShapeDtypeStruct((ATTN_WIDTH, t), BF16),
                   jax.ShapeDtypeStruct((depth, n_seq, ATTN_WIDTH, seq_len), F32)],
        input_output_aliases=aliases,
        compiler_params=_cparams(("parallel",)),
        name="in_proj_prompt",
    )(*args)


def _in_proj_sample_kernel(x_ref, g_ref, w_ref, u_ref, hc_ref, q_ref, kb_ref, vb_ref, k_ref, v_ref):
    _, k, v = _mixer_inputs(x_ref, g_ref, w_ref, u_ref, hc_ref, q_ref)
    k_ref[...] = k
    kb_ref[...] = k.astype(BF16)
    v_ref[...] = v
    vb_ref[...] = v.astype(BF16)


def _in_proj_sample(x, g, w_bf16, tm=512):
    t = x.shape[0]
    row = lambda width: pl.BlockSpec((tm, width), lambda i: (i, 0))
    return pl.pallas_call(
        _in_proj_sample_kernel,
        grid=(t // tm,),
        in_specs=[row(D_MODEL), pl.BlockSpec((1, D_MODEL), lambda i: (0, 0)),
                  pl.BlockSpec((D_MODEL, IN_WIDTH), lambda i: (0, 0))],
        out_specs=[row(256), row(256), row(512), row(512), row(512), row(512), row(512)],
        out_shape=[jax.ShapeDtypeStruct((t, 256), F32), jax.ShapeDtypeStruct((t, 256), F32),
                   jax.ShapeDtypeStruct((t, 512), BF16), jax.ShapeDtypeStruct((t, 512), BF16),
                   jax.ShapeDtypeStruct((t, 512), BF16), jax.ShapeDtypeStruct((t, 512), F32),
                   jax.ShapeDtypeStruct((t, 512), F32)],
        compiler_params=_cparams(("parallel",)),
        name="in_proj_sample",
    )(x, g.reshape(1, D_MODEL), w_bf16)


def _mm_kernel(x_ref, w_ref, o_ref):
    o_ref[...] = jnp.dot(x_ref[...].astype(BF16), w_ref[...], preferred_element_type=F32)


def _mm(x, w_bf16, tm, tn, name):
    m, k = x.shape
    n = w_bf16.shape[1]
    return pl.pallas_call(
        _mm_kernel,
        grid=(n // tn, m // tm),
        in_specs=[pl.BlockSpec((tm, k), lambda j, i: (i, 0)), pl.BlockSpec((k, tn), lambda j, i: (0, j))],
        out_specs=pl.BlockSpec((tm, tn), lambda j, i: (i, j)),
        out_shape=jax.ShapeDtypeStruct((m, n), F32),
        compiler_params=_cparams(("parallel", "parallel")),
        name=name,
    )(x, w_bf16)


def _s5_scan_kernel(d_ref, s0_ref, a_ref, ss_ref, sf_ref, st_ref, *, rows):
    t = pl.program_id(1)

    @pl.when(t == 0)
    def _():
        st_ref[...] = s0_ref[...]

    a_re = a_ref[:, :STATE_W]
    a_im = a_ref[:, STATE_W:]

    def step(c, carry):
        s_re, s_im = carry
        ss_ref[pl.ds(c, 1), :STATE_W] = s_re
        ss_ref[pl.ds(c, 1), STATE_W:] = s_im
        d = d_ref[pl.ds(c, 1), :]
        n_re = a_re * s_re - a_im * s_im + d[:, :STATE_W]
        n_im = a_re * s_im + a_im * s_re + d[:, STATE_W:]
        return n_re, n_im

    s_re, s_im = lax.fori_loop(0, rows, step, (st_ref[:, :STATE_W], st_ref[:, STATE_W:]))
    st_ref[:, :STATE_W] = s_re
    st_ref[:, STATE_W:] = s_im
    sf_ref[...] = st_ref[...]


def _s5_scan(d, row0, n_seq, n_chunks, s0, a_c):
    rows = min(n_chunks, 256)
    nt = n_chunks // rows
    blk0 = row0 // rows
    w = 2 * STATE_W
    return pl.pallas_call(
        functools.partial(_s5_scan_kernel, rows=rows),
        grid=(n_seq, nt),
        in_specs=[pl.BlockSpec((rows, w), lambda s, t: (blk0 + s * nt + t, 0)),
                  pl.BlockSpec((None, 1, w), lambda s, t: (s, 0, 0)),
                  pl.BlockSpec((1, w), lambda s, t: (0, 0))],
        out_specs=[pl.BlockSpec((rows, w), lambda s, t: (s * nt + t, 0)),
                   pl.BlockSpec((None, 1, w), lambda s, t: (s, 0, 0))],
        out_shape=[jax.ShapeDtypeStruct((n_seq * n_chunks, w), F32), jax.ShapeDtypeStruct((n_seq, 1, w), F32)],
        scratch_shapes=[pltpu.VMEM((1, w), F32)],
        compiler_params=_cparams(("parallel", "arbitrary")),
        name="s5_scan",
    )(d, s0, a_c)


def _s5_out_kernel(yi_ref, ss_ref, w_ref, u_ref, dsk_ref, y_ref):
    y_ref[...] = (yi_ref[...] + jnp.dot(ss_ref[...].astype(BF16), w_ref[...], preferred_element_type=F32)
                  + dsk_ref[...] * u_ref[...])


def _s5_out(y_intra, s_start, w_inter, u2, dsk, tm=256, tn=512):
    m, n = y_intra.shape
    k = s_start.shape[1]
    tile = pl.BlockSpec((tm, tn), lambda j, i: (i, j))
    return pl.pallas_call(
        _s5_out_kernel,
        grid=(n // tn, m // tm),
        in_specs=[tile, pl.BlockSpec((tm, k), lambda j, i: (i, 0)), pl.BlockSpec((k, tn), lambda j, i: (0, j)),
                  tile, pl.BlockSpec((1, tn), lambda j, i: (0, j))],
        out_specs=tile,
        out_shape=jax.ShapeDtypeStruct((m, n), F32),
        compiler_params=_cparams(("parallel", "parallel")),
        name="s5_out",
    )(y_intra, s_start, w_inter, u2, dsk)


def _s5_operators(a_re, a_im, log_dt, b_re, b_im, c_re, c_im, d_skip, n_c):
    g_n, p_n, h_n = N_SSM_GROUPS, SSM_STATE, SSM_GROUP
    lam = lax.complex(a_re.astype(F32), a_im.astype(F32))
    dt = jnp.exp(log_dt.astype(F32))[:, None]
    a_bar = jnp.exp(lam * dt)
    b_bar = ((a_bar - 1.0) / lam)[..., None] * lax.complex(b_re.astype(F32), b_im.astype(F32))
    c = lax.complex(c_re.astype(F32), c_im.astype(F32))
    pows = [jnp.ones_like(a_bar)]
    for _ in range(n_c):
        pows.append(pows[-1] * a_bar)
    a_pow = jnp.stack(pows)
    eye_g = jnp.eye(g_n, dtype=F32)

    ws = a_pow[:n_c][::-1][:, :, :, None] * b_bar[None]
    ws = jnp.transpose(ws, (0, 1, 3, 2))

    def to_state(x):
        return (x[:, :, :, None, :] * eye_g[None, :, None, :, None]).reshape(n_c * g_n * h_n, g_n * p_n)

    w_state = jnp.concatenate([to_state(jnp.real(ws)), to_state(jnp.imag(ws))], axis=1)

    ker = jnp.real(jnp.einsum('ghp,kgp,gpx->kghx', c, a_pow[:n_c], b_bar, precision=lax.Precision.HIGHEST))
    blk = jnp.transpose(ker, (0, 1, 3, 2))
    blk = (blk[:, :, :, None, :] * eye_g[None, :, None, :, None]).reshape(n_c, 256, 256).astype(BF16)
    zero = jnp.zeros((256, 256), BF16)
    w_intra = jnp.concatenate(
        [jnp.concatenate([blk[t - j] if t >= j else zero for t in range(n_c)], axis=1) for j in range(n_c)], axis=0)

    ca = c[None] * jnp.transpose(a_pow[1:], (0, 1, 2))[:, :, None, :]
    ca = jnp.transpose(ca, (1, 3, 0, 2))

    def from_state(x):
        return (x[:, :, :, None, :] * eye_g[:, None, None, :, None]).reshape(g_n * p_n, n_c * 256)

    w_inter = jnp.concatenate([from_state(jnp.real(ca)), from_state(-jnp.imag(ca))], axis=0)
    a_c = jnp.concatenate([jnp.real(a_pow[n_c]).reshape(1, -1), jnp.imag(a_pow[n_c]).reshape(1, -1)], axis=1)
    dsk = jnp.tile(d_skip.astype(F32).reshape(1, SSM_WIDTH), (1, n_c))
    return w_state.astype(BF16), w_intra, w_inter.astype(BF16), a_c, dsk


def _s5_group(u, n_seq, seq_len, s0, ops):
    w_state, w_intra, w_inter, a_c, dsk = ops
    n_chunks = seq_len // S5_CHUNK
    u2 = u.reshape(n_seq * n_chunks, S5_CHUNK * SSM_WIDTH)
    d_state = _mm(u2, w_state, 256, 1024, "s5_state")
    y_intra = _mm(u2, w_intra, 256, 1024, "s5_intra")
    s_start, s_final = _s5_scan(d_state, 0, n_seq, n_chunks, s0, a_c)
    y = _s5_out(y_intra, s_start, w_inter, u2, dsk)
    return y.reshape(n_seq * seq_len, SSM_WIDTH), s_final


_CONV_PAD = 32
_CONV_RB = 64


def _conv_kernel(h_ref, buf_ref, w_ref, g_ref, b_ref, y_ref, nb_ref, xp_ref, *, tl):
    t = pl.program_id(1)
    keep = CONV_KERNEL - 1
    lo = _CONV_PAD - keep

    @pl.when(t == 0)
    def _():
        xp_ref[lo:_CONV_PAD, :] = buf_ref[...]

    @pl.when(t > 0)
    def _():
        xp_ref[lo:_CONV_PAD, :] = xp_ref[tl + lo:tl + _CONV_PAD, :]

    xp_ref[_CONV_PAD:_CONV_PAD + tl, :] = h_ref[...]
    for r in range(tl // _CONV_RB):
        base = r * _CONV_RB
        acc = jnp.zeros((_CONV_RB, CONV_WIDTH), F32)
        for k in range(CONV_KERNEL):
            acc = acc + w_ref[k:k + 1, :] * xp_ref[base + lo + k:base + lo + k + _CONV_RB, :]
        xc = acc - jnp.mean(acc, axis=-1, keepdims=True)
        var = jnp.mean(xc * xc, axis=-1, keepdims=True)
        y = xc * lax.rsqrt(var + EPS) * g_ref[...] + b_ref[...]
        y_ref[base:base + _CONV_RB, :] = y * jax.nn.sigmoid(y)
    nb_ref[...] = xp_ref[tl + lo:tl + _CONV_PAD, :]


def _conv(h, buf, w, ln_g, ln_b, tl):
    s, l, c = h.shape
    keep = CONV_KERNEL - 1
    return pl.pallas_call(
        functools.partial(_conv_kernel, tl=tl),
        grid=(s, l // tl),
        in_specs=[pl.BlockSpec((None, tl, c), lambda i, t: (i, t, 0)),
                  pl.BlockSpec((None, keep, c), lambda i, t: (i, 0, 0)),
                  pl.BlockSpec((CONV_KERNEL, c), lambda i, t: (0, 0)),
                  pl.BlockSpec((1, c), lambda i, t: (0, 0)), pl.BlockSpec((1, c), lambda i, t: (0, 0))],
        out_specs=[pl.BlockSpec((None, tl, c), lambda i, t: (i, t, 0)),
                   pl.BlockSpec((None, keep, c), lambda i, t: (i, 0, 0))],
        out_shape=[jax.ShapeDtypeStruct((s, l, c), F32), jax.ShapeDtypeStruct((s, keep, c), F32)],
        scratch_shapes=[pltpu.VMEM((tl + _CONV_PAD, c), F32)],
        compiler_params=_cparams(("parallel", "arbitrary")),
        name="conv_module",
    )(h, buf, w, ln_g.reshape(1, c), ln_b.reshape(1, c))


def _stack_q(q_ref, qs_ref):
    tq = q_ref.shape[0]
    lane = lax.broadcasted_iota(jnp.int32, (tq, LANES), 1)
    for b in range(4):
        qb = q_ref[:, b * LANES:(b + 1) * LANES]
        qs_ref[b, :tq, :] = jnp.where(lane < HEAD_DIM, qb, jnp.zeros_like(qb))
        qs_ref[b, tq:, :] = jnp.where(lane >= HEAD_DIM, qb, jnp.zeros_like(qb))


def _attn_update(score, value, m_ref, l_ref, acc_ref, tq):
    def softmax(b, s):
        m_old = m_ref[b]
        m_new = jnp.maximum(m_old, jnp.max(s, axis=-1, keepdims=True))
        alpha = jnp.exp2(m_old - m_new)
        p = jnp.exp2(s - m_new)
        l_ref[b] = alpha * l_ref[b] + jnp.sum(p, axis=-1, keepdims=True)
        m_ref[b] = m_new
        return alpha, p.astype(BF16)

    def values(b, alpha, pb):
        h0 = 2 * (b % 2)
        pv0 = jnp.dot(pb[:tq], value(h0), preferred_element_type=F32)
        pv1 = jnp.dot(pb[tq:], value(h0 + 1), preferred_element_type=F32)
        acc_ref[b, :tq, :] = alpha[:tq] * acc_ref[b, :tq, :] + pv0
        acc_ref[b, tq:, :] = alpha[tq:] * acc_ref[b, tq:, :] + pv1

    s0 = score(0)
    s1 = score(1)
    s2 = score(2)
    a0, p0 = softmax(0, s0)
    s3 = score(3)
    a1, p1 = softmax(1, s1)
    values(0, a0, p0)
    a2, p2 = softmax(2, s2)
    values(1, a1, p1)
    a3, p3 = softmax(3, s3)
    values(2, a2, p2)
    values(3, a3, p3)


def _attn_finish(lp_ref, sg_ref, l_ref, acc_ref, o_ref, tq, lam_init):
    lam = _lambda(lp_ref, lam_init)
    for h in range(N_HEADS):
        b1, half = h // 2, h % 2
        rows = slice(half * tq, (half + 1) * tq)
        o1 = acc_ref[b1, rows, :] / l_ref[b1, rows, :]
        o2 = acc_ref[b1 + 2, rows, :] / l_ref[b1 + 2, rows, :]
        o = o1 - lam * o2
        o_ref[:, h * V_DIM:(h + 1) * V_DIM] = (_rms(o, sg_ref[...]) * (1.0 - lam_init)).astype(o_ref.dtype)


def _attn_init(m_ref, l_ref, acc_ref):
    m_ref[...] = jnp.full(m_ref.shape, NEG_INF, F32)
    l_ref[...] = jnp.zeros(l_ref.shape, F32)
    acc_ref[...] = jnp.zeros(acc_ref.shape, F32)


def _attn_prompt_update(qs_ref, k_ref, vt_ref, m_ref, l_ref, acc_ref, tq, mask):
    def scores(b):
        s = lax.dot_general(k_ref[:, b * LANES:(b + 1) * LANES], qs_ref[b], (((1,), (1,)), ((), ())),
                            preferred_element_type=F32)
        return s if mask is None else jnp.where(mask, s, NEG_INF)

    def softmax(b, s):
        m_old = m_ref[b]
        m_new = jnp.maximum(m_old, jnp.max(s, axis=0, keepdims=True))
        alpha = jnp.exp2(m_old - m_new)
        p = jnp.exp2(s - m_new)
        l_ref[b] = alpha * l_ref[b] + jnp.sum(p, axis=0, keepdims=True)
        m_ref[b] = m_new
        return alpha, p.astype(BF16)

    def values(b, alpha, pb):
        for r in range(2):
            h = 2 * (b % 2) + r
            cols = slice(r * tq, (r + 1) * tq)
            pv = jnp.dot(vt_ref[h * V_DIM:(h + 1) * V_DIM, :], pb[:, cols], preferred_element_type=F32)
            acc_ref[2 * b + r] = alpha[:, cols] * acc_ref[2 * b + r] + pv

    s0 = scores(0)
    s1 = scores(1)
    s2 = scores(2)
    a0, p0 = softmax(0, s0)
    s3 = scores(3)
    a1, p1 = softmax(1, s1)
    values(0, a0, p0)
    a2, p2 = softmax(2, s2)
    values(1, a1, p1)
    a3, p3 = softmax(3, s3)
    values(2, a2, p2)
    values(3, a3, p3)


def _lambda(lp_ref, lam_init):
    lp = lp_ref[...]
    return (jnp.exp(jnp.sum(lp[0:1] * lp[1:2], axis=-1, keepdims=True))
            - jnp.exp(jnp.sum(lp[2:3] * lp[3:4], axis=-1, keepdims=True)) + lam_init)


def _attn_prompt_finish(lp_ref, sgt_ref, l_ref, acc_ref, o_ref, tq, lam_init):
    lam = _lambda(lp_ref, lam_init)
    for h in range(N_HEADS):
        b1, r = h // 2, h % 2
        cols = slice(r * tq, (r + 1) * tq)
        o1 = acc_ref[2 * b1 + r] / l_ref[b1][:, cols]
        o2 = acc_ref[2 * (b1 + 2) + r] / l_ref[b1 + 2][:, cols]
        o = o1 - lam * o2
        on = o * lax.rsqrt(jnp.mean(o * o, axis=0, keepdims=True) + EPS) * sgt_ref[...] * (1.0 - lam_init)
        o_ref[:, h * V_DIM:(h + 1) * V_DIM] = on.T.astype(o_ref.dtype)


def _attn_prompt_kernel(it_ref, jt_ref, lp_ref, sgt_ref, q_ref, k_ref, vt_ref, o_ref, qs_ref, m_ref, l_ref, acc_ref,
                        *, tq, lam_init):
    step = pl.program_id(1)
    i = it_ref[step]
    j = jt_ref[step]

    @pl.when(j == 0)
    def _():
        _attn_init(m_ref, l_ref, acc_ref)
        _stack_q(q_ref, qs_ref)

    @pl.when(j < i)
    def _():
        _attn_prompt_update(qs_ref, k_ref, vt_ref, m_ref, l_ref, acc_ref, tq, None)

    @pl.when(j == i)
    def _():
        tk = k_ref.shape[0]
        key_chunk = lax.broadcasted_iota(jnp.int32, (tk, 2 * tq), 0) // CHUNK
        qry_chunk = (lax.broadcasted_iota(jnp.int32, (tk, 2 * tq), 1) % tq) // CHUNK
        _attn_prompt_update(qs_ref, k_ref, vt_ref, m_ref, l_ref, acc_ref, tq, qry_chunk >= key_chunk)
        _attn_prompt_finish(lp_ref, sgt_ref, l_ref, acc_ref, o_ref, tq, lam_init)


def _attn_prompt(q, kb, vt, lam_params, sub_g, n_seq, seq_len, lam_init, tq=512):
    nq = seq_len // tq
    pairs = [(i, j) for i in range(nq) for j in range(i + 1)]
    i_tab = jnp.asarray([p[0] for p in pairs], jnp.int32)
    j_tab = jnp.asarray([p[1] for p in pairs], jnp.int32)
    grid_spec = pltpu.PrefetchScalarGridSpec(
        num_scalar_prefetch=2,
        grid=(n_seq, len(pairs)),
        in_specs=[pl.BlockSpec((4, HEAD_DIM), lambda b, s, it, jt: (0, 0)),
                  pl.BlockSpec((V_DIM, 1), lambda b, s, it, jt: (0, 0)),
                  pl.BlockSpec((tq, 512), lambda b, s, it, jt: (b * nq + it[s], 0)),
                  pl.BlockSpec((tq, 512), lambda b, s, it, jt: (b * nq + jt[s], 0)),
                  pl.BlockSpec((512, tq), lambda b, s, it, jt: (0, b * nq + jt[s]))],
        out_specs=pl.BlockSpec((tq, 512), lambda b, s, it, jt: (b * nq + it[s], 0)),
        scratch_shapes=[pltpu.VMEM((4, 2 * tq, LANES), BF16), pltpu.VMEM((4, 1, 2 * tq), F32),
                        pltpu.VMEM((4, 1, 2 * tq), F32), pltpu.VMEM((8, V_DIM, tq), F32)],
    )
    return pl.pallas_call(
        functools.partial(_attn_prompt_kernel, tq=tq, lam_init=lam_init),
        grid_spec=grid_spec,
        out_shape=jax.ShapeDtypeStruct((n_seq * seq_len, 512), BF16),
        compiler_params=_cparams(("parallel", "arbitrary")),
        name="attn_prompt",
    )(i_tab, j_tab, lam_params, sub_g.reshape(V_DIM, 1), q, kb, vt)


def _attn_sample_kernel(lp_ref, sg_ref, q_ref, ckt_ref, cv_ref, kn_ref, vn_ref, o_ref, qs_ref, m_ref, l_ref, acc_ref,
                        *, tq, tk, lam_init):
    j = pl.program_id(1)

    @pl.when(j == 0)
    def _():
        _attn_init(m_ref, l_ref, acc_ref)
        _stack_q(q_ref, qs_ref)

    def cache_score(b):
        return jnp.dot(qs_ref[b], ckt_ref[b * LANES:(b + 1) * LANES, :].astype(BF16), preferred_element_type=F32)

    def cache_value(h):
        return cv_ref[pl.ds(h, tk, stride=N_HEADS), :].astype(BF16)

    _attn_update(cache_score, cache_value, m_ref, l_ref, acc_ref, tq)

    @pl.when(j == pl.num_programs(1) - 1)
    def _():
        def new_score(b):
            return lax.dot_general(qs_ref[b], kn_ref[:, b * LANES:(b + 1) * LANES], (((1,), (1,)), ((), ())),
                                   preferred_element_type=F32)

        def new_value(h):
            return vn_ref[:, h * V_DIM:(h + 1) * V_DIM]

        _attn_update(new_score, new_value, m_ref, l_ref, acc_ref, tq)
        _attn_finish(lp_ref, sg_ref, l_ref, acc_ref, o_ref, tq, lam_init)


def _attn_sample(q, kb, vb, cache_kt, cache_v, layer, lam_params, sub_g, lam_init, tk=2048):
    _, n_seq, _, past = cache_kt.shape
    tk = min(tk, past)
    assert past % tk == 0
    tq = q.shape[0] // n_seq
    new = pl.BlockSpec((tq, 512), lambda b, j: (b, 0))
    return pl.pallas_call(
        functools.partial(_attn_sample_kernel, tq=tq, tk=tk, lam_init=lam_init),
        grid=(n_seq, past // tk),
        in_specs=[pl.BlockSpec((4, HEAD_DIM), lambda b, j: (0, 0)), pl.BlockSpec((1, V_DIM), lambda b, j: (0, 0)),
                  new,
                  pl.BlockSpec((None, None, 512, tk), lambda b, j: (layer, b, 0, j)),
                  pl.BlockSpec((None, None, tk * N_HEADS, V_DIM), lambda b, j: (layer, b, j, 0)),
                  new, new],
        out_specs=new,
        out_shape=jax.ShapeDtypeStruct((n_seq * tq, 512), BF16),
        scratch_shapes=[pltpu.VMEM((4, 2 * tq, LANES), BF16), pltpu.VMEM((4, 2 * tq, 1), F32),
                        pltpu.VMEM((4, 2 * tq, 1), F32), pltpu.VMEM((4, 2 * tq, V_DIM), F32)],
        compiler_params=_cparams(("parallel", "arbitrary")),
        name="attn_sample",
    )(lam_params, sub_g.reshape(1, V_DIM), q, cache_kt, cache_v, kb, vb)


def _gelu_tanh(x):
    return 0.5 * x * (1.0 + jnp.tanh(math.sqrt(2.0 / math.pi) * (x + 0.044715 * (x * x * x))))


def _out_proj_kernel(x_ref, ys_ref, cv_ref, at_ref, wglu_ref, wo_ref, o_ref):
    z = _gelu_tanh(ys_ref[...])
    gate = jax.nn.sigmoid(jnp.dot(z.astype(BF16), wglu_ref[...], preferred_element_type=F32))
    ssm = (z * gate).astype(BF16)
    acc = jnp.dot(ssm, wo_ref[0:256, :], preferred_element_type=F32)
    acc = acc + jnp.dot(cv_ref[...].astype(BF16), wo_ref[256:512, :], preferred_element_type=F32)
    acc = acc + jnp.dot(at_ref[...], wo_ref[512:1024, :], preferred_element_type=F32)
    o_ref[...] = x_ref[...] + acc


def _out_proj(x, y_ssm, conv_out, attn_out, w_glu, w_out, tm=512):
    t = x.shape[0]
    row = lambda width: pl.BlockSpec((tm, width), lambda i: (i, 0))
    return pl.pallas_call(
        _out_proj_kernel,
        grid=(t // tm,),
        in_specs=[row(D_MODEL), row(256), row(256), row(512),
                  pl.BlockSpec((256, 256), lambda i: (0, 0)), pl.BlockSpec((D_MODEL, D_MODEL), lambda i: (0, 0))],
        out_specs=row(D_MODEL),
        out_shape=jax.ShapeDtypeStruct((t, D_MODEL), F32),
        compiler_params=_cparams(("parallel",)),
        name="out_proj",
    )(x, y_ssm, conv_out, attn_out, w_glu, w_out)


def _swiglu_step(h, wg, wu, wd):
    g = jnp.dot(h, wg, preferred_element_type=F32)
    u = jnp.dot(h, wu, preferred_element_type=F32)
    a = (g * jax.nn.sigmoid(g) * u).astype(BF16)
    return jnp.dot(a, wd, preferred_element_type=F32)


def _ffn_kernel(x_ref, g_ref, wg_ref, wu_ref, wd_ref, o_ref, h_ref, acc_ref):
    j = pl.program_id(1)

    @pl.when(j == 0)
    def _():
        h_ref[...] = _rms(x_ref[...], g_ref[...]).astype(BF16)
        acc_ref[...] = jnp.zeros(acc_ref.shape, F32)

    acc_ref[...] += _swiglu_step(h_ref[...], wg_ref[...], wu_ref[...], wd_ref[...])

    @pl.when(j == pl.num_programs(1) - 1)
    def _():
        o_ref[...] = x_ref[...] + acc_ref[...]


def _ffn(x, g, wg, wu, wd, tm=1024, tf=256):
    t = x.shape[0]
    return pl.pallas_call(
        _ffn_kernel,
        grid=(t // tm, D_FF // tf),
        in_specs=[pl.BlockSpec((tm, D_MODEL), lambda i, j: (i, 0)), pl.BlockSpec((1, D_MODEL), lambda i, j: (0, 0)),
                  pl.BlockSpec((D_MODEL, tf), lambda i, j: (0, j)), pl.BlockSpec((D_MODEL, tf), lambda i, j: (0, j)),
                  pl.BlockSpec((tf, D_MODEL), lambda i, j: (j, 0))],
        out_specs=pl.BlockSpec((tm, D_MODEL), lambda i, j: (i, 0)),
        out_shape=jax.ShapeDtypeStruct((t, D_MODEL), F32),
        scratch_shapes=[pltpu.VMEM((tm, D_MODEL), BF16), pltpu.VMEM((tm, D_MODEL), F32)],
        compiler_params=_cparams(("parallel", "arbitrary")),
        name="ffn_dense",
    )(x, g.reshape(1, D_MODEL), wg, wu, wd)


_MOE_PREFIX = 256


def _moe_route(x_ref, g_ref, r_ref, o_ref, hb_ref, rank_ref, sel_ref, gate_ref, cnt_ref):
    n_blk = x_ref.shape[0] // _MOE_PREFIX

    for blk in range(n_blk):
        rows = slice(blk * _MOE_PREFIX, (blk + 1) * _MOE_PREFIX)
        x = x_ref[rows, :]
        h32 = _rms(x, g_ref[...])
        hb_ref[rows, :] = h32.astype(BF16)
        o_ref[rows, :] = x
        logits = jnp.dot(h32, r_ref[...], preferred_element_type=F32, precision=lax.Precision.HIGHEST)
        rank_ref[:, rows] = logits.T[:N_EXPERTS, :]
    lt = rank_ref[...]
    row = lax.broadcasted_iota(jnp.int32, lt.shape, 0)
    m1 = jnp.max(lt, axis=0, keepdims=True)
    i1 = jnp.min(jnp.where(lt == m1, row, N_EXPERTS), axis=0, keepdims=True)
    lt2 = jnp.where(row == i1, -jnp.inf, lt)
    m2 = jnp.max(lt2, axis=0, keepdims=True)
    i2 = jnp.min(jnp.where(lt2 == m2, row, N_EXPERTS), axis=0, keepdims=True)
    e2 = jnp.exp(m2 - m1)
    den = 1.0 + e2
    gate_ref[...] = jnp.where(row == i1, 1.0 / den, 0.0) + jnp.where(row == i2, e2 / den, 0.0)
    sel = jnp.where(row == i1, 1.0, 0.0) + jnp.where(row == i2, 1.0, 0.0)
    sel_ref[...] = sel
    before = (lax.broadcasted_iota(jnp.int32, (_MOE_PREFIX, _MOE_PREFIX), 0)
              < lax.broadcasted_iota(jnp.int32, (_MOE_PREFIX, _MOE_PREFIX), 1))
    tri = jnp.where(before, 1.0, 0.0).astype(BF16)
    carry = jnp.zeros((N_EXPERTS, 1), F32)
    for blk in range(lt.shape[1] // _MOE_PREFIX):
        cols = slice(blk * _MOE_PREFIX, (blk + 1) * _MOE_PREFIX)
        rank_ref[:, cols] = jnp.dot(sel[:, cols].astype(BF16), tri, preferred_element_type=F32) + carry
        carry = carry + jnp.sum(sel[:, cols], axis=1, keepdims=True)
    for k in range(N_EXPERTS):
        cnt_ref[k] = carry[k, 0].astype(jnp.int32)


def _moe_kernel(x_ref, g_ref, r_ref, wg_ref, wu_ref, wd_ref, o_ref, hb_ref, rank_ref, sel_ref, gate_ref,
                xe_ref, ye_ref, cnt_ref, *, tb, g_rows, f_rows, s_rows):
    e = pl.program_id(1)
    f = pl.program_id(2)
    last_f = pl.num_programs(2) - 1

    @pl.when((e == 0) & (f == 0))
    def _():
        _moe_route(x_ref, g_ref, r_ref, o_ref, hb_ref, rank_ref, sel_ref, gate_ref, cnt_ref)

    cnt = cnt_ref[e]
    rank_e = rank_ref[pl.ds(e, 1), :]
    sel_e = sel_ref[pl.ds(e, 1), :]

    def groups(rows):
        return (cnt + rows - 1) // rows

    def onehot(first_row, rows):
        tgt = (lax.broadcasted_iota(jnp.int32, (rows, tb), 0) + first_row).astype(F32)
        return jnp.where((rank_e == tgt) & (sel_e > 0.0), 1.0, 0.0)

    @pl.when(f == 0)
    def _():
        def gather(s, c):
            off = pl.multiple_of(s * g_rows, g_rows)
            xe_ref[pl.ds(off, g_rows), :] = jnp.dot(onehot(off, g_rows).astype(BF16), hb_ref[...],
                                                    preferred_element_type=F32).astype(BF16)
            return c
        lax.fori_loop(0, groups(g_rows), gather, 0)

    def ffn(first):
        def body(s, c):
            off = pl.multiple_of(s * f_rows, f_rows)
            y = _swiglu_step(xe_ref[pl.ds(off, f_rows), :], wg_ref[...], wu_ref[...], wd_ref[...])
            if first:
                ye_ref[pl.ds(off, f_rows), :] = y
            else:
                ye_ref[pl.ds(off, f_rows), :] += y
            return c
        lax.fori_loop(0, groups(f_rows), body, 0)

    @pl.when(f == 0)
    def _():
        ffn(True)
        done = groups(f_rows) * f_rows
        @pl.when(done < groups(s_rows) * s_rows)
        def _():
            ye_ref[pl.ds(pl.multiple_of(done, f_rows), f_rows), :] = jnp.zeros((f_rows, D_MODEL), F32)

    @pl.when(f > 0)
    def _():
        ffn(False)

    @pl.when(f == last_f)
    def _():
        gate_e = gate_ref[pl.ds(e, 1), :]

        def scatter(s, c):
            off = pl.multiple_of(s * s_rows, s_rows)
            p = onehot(off, s_rows)
            gate_rows = jnp.sum(p * gate_e, axis=1, keepdims=True)
            valid = (lax.broadcasted_iota(jnp.int32, (s_rows, 1), 0) + off) < cnt
            yg = jnp.where(valid, gate_rows * ye_ref[pl.ds(off, s_rows), :], 0.0).astype(BF16)
            o_ref[...] += jnp.dot(p.T.astype(BF16), yg, preferred_element_type=F32)
            return c
        lax.fori_loop(0, groups(s_rows), scatter, 0)


def _moe(x, g, router_pad, wg, wu, wd, tb=2048, fc=1408, g_rows=256, f_rows=128, s_rows=256):
    t = x.shape[0]
    assert s_rows in (f_rows, 2 * f_rows) and g_rows % f_rows == 0 and tb % g_rows == 0 and tb % s_rows == 0
    once = pl.Buffered(1)
    return pl.pallas_call(
        functools.partial(_moe_kernel, tb=tb, g_rows=g_rows, f_rows=f_rows, s_rows=s_rows),
        grid=(t // tb, N_EXPERTS, D_FF // fc),
        in_specs=[pl.BlockSpec((tb, D_MODEL), lambda i, e, j: (i, 0), pipeline_mode=once),
                  pl.BlockSpec((1, D_MODEL), lambda i, e, j: (0, 0)),
                  pl.BlockSpec((D_MODEL, LANES), lambda i, e, j: (0, 0)),
                  pl.BlockSpec((None, D_MODEL, fc), lambda i, e, j: (e, 0, j)),
                  pl.BlockSpec((None, D_MODEL, fc), lambda i, e, j: (e, 0, j)),
                  pl.BlockSpec((None, fc, D_MODEL), lambda i, e, j: (e, j, 0))],
        out_specs=pl.BlockSpec((tb, D_MODEL), lambda i, e, j: (i, 0), pipeline_mode=once),
        out_shape=jax.ShapeDtypeStruct((t, D_MODEL), F32),
        scratch_shapes=[pltpu.VMEM((tb, D_MODEL), BF16), pltpu.VMEM((N_EXPERTS, tb), F32),
                        pltpu.VMEM((N_EXPERTS, tb), F32), pltpu.VMEM((N_EXPERTS, tb), F32),
                        pltpu.VMEM((tb, D_MODEL), BF16), pltpu.VMEM((tb, D_MODEL), F32),
                        pltpu.SMEM((N_EXPERTS,), jnp.int32)],
        compiler_params=pltpu.CompilerParams(dimension_semantics=("parallel", "arbitrary", "arbitrary"),
                                             vmem_limit_bytes=MOE_VMEM_LIMIT),
        name="ffn_moe",
    )(x, g.reshape(1, D_MODEL), router_pad, wg, wu, wd)


def _final_norm_kernel(x_ref, g_ref, o_ref):
    o_ref[...] = _rms(x_ref[...], g_ref[...])


def _final_norm(x, g, tm=1024):
    t = x.shape[0]
    return pl.pallas_call(
        _final_norm_kernel,
        grid=(t // tm,),
        in_specs=[pl.BlockSpec((tm, D_MODEL), lambda i: (i, 0)), pl.BlockSpec((1, D_MODEL), lambda i: (0, 0))],
        out_specs=pl.BlockSpec((tm, D_MODEL), lambda i: (i, 0)),
        out_shape=jax.ShapeDtypeStruct((t, D_MODEL), F32),
        compiler_params=_cparams(("parallel",)),
        name="final_norm",
    )(x, g.reshape(1, D_MODEL))


def kernel(x_prompt, x_sample, cache_k, cache_v, state_ssm_re, state_ssm_im, state_conv, norm_mix_g, w_in, ssm_a_re, ssm_a_im, ssm_log_dt, ssm_b_re, ssm_b_im, ssm_c_re, ssm_c_im, ssm_d, ssm_w_glu, conv_w, conv_ln_g, conv_ln_b, attn_lq1, attn_lk1, attn_lq2, attn_lk2, attn_sub_g, w_out, norm_ffn_g, ffn_w_gate, ffn_w_up, ffn_w_down, moe_router, moe_w_gate, moe_w_up, moe_w_down, final_norm_g):
    depth = w_in.shape[0]
    n_p, l_p, _ = x_prompt.shape
    n_s, l_s, _ = x_sample.shape
    past = cache_k.shape[2]
    keep = CONV_KERNEL - 1

    xp = x_prompt.reshape(n_p * l_p, D_MODEL)
    xs = x_sample.reshape(n_s * l_s, D_MODEL)
    cache_kt = jnp.transpose(cache_k, (0, 1, 3, 4, 5, 2)).reshape(depth, n_s, 512, past)
    cache_vr = cache_v.reshape(depth, n_s, past * N_HEADS, V_DIM)
    s0_p = jnp.zeros((n_p, 1, 2 * STATE_W), F32)
    buf_p = jnp.zeros((n_p, keep, CONV_WIDTH), F32)

    kt_buf = None
    outs = {name: [] for name in ("vp", "srp", "sip", "cp", "ks", "vs", "srs", "sis", "cs")}
    for l in range(depth):
        lam_init = 0.8 - 0.6 * math.exp(-0.3 * l)
        w_in_b = w_in[l].astype(BF16)
        wkvt = w_in_b[:, IN_WIDTH - 2 * ATTN_WIDTH:].T
        ops = _s5_operators(ssm_a_re[l], ssm_a_im[l], ssm_log_dt[l], ssm_b_re[l], ssm_b_im[l], ssm_c_re[l],
                            ssm_c_im[l], ssm_d[l], S5_CHUNK)
        lam_params = jnp.stack([attn_lq1[l], attn_lk1[l], attn_lq2[l], attn_lk2[l]]).astype(F32)
        w_glu_b = ssm_w_glu[l].astype(BF16)
        w_out_b = w_out[l].astype(BF16)
        conv_p = (conv_w[l], conv_ln_g[l], conv_ln_b[l])

        u, hc, q, kb, v_p, vt, kt_buf = _in_proj_prompt(xp, norm_mix_g[l], w_in_b, wkvt, kt_buf, l, depth, n_p)
        y_ssm, sf_p = _s5_group(u, n_p, l_p, s0_p, ops)
        cv, cb_p = _conv(hc.reshape(n_p, l_p, CONV_WIDTH), buf_p, *conv_p, tl=256)
        at = _attn_prompt(q, kb, vt, lam_params, attn_sub_g[l], n_p, l_p, lam_init)
        xp = _out_proj(xp, y_ssm, cv.reshape(n_p * l_p, CONV_WIDTH), at, w_glu_b, w_out_b)

        s0_s = jnp.concatenate([state_ssm_re[l].reshape(n_s, 1, STATE_W), state_ssm_im[l].reshape(n_s, 1, STATE_W)],
                               axis=-1).astype(F32)
        u, hc, q, kb, vb, k_s, v_s = _in_proj_sample(xs, norm_mix_g[l], w_in_b)
        y_ssm, sf_s = _s5_group(u, n_s, l_s, s0_s, ops)
        cv, cb_s = _conv(hc.reshape(n_s, l_s, CONV_WIDTH), state_conv[l].astype(F32), *conv_p, tl=l_s)
        at = _attn_sample(q, kb, vb, cache_kt, cache_vr, l, lam_params, attn_sub_g[l], lam_init)
        xs = _out_proj(xs, y_ssm, cv.reshape(n_s * l_s, CONV_WIDTH), at, w_glu_b, w_out_b)

        j = l // 2
        if l % 2 == 0:
            wg, wu, wd = ffn_w_gate[j].astype(BF16), ffn_w_up[j].astype(BF16), ffn_w_down[j].astype(BF16)
            xp = _ffn(xp, norm_ffn_g[l], wg, wu, wd)
            xs = _ffn(xs, norm_ffn_g[l], wg, wu, wd)
        else:
            router_pad = jnp.pad(moe_router[j].astype(F32), ((0, 0), (0, LANES - N_EXPERTS)))
            wg, wu, wd = moe_w_gate[j].astype(BF16), moe_w_up[j].astype(BF16), moe_w_down[j].astype(BF16)
            xp = _moe(xp, norm_ffn_g[l], router_pad, wg, wu, wd)
            xs = _moe(xs, norm_ffn_g[l], router_pad, wg, wu, wd)

        outs["vp"].append(v_p.reshape(n_p, l_p, N_HEADS, V_DIM))
        outs["srp"].append(sf_p[:, 0, :STATE_W].reshape(n_p, N_SSM_GROUPS, SSM_STATE))
        outs["sip"].append(sf_p[:, 0, STATE_W:].reshape(n_p, N_SSM_GROUPS, SSM_STATE))
        outs["cp"].append(cb_p)
        outs["ks"].append(k_s.reshape(n_s, l_s, 2, N_HEADS, HEAD_DIM))
        outs["vs"].append(v_s.reshape(n_s, l_s, N_HEADS, V_DIM))
        outs["srs"].append(sf_s[:, 0, :STATE_W].reshape(n_s, N_SSM_GROUPS, SSM_STATE))
        outs["sis"].append(sf_s[:, 0, STATE_W:].reshape(n_s, N_SSM_GROUPS, SSM_STATE))
        outs["cs"].append(cb_s)

    yp = _final_norm(xp, final_norm_g)
    ys = _final_norm(xs, final_norm_g)
    st = {name: jnp.stack(vals) for name, vals in outs.items()}
    k_prompt = jnp.transpose(kt_buf.reshape(depth, n_p, 2, N_HEADS, HEAD_DIM, l_p), (0, 1, 5, 2, 3, 4))
    return (yp.reshape(n_p, l_p, D_MODEL), ys.reshape(n_s, l_s, D_MODEL),
            k_prompt, st["vp"], st["srp"], st["sip"], st["cp"],
            st["ks"], st["vs"], st["srs"], st["sis"], st["cs"])
```

```python
import functools
import math

import jax
import jax.numpy as jnp
from jax import lax
from jax.experimental import pallas as pl
from jax.experimental.pallas import tpu as pltpu

F32 = jnp.float32
BF16 = jnp.bfloat16

D_MODEL = 1024
CHUNK = 64
SSM_GROUP = 16
N_SSM_GROUPS = 16
SSM_WIDTH = 256
SSM_STATE = 64
STATE_W = N_SSM_GROUPS * SSM_STATE
CONV_WIDTH = 256
CONV_KERNEL = 31
N_HEADS = 4
HEAD_DIM = 64
V_DIM = 128
VT_ROWS = V_DIM + 16
QK_WIDTH = 256
ATTN_WIDTH = 512
IN_WIDTH = 2304
ATTN_SCALE = 1.0 / math.sqrt(HEAD_DIM)
LOG2E = math.log2(math.e)
NEG_INF = -1e30
D_FF = 2816
N_EXPERTS = 8
EPS = 1e-6

S5_CHUNK = 8
LANES = 128
VMEM_LIMIT = 48 * 1024 * 1024
MOE_VMEM_LIMIT = 58 * 1024 * 1024


def _cparams(sem):
    return pltpu.CompilerParams(dimension_semantics=sem, vmem_limit_bytes=VMEM_LIMIT)


def _rms(x, g):
    return x * lax.rsqrt(jnp.mean(x * x, axis=-1, keepdims=True) + EPS) * g


def _mixer_inputs(x_ref, g_ref, w_ref, u_ref, hc_ref, q_ref):
    h = _rms(x_ref[...], g_ref[...]).astype(BF16)

    def proj(lo, hi):
        return jnp.dot(h, w_ref[:, lo:hi], preferred_element_type=F32)

    u_ref[...] = proj(0, 256)
    c_val = proj(256, 512)
    c_gate = proj(512, 768)
    hc_ref[...] = c_val * jax.nn.sigmoid(c_gate)
    q_ref[...] = (proj(768, 1280) * (ATTN_SCALE * LOG2E)).astype(BF16)
    return h, proj(1280, 1792), proj(1792, 2304)


def _in_proj_prompt_kernel(*refs, aliased):
    x_ref, g_ref, w_ref, wkvt_ref = refs[:4]
    u_ref, hc_ref, q_ref, kb_ref, v_ref, vt_ref, kt_ref = refs[5 if aliased else 4:]
    h, k, v = _mixer_inputs(x_ref, g_ref, w_ref, u_ref, hc_ref, q_ref)
    kb_ref[...] = k.astype(BF16)
    v_ref[...] = v
    nt = (((1,), (1,)), ((), ()))
    kt_ref[...] = lax.dot_general(wkvt_ref[:ATTN_WIDTH, :], h, nt, preferred_element_type=F32)
    vt = lax.dot_general(wkvt_ref[ATTN_WIDTH:, :], h, nt, preferred_element_type=F32).astype(BF16)
    for head in range(N_HEADS):
        vt_ref[head * VT_ROWS:head * VT_ROWS + V_DIM, :] = vt[head * V_DIM:(head + 1) * V_DIM, :]
        vt_ref[head * VT_ROWS + V_DIM:(head + 1) * VT_ROWS, :] = jnp.ones((VT_ROWS - V_DIM, vt.shape[1]), BF16)


def _in_proj_prompt(x, g, w_bf16, wkvt_bf16, kt_buf, layer, depth, n_seq, tm=512):
    t = x.shape[0]
    seq_len = t // n_seq
    per_seq = seq_len // tm
    row = lambda width: pl.BlockSpec((tm, width), lambda i: (i, 0))
    in_specs = [row(D_MODEL), pl.BlockSpec((1, D_MODEL), lambda i: (0, 0)),
                pl.BlockSpec((D_MODEL, IN_WIDTH), lambda i: (0, 0)),
                pl.BlockSpec((2 * ATTN_WIDTH, D_MODEL), lambda i: (0, 0))]
    args = [x, g.reshape(1, D_MODEL), w_bf16, wkvt_bf16]
    aliases = {}
    if kt_buf is not None:
        in_specs.append(pl.BlockSpec(memory_space=pl.ANY))
        args.append(kt_buf)
        aliases = {4: 6}
    return pl.pallas_call(
        functools.partial(_in_proj_prompt_kernel, aliased=kt_buf is not None),
        grid=(t // tm,),
        in_specs=in_specs,
        out_specs=[row(256), row(256), row(512), row(512), row(512),
                   pl.BlockSpec((N_HEADS * VT_ROWS, tm), lambda i: (0, i)),
                   pl.BlockSpec((None, None, ATTN_WIDTH, tm), lambda i: (layer, i // per_seq, 0, i % per_seq))],
        out_shape=[jax.ShapeDtypeStruct((t, 256), F32), jax.ShapeDtypeStruct((t, 256), F32),
                   jax.ShapeDtypeStruct((t, 512), BF16), jax.ShapeDtypeStruct((t, 512), BF16),
                   jax.ShapeDtypeStruct((t, 512), F32), jax.ShapeDtypeStruct((N_HEADS * VT_ROWS, t), BF16),
                   jax.ShapeDtypeStruct((depth, n_seq, ATTN_WIDTH, seq_len), F32)],
        input_output_aliases=aliases,
        compiler_params=_cparams(("parallel",)),
        name="in_proj_prompt",
    )(*args)


def _in_proj_sample_kernel(x_ref, g_ref, w_ref, u_ref, hc_ref, q_ref, kb_ref, vb_ref, k_ref, v_ref):
    _, k, v = _mixer_inputs(x_ref, g_ref, w_ref, u_ref, hc_ref, q_ref)
    k_ref[...] = k
    kb_ref[...] = k.astype(BF16)
    v_ref[...] = v
    vb_ref[...] = v.astype(BF16)


def _in_proj_sample(x, g, w_bf16, tm=512):
    t = x.shape[0]
    row = lambda width: pl.BlockSpec((tm, width), lambda i: (i, 0))
    return pl.pallas_call(
        _in_proj_sample_kernel,
        grid=(t // tm,),
        in_specs=[row(D_MODEL), pl.BlockSpec((1, D_MODEL), lambda i: (0, 0)),
                  pl.BlockSpec((D_MODEL, IN_WIDTH), lambda i: (0, 0))],
        out_specs=[row(256), row(256), row(512), row(512), row(512), row(512), row(512)],
        out_shape=[jax.ShapeDtypeStruct((t, 256), F32), jax.ShapeDtypeStruct((t, 256), F32),
                   jax.ShapeDtypeStruct((t, 512), BF16), jax.ShapeDtypeStruct((t, 512), BF16),
                   jax.ShapeDtypeStruct((t, 512), BF16), jax.ShapeDtypeStruct((t, 512), F32),
                   jax.ShapeDtypeStruct((t, 512), F32)],
        compiler_params=_cparams(("parallel",)),
        name="in_proj_sample",
    )(x, g.reshape(1, D_MODEL), w_bf16)


def _mm_kernel(x_ref, w_ref, o_ref):
    o_ref[...] = jnp.dot(x_ref[...].astype(BF16), w_ref[...], preferred_element_type=F32)


def _mm(x, w_bf16, tm, tn, name):
    m, k = x.shape
    n = w_bf16.shape[1]
    return pl.pallas_call(
        _mm_kernel,
        grid=(n // tn, m // tm),
        in_specs=[pl.BlockSpec((tm, k), lambda j, i: (i, 0)), pl.BlockSpec((k, tn), lambda j, i: (0, j))],
        out_specs=pl.BlockSpec((tm, tn), lambda j, i: (i, j)),
        out_shape=jax.ShapeDtypeStruct((m, n), F32),
        compiler_params=_cparams(("parallel", "parallel")),
        name=name,
    )(x, w_bf16)


def _s5_scan_kernel(d_ref, s0_ref, a_ref, ss_ref, sf_ref, st_ref, *, rows):
    t = pl.program_id(1)

    @pl.when(t == 0)
    def _():
        st_ref[...] = s0_ref[...]

    a_re = a_ref[:, :STATE_W]
    a_im = a_ref[:, STATE_W:]

    def step(c, carry):
        s_re, s_im = carry
        ss_ref[pl.ds(c, 1), :STATE_W] = s_re
        ss_ref[pl.ds(c, 1), STATE_W:] = s_im
        d = d_ref[pl.ds(c, 1), :]
        n_re = a_re * s_re - a_im * s_im + d[:, :STATE_W]
        n_im = a_re * s_im + a_im * s_re + d[:, STATE_W:]
        return n_re, n_im

    s_re, s_im = lax.fori_loop(0, rows, step, (st_ref[:, :STATE_W], st_ref[:, STATE_W:]))
    st_ref[:, :STATE_W] = s_re
    st_ref[:, STATE_W:] = s_im
    sf_ref[...] = st_ref[...]


def _s5_scan(d, row0, n_seq, n_chunks, s0, a_c):
    rows = min(n_chunks, 256)
    nt = n_chunks // rows
    blk0 = row0 // rows
    w = 2 * STATE_W
    return pl.pallas_call(
        functools.partial(_s5_scan_kernel, rows=rows),
        grid=(n_seq, nt),
        in_specs=[pl.BlockSpec((rows, w), lambda s, t: (blk0 + s * nt + t, 0)),
                  pl.BlockSpec((None, 1, w), lambda s, t: (s, 0, 0)),
                  pl.BlockSpec((1, w), lambda s, t: (0, 0))],
        out_specs=[pl.BlockSpec((rows, w), lambda s, t: (s * nt + t, 0)),
                   pl.BlockSpec((None, 1, w), lambda s, t: (s, 0, 0))],
        out_shape=[jax.ShapeDtypeStruct((n_seq * n_chunks, w), F32), jax.ShapeDtypeStruct((n_seq, 1, w), F32)],
        scratch_shapes=[pltpu.VMEM((1, w), F32)],
        compiler_params=_cparams(("parallel", "arbitrary")),
        name="s5_scan",
    )(d, s0, a_c)


def _s5_out_kernel(yi_ref, ss_ref, w_ref, u_ref, dsk_ref, y_ref):
    y_ref[...] = (yi_ref[...] + jnp.dot(ss_ref[...].astype(BF16), w_ref[...], preferred_element_type=F32)
                  + dsk_ref[...] * u_ref[...])


def _s5_out(y_intra, s_start, w_inter, u2, dsk, tm=256, tn=512):
    m, n = y_intra.shape
    k = s_start.shape[1]
    tile = pl.BlockSpec((tm, tn), lambda j, i: (i, j))
    return pl.pallas_call(
        _s5_out_kernel,
        grid=(n // tn, m // tm),
        in_specs=[tile, pl.BlockSpec((tm, k), lambda j, i: (i, 0)), pl.BlockSpec((k, tn), lambda j, i: (0, j)),
                  tile, pl.BlockSpec((1, tn), lambda j, i: (0, j))],
        out_specs=tile,
        out_shape=jax.ShapeDtypeStruct((m, n), F32),
        compiler_params=_cparams(("parallel", "parallel")),
        name="s5_out",
    )(y_intra, s_start, w_inter, u2, dsk)


def _s5_operators(a_re, a_im, log_dt, b_re, b_im, c_re, c_im, d_skip, n_c):
    g_n, p_n, h_n = N_SSM_GROUPS, SSM_STATE, SSM_GROUP
    lam = lax.complex(a_re.astype(F32), a_im.astype(F32))
    dt = jnp.exp(log_dt.astype(F32))[:, None]
    a_bar = jnp.exp(lam * dt)
    b_bar = ((a_bar - 1.0) / lam)[..., None] * lax.complex(b_re.astype(F32), b_im.astype(F32))
    c = lax.complex(c_re.astype(F32), c_im.astype(F32))
    pows = [jnp.ones_like(a_bar)]
    for _ in range(n_c):
        pows.append(pows[-1] * a_bar)
    a_pow = jnp.stack(pows)
    eye_g = jnp.eye(g_n, dtype=F32)

    ws = a_pow[:n_c][::-1][:, :, :, None] * b_bar[None]
    ws = jnp.transpose(ws, (0, 1, 3, 2))

    def to_state(x):
        return (x[:, :, :, None, :] * eye_g[None, :, None, :, None]).reshape(n_c * g_n * h_n, g_n * p_n)

    w_state = jnp.concatenate([to_state(jnp.real(ws)), to_state(jnp.imag(ws))], axis=1)

    ker = jnp.real(jnp.einsum('ghp,kgp,gpx->kghx', c, a_pow[:n_c], b_bar, precision=lax.Precision.HIGHEST))
    blk = jnp.transpose(ker, (0, 1, 3, 2))
    blk = (blk[:, :, :, None, :] * eye_g[None, :, None, :, None]).reshape(n_c, 256, 256).astype(BF16)
    zero = jnp.zeros((256, 256), BF16)
    w_intra = jnp.concatenate(
        [jnp.concatenate([blk[t - j] if t >= j else zero for t in range(n_c)], axis=1) for j in range(n_c)], axis=0)

    ca = c[None] * jnp.transpose(a_pow[1:], (0, 1, 2))[:, :, None, :]
    ca = jnp.transpose(ca, (1, 3, 0, 2))

    def from_state(x):
        return (x[:, :, :, None, :] * eye_g[:, None, None, :, None]).reshape(g_n * p_n, n_c * 256)

    w_inter = jnp.concatenate([from_state(jnp.real(ca)), from_state(-jnp.imag(ca))], axis=0)
    a_c = jnp.concatenate([jnp.real(a_pow[n_c]).reshape(1, -1), jnp.imag(a_pow[n_c]).reshape(1, -1)], axis=1)
    dsk = jnp.tile(d_skip.astype(F32).reshape(1, SSM_WIDTH), (1, n_c))
    return w_state.astype(BF16), w_intra, w_inter.astype(BF16), a_c, dsk


def _s5_group(u, n_seq, seq_len, s0, ops):
    w_state, w_intra, w_inter, a_c, dsk = ops
    n_chunks = seq_len // S5_CHUNK
    u2 = u.reshape(n_seq * n_chunks, S5_CHUNK * SSM_WIDTH)
    d_state = _mm(u2, w_state, 256, 1024, "s5_state")
    y_intra = _mm(u2, w_intra, 256, 1024, "s5_intra")
    s_start, s_final = _s5_scan(d_state, 0, n_seq, n_chunks, s0, a_c)
    y = _s5_out(y_intra, s_start, w_inter, u2, dsk)
    return y.reshape(n_seq * seq_len, SSM_WIDTH), s_final


_CONV_PAD = 32
_CONV_RB = 64


def _conv_kernel(h_ref, buf_ref, w_ref, g_ref, b_ref, y_ref, nb_ref, xp_ref, *, tl):
    t = pl.program_id(1)
    keep = CONV_KERNEL - 1
    lo = _CONV_PAD - keep

    @pl.when(t == 0)
    def _():
        xp_ref[lo:_CONV_PAD, :] = buf_ref[...]

    @pl.when(t > 0)
    def _():
        xp_ref[lo:_CONV_PAD, :] = xp_ref[tl + lo:tl + _CONV_PAD, :]

    xp_ref[_CONV_PAD:_CONV_PAD + tl, :] = h_ref[...]
    for r in range(tl // _CONV_RB):
        base = r * _CONV_RB
        acc = jnp.zeros((_CONV_RB, CONV_WIDTH), F32)
        for k in range(CONV_KERNEL):
            acc = acc + w_ref[k:k + 1, :] * xp_ref[base + lo + k:base + lo + k + _CONV_RB, :]
        xc = acc - jnp.mean(acc, axis=-1, keepdims=True)
        var = jnp.mean(xc * xc, axis=-1, keepdims=True)
        y = xc * lax.rsqrt(var + EPS) * g_ref[...] + b_ref[...]
        y_ref[base:base + _CONV_RB, :] = y * jax.nn.sigmoid(y)
    nb_ref[...] = xp_ref[tl + lo:tl + _CONV_PAD, :]


def _conv(h, buf, w, ln_g, ln_b, tl):
    s, l, c = h.shape
    keep = CONV_KERNEL - 1
    return pl.pallas_call(
        functools.partial(_conv_kernel, tl=tl),
        grid=(s, l // tl),
        in_specs=[pl.BlockSpec((None, tl, c), lambda i, t: (i, t, 0)),
                  pl.BlockSpec((None, keep, c), lambda i, t: (i, 0, 0)),
                  pl.BlockSpec((CONV_KERNEL, c), lambda i, t: (0, 0)),
                  pl.BlockSpec((1, c), lambda i, t: (0, 0)), pl.BlockSpec((1, c), lambda i, t: (0, 0))],
        out_specs=[pl.BlockSpec((None, tl, c), lambda i, t: (i, t, 0)),
                   pl.BlockSpec((None, keep, c), lambda i, t: (i, 0, 0))],
        out_shape=[jax.ShapeDtypeStruct((s, l, c), F32), jax.ShapeDtypeStruct((s, keep, c), F32)],
        scratch_shapes=[pltpu.VMEM((tl + _CONV_PAD, c), F32)],
        compiler_params=_cparams(("parallel", "arbitrary")),
        name="conv_module",
    )(h, buf, w, ln_g.reshape(1, c), ln_b.reshape(1, c))


def _stack_q(q_ref, qs_ref):
    tq = q_ref.shape[0]
    lane = lax.broadcasted_iota(jnp.int32, (tq, LANES), 1)
    for b in range(4):
        qb = q_ref[:, b * LANES:(b + 1) * LANES]
        qs_ref[b, :tq, :] = jnp.where(lane < HEAD_DIM, qb, jnp.zeros_like(qb))
        qs_ref[b, tq:, :] = jnp.where(lane >= HEAD_DIM, qb, jnp.zeros_like(qb))


def _attn_update(score, value, m_ref, l_ref, acc_ref, tq):
    def softmax(b, s):
        m_old = m_ref[b]
        m_new = jnp.maximum(m_old, jnp.max(s, axis=-1, keepdims=True))
        alpha = jnp.exp2(m_old - m_new)
        p = jnp.exp2(s - m_new)
        l_ref[b] = alpha * l_ref[b] + jnp.sum(p, axis=-1, keepdims=True)
        m_ref[b] = m_new
        return alpha, p.astype(BF16)

    def values(b, alpha, pb):
        h0 = 2 * (b % 2)
        pv0 = jnp.dot(pb[:tq], value(h0), preferred_element_type=F32)
        pv1 = jnp.dot(pb[tq:], value(h0 + 1), preferred_element_type=F32)
        acc_ref[b, :tq, :] = alpha[:tq] * acc_ref[b, :tq, :] + pv0
        acc_ref[b, tq:, :] = alpha[tq:] * acc_ref[b, tq:, :] + pv1

    s0 = score(0)
    s1 = score(1)
    s2 = score(2)
    a0, p0 = softmax(0, s0)
    s3 = score(3)
    a1, p1 = softmax(1, s1)
    values(0, a0, p0)
    a2, p2 = softmax(2, s2)
    values(1, a1, p1)
    a3, p3 = softmax(3, s3)
    values(2, a2, p2)
    values(3, a3, p3)


def _attn_finish(lp_ref, sg_ref, l_ref, acc_ref, o_ref, tq, lam_init):
    lam = _lambda(lp_ref, lam_init)
    for h in range(N_HEADS):
        b1, half = h // 2, h % 2
        rows = slice(half * tq, (half + 1) * tq)
        o1 = acc_ref[b1, rows, :] / l_ref[b1, rows, :]
        o2 = acc_ref[b1 + 2, rows, :] / l_ref[b1 + 2, rows, :]
        o = o1 - lam * o2
        o_ref[:, h * V_DIM:(h + 1) * V_DIM] = (_rms(o, sg_ref[...]) * (1.0 - lam_init)).astype(o_ref.dtype)


def _attn_init(m_ref, l_ref, acc_ref):
    m_ref[...] = jnp.full(m_ref.shape, NEG_INF, F32)
    l_ref[...] = jnp.zeros(l_ref.shape, F32)
    acc_ref[...] = jnp.zeros(acc_ref.shape, F32)


def _attn_prompt_update(qs_ref, k_ref, vt_ref, m_ref, acc_ref, tq, mask):
    def scores(b):
        s = lax.dot_general(k_ref[:, b * LANES:(b + 1) * LANES], qs_ref[b], (((1,), (1,)), ((), ())),
                            preferred_element_type=F32)
        return s if mask is None else jnp.where(mask, s, NEG_INF)

    def softmax(b, s):
        sb = s.astype(BF16)
        m_old = m_ref[b]
        m_new = jnp.maximum(m_old, jnp.max(sb, axis=0, keepdims=True).astype(F32))
        alpha = jnp.exp2(m_old - m_new)
        m_ref[b] = m_new
        return alpha, jnp.exp2(sb - m_new.astype(BF16))

    def values(b, alpha, pb):
        for r in range(2):
            h = 2 * (b % 2) + r
            cols = slice(r * tq, (r + 1) * tq)
            pv = jnp.dot(vt_ref[h * VT_ROWS:(h + 1) * VT_ROWS, :], pb[:, cols], preferred_element_type=F32)
            acc_ref[2 * b + r] = alpha[:, cols] * acc_ref[2 * b + r] + pv

    s0 = scores(0)
    s1 = scores(1)
    s2 = scores(2)
    a0, p0 = softmax(0, s0)
    s3 = scores(3)
    a1, p1 = softmax(1, s1)
    values(0, a0, p0)
    a2, p2 = softmax(2, s2)
    values(1, a1, p1)
    a3, p3 = softmax(3, s3)
    values(2, a2, p2)
    values(3, a3, p3)


def _lambda(lp_ref, lam_init):
    lp = lp_ref[...]
    return (jnp.exp(jnp.sum(lp[0:1] * lp[1:2], axis=-1, keepdims=True))
            - jnp.exp(jnp.sum(lp[2:3] * lp[3:4], axis=-1, keepdims=True)) + lam_init)


def _attn_prompt_finish(lp_ref, sgt_ref, acc_ref, o_ref, tq, lam_init):
    lam = _lambda(lp_ref, lam_init)
    for h in range(N_HEADS):
        i1 = 2 * (h // 2) + h % 2
        o1 = acc_ref[i1, :V_DIM, :] / acc_ref[i1, V_DIM:V_DIM + 1, :]
        o2 = acc_ref[i1 + 4, :V_DIM, :] / acc_ref[i1 + 4, V_DIM:V_DIM + 1, :]
        o = o1 - lam * o2
        on = o * lax.rsqrt(jnp.mean(o * o, axis=0, keepdims=True) + EPS) * sgt_ref[...] * (1.0 - lam_init)
        o_ref[:, h * V_DIM:(h + 1) * V_DIM] = on.T.astype(o_ref.dtype)


def _attn_prompt_kernel(it_ref, jt_ref, lp_ref, sgt_ref, q_ref, k_ref, vt_ref, o_ref, qs_ref, m_ref, acc_ref,
                        *, tq, lam_init):
    step = pl.program_id(1)
    i = it_ref[step]
    j = jt_ref[step]

    @pl.when(j == 0)
    def _():
        m_ref[...] = jnp.full(m_ref.shape, NEG_INF, F32)
        acc_ref[...] = jnp.zeros(acc_ref.shape, F32)
        _stack_q(q_ref, qs_ref)

    @pl.when(j < i)
    def _():
        _attn_prompt_update(qs_ref, k_ref, vt_ref, m_ref, acc_ref, tq, None)

    @pl.when(j == i)
    def _():
        tk = k_ref.shape[0]
        key_chunk = lax.broadcasted_iota(jnp.int32, (tk, 2 * tq), 0) // CHUNK
        qry_chunk = (lax.broadcasted_iota(jnp.int32, (tk, 2 * tq), 1) % tq) // CHUNK
        _attn_prompt_update(qs_ref, k_ref, vt_ref, m_ref, acc_ref, tq, qry_chunk >= key_chunk)
        _attn_prompt_finish(lp_ref, sgt_ref, acc_ref, o_ref, tq, lam_init)


def _attn_prompt(q, kb, vt, lam_params, sub_g, n_seq, seq_len, lam_init, tq=512):
    nq = seq_len // tq
    pairs = [(i, j) for i in range(nq) for j in range(i + 1)]
    i_tab = jnp.asarray([p[0] for p in pairs], jnp.int32)
    j_tab = jnp.asarray([p[1] for p in pairs], jnp.int32)
    grid_spec = pltpu.PrefetchScalarGridSpec(
        num_scalar_prefetch=2,
        grid=(n_seq, len(pairs)),
        in_specs=[pl.BlockSpec((4, HEAD_DIM), lambda b, s, it, jt: (0, 0)),
                  pl.BlockSpec((V_DIM, 1), lambda b, s, it, jt: (0, 0)),
                  pl.BlockSpec((tq, 512), lambda b, s, it, jt: (b * nq + it[s], 0)),
                  pl.BlockSpec((tq, 512), lambda b, s, it, jt: (b * nq + jt[s], 0)),
                  pl.BlockSpec((N_HEADS * VT_ROWS, tq), lambda b, s, it, jt: (0, b * nq + jt[s]))],
        out_specs=pl.BlockSpec((tq, 512), lambda b, s, it, jt: (b * nq + it[s], 0)),
        scratch_shapes=[pltpu.VMEM((4, 2 * tq, LANES), BF16), pltpu.VMEM((4, 1, 2 * tq), F32),
                        pltpu.VMEM((8, VT_ROWS, tq), F32)],
    )
    return pl.pallas_call(
        functools.partial(_attn_prompt_kernel, tq=tq, lam_init=lam_init),
        grid_spec=grid_spec,
        out_shape=jax.ShapeDtypeStruct((n_seq * seq_len, 512), BF16),
        compiler_params=_cparams(("parallel", "arbitrary")),
        name="attn_prompt",
    )(i_tab, j_tab, lam_params, sub_g.reshape(V_DIM, 1), q, kb, vt)


def _attn_sample_kernel(lp_ref, sg_ref, q_ref, ckt_ref, cv_ref, kn_ref, vn_ref, o_ref, qs_ref, m_ref, l_ref, acc_ref,
                        *, tq, tk, lam_init):
    j = pl.program_id(1)

    @pl.when(j == 0)
    def _():
        _attn_init(m_ref, l_ref, acc_ref)
        _stack_q(q_ref, qs_ref)

    def cache_score(b):
        return jnp.dot(qs_ref[b], ckt_ref[b * LANES:(b + 1) * LANES, :].astype(BF16), preferred_element_type=F32)

    def cache_value(h):
        return cv_ref[pl.ds(h, tk, stride=N_HEADS), :].astype(BF16)

    _attn_update(cache_score, cache_value, m_ref, l_ref, acc_ref, tq)

    @pl.when(j == pl.num_programs(1) - 1)
    def _():
        def new_score(b):
            return lax.dot_general(qs_ref[b], kn_ref[:, b * LANES:(b + 1) * LANES], (((1,), (1,)), ((), ())),
                                   preferred_element_type=F32)

        def new_value(h):
            return vn_ref[:, h * V_DIM:(h + 1) * V_DIM]

        _attn_update(new_score, new_value, m_ref, l_ref, acc_ref, tq)
        _attn_finish(lp_ref, sg_ref, l_ref, acc_ref, o_ref, tq, lam_init)


def _attn_sample(q, kb, vb, cache_kt, cache_v, layer, lam_params, sub_g, lam_init, tk=2048):
    _, n_seq, _, past = cache_kt.shape
    tk = min(tk, past)
    assert past % tk == 0
    tq = q.shape[0] // n_seq
    new = pl.BlockSpec((tq, 512), lambda b, j: (b, 0))
    return pl.pallas_call(
        functools.partial(_attn_sample_kernel, tq=tq, tk=tk, lam_init=lam_init),
        grid=(n_seq, past // tk),
        in_specs=[pl.BlockSpec((4, HEAD_DIM), lambda b, j: (0, 0)), pl.BlockSpec((1, V_DIM), lambda b, j: (0, 0)),
                  new,
                  pl.BlockSpec((None, None, 512, tk), lambda b, j: (layer, b, 0, j)),
                  pl.BlockSpec((None, None, tk * N_HEADS, V_DIM), lambda b, j: (layer, b, j, 0)),
                  new, new],
        out_specs=new,
        out_shape=jax.ShapeDtypeStruct((n_seq * tq, 512), BF16),
        scratch_shapes=[pltpu.VMEM((4, 2 * tq, LANES), BF16), pltpu.VMEM((4, 2 * tq, 1), F32),
                        pltpu.VMEM((4, 2 * tq, 1), F32), pltpu.VMEM((4, 2 * tq, V_DIM), F32)],
        compiler_params=_cparams(("parallel", "arbitrary")),
        name="attn_sample",
    )(lam_params, sub_g.reshape(1, V_DIM), q, cache_kt, cache_v, kb, vb)


def _gelu_tanh(x):
    return 0.5 * x * (1.0 + jnp.tanh(math.sqrt(2.0 / math.pi) * (x + 0.044715 * (x * x * x))))


def _out_proj_kernel(x_ref, ys_ref, cv_ref, at_ref, wglu_ref, wo_ref, o_ref):
    z = _gelu_tanh(ys_ref[...])
    gate = jax.nn.sigmoid(jnp.dot(z.astype(BF16), wglu_ref[...], preferred_element_type=F32))
    ssm = (z * gate).astype(BF16)
    acc = jnp.dot(ssm, wo_ref[0:256, :], preferred_element_type=F32)
    acc = acc + jnp.dot(cv_ref[...].astype(BF16), wo_ref[256:512, :], preferred_element_type=F32)
    acc = acc + jnp.dot(at_ref[...], wo_ref[512:1024, :], preferred_element_type=F32)
    o_ref[...] = x_ref[...] + acc


def _out_proj(x, y_ssm, conv_out, attn_out, w_glu, w_out, tm=512):
    t = x.shape[0]
    row = lambda width: pl.BlockSpec((tm, width), lambda i: (i, 0))
    return pl.pallas_call(
        _out_proj_kernel,
        grid=(t // tm,),
        in_specs=[row(D_MODEL), row(256), row(256), row(512),
                  pl.BlockSpec((256, 256), lambda i: (0, 0)), pl.BlockSpec((D_MODEL, D_MODEL), lambda i: (0, 0))],
        out_specs=row(D_MODEL),
        out_shape=jax.ShapeDtypeStruct((t, D_MODEL), F32),
        compiler_params=_cparams(("parallel",)),
        name="out_proj",
    )(x, y_ssm, conv_out, attn_out, w_glu, w_out)


def _swiglu_step(h, wg, wu, wd):
    g = jnp.dot(h, wg, preferred_element_type=F32)
    u = jnp.dot(h, wu, preferred_element_type=F32)
    a = (g * jax.nn.sigmoid(g) * u).astype(BF16)
    return jnp.dot(a, wd, preferred_element_type=F32)


_FFN_SUB = 512


def _ffn_kernel(x_ref, g_ref, wg_ref, wu_ref, wd_ref, o_ref, h_ref, a_ref, acc_ref):
    j = pl.program_id(1)
    tf = a_ref.shape[1]

    @pl.when(j == 0)
    def _():
        h_ref[...] = _rms(x_ref[...], g_ref[...]).astype(BF16)

    for lo in range(0, tf, _FFN_SUB):
        cols = slice(lo, min(lo + _FFN_SUB, tf))
        gate = jnp.dot(h_ref[...], wg_ref[:, cols], preferred_element_type=F32)
        up = jnp.dot(h_ref[...], wu_ref[:, cols], preferred_element_type=F32)
        a_ref[:, cols] = (gate * jax.nn.sigmoid(gate) * up).astype(BF16)
    y = jnp.dot(a_ref[...], wd_ref[...], preferred_element_type=F32)

    @pl.when(j == 0)
    def _():
        acc_ref[...] = y

    @pl.when(j > 0)
    def _():
        acc_ref[...] += y

    @pl.when(j == pl.num_programs(1) - 1)
    def _():
        o_ref[...] = x_ref[...] + acc_ref[...]


def _ffn(x, g, wg, wu, wd, tm=1024, tf=1408):
    t = x.shape[0]
    return pl.pallas_call(
        _ffn_kernel,
        grid=(t // tm, D_FF // tf),
        in_specs=[pl.BlockSpec((tm, D_MODEL), lambda i, j: (i, 0)), pl.BlockSpec((1, D_MODEL), lambda i, j: (0, 0)),
                  pl.BlockSpec((D_MODEL, tf), lambda i, j: (0, j)), pl.BlockSpec((D_MODEL, tf), lambda i, j: (0, j)),
                  pl.BlockSpec((tf, D_MODEL), lambda i, j: (j, 0))],
        out_specs=pl.BlockSpec((tm, D_MODEL), lambda i, j: (i, 0)),
        out_shape=jax.ShapeDtypeStruct((t, D_MODEL), F32),
        scratch_shapes=[pltpu.VMEM((tm, D_MODEL), BF16), pltpu.VMEM((tm, tf), BF16), pltpu.VMEM((tm, D_MODEL), F32)],
        compiler_params=_cparams(("parallel", "arbitrary")),
        name="ffn_dense",
    )(x, g.reshape(1, D_MODEL), wg, wu, wd)


_MOE_PREFIX = 256


def _moe_route(x_ref, g_ref, r_ref, o_ref, hb_ref, rank_ref, sel_ref, gate_ref, cnt_ref):
    n_blk = x_ref.shape[0] // _MOE_PREFIX

    for blk in range(n_blk):
        rows = slice(blk * _MOE_PREFIX, (blk + 1) * _MOE_PREFIX)
        x = x_ref[rows, :]
        h32 = _rms(x, g_ref[...])
        hb_ref[rows, :] = h32.astype(BF16)
        o_ref[rows, :] = x
        logits = jnp.dot(h32, r_ref[...], preferred_element_type=F32, precision=lax.Precision.HIGHEST)
        rank_ref[:, rows] = logits.T[:N_EXPERTS, :]
    lt = rank_ref[...]
    row = lax.broadcasted_iota(jnp.int32, lt.shape, 0)
    m1 = jnp.max(lt, axis=0, keepdims=True)
    i1 = jnp.min(jnp.where(lt == m1, row, N_EXPERTS), axis=0, keepdims=True)
    lt2 = jnp.where(row == i1, -jnp.inf, lt)
    m2 = jnp.max(lt2, axis=0, keepdims=True)
    i2 = jnp.min(jnp.where(lt2 == m2, row, N_EXPERTS), axis=0, keepdims=True)
    e2 = jnp.exp(m2 - m1)
    den = 1.0 + e2
    gate_ref[...] = jnp.where(row == i1, 1.0 / den, 0.0) + jnp.where(row == i2, e2 / den, 0.0)
    sel = jnp.where(row == i1, 1.0, 0.0) + jnp.where(row == i2, 1.0, 0.0)
    sel_ref[...] = sel
    before = (lax.broadcasted_iota(jnp.int32, (_MOE_PREFIX, _MOE_PREFIX), 0)
              < lax.broadcasted_iota(jnp.int32, (_MOE_PREFIX, _MOE_PREFIX), 1))
    tri = jnp.where(before, 1.0, 0.0).astype(BF16)
    carry = jnp.zeros((N_EXPERTS, 1), F32)
    for blk in range(lt.shape[1] // _MOE_PREFIX):
        cols = slice(blk * _MOE_PREFIX, (blk + 1) * _MOE_PREFIX)
        rank_ref[:, cols] = jnp.dot(sel[:, cols].astype(BF16), tri, preferred_element_type=F32) + carry
        carry = carry + jnp.sum(sel[:, cols], axis=1, keepdims=True)
    for k in range(N_EXPERTS):
        cnt_ref[k] = carry[k, 0].astype(jnp.int32)


def _moe_kernel(x_ref, g_ref, r_ref, gf_ref, wg_ref, wu_ref, wd_ref, o_ref, hb_ref, rank_ref, sel_ref, gate_ref,
                xe_ref, ye_ref, cnt_ref, *, tb, g_rows, f_rows, s_rows, final_norm):
    e = pl.program_id(1)
    f = pl.program_id(2)
    last_f = pl.num_programs(2) - 1

    @pl.when((e == 0) & (f == 0))
    def _():
        _moe_route(x_ref, g_ref, r_ref, o_ref, hb_ref, rank_ref, sel_ref, gate_ref, cnt_ref)

    cnt = cnt_ref[e]
    rank_e = rank_ref[pl.ds(e, 1), :]
    sel_e = sel_ref[pl.ds(e, 1), :]

    def groups(rows):
        return (cnt + rows - 1) // rows

    def onehot(first_row, rows):
        tgt = (lax.broadcasted_iota(jnp.int32, (rows, tb), 0) + first_row).astype(F32)
        return jnp.where((rank_e == tgt) & (sel_e > 0.0), 1.0, 0.0)

    @pl.when(f == 0)
    def _():
        def gather(s, c):
            off = pl.multiple_of(s * g_rows, g_rows)
            xe_ref[pl.ds(off, g_rows), :] = jnp.dot(onehot(off, g_rows).astype(BF16), hb_ref[...],
                                                    preferred_element_type=F32).astype(BF16)
            return c
        lax.fori_loop(0, groups(g_rows), gather, 0)

    def ffn(first):
        def body(s, c):
            off = pl.multiple_of(s * f_rows, f_rows)
            y = _swiglu_step(xe_ref[pl.ds(off, f_rows), :], wg_ref[...], wu_ref[...], wd_ref[...])
            if first:
                ye_ref[pl.ds(off, f_rows), :] = y
            else:
                ye_ref[pl.ds(off, f_rows), :] += y
            return c
        lax.fori_loop(0, groups(f_rows), body, 0)

    @pl.when(f == 0)
    def _():
        ffn(True)
        done = groups(f_rows) * f_rows
        @pl.when(done < groups(s_rows) * s_rows)
        def _():
            ye_ref[pl.ds(pl.multiple_of(done, f_rows), f_rows), :] = jnp.zeros((f_rows, D_MODEL), F32)

    @pl.when(f > 0)
    def _():
        ffn(False)

    @pl.when(f == last_f)
    def _():
        gate_e = gate_ref[pl.ds(e, 1), :]

        def scatter(s, c):
            off = pl.multiple_of(s * s_rows, s_rows)
            p = onehot(off, s_rows)
            gate_rows = jnp.sum(p * gate_e, axis=1, keepdims=True)
            valid = (lax.broadcasted_iota(jnp.int32, (s_rows, 1), 0) + off) < cnt
            yg = jnp.where(valid, gate_rows * ye_ref[pl.ds(off, s_rows), :], 0.0).astype(BF16)
            o_ref[...] += jnp.dot(p.T.astype(BF16), yg, preferred_element_type=F32)
            return c
        lax.fori_loop(0, groups(s_rows), scatter, 0)

    if final_norm:
        @pl.when((e == pl.num_programs(1) - 1) & (f == last_f))
        def _():
            for lo in range(0, tb, _MOE_PREFIX):
                rows = slice(lo, lo + _MOE_PREFIX)
                o_ref[rows, :] = _rms(o_ref[rows, :], gf_ref[...])


def _moe(x, g, router_pad, wg, wu, wd, final_g=None, tb=2048, fc=1408, g_rows=256, f_rows=128, s_rows=256):
    t = x.shape[0]
    gf = jnp.ones((D_MODEL,), F32) if final_g is None else final_g
    assert s_rows in (f_rows, 2 * f_rows) and g_rows % f_rows == 0 and tb % g_rows == 0 and tb % s_rows == 0
    once = pl.Buffered(1)
    return pl.pallas_call(
        functools.partial(_moe_kernel, tb=tb, g_rows=g_rows, f_rows=f_rows, s_rows=s_rows,
                          final_norm=final_g is not None),
        grid=(t // tb, N_EXPERTS, D_FF // fc),
        in_specs=[pl.BlockSpec((tb, D_MODEL), lambda i, e, j: (i, 0), pipeline_mode=once),
                  pl.BlockSpec((1, D_MODEL), lambda i, e, j: (0, 0)),
                  pl.BlockSpec((D_MODEL, LANES), lambda i, e, j: (0, 0)),
                  pl.BlockSpec((1, D_MODEL), lambda i, e, j: (0, 0)),
                  pl.BlockSpec((None, D_MODEL, fc), lambda i, e, j: (e, 0, j)),
                  pl.BlockSpec((None, D_MODEL, fc), lambda i, e, j: (e, 0, j)),
                  pl.BlockSpec((None, fc, D_MODEL), lambda i, e, j: (e, j, 0))],
        out_specs=pl.BlockSpec((tb, D_MODEL), lambda i, e, j: (i, 0), pipeline_mode=once),
        out_shape=jax.ShapeDtypeStruct((t, D_MODEL), F32),
        scratch_shapes=[pltpu.VMEM((tb, D_MODEL), BF16), pltpu.VMEM((N_EXPERTS, tb), F32),
                        pltpu.VMEM((N_EXPERTS, tb), F32), pltpu.VMEM((N_EXPERTS, tb), F32),
                        pltpu.VMEM((tb, D_MODEL), BF16), pltpu.VMEM((tb, D_MODEL), F32),
                        pltpu.SMEM((N_EXPERTS,), jnp.int32)],
        compiler_params=pltpu.CompilerParams(dimension_semantics=("parallel", "arbitrary", "arbitrary"),
                                             vmem_limit_bytes=MOE_VMEM_LIMIT),
        name="ffn_moe",
    )(x, g.reshape(1, D_MODEL), router_pad, gf.reshape(1, D_MODEL), wg, wu, wd)


def _final_norm_kernel(x_ref, g_ref, o_ref):
    o_ref[...] = _rms(x_ref[...], g_ref[...])


def _final_norm(x, g, tm=1024):
    t = x.shape[0]
    return pl.pallas_call(
        _final_norm_kernel,
        grid=(t // tm,),
        in_specs=[pl.BlockSpec((tm, D_MODEL), lambda i: (i, 0)), pl.BlockSpec((1, D_MODEL), lambda i: (0, 0))],
        out_specs=pl.BlockSpec((tm, D_MODEL), lambda i: (i, 0)),
        out_shape=jax.ShapeDtypeStruct((t, D_MODEL), F32),
        compiler_params=_cparams(("parallel",)),
        name="final_norm",
    )(x, g.reshape(1, D_MODEL))


def kernel(x_prompt, x_sample, cache_k, cache_v, state_ssm_re, state_ssm_im, state_conv, norm_mix_g, w_in, ssm_a_re, ssm_a_im, ssm_log_dt, ssm_b_re, ssm_b_im, ssm_c_re, ssm_c_im, ssm_d, ssm_w_glu, conv_w, conv_ln_g, conv_ln_b, attn_lq1, attn_lk1, attn_lq2, attn_lk2, attn_sub_g, w_out, norm_ffn_g, ffn_w_gate, ffn_w_up, ffn_w_down, moe_router, moe_w_gate, moe_w_up, moe_w_down, final_norm_g):
    depth = w_in.shape[0]
    n_p, l_p, _ = x_prompt.shape
    n_s, l_s, _ = x_sample.shape
    past = cache_k.shape[2]
    keep = CONV_KERNEL - 1

    xp = x_prompt.reshape(n_p * l_p, D_MODEL)
    xs = x_sample.reshape(n_s * l_s, D_MODEL)
    cache_kt = jnp.transpose(cache_k, (0, 1, 3, 4, 5, 2)).reshape(depth, n_s, 512, past)
    cache_vr = cache_v.reshape(depth, n_s, past * N_HEADS, V_DIM)
    s0_p = jnp.zeros((n_p, 1, 2 * STATE_W), F32)
    buf_p = jnp.zeros((n_p, keep, CONV_WIDTH), F32)

    kt_buf = None
    outs = {name: [] for name in ("vp", "srp", "sip", "cp", "ks", "vs", "srs", "sis", "cs")}
    for l in range(depth):
        lam_init = 0.8 - 0.6 * math.exp(-0.3 * l)
        w_in_b = w_in[l].astype(BF16)
        wkvt = w_in_b[:, IN_WIDTH - 2 * ATTN_WIDTH:].T
        ops = _s5_operators(ssm_a_re[l], ssm_a_im[l], ssm_log_dt[l], ssm_b_re[l], ssm_b_im[l], ssm_c_re[l],
                            ssm_c_im[l], ssm_d[l], S5_CHUNK)
        lam_params = jnp.stack([attn_lq1[l], attn_lk1[l], attn_lq2[l], attn_lk2[l]]).astype(F32)
        w_glu_b = ssm_w_glu[l].astype(BF16)
        w_out_b = w_out[l].astype(BF16)
        conv_p = (conv_w[l], conv_ln_g[l], conv_ln_b[l])

        u, hc, q, kb, v_p, vt, kt_buf = _in_proj_prompt(xp, norm_mix_g[l], w_in_b, wkvt, kt_buf, l, depth, n_p)
        y_ssm, sf_p = _s5_group(u, n_p, l_p, s0_p, ops)
        cv, cb_p = _conv(hc.reshape(n_p, l_p, CONV_WIDTH), buf_p, *conv_p, tl=256)
        at = _attn_prompt(q, kb, vt, lam_params, attn_sub_g[l], n_p, l_p, lam_init)
        xp = _out_proj(xp, y_ssm, cv.reshape(n_p * l_p, CONV_WIDTH), at, w_glu_b, w_out_b)

        s0_s = jnp.concatenate([state_ssm_re[l].reshape(n_s, 1, STATE_W), state_ssm_im[l].reshape(n_s, 1, STATE_W)],
                               axis=-1).astype(F32)
        u, hc, q, kb, vb, k_s, v_s = _in_proj_sample(xs, norm_mix_g[l], w_in_b)
        y_ssm, sf_s = _s5_group(u, n_s, l_s, s0_s, ops)
        cv, cb_s = _conv(hc.reshape(n_s, l_s, CONV_WIDTH), state_conv[l].astype(F32), *conv_p, tl=l_s)
        at = _attn_sample(q, kb, vb, cache_kt, cache_vr, l, lam_params, attn_sub_g[l], lam_init)
        xs = _out_proj(xs, y_ssm, cv.reshape(n_s * l_s, CONV_WIDTH), at, w_glu_b, w_out_b)

        j = l // 2
        if l % 2 == 0:
            wg, wu, wd = ffn_w_gate[j].astype(BF16), ffn_w_up[j].astype(BF16), ffn_w_down[j].astype(BF16)
            xp = _ffn(xp, norm_ffn_g[l], wg, wu, wd)
            xs = _ffn(xs, norm_ffn_g[l], wg, wu, wd)
        else:
            router_pad = jnp.pad(moe_router[j].astype(F32), ((0, 0), (0, LANES - N_EXPERTS)))
            wg, wu, wd = moe_w_gate[j].astype(BF16), moe_w_up[j].astype(BF16), moe_w_down[j].astype(BF16)
            final_g = final_norm_g if l == depth - 1 else None
            xp = _moe(xp, norm_ffn_g[l], router_pad, wg, wu, wd, final_g)
            xs = _moe(xs, norm_ffn_g[l], router_pad, wg, wu, wd, final_g)

        outs["vp"].append(v_p.reshape(n_p, l_p, N_HEADS, V_DIM))
        outs["srp"].append(sf_p[:, 0, :STATE_W].reshape(n_p, N_SSM_GROUPS, SSM_STATE))
        outs["sip"].append(sf_p[:, 0, STATE_W:].reshape(n_p, N_SSM_GROUPS, SSM_STATE))
        outs["cp"].append(cb_p)
        outs["ks"].append(k_s.reshape(n_s, l_s, 2, N_HEADS, HEAD_DIM))
        outs["vs"].append(v_s.reshape(n_s, l_s, N_HEADS, V_DIM))
        outs["srs"].append(sf_s[:, 0, :STATE_W].reshape(n_s, N_SSM_GROUPS, SSM_STATE))
        outs["sis"].append(sf_s[:, 0, STATE_W:].reshape(n_s, N_SSM_GROUPS, SSM_STATE))
        outs["cs"].append(cb_s)

    if depth % 2 == 0:
        yp, ys = xp, xs
    else:
        yp = _final_norm(xp, final_norm_g)
        ys = _final_norm(xs, final_norm_g)
    st = {name: jnp.stack(vals) for name, vals in outs.items()}
    k_prompt = jnp.transpose(kt_buf.reshape(depth, n_p, 2, N_HEADS, HEAD_DIM, l_p), (0, 1, 5, 2, 3, 4))
    return (yp.reshape(n_p, l_p, D_MODEL), ys.reshape(n_s, l_s, D_MODEL),
            k_prompt, st["vp"], st["srp"], st["sip"], st["cp"],
            st["ks"], st["vs"], st["srs"], st["sis"], st["cs"])
```

```python
import functools
import math

import jax
import jax.numpy as jnp
from jax import lax
from jax.experimental import pallas as pl
from jax.experimental.pallas import tpu as pltpu

F32 = jnp.float32
BF16 = jnp.bfloat16

D_MODEL = 1024
CHUNK = 64
SSM_GROUP = 16
N_SSM_GROUPS = 16
SSM_WIDTH = 256
SSM_STATE = 64
STATE_W = N_SSM_GROUPS * SSM_STATE
CONV_WIDTH = 256
CONV_KERNEL = 31
N_HEADS = 4
HEAD_DIM = 64
V_DIM = 128
VT_ROWS = V_DIM + 16
QK_WIDTH = 256
ATTN_WIDTH = 512
IN_WIDTH = 2304
ATTN_SCALE = 1.0 / math.sqrt(HEAD_DIM)
LOG2E = math.log2(math.e)
NEG_INF = -1e30
D_FF = 2816
N_EXPERTS = 8
EPS = 1e-6

S5_CHUNK = 8
LANES = 128
SUBLANES = 8
VMEM_LIMIT = 48 * 1024 * 1024
MOE_VMEM_LIMIT = 58 * 1024 * 1024


def _cparams(sem):
    return pltpu.CompilerParams(dimension_semantics=sem, vmem_limit_bytes=VMEM_LIMIT)


def _rms(x, g):
    return x * lax.rsqrt(jnp.mean(x * x, axis=-1, keepdims=True) + EPS) * g


def _mixer_inputs(x_ref, g_ref, w_ref, u2_ref, hc_ref, q_ref, us_ref):
    h = _rms(x_ref[...], g_ref[...]).astype(BF16)

    def proj(lo, hi):
        return jnp.dot(h, w_ref[:, lo:hi], preferred_element_type=F32)

    u = proj(0, 256)
    n_chunks = us_ref.shape[1] // S5_CHUNK
    for c in range(SSM_WIDTH // LANES):
        us_ref[c] = u[:, c * LANES:(c + 1) * LANES]
        for j in range(S5_CHUNK):
            lo = j * SSM_WIDTH + c * LANES
            u2_ref[:, lo:lo + LANES] = us_ref[c, pl.ds(j, n_chunks, stride=S5_CHUNK), :]
    c_val = proj(256, 512)
    c_gate = proj(512, 768)
    hc_ref[...] = c_val * jax.nn.sigmoid(c_gate)
    q_ref[...] = (proj(768, 1280) * (ATTN_SCALE * LOG2E)).astype(BF16)
    return h, proj(1280, 1792), proj(1792, 2304)


def _in_proj_prompt_kernel(*refs, aliased):
    x_ref, g_ref, w_ref, wkvt_ref = refs[:4]
    u2_ref, hc_ref, q_ref, kb_ref, vt_ref, kt_ref, v_ref, us_ref = refs[6 if aliased else 4:]
    h, k, v = _mixer_inputs(x_ref, g_ref, w_ref, u2_ref, hc_ref, q_ref, us_ref)
    kb_ref[...] = k.astype(BF16)
    for head in range(N_HEADS):
        v_ref[pl.ds(head, v.shape[0], stride=N_HEADS), :] = v[:, head * V_DIM:(head + 1) * V_DIM]
    nt = (((1,), (1,)), ((), ()))
    kt_ref[...] = lax.dot_general(wkvt_ref[:ATTN_WIDTH, :], h, nt, preferred_element_type=F32)
    vt = lax.dot_general(wkvt_ref[ATTN_WIDTH:, :], h, nt, preferred_element_type=F32).astype(BF16)
    for head in range(N_HEADS):
        vt_ref[head * VT_ROWS:head * VT_ROWS + V_DIM, :] = vt[head * V_DIM:(head + 1) * V_DIM, :]
        vt_ref[head * VT_ROWS + V_DIM:(head + 1) * VT_ROWS, :] = jnp.ones((VT_ROWS - V_DIM, vt.shape[1]), BF16)


def _in_proj_prompt(x, g, w_bf16, wkvt_bf16, kv_bufs, layer, depth, n_seq, tm=512):
    t = x.shape[0]
    seq_len = t // n_seq
    per_seq = seq_len // tm
    row = lambda width: pl.BlockSpec((tm, width), lambda i: (i, 0))
    in_specs = [row(D_MODEL), pl.BlockSpec((1, D_MODEL), lambda i: (0, 0)),
                pl.BlockSpec((D_MODEL, IN_WIDTH), lambda i: (0, 0)),
                pl.BlockSpec((2 * ATTN_WIDTH, D_MODEL), lambda i: (0, 0))]
    args = [x, g.reshape(1, D_MODEL), w_bf16, wkvt_bf16]
    aliases = {}
    if kv_bufs is not None:
        in_specs += [pl.BlockSpec(memory_space=pl.ANY), pl.BlockSpec(memory_space=pl.ANY)]
        args += list(kv_bufs)
        aliases = {4: 5, 5: 6}
    return pl.pallas_call(
        functools.partial(_in_proj_prompt_kernel, aliased=kv_bufs is not None),
        grid=(t // tm,),
        in_specs=in_specs,
        out_specs=[pl.BlockSpec((tm // S5_CHUNK, S5_CHUNK * SSM_WIDTH), lambda i: (i, 0)),
                   row(256), row(512), row(512),
                   pl.BlockSpec((N_HEADS * VT_ROWS, tm), lambda i: (0, i)),
                   pl.BlockSpec((None, None, ATTN_WIDTH, tm), lambda i: (layer, i // per_seq, 0, i % per_seq)),
                   pl.BlockSpec((None, None, tm * N_HEADS, V_DIM), lambda i: (layer, i // per_seq, i % per_seq, 0))],
        out_shape=[jax.ShapeDtypeStruct((t // S5_CHUNK, S5_CHUNK * SSM_WIDTH), F32),
                   jax.ShapeDtypeStruct((t, 256), F32),
                   jax.ShapeDtypeStruct((t, 512), BF16), jax.ShapeDtypeStruct((t, 512), BF16),
                   jax.ShapeDtypeStruct((N_HEADS * VT_ROWS, t), BF16),
                   jax.ShapeDtypeStruct((depth, n_seq, ATTN_WIDTH, seq_len), F32),
                   jax.ShapeDtypeStruct((depth, n_seq, seq_len * N_HEADS, V_DIM), F32)],
        scratch_shapes=[pltpu.VMEM((SSM_WIDTH // LANES, tm, LANES), F32)],
        input_output_aliases=aliases,
        compiler_params=_cparams(("parallel",)),
        name="in_proj_prompt",
    )(*args)


def _in_proj_sample_kernel(x_ref, g_ref, w_ref, u2_ref, hc_ref, q_ref, kb_ref, vb_ref, k_ref, v_ref, us_ref):
    _, k, v = _mixer_inputs(x_ref, g_ref, w_ref, u2_ref, hc_ref, q_ref, us_ref)
    k_ref[...] = k
    kb_ref[...] = k.astype(BF16)
    v_ref[...] = v
    vb_ref[...] = v.astype(BF16)


def _in_proj_sample(x, g, w_bf16, tm=512):
    t = x.shape[0]
    tm = min(tm, t)
    row = lambda width: pl.BlockSpec((tm, width), lambda i: (i, 0))
    return pl.pallas_call(
        _in_proj_sample_kernel,
        grid=(t // tm,),
        in_specs=[row(D_MODEL), pl.BlockSpec((1, D_MODEL), lambda i: (0, 0)),
                  pl.BlockSpec((D_MODEL, IN_WIDTH), lambda i: (0, 0))],
        out_specs=[pl.BlockSpec((tm // S5_CHUNK, S5_CHUNK * SSM_WIDTH), lambda i: (i, 0)),
                   row(256), row(512), row(512), row(512), row(512), row(512)],
        out_shape=[jax.ShapeDtypeStruct((t // S5_CHUNK, S5_CHUNK * SSM_WIDTH), F32),
                   jax.ShapeDtypeStruct((t, 256), F32),
                   jax.ShapeDtypeStruct((t, 512), BF16), jax.ShapeDtypeStruct((t, 512), BF16),
                   jax.ShapeDtypeStruct((t, 512), BF16), jax.ShapeDtypeStruct((t, 512), F32),
                   jax.ShapeDtypeStruct((t, 512), F32)],
        scratch_shapes=[pltpu.VMEM((SSM_WIDTH // LANES, tm, LANES), F32)],
        compiler_params=_cparams(("parallel",)),
        name="in_proj_sample",
    )(x, g.reshape(1, D_MODEL), w_bf16)


def _mm_kernel(x_ref, w_ref, o_ref):
    o_ref[...] = jnp.dot(x_ref[...].astype(BF16), w_ref[...], preferred_element_type=F32)


def _mm(x, w_bf16, tm, tn, name):
    m, k = x.shape
    n = w_bf16.shape[1]
    return pl.pallas_call(
        _mm_kernel,
        grid=(n // tn, m // tm),
        in_specs=[pl.BlockSpec((tm, k), lambda j, i: (i, 0)), pl.BlockSpec((k, tn), lambda j, i: (0, j))],
        out_specs=pl.BlockSpec((tm, tn), lambda j, i: (i, j)),
        out_shape=jax.ShapeDtypeStruct((m, n), F32),
        compiler_params=_cparams(("parallel", "parallel")),
        name=name,
    )(x, w_bf16)


def _s5_scan_kernel(d_ref, s0_ref, a_ref, ss_ref, sf_ref, st_ref, *, rows):
    t = pl.program_id(1)

    @pl.when(t == 0)
    def _():
        st_ref[...] = s0_ref[...]

    a_re = a_ref[:, :STATE_W]
    a_im = a_ref[:, STATE_W:]

    def step(c, carry):
        s_re, s_im = carry
        ss_ref[pl.ds(c, 1), :STATE_W] = s_re
        ss_ref[pl.ds(c, 1), STATE_W:] = s_im
        d = d_ref[pl.ds(c, 1), :]
        n_re = a_re * s_re - a_im * s_im + d[:, :STATE_W]
        n_im = a_re * s_im + a_im * s_re + d[:, STATE_W:]
        return n_re, n_im

    s_re, s_im = lax.fori_loop(0, rows, step, (st_ref[:, :STATE_W], st_ref[:, STATE_W:]))
    st_ref[:, :STATE_W] = s_re
    st_ref[:, STATE_W:] = s_im
    sf_ref[...] = st_ref[...]


def _s5_scan(d, row0, n_seq, n_chunks, s0, a_c):
    rows = min(n_chunks, 256)
    nt = n_chunks // rows
    blk0 = row0 // rows
    w = 2 * STATE_W
    return pl.pallas_call(
        functools.partial(_s5_scan_kernel, rows=rows),
        grid=(n_seq, nt),
        in_specs=[pl.BlockSpec((rows, w), lambda s, t: (blk0 + s * nt + t, 0)),
                  pl.BlockSpec((None, 1, w), lambda s, t: (s, 0, 0)),
                  pl.BlockSpec((1, w), lambda s, t: (0, 0))],
        out_specs=[pl.BlockSpec((rows, w), lambda s, t: (s * nt + t, 0)),
                   pl.BlockSpec((None, 1, w), lambda s, t: (s, 0, 0))],
        out_shape=[jax.ShapeDtypeStruct((n_seq * n_chunks, w), F32), jax.ShapeDtypeStruct((n_seq, 1, w), F32)],
        scratch_shapes=[pltpu.VMEM((1, w), F32)],
        compiler_params=_cparams(("parallel", "arbitrary")),
        name="s5_scan",
    )(d, s0, a_c)


def _s5_out_kernel(yi_ref, ss_ref, w_ref, u_ref, dsk_ref, y_ref):
    y_ref[...] = (yi_ref[...] + jnp.dot(ss_ref[...].astype(BF16), w_ref[...], preferred_element_type=F32)
                  + dsk_ref[...] * u_ref[...])


def _s5_out(y_intra, s_start, w_inter, u2, dsk, tm=256, tn=512):
    m, n = y_intra.shape
    k = s_start.shape[1]
    tile = pl.BlockSpec((tm, tn), lambda j, i: (i, j))
    return pl.pallas_call(
        _s5_out_kernel,
        grid=(n // tn, m // tm),
        in_specs=[tile, pl.BlockSpec((tm, k), lambda j, i: (i, 0)), pl.BlockSpec((k, tn), lambda j, i: (0, j)),
                  tile, pl.BlockSpec((1, tn), lambda j, i: (0, j))],
        out_specs=tile,
        out_shape=jax.ShapeDtypeStruct((m, n), F32),
        compiler_params=_cparams(("parallel", "parallel")),
        name="s5_out",
    )(y_intra, s_start, w_inter, u2, dsk)


def _s5_operators(a_re, a_im, log_dt, b_re, b_im, c_re, c_im, d_skip, n_c):
    g_n, p_n, h_n = N_SSM_GROUPS, SSM_STATE, SSM_GROUP
    lam = lax.complex(a_re.astype(F32), a_im.astype(F32))
    dt = jnp.exp(log_dt.astype(F32))[:, None]
    a_bar = jnp.exp(lam * dt)
    b_bar = ((a_bar - 1.0) / lam)[..., None] * lax.complex(b_re.astype(F32), b_im.astype(F32))
    c = lax.complex(c_re.astype(F32), c_im.astype(F32))
    pows = [jnp.ones_like(a_bar)]
    for _ in range(n_c):
        pows.append(pows[-1] * a_bar)
    a_pow = jnp.stack(pows)
    eye_g = jnp.eye(g_n, dtype=F32)

    ws = a_pow[:n_c][::-1][:, :, :, None] * b_bar[None]
    ws = jnp.transpose(ws, (0, 1, 3, 2))

    def to_state(x):
        return (x[:, :, :, None, :] * eye_g[None, :, None, :, None]).reshape(n_c * g_n * h_n, g_n * p_n)

    w_state = jnp.concatenate([to_state(jnp.real(ws)), to_state(jnp.imag(ws))], axis=1)

    ker = jnp.real(jnp.einsum('ghp,kgp,gpx->kghx', c, a_pow[:n_c], b_bar, precision=lax.Precision.HIGHEST))
    blk = jnp.transpose(ker, (0, 1, 3, 2))
    blk = (blk[:, :, :, None, :] * eye_g[None, :, None, :, None]).reshape(n_c, 256, 256).astype(BF16)
    zero = jnp.zeros((256, 256), BF16)
    w_intra = jnp.concatenate(
        [jnp.concatenate([blk[t - j] if t >= j else zero for t in range(n_c)], axis=1) for j in range(n_c)], axis=0)

    ca = c[None] * jnp.transpose(a_pow[1:], (0, 1, 2))[:, :, None, :]
    ca = jnp.transpose(ca, (1, 3, 0, 2))

    def from_state(x):
        return (x[:, :, :, None, :] * eye_g[:, None, None, :, None]).reshape(g_n * p_n, n_c * 256)

    w_inter = jnp.concatenate([from_state(jnp.real(ca)), from_state(-jnp.imag(ca))], axis=0)
    a_c = jnp.concatenate([jnp.real(a_pow[n_c]).reshape(1, -1), jnp.imag(a_pow[n_c]).reshape(1, -1)], axis=1)
    dsk = jnp.tile(d_skip.astype(F32).reshape(1, SSM_WIDTH), (1, n_c))
    return w_state.astype(BF16), w_intra, w_inter.astype(BF16), a_c, dsk


def _s5_group(u2, n_seq, seq_len, s0, ops):
    w_state, w_intra, w_inter, a_c, dsk = ops
    n_chunks = seq_len // S5_CHUNK
    d_state = _mm(u2, w_state, 256, 1024, "s5_state")
    y_intra = _mm(u2, w_intra, 256, 1024, "s5_intra")
    s_start, s_final = _s5_scan(d_state, 0, n_seq, n_chunks, s0, a_c)
    return _s5_out(y_intra, s_start, w_inter, u2, dsk), s_final


_CONV_PAD = 32
_CONV_RB = 64


def _conv_kernel(h_ref, buf_ref, w_ref, g_ref, b_ref, y_ref, nb_ref, xp_ref, xs_ref, *, tl):
    t = pl.program_id(1)
    keep = CONV_KERNEL - 1
    lo = _CONV_PAD - keep

    @pl.when(t == 0)
    def _():
        xp_ref[lo:_CONV_PAD, :] = buf_ref[...]

    @pl.when(t > 0)
    def _():
        xp_ref[lo:_CONV_PAD, :] = xp_ref[tl + lo:tl + _CONV_PAD, :]

    xp_ref[_CONV_PAD:_CONV_PAD + tl, :] = h_ref[...]
    for phase in range(SUBLANES):
        span = tl + (keep - phase) // SUBLANES * SUBLANES
        xs_ref[phase, :span, :] = xp_ref[lo + phase:lo + phase + span, :]
    for r in range(tl // _CONV_RB):
        base = r * _CONV_RB
        acc = jnp.zeros((_CONV_RB, CONV_WIDTH), F32)
        for k in range(CONV_KERNEL):
            start = base + k - k % SUBLANES
            acc = acc + w_ref[k:k + 1, :] * xs_ref[k % SUBLANES, start:start + _CONV_RB, :]
        xc = acc - jnp.mean(acc, axis=-1, keepdims=True)
        var = jnp.mean(xc * xc, axis=-1, keepdims=True)
        y = xc * lax.rsqrt(var + EPS) * g_ref[...] + b_ref[...]
        y_ref[base:base + _CONV_RB, :] = y * jax.nn.sigmoid(y)
    nb_ref[...] = xp_ref[tl + lo:tl + _CONV_PAD, :]


def _conv(h, buf, w, ln_g, ln_b, tl):
    s, l, c = h.shape
    keep = CONV_KERNEL - 1
    return pl.pallas_call(
        functools.partial(_conv_kernel, tl=tl),
        grid=(s, l // tl),
        in_specs=[pl.BlockSpec((None, tl, c), lambda i, t: (i, t, 0)),
                  pl.BlockSpec((None, keep, c), lambda i, t: (i, 0, 0)),
                  pl.BlockSpec((CONV_KERNEL, c), lambda i, t: (0, 0)),
                  pl.BlockSpec((1, c), lambda i, t: (0, 0)), pl.BlockSpec((1, c), lambda i, t: (0, 0))],
        out_specs=[pl.BlockSpec((None, tl, c), lambda i, t: (i, t, 0)),
                   pl.BlockSpec((None, keep, c), lambda i, t: (i, 0, 0))],
        out_shape=[jax.ShapeDtypeStruct((s, l, c), F32), jax.ShapeDtypeStruct((s, keep, c), F32)],
        scratch_shapes=[pltpu.VMEM((tl + _CONV_PAD, c), F32),
                        pltpu.VMEM((SUBLANES, tl + (keep // SUBLANES) * SUBLANES, c), F32)],
        compiler_params=_cparams(("parallel", "arbitrary")),
        name="conv_module",
    )(h, buf, w, ln_g.reshape(1, c), ln_b.reshape(1, c))


def _stack_q(q_ref, qs_ref):
    tq = q_ref.shape[0]
    lane = lax.broadcasted_iota(jnp.int32, (tq, LANES), 1)
    for b in range(4):
        qb = q_ref[:, b * LANES:(b + 1) * LANES]
        qs_ref[b, :tq, :] = jnp.where(lane < HEAD_DIM, qb, jnp.zeros_like(qb))
        qs_ref[b, tq:, :] = jnp.where(lane >= HEAD_DIM, qb, jnp.zeros_like(qb))


def _attn_update(score, value, m_ref, l_ref, acc_ref, tq):
    def softmax(b, s):
        m_old = m_ref[b]
        m_new = jnp.maximum(m_old, jnp.max(s, axis=-1, keepdims=True))
        alpha = jnp.exp2(m_old - m_new)
        p = jnp.exp2(s - m_new)
        l_ref[b] = alpha * l_ref[b] + jnp.sum(p, axis=-1, keepdims=True)
        m_ref[b] = m_new
        return alpha, p.astype(BF16)

    def values(b, alpha, pb):
        h0 = 2 * (b % 2)
        pv0 = jnp.dot(pb[:tq], value(h0), preferred_element_type=F32)
        pv1 = jnp.dot(pb[tq:], value(h0 + 1), preferred_element_type=F32)
        acc_ref[b, :tq, :] = alpha[:tq] * acc_ref[b, :tq, :] + pv0
        acc_ref[b, tq:, :] = alpha[tq:] * acc_ref[b, tq:, :] + pv1

    s0 = score(0)
    s1 = score(1)
    s2 = score(2)
    a0, p0 = softmax(0, s0)
    s3 = score(3)
    a1, p1 = softmax(1, s1)
    values(0, a0, p0)
    a2, p2 = softmax(2, s2)
    values(1, a1, p1)
    a3, p3 = softmax(3, s3)
    values(2, a2, p2)
    values(3, a3, p3)


def _attn_finish(lp_ref, sg_ref, l_ref, acc_ref, o_ref, tq, lam_init):
    lam = _lambda(lp_ref, lam_init)
    for h in range(N_HEADS):
        b1, half = h // 2, h % 2
        rows = slice(half * tq, (half + 1) * tq)
        o1 = acc_ref[b1, rows, :] / l_ref[b1, rows, :]
        o2 = acc_ref[b1 + 2, rows, :] / l_ref[b1 + 2, rows, :]
        o = o1 - lam * o2
        o_ref[:, h * V_DIM:(h + 1) * V_DIM] = (_rms(o, sg_ref[...]) * (1.0 - lam_init)).astype(o_ref.dtype)


def _attn_init(m_ref, l_ref, acc_ref):
    m_ref[...] = jnp.full(m_ref.shape, NEG_INF, F32)
    l_ref[...] = jnp.zeros(l_ref.shape, F32)
    acc_ref[...] = jnp.zeros(acc_ref.shape, F32)


def _attn_prompt_update(qs_ref, k_ref, vt_ref, m_ref, acc_ref, tq, mask):
    def scores(b):
        s = lax.dot_general(k_ref[:, b * LANES:(b + 1) * LANES], qs_ref[b], (((1,), (1,)), ((), ())),
                            preferred_element_type=F32)
        return s if mask is None else jnp.where(mask, s, NEG_INF)

    def softmax(b, s):
        sb = s.astype(BF16)
        m_old = m_ref[b]
        m_new = jnp.maximum(m_old, jnp.max(sb, axis=0, keepdims=True).astype(F32))
        alpha = jnp.exp2(m_old - m_new)
        m_ref[b] = m_new
        return alpha, jnp.exp2(sb - m_new.astype(BF16))

    def values(b, alpha, pb):
        for r in range(2):
            h = 2 * (b % 2) + r
            cols = slice(r * tq, (r + 1) * tq)
            pv = jnp.dot(vt_ref[h * VT_ROWS:(h + 1) * VT_ROWS, :], pb[:, cols], preferred_element_type=F32)
            acc_ref[2 * b + r] = alpha[:, cols] * acc_ref[2 * b + r] + pv

    s0 = scores(0)
    s1 = scores(1)
    s2 = scores(2)
    a0, p0 = softmax(0, s0)
    s3 = scores(3)
    a1, p1 = softmax(1, s1)
    values(0, a0, p0)
    a2, p2 = softmax(2, s2)
    values(1, a1, p1)
    a3, p3 = softmax(3, s3)
    values(2, a2, p2)
    values(3, a3, p3)


def _lambda(lp_ref, lam_init):
    lp = lp_ref[...]
    return (jnp.exp(jnp.sum(lp[0:1] * lp[1:2], axis=-1, keepdims=True))
            - jnp.exp(jnp.sum(lp[2:3] * lp[3:4], axis=-1, keepdims=True)) + lam_init)


def _attn_prompt_finish(lp_ref, sgt_ref, acc_ref, o_ref, tq, lam_init):
    lam = _lambda(lp_ref, lam_init)
    for h in range(N_HEADS):
        i1 = 2 * (h // 2) + h % 2
        o1 = acc_ref[i1, :V_DIM, :] / acc_ref[i1, V_DIM:V_DIM + 1, :]
        o2 = acc_ref[i1 + 4, :V_DIM, :] / acc_ref[i1 + 4, V_DIM:V_DIM + 1, :]
        o = o1 - lam * o2
        on = o * lax.rsqrt(jnp.mean(o * o, axis=0, keepdims=True) + EPS) * sgt_ref[...] * (1.0 - lam_init)
        o_ref[:, h * V_DIM:(h + 1) * V_DIM] = on.T.astype(o_ref.dtype)


def _attn_prompt_kernel(it_ref, jt_ref, lp_ref, sgt_ref, q_ref, k_ref, vt_ref, o_ref, qs_ref, m_ref, acc_ref,
                        *, tq, lam_init):
    step = pl.program_id(1)
    i = it_ref[step]
    j = jt_ref[step]

    @pl.when(j == 0)
    def _():
        m_ref[...] = jnp.full(m_ref.shape, NEG_INF, F32)
        acc_ref[...] = jnp.zeros(acc_ref.shape, F32)
        _stack_q(q_ref, qs_ref)

    @pl.when(j < i)
    def _():
        _attn_prompt_update(qs_ref, k_ref, vt_ref, m_ref, acc_ref, tq, None)

    @pl.when(j == i)
    def _():
        tk = k_ref.shape[0]
        key_chunk = lax.broadcasted_iota(jnp.int32, (tk, 2 * tq), 0) // CHUNK
        qry_chunk = (lax.broadcasted_iota(jnp.int32, (tk, 2 * tq), 1) % tq) // CHUNK
        _attn_prompt_update(qs_ref, k_ref, vt_ref, m_ref, acc_ref, tq, qry_chunk >= key_chunk)
        _attn_prompt_finish(lp_ref, sgt_ref, acc_ref, o_ref, tq, lam_init)


def _attn_prompt(q, kb, vt, lam_params, sub_g, n_seq, seq_len, lam_init, tq=512):
    nq = seq_len // tq
    pairs = [(i, j) for i in range(nq) for j in range(i + 1)]
    i_tab = jnp.asarray([p[0] for p in pairs], jnp.int32)
    j_tab = jnp.asarray([p[1] for p in pairs], jnp.int32)
    grid_spec = pltpu.PrefetchScalarGridSpec(
        num_scalar_prefetch=2,
        grid=(n_seq, len(pairs)),
        in_specs=[pl.BlockSpec((4, HEAD_DIM), lambda b, s, it, jt: (0, 0)),
                  pl.BlockSpec((V_DIM, 1), lambda b, s, it, jt: (0, 0)),
                  pl.BlockSpec((tq, 512), lambda b, s, it, jt: (b * nq + it[s], 0)),
                  pl.BlockSpec((tq, 512), lambda b, s, it, jt: (b * nq + jt[s], 0)),
                  pl.BlockSpec((N_HEADS * VT_ROWS, tq), lambda b, s, it, jt: (0, b * nq + jt[s]))],
        out_specs=pl.BlockSpec((tq, 512), lambda b, s, it, jt: (b * nq + it[s], 0)),
        scratch_shapes=[pltpu.VMEM((4, 2 * tq, LANES), BF16), pltpu.VMEM((4, 1, 2 * tq), F32),
                        pltpu.VMEM((8, VT_ROWS, tq), F32)],
    )
    return pl.pallas_call(
        functools.partial(_attn_prompt_kernel, tq=tq, lam_init=lam_init),
        grid_spec=grid_spec,
        out_shape=jax.ShapeDtypeStruct((n_seq * seq_len, 512), BF16),
        compiler_params=_cparams(("parallel", "arbitrary")),
        name="attn_prompt",
    )(i_tab, j_tab, lam_params, sub_g.reshape(V_DIM, 1), q, kb, vt)


def _attn_sample_kernel(lp_ref, sg_ref, q_ref, ckt_ref, cv_ref, kn_ref, vn_ref, o_ref, qs_ref, m_ref, l_ref, acc_ref,
                        *, tq, tk, lam_init):
    j = pl.program_id(1)

    @pl.when(j == 0)
    def _():
        _attn_init(m_ref, l_ref, acc_ref)
        _stack_q(q_ref, qs_ref)

    def cache_score(b):
        return jnp.dot(qs_ref[b], ckt_ref[b * LANES:(b + 1) * LANES, :].astype(BF16), preferred_element_type=F32)

    def cache_value(h):
        return cv_ref[pl.ds(h, tk, stride=N_HEADS), :].astype(BF16)

    _attn_update(cache_score, cache_value, m_ref, l_ref, acc_ref, tq)

    @pl.when(j == pl.num_programs(1) - 1)
    def _():
        def new_score(b):
            return lax.dot_general(qs_ref[b], kn_ref[:, b * LANES:(b + 1) * LANES], (((1,), (1,)), ((), ())),
                                   preferred_element_type=F32)

        def new_value(h):
            return vn_ref[:, h * V_DIM:(h + 1) * V_DIM]

        _attn_update(new_score, new_value, m_ref, l_ref, acc_ref, tq)
        _attn_finish(lp_ref, sg_ref, l_ref, acc_ref, o_ref, tq, lam_init)


def _attn_sample(q, kb, vb, cache_kt, cache_v, layer, lam_params, sub_g, lam_init, tk=2048):
    _, n_seq, _, past = cache_kt.shape
    tk = min(tk, past)
    assert past % tk == 0
    tq = q.shape[0] // n_seq
    new = pl.BlockSpec((tq, 512), lambda b, j: (b, 0))
    return pl.pallas_call(
        functools.partial(_attn_sample_kernel, tq=tq, tk=tk, lam_init=lam_init),
        grid=(n_seq, past // tk),
        in_specs=[pl.BlockSpec((4, HEAD_DIM), lambda b, j: (0, 0)), pl.BlockSpec((1, V_DIM), lambda b, j: (0, 0)),
                  new,
                  pl.BlockSpec((None, None, 512, tk), lambda b, j: (layer, b, 0, j)),
                  pl.BlockSpec((None, None, tk * N_HEADS, V_DIM), lambda b, j: (layer, b, j, 0)),
                  new, new],
        out_specs=new,
        out_shape=jax.ShapeDtypeStruct((n_seq * tq, 512), BF16),
        scratch_shapes=[pltpu.VMEM((4, 2 * tq, LANES), BF16), pltpu.VMEM((4, 2 * tq, 1), F32),
                        pltpu.VMEM((4, 2 * tq, 1), F32), pltpu.VMEM((4, 2 * tq, V_DIM), F32)],
        compiler_params=_cparams(("parallel", "arbitrary")),
        name="attn_sample",
    )(lam_params, sub_g.reshape(1, V_DIM), q, cache_kt, cache_v, kb, vb)


def _gelu_tanh(x):
    return 0.5 * x * (1.0 + jnp.tanh(math.sqrt(2.0 / math.pi) * (x + 0.044715 * (x * x * x))))


def _out_proj_kernel(x_ref, y2_ref, cv_ref, at_ref, wglu_ref, wo_ref, o_ref, ys_ref):
    n_chunks = y2_ref.shape[0]
    for c in range(SSM_WIDTH // LANES):
        for j in range(S5_CHUNK):
            lo = j * SSM_WIDTH + c * LANES
            ys_ref[c, pl.ds(j, n_chunks, stride=S5_CHUNK), :] = y2_ref[:, lo:lo + LANES]
    z = _gelu_tanh(jnp.concatenate([ys_ref[c] for c in range(SSM_WIDTH // LANES)], axis=1))
    gate = jax.nn.sigmoid(jnp.dot(z.astype(BF16), wglu_ref[...], preferred_element_type=F32))
    ssm = (z * gate).astype(BF16)
    acc = jnp.dot(ssm, wo_ref[0:256, :], preferred_element_type=F32)
    acc = acc + jnp.dot(cv_ref[...].astype(BF16), wo_ref[256:512, :], preferred_element_type=F32)
    acc = acc + jnp.dot(at_ref[...], wo_ref[512:1024, :], preferred_element_type=F32)
    o_ref[...] = x_ref[...] + acc


def _out_proj(x, y_ssm, conv_out, attn_out, w_glu, w_out, tm=512):
    t = x.shape[0]
    tm = min(tm, t)
    row = lambda width: pl.BlockSpec((tm, width), lambda i: (i, 0))
    return pl.pallas_call(
        _out_proj_kernel,
        grid=(t // tm,),
        in_specs=[row(D_MODEL), pl.BlockSpec((tm // S5_CHUNK, S5_CHUNK * SSM_WIDTH), lambda i: (i, 0)),
                  row(256), row(512),
                  pl.BlockSpec((256, 256), lambda i: (0, 0)), pl.BlockSpec((D_MODEL, D_MODEL), lambda i: (0, 0))],
        out_specs=row(D_MODEL),
        out_shape=jax.ShapeDtypeStruct((t, D_MODEL), F32),
        scratch_shapes=[pltpu.VMEM((SSM_WIDTH // LANES, tm, LANES), F32)],
        compiler_params=_cparams(("parallel",)),
        name="out_proj",
    )(x, y_ssm, conv_out, attn_out, w_glu, w_out)


def _swiglu_step(h, wg, wu, wd):
    g = jnp.dot(h, wg, preferred_element_type=F32)
    u = jnp.dot(h, wu, preferred_element_type=F32)
    a = (g * jax.nn.sigmoid(g) * u).astype(BF16)
    return jnp.dot(a, wd, preferred_element_type=F32)


_FFN_SUB = 512


def _ffn_kernel(x_ref, g_ref, wg_ref, wu_ref, wd_ref, o_ref, h_ref, a_ref, acc_ref):
    j = pl.program_id(1)
    tf = a_ref.shape[1]

    @pl.when(j == 0)
    def _():
        h_ref[...] = _rms(x_ref[...], g_ref[...]).astype(BF16)

    for lo in range(0, tf, _FFN_SUB):
        cols = slice(lo, min(lo + _FFN_SUB, tf))
        gate = jnp.dot(h_ref[...], wg_ref[:, cols], preferred_element_type=F32)
        up = jnp.dot(h_ref[...], wu_ref[:, cols], preferred_element_type=F32)
        a_ref[:, cols] = (gate * jax.nn.sigmoid(gate) * up).astype(BF16)
    y = jnp.dot(a_ref[...], wd_ref[...], preferred_element_type=F32)

    @pl.when(j == 0)
    def _():
        acc_ref[...] = y

    @pl.when(j > 0)
    def _():
        acc_ref[...] += y

    @pl.when(j == pl.num_programs(1) - 1)
    def _():
        o_ref[...] = x_ref[...] + acc_ref[...]


def _ffn(x, g, wg, wu, wd, tm=1024, tf=1408):
    t = x.shape[0]
    tm = min(tm, t)
    return pl.pallas_call(
        _ffn_kernel,
        grid=(t // tm, D_FF // tf),
        in_specs=[pl.BlockSpec((tm, D_MODEL), lambda i, j: (i, 0)), pl.BlockSpec((1, D_MODEL), lambda i, j: (0, 0)),
                  pl.BlockSpec((D_MODEL, tf), lambda i, j: (0, j)), pl.BlockSpec((D_MODEL, tf), lambda i, j: (0, j)),
                  pl.BlockSpec((tf, D_MODEL), lambda i, j: (j, 0))],
        out_specs=pl.BlockSpec((tm, D_MODEL), lambda i, j: (i, 0)),
        out_shape=jax.ShapeDtypeStruct((t, D_MODEL), F32),
        scratch_shapes=[pltpu.VMEM((tm, D_MODEL), BF16), pltpu.VMEM((tm, tf), BF16), pltpu.VMEM((tm, D_MODEL), F32)],
        compiler_params=_cparams(("parallel", "arbitrary")),
        name="ffn_dense",
    )(x, g.reshape(1, D_MODEL), wg, wu, wd)


_MOE_PREFIX = 256


def _moe_route(x_ref, g_ref, r_ref, o_ref, hb_ref, rank_ref, sel_ref, gate_ref, cnt_ref):
    n_blk = x_ref.shape[0] // _MOE_PREFIX

    for blk in range(n_blk):
        rows = slice(blk * _MOE_PREFIX, (blk + 1) * _MOE_PREFIX)
        x = x_ref[rows, :]
        h32 = _rms(x, g_ref[...])
        hb_ref[rows, :] = h32.astype(BF16)
        o_ref[rows, :] = x
        logits = jnp.dot(h32, r_ref[...], preferred_element_type=F32, precision=lax.Precision.HIGHEST)
        rank_ref[:, rows] = logits.T[:N_EXPERTS, :]
    lt = rank_ref[...]
    row = lax.broadcasted_iota(jnp.int32, lt.shape, 0)
    m1 = jnp.max(lt, axis=0, keepdims=True)
    i1 = jnp.min(jnp.where(lt == m1, row, N_EXPERTS), axis=0, keepdims=True)
    lt2 = jnp.where(row == i1, -jnp.inf, lt)
    m2 = jnp.max(lt2, axis=0, keepdims=True)
    i2 = jnp.min(jnp.where(lt2 == m2, row, N_EXPERTS), axis=0, keepdims=True)
    e2 = jnp.exp(m2 - m1)
    den = 1.0 + e2
    gate_ref[...] = jnp.where(row == i1, 1.0 / den, 0.0) + jnp.where(row == i2, e2 / den, 0.0)
    sel = jnp.where(row == i1, 1.0, 0.0) + jnp.where(row == i2, 1.0, 0.0)
    sel_ref[...] = sel
    before = (lax.broadcasted_iota(jnp.int32, (_MOE_PREFIX, _MOE_PREFIX), 0)
              < lax.broadcasted_iota(jnp.int32, (_MOE_PREFIX, _MOE_PREFIX), 1))
    tri = jnp.where(before, 1.0, 0.0).astype(BF16)
    carry = jnp.zeros((N_EXPERTS, 1), F32)
    for blk in range(lt.shape[1] // _MOE_PREFIX):
        cols = slice(blk * _MOE_PREFIX, (blk + 1) * _MOE_PREFIX)
        rank_ref[:, cols] = jnp.dot(sel[:, cols].astype(BF16), tri, preferred_element_type=F32) + carry
        carry = carry + jnp.sum(sel[:, cols], axis=1, keepdims=True)
    for k in range(N_EXPERTS):
        cnt_ref[k] = carry[k, 0].astype(jnp.int32)


def _moe_kernel(x_ref, g_ref, r_ref, gf_ref, wg_ref, wu_ref, wd_ref, o_ref, hb_ref, rank_ref, sel_ref, gate_ref,
                xe_ref, ye_ref, cnt_ref, *, tb, g_rows, f_rows, s_rows, final_norm):
    e = pl.program_id(1)
    f = pl.program_id(2)
    last_f = pl.num_programs(2) - 1

    @pl.when((e == 0) & (f == 0))
    def _():
        _moe_route(x_ref, g_ref, r_ref, o_ref, hb_ref, rank_ref, sel_ref, gate_ref, cnt_ref)

    cnt = cnt_ref[e]
    rank_e = rank_ref[pl.ds(e, 1), :]
    sel_e = sel_ref[pl.ds(e, 1), :]

    def groups(rows):
        return (cnt + rows - 1) // rows

    def onehot(first_row, rows):
        tgt = (lax.broadcasted_iota(jnp.int32, (rows, tb), 0) + first_row).astype(F32)
        return jnp.where((rank_e == tgt) & (sel_e > 0.0), 1.0, 0.0)

    @pl.when(f == 0)
    def _():
        def gather(s, c):
            off = pl.multiple_of(s * g_rows, g_rows)
            xe_ref[pl.ds(off, g_rows), :] = jnp.dot(onehot(off, g_rows).astype(BF16), hb_ref[...],
                                                    preferred_element_type=F32).astype(BF16)
            return c
        lax.fori_loop(0, groups(g_rows), gather, 0)

    def ffn(first):
        def body(s, c):
            off = pl.multiple_of(s * f_rows, f_rows)
            y = _swiglu_step(xe_ref[pl.ds(off, f_rows), :], wg_ref[...], wu_ref[...], wd_ref[...])
            if first:
                ye_ref[pl.ds(off, f_rows), :] = y
            else:
                ye_ref[pl.ds(off, f_rows), :] += y
            return c
        lax.fori_loop(0, groups(f_rows), body, 0)

    @pl.when(f == 0)
    def _():
        ffn(True)
        done = groups(f_rows) * f_rows
        @pl.when(done < groups(s_rows) * s_rows)
        def _():
            ye_ref[pl.ds(pl.multiple_of(done, f_rows), f_rows), :] = jnp.zeros((f_rows, D_MODEL), F32)

    @pl.when(f > 0)
    def _():
        ffn(False)

    @pl.when(f == last_f)
    def _():
        gate_e = gate_ref[pl.ds(e, 1), :]

        def scatter(s, c):
            off = pl.multiple_of(s * s_rows, s_rows)
            p = onehot(off, s_rows)
            gate_rows = jnp.sum(p * gate_e, axis=1, keepdims=True)
            valid = (lax.broadcasted_iota(jnp.int32, (s_rows, 1), 0) + off) < cnt
            yg = jnp.where(valid, gate_rows * ye_ref[pl.ds(off, s_rows), :], 0.0).astype(BF16)
            o_ref[...] += jnp.dot(p.T.astype(BF16), yg, preferred_element_type=F32)
            return c
        lax.fori_loop(0, groups(s_rows), scatter, 0)

    if final_norm:
        @pl.when((e == pl.num_programs(1) - 1) & (f == last_f))
        def _():
            for lo in range(0, tb, _MOE_PREFIX):
                rows = slice(lo, lo + _MOE_PREFIX)
                o_ref[rows, :] = _rms(o_ref[rows, :], gf_ref[...])


def _moe(x, g, router_pad, wg, wu, wd, final_g=None, tb=2048, fc=1408, g_rows=256, f_rows=128, s_rows=256):
    t = x.shape[0]
    tb = min(tb, t)
    gf = jnp.ones((D_MODEL,), F32) if final_g is None else final_g
    assert s_rows in (f_rows, 2 * f_rows) and g_rows % f_rows == 0 and tb % g_rows == 0 and tb % s_rows == 0
    once = pl.Buffered(1)
    return pl.pallas_call(
        functools.partial(_moe_kernel, tb=tb, g_rows=g_rows, f_rows=f_rows, s_rows=s_rows,
                          final_norm=final_g is not None),
        grid=(t // tb, N_EXPERTS, D_FF // fc),
        in_specs=[pl.BlockSpec((tb, D_MODEL), lambda i, e, j: (i, 0), pipeline_mode=once),
                  pl.BlockSpec((1, D_MODEL), lambda i, e, j: (0, 0)),
                  pl.BlockSpec((D_MODEL, LANES), lambda i, e, j: (0, 0)),
                  pl.BlockSpec((1, D_MODEL), lambda i, e, j: (0, 0)),
                  pl.BlockSpec((None, D_MODEL, fc), lambda i, e, j: (e, 0, j)),
                  pl.BlockSpec((None, D_MODEL, fc), lambda i, e, j: (e, 0, j)),
                  pl.BlockSpec((None, fc, D_MODEL), lambda i, e, j: (e, j, 0))],
        out_specs=pl.BlockSpec((tb, D_MODEL), lambda i, e, j: (i, 0), pipeline_mode=once),
        out_shape=jax.ShapeDtypeStruct((t, D_MODEL), F32),
        scratch_shapes=[pltpu.VMEM((tb, D_MODEL), BF16), pltpu.VMEM((N_EXPERTS, tb), F32),
                        pltpu.VMEM((N_EXPERTS, tb), F32), pltpu.VMEM((N_EXPERTS, tb), F32),
                        pltpu.VMEM((tb, D_MODEL), BF16), pltpu.VMEM((tb, D_MODEL), F32),
                        pltpu.SMEM((N_EXPERTS,), jnp.int32)],
        compiler_params=pltpu.CompilerParams(dimension_semantics=("parallel", "arbitrary", "arbitrary"),
                                             vmem_limit_bytes=MOE_VMEM_LIMIT),
        name="ffn_moe",
    )(x, g.reshape(1, D_MODEL), router_pad, gf.reshape(1, D_MODEL), wg, wu, wd)


def _final_norm_kernel(x_ref, g_ref, o_ref):
    o_ref[...] = _rms(x_ref[...], g_ref[...])


def _final_norm(x, g, tm=1024):
    t = x.shape[0]
    tm = min(tm, t)
    return pl.pallas_call(
        _final_norm_kernel,
        grid=(t // tm,),
        in_specs=[pl.BlockSpec((tm, D_MODEL), lambda i: (i, 0)), pl.BlockSpec((1, D_MODEL), lambda i: (0, 0))],
        out_specs=pl.BlockSpec((tm, D_MODEL), lambda i: (i, 0)),
        out_shape=jax.ShapeDtypeStruct((t, D_MODEL), F32),
        compiler_params=_cparams(("parallel",)),
        name="final_norm",
    )(x, g.reshape(1, D_MODEL))


def kernel(x_prompt, x_sample, cache_k, cache_v, state_ssm_re, state_ssm_im, state_conv, norm_mix_g, w_in, ssm_a_re, ssm_a_im, ssm_log_dt, ssm_b_re, ssm_b_im, ssm_c_re, ssm_c_im, ssm_d, ssm_w_glu, conv_w, conv_ln_g, conv_ln_b, attn_lq1, attn_lk1, attn_lq2, attn_lk2, attn_sub_g, w_out, norm_ffn_g, ffn_w_gate, ffn_w_up, ffn_w_down, moe_router, moe_w_gate, moe_w_up, moe_w_down, final_norm_g):
    depth = w_in.shape[0]
    n_p, l_p, _ = x_prompt.shape
    n_s, l_s, _ = x_sample.shape
    past = cache_k.shape[2]
    keep = CONV_KERNEL - 1

    xp = x_prompt.reshape(n_p * l_p, D_MODEL)
    xs = x_sample.reshape(n_s * l_s, D_MODEL)
    cache_kt = jnp.transpose(cache_k, (0, 1, 3, 4, 5, 2)).reshape(depth, n_s, 512, past)
    cache_vr = cache_v.reshape(depth, n_s, past * N_HEADS, V_DIM)
    s0_p = jnp.zeros((n_p, 1, 2 * STATE_W), F32)
    buf_p = jnp.zeros((n_p, keep, CONV_WIDTH), F32)

    kv_bufs = None
    outs = {name: [] for name in ("srp", "sip", "cp", "ks", "vs", "srs", "sis", "cs")}
    for l in range(depth):
        lam_init = 0.8 - 0.6 * math.exp(-0.3 * l)
        w_in_b = w_in[l].astype(BF16)
        wkvt = w_in_b[:, IN_WIDTH - 2 * ATTN_WIDTH:].T
        ops = _s5_operators(ssm_a_re[l], ssm_a_im[l], ssm_log_dt[l], ssm_b_re[l], ssm_b_im[l], ssm_c_re[l],
                            ssm_c_im[l], ssm_d[l], S5_CHUNK)
        lam_params = jnp.stack([attn_lq1[l], attn_lk1[l], attn_lq2[l], attn_lk2[l]]).astype(F32)
        w_glu_b = ssm_w_glu[l].astype(BF16)
        w_out_b = w_out[l].astype(BF16)
        conv_p = (conv_w[l], conv_ln_g[l], conv_ln_b[l])

        u2, hc, q, kb, vt, *kv_bufs = _in_proj_prompt(xp, norm_mix_g[l], w_in_b, wkvt, kv_bufs, l, depth, n_p)
        y_ssm, sf_p = _s5_group(u2, n_p, l_p, s0_p, ops)
        cv, cb_p = _conv(hc.reshape(n_p, l_p, CONV_WIDTH), buf_p, *conv_p, tl=256)
        at = _attn_prompt(q, kb, vt, lam_params, attn_sub_g[l], n_p, l_p, lam_init)
        xp = _out_proj(xp, y_ssm, cv.reshape(n_p * l_p, CONV_WIDTH), at, w_glu_b, w_out_b)

        s0_s = jnp.concatenate([state_ssm_re[l].reshape(n_s, 1, STATE_W), state_ssm_im[l].reshape(n_s, 1, STATE_W)],
                               axis=-1).astype(F32)
        u2, hc, q, kb, vb, k_s, v_s = _in_proj_sample(xs, norm_mix_g[l], w_in_b)
        y_ssm, sf_s = _s5_group(u2, n_s, l_s, s0_s, ops)
        cv, cb_s = _conv(hc.reshape(n_s, l_s, CONV_WIDTH), state_conv[l].astype(F32), *conv_p, tl=l_s)
        at = _attn_sample(q, kb, vb, cache_kt, cache_vr, l, lam_params, attn_sub_g[l], lam_init)
        xs = _out_proj(xs, y_ssm, cv.reshape(n_s * l_s, CONV_WIDTH), at, w_glu_b, w_out_b)

        j = l // 2
        if l % 2 == 0:
            wg, wu, wd = ffn_w_gate[j].astype(BF16), ffn_w_up[j].astype(BF16), ffn_w_down[j].astype(BF16)
            xp = _ffn(xp, norm_ffn_g[l], wg, wu, wd)
            xs = _ffn(xs, norm_ffn_g[l], wg, wu, wd)
        else:
            router_pad = jnp.pad(moe_router[j].astype(F32), ((0, 0), (0, LANES - N_EXPERTS)))
            wg, wu, wd = moe_w_gate[j].astype(BF16), moe_w_up[j].astype(BF16), moe_w_down[j].astype(BF16)
            final_g = final_norm_g if l == depth - 1 else None
            xp = _moe(xp, norm_ffn_g[l], router_pad, wg, wu, wd, final_g)
            xs = _moe(xs, norm_ffn_g[l], router_pad, wg, wu, wd, final_g)

        outs["srp"].append(sf_p[:, 0, :STATE_W].reshape(n_p, N_SSM_GROUPS, SSM_STATE))
        outs["sip"].append(sf_p[:, 0, STATE_W:].reshape(n_p, N_SSM_GROUPS, SSM_STATE))
        outs["cp"].append(cb_p)
        outs["ks"].append(k_s.reshape(n_s, l_s, 2, N_HEADS, HEAD_DIM))
        outs["vs"].append(v_s.reshape(n_s, l_s, N_HEADS, V_DIM))
        outs["srs"].append(sf_s[:, 0, :STATE_W].reshape(n_s, N_SSM_GROUPS, SSM_STATE))
        outs["sis"].append(sf_s[:, 0, STATE_W:].reshape(n_s, N_SSM_GROUPS, SSM_STATE))
        outs["cs"].append(cb_s)

    if depth % 2 == 0:
        yp, ys = xp, xs
    else:
        yp = _final_norm(xp, final_norm_g)
        ys = _final_norm(xs, final_norm_g)
    st = {name: jnp.stack(vals) for name, vals in outs.items()}
    kt_buf, v_buf = kv_bufs
    k_prompt = jnp.transpose(kt_buf.reshape(depth, n_p, 2, N_HEADS, HEAD_DIM, l_p), (0, 1, 5, 2, 3, 4))
    v_prompt = v_buf.reshape(depth, n_p, l_p, N_HEADS, V_DIM)
    return (yp.reshape(n_p, l_p, D_MODEL), ys.reshape(n_s, l_s, D_MODEL),
            k_prompt, v_prompt, st["srp"], st["sip"], st["cp"],
            st["ks"], st["vs"], st["srs"], st["sis"], st["cs"])
```

```python
import functools
import math

import jax
import jax.numpy as jnp
from jax import lax
from jax.experimental import pallas as pl
from jax.experimental.pallas import tpu as pltpu

F32 = jnp.float32
BF16 = jnp.bfloat16

D_MODEL = 1024
CHUNK = 64
SSM_GROUP = 16
N_SSM_GROUPS = 16
SSM_WIDTH = 256
SSM_STATE = 64
STATE_W = N_SSM_GROUPS * SSM_STATE
CONV_WIDTH = 256
CONV_KERNEL = 31
N_HEADS = 4
HEAD_DIM = 64
V_DIM = 128
VT_ROWS = V_DIM + 16
QK_WIDTH = 256
ATTN_WIDTH = 512
IN_WIDTH = 2304
ATTN_SCALE = 1.0 / math.sqrt(HEAD_DIM)
LOG2E = math.log2(math.e)
NEG_INF = -1e30
D_FF = 2816
N_EXPERTS = 8
EPS = 1e-6

S5_CHUNK = 8
LANES = 128
SUBLANES = 8
VMEM_LIMIT = 48 * 1024 * 1024
MOE_VMEM_LIMIT = 58 * 1024 * 1024


def _cparams(sem):
    return pltpu.CompilerParams(dimension_semantics=sem, vmem_limit_bytes=VMEM_LIMIT)


def _rms(x, g):
    return x * lax.rsqrt(jnp.mean(x * x, axis=-1, keepdims=True) + EPS) * g


def _mixer_inputs(x_ref, g_ref, w_ref, u2_ref, hc_ref, q_ref, us_ref):
    h = _rms(x_ref[...], g_ref[...]).astype(BF16)

    def proj(lo, hi):
        return jnp.dot(h, w_ref[:, lo:hi], preferred_element_type=F32)

    u = proj(0, 256)
    n_chunks = us_ref.shape[1] // S5_CHUNK
    for c in range(SSM_WIDTH // LANES):
        us_ref[c] = u[:, c * LANES:(c + 1) * LANES]
        for j in range(S5_CHUNK):
            lo = j * SSM_WIDTH + c * LANES
            u2_ref[:, lo:lo + LANES] = us_ref[c, pl.ds(j, n_chunks, stride=S5_CHUNK), :]
    c_val = proj(256, 512)
    c_gate = proj(512, 768)
    hc_ref[...] = c_val * jax.nn.sigmoid(c_gate)
    q_ref[...] = (proj(768, 1280) * (ATTN_SCALE * LOG2E)).astype(BF16)
    return h, proj(1280, 1792), proj(1792, 2304)


def _in_proj_prompt_kernel(*refs, aliased):
    x_ref, g_ref, w_ref, wkvt_ref = refs[:4]
    u2_ref, hc_ref, q_ref, kb_ref, vt_ref, kt_ref, v_ref, us_ref = refs[6 if aliased else 4:]
    h, k, v = _mixer_inputs(x_ref, g_ref, w_ref, u2_ref, hc_ref, q_ref, us_ref)
    kb_ref[...] = k.astype(BF16)
    for head in range(N_HEADS):
        v_ref[pl.ds(head, v.shape[0], stride=N_HEADS), :] = v[:, head * V_DIM:(head + 1) * V_DIM]
    nt = (((1,), (1,)), ((), ()))
    kt_ref[...] = lax.dot_general(wkvt_ref[:ATTN_WIDTH, :], h, nt, preferred_element_type=F32)
    vt = lax.dot_general(wkvt_ref[ATTN_WIDTH:, :], h, nt, preferred_element_type=F32).astype(BF16)
    for head in range(N_HEADS):
        vt_ref[head * VT_ROWS:head * VT_ROWS + V_DIM, :] = vt[head * V_DIM:(head + 1) * V_DIM, :]
        vt_ref[head * VT_ROWS + V_DIM:(head + 1) * VT_ROWS, :] = jnp.ones((VT_ROWS - V_DIM, vt.shape[1]), BF16)


def _in_proj_prompt(x, g, w_bf16, wkvt_bf16, kv_bufs, layer, depth, n_seq, tm=512):
    t = x.shape[0]
    seq_len = t // n_seq
    per_seq = seq_len // tm
    row = lambda width: pl.BlockSpec((tm, width), lambda i: (i, 0))
    in_specs = [row(D_MODEL), pl.BlockSpec((1, D_MODEL), lambda i: (0, 0)),
                pl.BlockSpec((D_MODEL, IN_WIDTH), lambda i: (0, 0)),
                pl.BlockSpec((2 * ATTN_WIDTH, D_MODEL), lambda i: (0, 0))]
    args = [x, g.reshape(1, D_MODEL), w_bf16, wkvt_bf16]
    aliases = {}
    if kv_bufs is not None:
        in_specs += [pl.BlockSpec(memory_space=pl.ANY), pl.BlockSpec(memory_space=pl.ANY)]
        args += list(kv_bufs)
        aliases = {4: 5, 5: 6}
    return pl.pallas_call(
        functools.partial(_in_proj_prompt_kernel, aliased=kv_bufs is not None),
        grid=(t // tm,),
        in_specs=in_specs,
        out_specs=[pl.BlockSpec((tm // S5_CHUNK, S5_CHUNK * SSM_WIDTH), lambda i: (i, 0)),
                   row(256), row(512), row(512),
                   pl.BlockSpec((N_HEADS * VT_ROWS, tm), lambda i: (0, i)),
                   pl.BlockSpec((None, None, ATTN_WIDTH, tm), lambda i: (layer, i // per_seq, 0, i % per_seq)),
                   pl.BlockSpec((None, None, tm * N_HEADS, V_DIM), lambda i: (layer, i // per_seq, i % per_seq, 0))],
        out_shape=[jax.ShapeDtypeStruct((t // S5_CHUNK, S5_CHUNK * SSM_WIDTH), F32),
                   jax.ShapeDtypeStruct((t, 256), F32),
                   jax.ShapeDtypeStruct((t, 512), BF16), jax.ShapeDtypeStruct((t, 512), BF16),
                   jax.ShapeDtypeStruct((N_HEADS * VT_ROWS, t), BF16),
                   jax.ShapeDtypeStruct((depth, n_seq, ATTN_WIDTH, seq_len), F32),
                   jax.ShapeDtypeStruct((depth, n_seq, seq_len * N_HEADS, V_DIM), F32)],
        scratch_shapes=[pltpu.VMEM((SSM_WIDTH // LANES, tm, LANES), F32)],
        input_output_aliases=aliases,
        compiler_params=_cparams(("parallel",)),
        name="in_proj_prompt",
    )(*args)


def _in_proj_sample_kernel(x_ref, g_ref, w_ref, u2_ref, hc_ref, q_ref, kb_ref, vb_ref, k_ref, v_ref, us_ref):
    _, k, v = _mixer_inputs(x_ref, g_ref, w_ref, u2_ref, hc_ref, q_ref, us_ref)
    k_ref[...] = k
    kb_ref[...] = k.astype(BF16)
    v_ref[...] = v
    vb_ref[...] = v.astype(BF16)


def _in_proj_sample(x, g, w_bf16, tm=512):
    t = x.shape[0]
    tm = min(tm, t)
    row = lambda width: pl.BlockSpec((tm, width), lambda i: (i, 0))
    return pl.pallas_call(
        _in_proj_sample_kernel,
        grid=(t // tm,),
        in_specs=[row(D_MODEL), pl.BlockSpec((1, D_MODEL), lambda i: (0, 0)),
                  pl.BlockSpec((D_MODEL, IN_WIDTH), lambda i: (0, 0))],
        out_specs=[pl.BlockSpec((tm // S5_CHUNK, S5_CHUNK * SSM_WIDTH), lambda i: (i, 0)),
                   row(256), row(512), row(512), row(512), row(512), row(512)],
        out_shape=[jax.ShapeDtypeStruct((t // S5_CHUNK, S5_CHUNK * SSM_WIDTH), F32),
                   jax.ShapeDtypeStruct((t, 256), F32),
                   jax.ShapeDtypeStruct((t, 512), BF16), jax.ShapeDtypeStruct((t, 512), BF16),
                   jax.ShapeDtypeStruct((t, 512), BF16), jax.ShapeDtypeStruct((t, 512), F32),
                   jax.ShapeDtypeStruct((t, 512), F32)],
        scratch_shapes=[pltpu.VMEM((SSM_WIDTH // LANES, tm, LANES), F32)],
        compiler_params=_cparams(("parallel",)),
        name="in_proj_sample",
    )(x, g.reshape(1, D_MODEL), w_bf16)


def _mm_kernel(x_ref, w_ref, o_ref):
    o_ref[...] = jnp.dot(x_ref[...].astype(BF16), w_ref[...], preferred_element_type=F32)


def _mm(x, w_bf16, layer, tm, tn, name):
    m, k = x.shape
    n = w_bf16.shape[2]
    return pl.pallas_call(
        _mm_kernel,
        grid=(n // tn, m // tm),
        in_specs=[pl.BlockSpec((tm, k), lambda j, i: (i, 0)),
                  pl.BlockSpec((None, k, tn), lambda j, i: (layer, 0, j))],
        out_specs=pl.BlockSpec((tm, tn), lambda j, i: (i, j)),
        out_shape=jax.ShapeDtypeStruct((m, n), F32),
        compiler_params=_cparams(("parallel", "parallel")),
        name=name,
    )(x, w_bf16)


def _s5_scan_kernel(d_ref, s0_ref, a_ref, ss_ref, sf_ref, st_ref, *, rows):
    t = pl.program_id(1)

    @pl.when(t == 0)
    def _():
        st_ref[...] = s0_ref[...]

    a_re = a_ref[:, :STATE_W]
    a_im = a_ref[:, STATE_W:]

    def step(c, carry):
        s_re, s_im = carry
        ss_ref[pl.ds(c, 1), :STATE_W] = s_re
        ss_ref[pl.ds(c, 1), STATE_W:] = s_im
        d = d_ref[pl.ds(c, 1), :]
        n_re = a_re * s_re - a_im * s_im + d[:, :STATE_W]
        n_im = a_re * s_im + a_im * s_re + d[:, STATE_W:]
        return n_re, n_im

    s_re, s_im = lax.fori_loop(0, rows, step, (st_ref[:, :STATE_W], st_ref[:, STATE_W:]))
    st_ref[:, :STATE_W] = s_re
    st_ref[:, STATE_W:] = s_im
    sf_ref[...] = st_ref[...]


def _s5_scan(d, n_seq, n_chunks, s0, a_c, layer):
    rows = min(n_chunks, 256)
    nt = n_chunks // rows
    w = 2 * STATE_W
    return pl.pallas_call(
        functools.partial(_s5_scan_kernel, rows=rows),
        grid=(n_seq, nt),
        in_specs=[pl.BlockSpec((rows, w), lambda s, t: (s * nt + t, 0)),
                  pl.BlockSpec((None, 1, w), lambda s, t: (s, 0, 0)),
                  pl.BlockSpec((None, 1, w), lambda s, t: (layer, 0, 0))],
        out_specs=[pl.BlockSpec((rows, w), lambda s, t: (s * nt + t, 0)),
                   pl.BlockSpec((None, 1, w), lambda s, t: (s, 0, 0))],
        out_shape=[jax.ShapeDtypeStruct((n_seq * n_chunks, w), F32), jax.ShapeDtypeStruct((n_seq, 1, w), F32)],
        scratch_shapes=[pltpu.VMEM((1, w), F32)],
        compiler_params=_cparams(("parallel", "arbitrary")),
        name="s5_scan",
    )(d, s0, a_c)


def _s5_out_kernel(u_ref, ut_ref, ss_ref, wi_ref, ws_ref, dsk_ref, y_ref):
    y_ref[...] = (jnp.dot(u_ref[...].astype(BF16), wi_ref[...], preferred_element_type=F32)
                  + jnp.dot(ss_ref[...].astype(BF16), ws_ref[...], preferred_element_type=F32)
                  + dsk_ref[...] * ut_ref[...])


def _s5_out(u2, s_start, w_intra, w_inter, dsk, layer, tm=256, tn=512):
    m, n = u2.shape
    k = s_start.shape[1]
    return pl.pallas_call(
        _s5_out_kernel,
        grid=(n // tn, m // tm),
        in_specs=[pl.BlockSpec((tm, n), lambda j, i: (i, 0)), pl.BlockSpec((tm, tn), lambda j, i: (i, j)),
                  pl.BlockSpec((tm, k), lambda j, i: (i, 0)),
                  pl.BlockSpec((None, n, tn), lambda j, i: (layer, 0, j)),
                  pl.BlockSpec((None, k, tn), lambda j, i: (layer, 0, j)),
                  pl.BlockSpec((None, 1, tn), lambda j, i: (layer, 0, j))],
        out_specs=pl.BlockSpec((tm, tn), lambda j, i: (i, j)),
        out_shape=jax.ShapeDtypeStruct((m, n), F32),
        compiler_params=_cparams(("parallel", "parallel")),
        name="s5_out",
    )(u2, u2, s_start, w_intra, w_inter, dsk)


def _s5_operators(a_re, a_im, log_dt, b_re, b_im, c_re, c_im, d_skip, n_c):
    f32 = lambda x: x.astype(F32)
    a_re, a_im, b_re, b_im, c_re, c_im = map(f32, (a_re, a_im, b_re, b_im, c_re, c_im))
    n_l = a_re.shape[0]
    g_n = N_SSM_GROUPS
    dt = jnp.exp(f32(log_dt))[:, None, :, None]
    ks = jnp.arange(n_c + 1, dtype=F32)[None, :, None, None]
    mag = jnp.exp(a_re[:, None] * dt * ks)
    ang = a_im[:, None] * dt * ks
    pw_re, pw_im = mag * jnp.cos(ang), mag * jnp.sin(ang)
    den = a_re * a_re + a_im * a_im
    q_re = ((pw_re[:, 1] - 1.0) * a_re + pw_im[:, 1] * a_im) / den
    q_im = (pw_im[:, 1] * a_re - (pw_re[:, 1] - 1.0) * a_im) / den
    bb_re = q_re[..., None] * b_re - q_im[..., None] * b_im
    bb_im = q_re[..., None] * b_im + q_im[..., None] * b_re
    eye_g = jnp.eye(g_n, dtype=F32)

    def block_diag(x, rows, cols):
        return (x[:, :, :, None, :] * eye_g[None, :, None, :, None]).reshape(n_l, rows, cols)

    bt_re = block_diag(jnp.transpose(bb_re, (0, 1, 3, 2)), SSM_WIDTH, STATE_W)[:, None]
    bt_im = block_diag(jnp.transpose(bb_im, (0, 1, 3, 2)), SSM_WIDTH, STATE_W)[:, None]
    pj_re = pw_re[:, :n_c][:, ::-1].reshape(n_l, n_c, 1, STATE_W)
    pj_im = pw_im[:, :n_c][:, ::-1].reshape(n_l, n_c, 1, STATE_W)
    w_state = jnp.concatenate([(pj_re * bt_re - pj_im * bt_im).reshape(n_l, n_c * SSM_WIDTH, STATE_W),
                               (pj_re * bt_im + pj_im * bt_re).reshape(n_l, n_c * SSM_WIDTH, STATE_W)], axis=2)

    m_re = c_re[:, None] * pw_re[:, :n_c, :, None, :] - c_im[:, None] * pw_im[:, :n_c, :, None, :]
    m_im = c_re[:, None] * pw_im[:, :n_c, :, None, :] + c_im[:, None] * pw_re[:, :n_c, :, None, :]
    hi = lax.Precision.HIGHEST
    ker = (jnp.einsum('lkghp,lgpx->lkghx', m_re, bb_re, precision=hi)
           - jnp.einsum('lkghp,lgpx->lkghx', m_im, bb_im, precision=hi))
    blk = jnp.transpose(ker, (0, 1, 2, 4, 3))
    blk = (blk[:, :, :, :, None, :] * eye_g[None, None, :, None, :, None]).reshape(n_l, n_c, 256, 256).astype(BF16)
    zero = jnp.zeros((n_l, 256, 256), BF16)
    w_intra = jnp.concatenate(
        [jnp.concatenate([blk[:, t - j] if t >= j else zero for t in range(n_c)], axis=2) for j in range(n_c)],
        axis=1)

    ct_re = block_diag(jnp.transpose(c_re, (0, 1, 3, 2)), STATE_W, SSM_WIDTH)
    ct_im = block_diag(jnp.transpose(c_im, (0, 1, 3, 2)), STATE_W, SSM_WIDTH)
    pt_re = pw_re[:, 1:].reshape(n_l, n_c, STATE_W, 1)
    pt_im = pw_im[:, 1:].reshape(n_l, n_c, STATE_W, 1)
    top = jnp.concatenate([pt_re[:, t] * ct_re - pt_im[:, t] * ct_im for t in range(n_c)], axis=2)
    bot = jnp.concatenate([-(pt_re[:, t] * ct_im + pt_im[:, t] * ct_re) for t in range(n_c)], axis=2)
    w_inter = jnp.concatenate([top, bot], axis=1)

    a_c = jnp.concatenate([pw_re[:, n_c].reshape(n_l, 1, STATE_W), pw_im[:, n_c].reshape(n_l, 1, STATE_W)], axis=2)
    dsk = jnp.tile(f32(d_skip).reshape(n_l, 1, SSM_WIDTH), (1, 1, n_c))
    return w_state.astype(BF16), w_intra, w_inter.astype(BF16), a_c, dsk


def _s5_group(u2, n_seq, seq_len, s0, ops, layer):
    w_state, w_intra, w_inter, a_c, dsk = ops
    n_chunks = seq_len // S5_CHUNK
    d_state = _mm(u2, w_state, layer, 256, 1024, "s5_state")
    s_start, s_final = _s5_scan(d_state, n_seq, n_chunks, s0, a_c, layer)
    return _s5_out(u2, s_start, w_intra, w_inter, dsk, layer), s_final


_CONV_PAD = 32
_CONV_RB = 64


def _conv_kernel(h_ref, buf_ref, w_ref, g_ref, b_ref, y_ref, nb_ref, xp_ref, xs_ref, *, tl):
    t = pl.program_id(1)
    keep = CONV_KERNEL - 1
    lo = _CONV_PAD - keep

    @pl.when(t == 0)
    def _():
        xp_ref[lo:_CONV_PAD, :] = buf_ref[...]

    @pl.when(t > 0)
    def _():
        xp_ref[lo:_CONV_PAD, :] = xp_ref[tl + lo:tl + _CONV_PAD, :]

    xp_ref[_CONV_PAD:_CONV_PAD + tl, :] = h_ref[...]
    for phase in range(SUBLANES):
        span = tl + (keep - phase) // SUBLANES * SUBLANES
        xs_ref[phase, :span, :] = xp_ref[lo + phase:lo + phase + span, :]
    for r in range(tl // _CONV_RB):
        base = r * _CONV_RB
        acc = jnp.zeros((_CONV_RB, CONV_WIDTH), F32)
        for k in range(CONV_KERNEL):
            start = base + k - k % SUBLANES
            acc = acc + w_ref[k:k + 1, :] * xs_ref[k % SUBLANES, start:start + _CONV_RB, :]
        xc = acc - jnp.mean(acc, axis=-1, keepdims=True)
        var = jnp.mean(xc * xc, axis=-1, keepdims=True)
        y = xc * lax.rsqrt(var + EPS) * g_ref[...] + b_ref[...]
        y_ref[base:base + _CONV_RB, :] = y * jax.nn.sigmoid(y)
    nb_ref[...] = xp_ref[tl + lo:tl + _CONV_PAD, :]


def _conv(h, buf, w, ln_g, ln_b, tl):
    s, l, c = h.shape
    keep = CONV_KERNEL - 1
    return pl.pallas_call(
        functools.partial(_conv_kernel, tl=tl),
        grid=(s, l // tl),
        in_specs=[pl.BlockSpec((None, tl, c), lambda i, t: (i, t, 0)),
                  pl.BlockSpec((None, keep, c), lambda i, t: (i, 0, 0)),
                  pl.BlockSpec((CONV_KERNEL, c), lambda i, t: (0, 0)),
                  pl.BlockSpec((1, c), lambda i, t: (0, 0)), pl.BlockSpec((1, c), lambda i, t: (0, 0))],
        out_specs=[pl.BlockSpec((None, tl, c), lambda i, t: (i, t, 0)),
                   pl.BlockSpec((None, keep, c), lambda i, t: (i, 0, 0))],
        out_shape=[jax.ShapeDtypeStruct((s, l, c), F32), jax.ShapeDtypeStruct((s, keep, c), F32)],
        scratch_shapes=[pltpu.VMEM((tl + _CONV_PAD, c), F32),
                        pltpu.VMEM((SUBLANES, tl + (keep // SUBLANES) * SUBLANES, c), F32)],
        compiler_params=_cparams(("parallel", "arbitrary")),
        name="conv_module",
    )(h, buf, w, ln_g.reshape(1, c), ln_b.reshape(1, c))


def _stack_q(q_ref, qs_ref):
    tq = q_ref.shape[0]
    lane = lax.broadcasted_iota(jnp.int32, (tq, LANES), 1)
    for b in range(4):
        qb = q_ref[:, b * LANES:(b + 1) * LANES]
        qs_ref[b, :tq, :] = jnp.where(lane < HEAD_DIM, qb, jnp.zeros_like(qb))
        qs_ref[b, tq:, :] = jnp.where(lane >= HEAD_DIM, qb, jnp.zeros_like(qb))


def _attn_update(score, value, m_ref, l_ref, acc_ref, tq):
    def softmax(b, s):
        m_old = m_ref[b]
        m_new = jnp.maximum(m_old, jnp.max(s, axis=-1, keepdims=True))
        alpha = jnp.exp2(m_old - m_new)
        p = jnp.exp2(s - m_new)
        l_ref[b] = alpha * l_ref[b] + jnp.sum(p, axis=-1, keepdims=True)
        m_ref[b] = m_new
        return alpha, p.astype(BF16)

    def values(b, alpha, pb):
        h0 = 2 * (b % 2)
        pv0 = jnp.dot(pb[:tq], value(h0), preferred_element_type=F32)
        pv1 = jnp.dot(pb[tq:], value(h0 + 1), preferred_element_type=F32)
        acc_ref[b, :tq, :] = alpha[:tq] * acc_ref[b, :tq, :] + pv0
        acc_ref[b, tq:, :] = alpha[tq:] * acc_ref[b, tq:, :] + pv1

    s0 = score(0)
    s1 = score(1)
    s2 = score(2)
    a0, p0 = softmax(0, s0)
    s3 = score(3)
    a1, p1 = softmax(1, s1)
    values(0, a0, p0)
    a2, p2 = softmax(2, s2)
    values(1, a1, p1)
    a3, p3 = softmax(3, s3)
    values(2, a2, p2)
    values(3, a3, p3)


def _attn_finish(lp_ref, sg_ref, l_ref, acc_ref, o_ref, tq, lam_init):
    lam = _lambda(lp_ref, lam_init)
    for h in range(N_HEADS):
        b1, half = h // 2, h % 2
        rows = slice(half * tq, (half + 1) * tq)
        o1 = acc_ref[b1, rows, :] / l_ref[b1, rows, :]
        o2 = acc_ref[b1 + 2, rows, :] / l_ref[b1 + 2, rows, :]
        o = o1 - lam * o2
        o_ref[:, h * V_DIM:(h + 1) * V_DIM] = (_rms(o, sg_ref[...]) * (1.0 - lam_init)).astype(o_ref.dtype)


def _attn_init(m_ref, l_ref, acc_ref):
    m_ref[...] = jnp.full(m_ref.shape, NEG_INF, F32)
    l_ref[...] = jnp.zeros(l_ref.shape, F32)
    acc_ref[...] = jnp.zeros(acc_ref.shape, F32)


def _attn_prompt_update(qs_ref, k_ref, vt_ref, m_ref, acc_ref, tq, mask):
    def scores(b):
        s = lax.dot_general(k_ref[:, b * LANES:(b + 1) * LANES], qs_ref[b], (((1,), (1,)), ((), ())),
                            preferred_element_type=F32)
        return s if mask is None else jnp.where(mask, s, NEG_INF)

    def softmax(b, s):
        sb = s.astype(BF16)
        m_old = m_ref[b]
        m_new = jnp.maximum(m_old, jnp.max(sb, axis=0, keepdims=True).astype(F32))
        alpha = jnp.exp2(m_old - m_new)
        m_ref[b] = m_new
        return alpha, jnp.exp2(sb - m_new.astype(BF16))

    def values(b, alpha, pb):
        for r in range(2):
            h = 2 * (b % 2) + r
            cols = slice(r * tq, (r + 1) * tq)
            pv = jnp.dot(vt_ref[h * VT_ROWS:(h + 1) * VT_ROWS, :], pb[:, cols], preferred_element_type=F32)
            acc_ref[2 * b + r] = alpha[:, cols] * acc_ref[2 * b + r] + pv

    s0 = scores(0)
    s1 = scores(1)
    s2 = scores(2)
    a0, p0 = softmax(0, s0)
    s3 = scores(3)
    a1, p1 = softmax(1, s1)
    values(0, a0, p0)
    a2, p2 = softmax(2, s2)
    values(1, a1, p1)
    a3, p3 = softmax(3, s3)
    values(2, a2, p2)
    values(3, a3, p3)


def _lambda(lp_ref, lam_init):
    lp = lp_ref[...]
    return (jnp.exp(jnp.sum(lp[0:1] * lp[1:2], axis=-1, keepdims=True))
            - jnp.exp(jnp.sum(lp[2:3] * lp[3:4], axis=-1, keepdims=True)) + lam_init)


def _attn_prompt_finish(lp_ref, sgt_ref, acc_ref, o_ref, tq, lam_init):
    lam = _lambda(lp_ref, lam_init)
    for h in range(N_HEADS):
        i1 = 2 * (h // 2) + h % 2
        o1 = acc_ref[i1, :V_DIM, :] / acc_ref[i1, V_DIM:V_DIM + 1, :]
        o2 = acc_ref[i1 + 4, :V_DIM, :] / acc_ref[i1 + 4, V_DIM:V_DIM + 1, :]
        o = o1 - lam * o2
        on = o * lax.rsqrt(jnp.mean(o * o, axis=0, keepdims=True) + EPS) * sgt_ref[...] * (1.0 - lam_init)
        o_ref[:, h * V_DIM:(h + 1) * V_DIM] = on.T.astype(o_ref.dtype)


def _attn_prompt_kernel(it_ref, jt_ref, lp_ref, sgt_ref, q_ref, k_ref, vt_ref, o_ref, qs_ref, m_ref, acc_ref,
                        *, tq, lam_init):
    step = pl.program_id(1)
    i = it_ref[step]
    j = jt_ref[step]

    @pl.when(j == 0)
    def _():
        m_ref[...] = jnp.full(m_ref.shape, NEG_INF, F32)
        acc_ref[...] = jnp.zeros(acc_ref.shape, F32)
        _stack_q(q_ref, qs_ref)

    @pl.when(j < i)
    def _():
        _attn_prompt_update(qs_ref, k_ref, vt_ref, m_ref, acc_ref, tq, None)

    @pl.when(j == i)
    def _():
        tk = k_ref.shape[0]
        key_chunk = lax.broadcasted_iota(jnp.int32, (tk, 2 * tq), 0) // CHUNK
        qry_chunk = (lax.broadcasted_iota(jnp.int32, (tk, 2 * tq), 1) % tq) // CHUNK
        _attn_prompt_update(qs_ref, k_ref, vt_ref, m_ref, acc_ref, tq, qry_chunk >= key_chunk)
        _attn_prompt_finish(lp_ref, sgt_ref, acc_ref, o_ref, tq, lam_init)


def _attn_prompt(q, kb, vt, lam_params, sub_g, n_seq, seq_len, lam_init, tq=512):
    nq = seq_len // tq
    pairs = [(i, j) for i in range(nq) for j in range(i + 1)]
    i_tab = jnp.asarray([p[0] for p in pairs], jnp.int32)
    j_tab = jnp.asarray([p[1] for p in pairs], jnp.int32)
    grid_spec = pltpu.PrefetchScalarGridSpec(
        num_scalar_prefetch=2,
        grid=(n_seq, len(pairs)),
        in_specs=[pl.BlockSpec((4, HEAD_DIM), lambda b, s, it, jt: (0, 0)),
                  pl.BlockSpec((V_DIM, 1), lambda b, s, it, jt: (0, 0)),
                  pl.BlockSpec((tq, 512), lambda b, s, it, jt: (b * nq + it[s], 0)),
                  pl.BlockSpec((tq, 512), lambda b, s, it, jt: (b * nq + jt[s], 0)),
                  pl.BlockSpec((N_HEADS * VT_ROWS, tq), lambda b, s, it, jt: (0, b * nq + jt[s]))],
        out_specs=pl.BlockSpec((tq, 512), lambda b, s, it, jt: (b * nq + it[s], 0)),
        scratch_shapes=[pltpu.VMEM((4, 2 * tq, LANES), BF16), pltpu.VMEM((4, 1, 2 * tq), F32),
                        pltpu.VMEM((8, VT_ROWS, tq), F32)],
    )
    return pl.pallas_call(
        functools.partial(_attn_prompt_kernel, tq=tq, lam_init=lam_init),
        grid_spec=grid_spec,
        out_shape=jax.ShapeDtypeStruct((n_seq * seq_len, 512), BF16),
        compiler_params=_cparams(("parallel", "arbitrary")),
        name="attn_prompt",
    )(i_tab, j_tab, lam_params, sub_g.reshape(V_DIM, 1), q, kb, vt)


def _attn_sample_kernel(lp_ref, sg_ref, q_ref, ckt_ref, cv_ref, kn_ref, vn_ref, o_ref, qs_ref, m_ref, l_ref, acc_ref,
                        *, tq, tk, lam_init):
    j = pl.program_id(1)

    @pl.when(j == 0)
    def _():
        _attn_init(m_ref, l_ref, acc_ref)
        _stack_q(q_ref, qs_ref)

    def cache_score(b):
        return jnp.dot(qs_ref[b], ckt_ref[b * LANES:(b + 1) * LANES, :].astype(BF16), preferred_element_type=F32)

    def cache_value(h):
        return cv_ref[pl.ds(h, tk, stride=N_HEADS), :].astype(BF16)

    _attn_update(cache_score, cache_value, m_ref, l_ref, acc_ref, tq)

    @pl.when(j == pl.num_programs(1) - 1)
    def _():
        def new_score(b):
            return lax.dot_general(qs_ref[b], kn_ref[:, b * LANES:(b + 1) * LANES], (((1,), (1,)), ((), ())),
                                   preferred_element_type=F32)

        def new_value(h):
            return vn_ref[:, h * V_DIM:(h + 1) * V_DIM]

        _attn_update(new_score, new_value, m_ref, l_ref, acc_ref, tq)
        _attn_finish(lp_ref, sg_ref, l_ref, acc_ref, o_ref, tq, lam_init)


def _attn_sample(q, kb, vb, cache_kt, cache_v, layer, lam_params, sub_g, lam_init, tk=2048):
    _, n_seq, _, past = cache_kt.shape
    tk = min(tk, past)
    assert past % tk == 0
    tq = q.shape[0] // n_seq
    new = pl.BlockSpec((tq, 512), lambda b, j: (b, 0))
    return pl.pallas_call(
        functools.partial(_attn_sample_kernel, tq=tq, tk=tk, lam_init=lam_init),
        grid=(n_seq, past // tk),
        in_specs=[pl.BlockSpec((4, HEAD_DIM), lambda b, j: (0, 0)), pl.BlockSpec((1, V_DIM), lambda b, j: (0, 0)),
                  new,
                  pl.BlockSpec((None, None, 512, tk), lambda b, j: (layer, b, 0, j)),
                  pl.BlockSpec((None, None, tk * N_HEADS, V_DIM), lambda b, j: (layer, b, j, 0)),
                  new, new],
        out_specs=new,
        out_shape=jax.ShapeDtypeStruct((n_seq * tq, 512), BF16),
        scratch_shapes=[pltpu.VMEM((4, 2 * tq, LANES), BF16), pltpu.VMEM((4, 2 * tq, 1), F32),
                        pltpu.VMEM((4, 2 * tq, 1), F32), pltpu.VMEM((4, 2 * tq, V_DIM), F32)],
        compiler_params=_cparams(("parallel", "arbitrary")),
        name="attn_sample",
    )(lam_params, sub_g.reshape(1, V_DIM), q, cache_kt, cache_v, kb, vb)


def _gelu_tanh(x):
    return 0.5 * x * (1.0 + jnp.tanh(math.sqrt(2.0 / math.pi) * (x + 0.044715 * (x * x * x))))


def _out_proj_kernel(x_ref, y2_ref, cv_ref, at_ref, wglu_ref, wo_ref, o_ref, ys_ref):
    n_chunks = y2_ref.shape[0]
    for c in range(SSM_WIDTH // LANES):
        for j in range(S5_CHUNK):
            lo = j * SSM_WIDTH + c * LANES
            ys_ref[c, pl.ds(j, n_chunks, stride=S5_CHUNK), :] = y2_ref[:, lo:lo + LANES]
    z = _gelu_tanh(jnp.concatenate([ys_ref[c] for c in range(SSM_WIDTH // LANES)], axis=1))
    gate = jax.nn.sigmoid(jnp.dot(z.astype(BF16), wglu_ref[...], preferred_element_type=F32))
    ssm = (z * gate).astype(BF16)
    acc = jnp.dot(ssm, wo_ref[0:256, :], preferred_element_type=F32)
    acc = acc + jnp.dot(cv_ref[...].astype(BF16), wo_ref[256:512, :], preferred_element_type=F32)
    acc = acc + jnp.dot(at_ref[...], wo_ref[512:1024, :], preferred_element_type=F32)
    o_ref[...] = x_ref[...] + acc


def _out_proj(x, y_ssm, conv_out, attn_out, w_glu, w_out, tm=512):
    t = x.shape[0]
    tm = min(tm, t)
    row = lambda width: pl.BlockSpec((tm, width), lambda i: (i, 0))
    return pl.pallas_call(
        _out_proj_kernel,
        grid=(t // tm,),
        in_specs=[row(D_MODEL), pl.BlockSpec((tm // S5_CHUNK, S5_CHUNK * SSM_WIDTH), lambda i: (i, 0)),
                  row(256), row(512),
                  pl.BlockSpec((256, 256), lambda i: (0, 0)), pl.BlockSpec((D_MODEL, D_MODEL), lambda i: (0, 0))],
        out_specs=row(D_MODEL),
        out_shape=jax.ShapeDtypeStruct((t, D_MODEL), F32),
        scratch_shapes=[pltpu.VMEM((SSM_WIDTH // LANES, tm, LANES), F32)],
        compiler_params=_cparams(("parallel",)),
        name="out_proj",
    )(x, y_ssm, conv_out, attn_out, w_glu, w_out)


def _swiglu_step(h, wg, wu, wd):
    g = jnp.dot(h, wg, preferred_element_type=F32)
    u = jnp.dot(h, wu, preferred_element_type=F32)
    a = (g * jax.nn.sigmoid(g) * u).astype(BF16)
    return jnp.dot(a, wd, preferred_element_type=F32)


_FFN_SUB = 512


def _ffn_kernel(x_ref, g_ref, wg_ref, wu_ref, wd_ref, o_ref, h_ref, a_ref, acc_ref):
    j = pl.program_id(1)
    tf = a_ref.shape[1]

    @pl.when(j == 0)
    def _():
        h_ref[...] = _rms(x_ref[...], g_ref[...]).astype(BF16)

    for lo in range(0, tf, _FFN_SUB):
        cols = slice(lo, min(lo + _FFN_SUB, tf))
        gate = jnp.dot(h_ref[...], wg_ref[:, cols], preferred_element_type=F32)
        up = jnp.dot(h_ref[...], wu_ref[:, cols], preferred_element_type=F32)
        a_ref[:, cols] = (gate * jax.nn.sigmoid(gate) * up).astype(BF16)
    y = jnp.dot(a_ref[...], wd_ref[...], preferred_element_type=F32)

    @pl.when(j == 0)
    def _():
        acc_ref[...] = y

    @pl.when(j > 0)
    def _():
        acc_ref[...] += y

    @pl.when(j == pl.num_programs(1) - 1)
    def _():
        o_ref[...] = x_ref[...] + acc_ref[...]


def _ffn(x, g, wg, wu, wd, tm=1024, tf=1408):
    t = x.shape[0]
    tm = min(tm, t)
    return pl.pallas_call(
        _ffn_kernel,
        grid=(t // tm, D_FF // tf),
        in_specs=[pl.BlockSpec((tm, D_MODEL), lambda i, j: (i, 0)), pl.BlockSpec((1, D_MODEL), lambda i, j: (0, 0)),
                  pl.BlockSpec((D_MODEL, tf), lambda i, j: (0, j)), pl.BlockSpec((D_MODEL, tf), lambda i, j: (0, j)),
                  pl.BlockSpec((tf, D_MODEL), lambda i, j: (j, 0))],
        out_specs=pl.BlockSpec((tm, D_MODEL), lambda i, j: (i, 0)),
        out_shape=jax.ShapeDtypeStruct((t, D_MODEL), F32),
        scratch_shapes=[pltpu.VMEM((tm, D_MODEL), BF16), pltpu.VMEM((tm, tf), BF16), pltpu.VMEM((tm, D_MODEL), F32)],
        compiler_params=_cparams(("parallel", "arbitrary")),
        name="ffn_dense",
    )(x, g.reshape(1, D_MODEL), wg, wu, wd)


_MOE_PREFIX = 256


def _moe_route(x_ref, g_ref, r_ref, o_ref, hb_ref, rank_ref, sel_ref, gate_ref, cnt_ref):
    n_blk = x_ref.shape[0] // _MOE_PREFIX

    for blk in range(n_blk):
        rows = slice(blk * _MOE_PREFIX, (blk + 1) * _MOE_PREFIX)
        x = x_ref[rows, :]
        h32 = _rms(x, g_ref[...])
        hb_ref[rows, :] = h32.astype(BF16)
        o_ref[rows, :] = x
        logits = jnp.dot(h32, r_ref[...], preferred_element_type=F32, precision=lax.Precision.HIGHEST)
        rank_ref[:, rows] = logits.T[:N_EXPERTS, :]
    lt = rank_ref[...]
    row = lax.broadcasted_iota(jnp.int32, lt.shape, 0)
    m1 = jnp.max(lt, axis=0, keepdims=True)
    i1 = jnp.min(jnp.where(lt == m1, row, N_EXPERTS), axis=0, keepdims=True)
    lt2 = jnp.where(row == i1, -jnp.inf, lt)
    m2 = jnp.max(lt2, axis=0, keepdims=True)
    i2 = jnp.min(jnp.where(lt2 == m2, row, N_EXPERTS), axis=0, keepdims=True)
    e2 = jnp.exp(m2 - m1)
    den = 1.0 + e2
    gate_ref[...] = jnp.where(row == i1, 1.0 / den, 0.0) + jnp.where(row == i2, e2 / den, 0.0)
    sel = jnp.where(row == i1, 1.0, 0.0) + jnp.where(row == i2, 1.0, 0.0)
    sel_ref[...] = sel
    before = (lax.broadcasted_iota(jnp.int32, (_MOE_PREFIX, _MOE_PREFIX), 0)
              < lax.broadcasted_iota(jnp.int32, (_MOE_PREFIX, _MOE_PREFIX), 1))
    tri = jnp.where(before, 1.0, 0.0).astype(BF16)
    carry = jnp.zeros((N_EXPERTS, 1), F32)
    for blk in range(lt.shape[1] // _MOE_PREFIX):
        cols = slice(blk * _MOE_PREFIX, (blk + 1) * _MOE_PREFIX)
        rank_ref[:, cols] = jnp.dot(sel[:, cols].astype(BF16), tri, preferred_element_type=F32) + carry
        carry = carry + jnp.sum(sel[:, cols], axis=1, keepdims=True)
    for k in range(N_EXPERTS):
        cnt_ref[k] = carry[k, 0].astype(jnp.int32)


def _moe_kernel(x_ref, g_ref, r_ref, gf_ref, wg_ref, wu_ref, wd_ref, o_ref, hb_ref, rank_ref, sel_ref, gate_ref,
                xe_ref, ye_ref, cnt_ref, *, tb, g_rows, f_rows, s_rows, final_norm):
    e = pl.program_id(1)
    f = pl.program_id(2)
    last_f = pl.num_programs(2) - 1

    @pl.when((e == 0) & (f == 0))
    def _():
        _moe_route(x_ref, g_ref, r_ref, o_ref, hb_ref, rank_ref, sel_ref, gate_ref, cnt_ref)

    cnt = cnt_ref[e]
    rank_e = rank_ref[pl.ds(e, 1), :]
    sel_e = sel_ref[pl.ds(e, 1), :]

    def groups(rows):
        return (cnt + rows - 1) // rows

    def onehot(first_row, rows):
        tgt = (lax.broadcasted_iota(jnp.int32, (rows, tb), 0) + first_row).astype(F32)
        return jnp.where((rank_e == tgt) & (sel_e > 0.0), 1.0, 0.0)

    @pl.when(f == 0)
    def _():
        def gather(s, c):
            off = pl.multiple_of(s * g_rows, g_rows)
            xe_ref[pl.ds(off, g_rows), :] = jnp.dot(onehot(off, g_rows).astype(BF16), hb_ref[...],
                                                    preferred_element_type=F32).astype(BF16)
            return c
        lax.fori_loop(0, groups(g_rows), gather, 0)

    def ffn(first):
        def body(s, c):
            off = pl.multiple_of(s * f_rows, f_rows)
            y = _swiglu_step(xe_ref[pl.ds(off, f_rows), :], wg_ref[...], wu_ref[...], wd_ref[...])
            if first:
                ye_ref[pl.ds(off, f_rows), :] = y
            else:
                ye_ref[pl.ds(off, f_rows), :] += y
            return c
        lax.fori_loop(0, groups(f_rows), body, 0)

    @pl.when(f == 0)
    def _():
        ffn(True)
        done = groups(f_rows) * f_rows
        @pl.when(done < groups(s_rows) * s_rows)
        def _():
            ye_ref[pl.ds(pl.multiple_of(done, f_rows), f_rows), :] = jnp.zeros((f_rows, D_MODEL), F32)

    @pl.when(f > 0)
    def _():
        ffn(False)

    @pl.when(f == last_f)
    def _():
        gate_e = gate_ref[pl.ds(e, 1), :]

        def scatter(s, c):
            off = pl.multiple_of(s * s_rows, s_rows)
            p = onehot(off, s_rows)
            gate_rows = jnp.sum(p * gate_e, axis=1, keepdims=True)
            valid = (lax.broadcasted_iota(jnp.int32, (s_rows, 1), 0) + off) < cnt
            yg = jnp.where(valid, gate_rows * ye_ref[pl.ds(off, s_rows), :], 0.0).astype(BF16)
            o_ref[...] += jnp.dot(p.T.astype(BF16), yg, preferred_element_type=F32)
            return c
        lax.fori_loop(0, groups(s_rows), scatter, 0)

    if final_norm:
        @pl.when((e == pl.num_programs(1) - 1) & (f == last_f))
        def _():
            for lo in range(0, tb, _MOE_PREFIX):
                rows = slice(lo, lo + _MOE_PREFIX)
                o_ref[rows, :] = _rms(o_ref[rows, :], gf_ref[...])


def _moe(x, g, router_pad, wg, wu, wd, final_g=None, tb=2048, fc=1408, g_rows=256, f_rows=128, s_rows=256):
    t = x.shape[0]
    tb = min(tb, t)
    gf = jnp.ones((D_MODEL,), F32) if final_g is None else final_g
    assert s_rows in (f_rows, 2 * f_rows) and g_rows % f_rows == 0 and tb % g_rows == 0 and tb % s_rows == 0
    once = pl.Buffered(1)
    return pl.pallas_call(
        functools.partial(_moe_kernel, tb=tb, g_rows=g_rows, f_rows=f_rows, s_rows=s_rows,
                          final_norm=final_g is not None),
        grid=(t // tb, N_EXPERTS, D_FF // fc),
        in_specs=[pl.BlockSpec((tb, D_MODEL), lambda i, e, j: (i, 0), pipeline_mode=once),
                  pl.BlockSpec((1, D_MODEL), lambda i, e, j: (0, 0)),
                  pl.BlockSpec((D_MODEL, LANES), lambda i, e, j: (0, 0)),
                  pl.BlockSpec((1, D_MODEL), lambda i, e, j: (0, 0)),
                  pl.BlockSpec((None, D_MODEL, fc), lambda i, e, j: (e, 0, j)),
                  pl.BlockSpec((None, D_MODEL, fc), lambda i, e, j: (e, 0, j)),
                  pl.BlockSpec((None, fc, D_MODEL), lambda i, e, j: (e, j, 0))],
        out_specs=pl.BlockSpec((tb, D_MODEL), lambda i, e, j: (i, 0), pipeline_mode=once),
        out_shape=jax.ShapeDtypeStruct((t, D_MODEL), F32),
        scratch_shapes=[pltpu.VMEM((tb, D_MODEL), BF16), pltpu.VMEM((N_EXPERTS, tb), F32),
                        pltpu.VMEM((N_EXPERTS, tb), F32), pltpu.VMEM((N_EXPERTS, tb), F32),
                        pltpu.VMEM((tb, D_MODEL), BF16), pltpu.VMEM((tb, D_MODEL), F32),
                        pltpu.SMEM((N_EXPERTS,), jnp.int32)],
        compiler_params=pltpu.CompilerParams(dimension_semantics=("parallel", "arbitrary", "arbitrary"),
                                             vmem_limit_bytes=MOE_VMEM_LIMIT),
        name="ffn_moe",
    )(x, g.reshape(1, D_MODEL), router_pad, gf.reshape(1, D_MODEL), wg, wu, wd)


def _final_norm_kernel(x_ref, g_ref, o_ref):
    o_ref[...] = _rms(x_ref[...], g_ref[...])


def _final_norm(x, g, tm=1024):
    t = x.shape[0]
    tm = min(tm, t)
    return pl.pallas_call(
        _final_norm_kernel,
        grid=(t // tm,),
        in_specs=[pl.BlockSpec((tm, D_MODEL), lambda i: (i, 0)), pl.BlockSpec((1, D_MODEL), lambda i: (0, 0))],
        out_specs=pl.BlockSpec((tm, D_MODEL), lambda i: (i, 0)),
        out_shape=jax.ShapeDtypeStruct((t, D_MODEL), F32),
        compiler_params=_cparams(("parallel",)),
        name="final_norm",
    )(x, g.reshape(1, D_MODEL))


def kernel(x_prompt, x_sample, cache_k, cache_v, state_ssm_re, state_ssm_im, state_conv, norm_mix_g, w_in, ssm_a_re, ssm_a_im, ssm_log_dt, ssm_b_re, ssm_b_im, ssm_c_re, ssm_c_im, ssm_d, ssm_w_glu, conv_w, conv_ln_g, conv_ln_b, attn_lq1, attn_lk1, attn_lq2, attn_lk2, attn_sub_g, w_out, norm_ffn_g, ffn_w_gate, ffn_w_up, ffn_w_down, moe_router, moe_w_gate, moe_w_up, moe_w_down, final_norm_g):
    depth = w_in.shape[0]
    n_p, l_p, _ = x_prompt.shape
    n_s, l_s, _ = x_sample.shape
    past = cache_k.shape[2]
    keep = CONV_KERNEL - 1

    xp = x_prompt.reshape(n_p * l_p, D_MODEL)
    xs = x_sample.reshape(n_s * l_s, D_MODEL)
    cache_kt = jnp.transpose(cache_k, (0, 1, 3, 4, 5, 2)).reshape(depth, n_s, 512, past)
    cache_vr = cache_v.reshape(depth, n_s, past * N_HEADS, V_DIM)
    s0_p = jnp.zeros((n_p, 1, 2 * STATE_W), F32)
    buf_p = jnp.zeros((n_p, keep, CONV_WIDTH), F32)

    ops = _s5_operators(ssm_a_re, ssm_a_im, ssm_log_dt, ssm_b_re, ssm_b_im, ssm_c_re, ssm_c_im, ssm_d, S5_CHUNK)
    kv_bufs = None
    outs = {name: [] for name in ("srp", "sip", "cp", "ks", "vs", "srs", "sis", "cs")}
    for l in range(depth):
        lam_init = 0.8 - 0.6 * math.exp(-0.3 * l)
        w_in_b = w_in[l].astype(BF16)
        wkvt = w_in_b[:, IN_WIDTH - 2 * ATTN_WIDTH:].T
        lam_params = jnp.stack([attn_lq1[l], attn_lk1[l], attn_lq2[l], attn_lk2[l]]).astype(F32)
        w_glu_b = ssm_w_glu[l].astype(BF16)
        w_out_b = w_out[l].astype(BF16)
        conv_p = (conv_w[l], conv_ln_g[l], conv_ln_b[l])

        u2, hc, q, kb, vt, *kv_bufs = _in_proj_prompt(xp, norm_mix_g[l], w_in_b, wkvt, kv_bufs, l, depth, n_p)
        y_ssm, sf_p = _s5_group(u2, n_p, l_p, s0_p, ops, l)
        cv, cb_p = _conv(hc.reshape(n_p, l_p, CONV_WIDTH), buf_p, *conv_p, tl=256)
        at = _attn_prompt(q, kb, vt, lam_params, attn_sub_g[l], n_p, l_p, lam_init)
        xp = _out_proj(xp, y_ssm, cv.reshape(n_p * l_p, CONV_WIDTH), at, w_glu_b, w_out_b)

        s0_s = jnp.concatenate([state_ssm_re[l].reshape(n_s, 1, STATE_W), state_ssm_im[l].reshape(n_s, 1, STATE_W)],
                               axis=-1).astype(F32)
        u2, hc, q, kb, vb, k_s, v_s = _in_proj_sample(xs, norm_mix_g[l], w_in_b)
        y_ssm, sf_s = _s5_group(u2, n_s, l_s, s0_s, ops, l)
        cv, cb_s = _conv(hc.reshape(n_s, l_s, CONV_WIDTH), state_conv[l].astype(F32), *conv_p, tl=l_s)
        at = _attn_sample(q, kb, vb, cache_kt, cache_vr, l, lam_params, attn_sub_g[l], lam_init)
        xs = _out_proj(xs, y_ssm, cv.reshape(n_s * l_s, CONV_WIDTH), at, w_glu_b, w_out_b)

        j = l // 2
        if l % 2 == 0:
            wg, wu, wd = ffn_w_gate[j].astype(BF16), ffn_w_up[j].astype(BF16), ffn_w_down[j].astype(BF16)
            xp = _ffn(xp, norm_ffn_g[l], wg, wu, wd)
            xs = _ffn(xs, norm_ffn_g[l], wg, wu, wd)
        else:
            router_pad = jnp.pad(moe_router[j].astype(F32), ((0, 0), (0, LANES - N_EXPERTS)))
            wg, wu, wd = moe_w_gate[j].astype(BF16), moe_w_up[j].astype(BF16), moe_w_down[j].astype(BF16)
            final_g = final_norm_g if l == depth - 1 else None
            xp = _moe(xp, norm_ffn_g[l], router_pad, wg, wu, wd, final_g)
            xs = _moe(xs, norm_ffn_g[l], router_pad, wg, wu, wd, final_g)

        outs["srp"].append(sf_p[:, 0, :STATE_W].reshape(n_p, N_SSM_GROUPS, SSM_STATE))
        outs["sip"].append(sf_p[:, 0, STATE_W:].reshape(n_p, N_SSM_GROUPS, SSM_STATE))
        outs["cp"].append(cb_p)
        outs["ks"].append(k_s.reshape(n_s, l_s, 2, N_HEADS, HEAD_DIM))
        outs["vs"].append(v_s.reshape(n_s, l_s, N_HEADS, V_DIM))
        outs["srs"].append(sf_s[:, 0, :STATE_W].reshape(n_s, N_SSM_GROUPS, SSM_STATE))
        outs["sis"].append(sf_s[:, 0, STATE_W:].reshape(n_s, N_SSM_GROUPS, SSM_STATE))
        outs["cs"].append(cb_s)

    if depth % 2 == 0:
        yp, ys = xp, xs
    else:
        yp = _final_norm(xp, final_norm_g)
        ys = _final_norm(xs, final_norm_g)
    st = {name: jnp.stack(vals) for name, vals in outs.items()}
    kt_buf, v_buf = kv_bufs
    k_prompt = jnp.transpose(kt_buf.reshape(depth, n_p, 2, N_HEADS, HEAD_DIM, l_p), (0, 1, 5, 2, 3, 4))
    v_prompt = v_buf.reshape(depth, n_p, l_p, N_HEADS, V_DIM)
    return (yp.reshape(n_p, l_p, D_MODEL), ys.reshape(n_s, l_s, D_MODEL),
            k_prompt, v_prompt, st["srp"], st["sip"], st["cp"],
            st["ks"], st["vs"], st["srs"], st["sis"], st["cs"])
```

```python
import functools
import math

import jax
import jax.numpy as jnp
from jax import lax
from jax.experimental import pallas as pl
from jax.experimental.pallas import tpu as pltpu

F32 = jnp.float32
BF16 = jnp.bfloat16

D_MODEL = 1024
CHUNK = 64
SSM_GROUP = 16
N_SSM_GROUPS = 16
SSM_WIDTH = 256
SSM_STATE = 64
STATE_W = N_SSM_GROUPS * SSM_STATE
CONV_WIDTH = 256
CONV_KERNEL = 31
N_HEADS = 4
HEAD_DIM = 64
V_DIM = 128
VT_ROWS = V_DIM + 16
QK_WIDTH = 256
ATTN_WIDTH = 512
IN_WIDTH = 2304
ATTN_SCALE = 1.0 / math.sqrt(HEAD_DIM)
LOG2E = math.log2(math.e)
NEG_INF = -1e30
D_FF = 2816
N_EXPERTS = 8
EPS = 1e-6

S5_CHUNK = 8
LANES = 128
SUBLANES = 8
VMEM_LIMIT = 48 * 1024 * 1024
MOE_VMEM_LIMIT = 58 * 1024 * 1024


def _cparams(sem):
    return pltpu.CompilerParams(dimension_semantics=sem, vmem_limit_bytes=VMEM_LIMIT)


def _rms(x, g):
    return x * lax.rsqrt(jnp.mean(x * x, axis=-1, keepdims=True) + EPS) * g


def _mixer_inputs(x_ref, g_ref, w_ref, u2_ref, hc_ref, q_ref, us_ref):
    h = _rms(x_ref[...], g_ref[...]).astype(BF16)

    def proj(lo, hi):
        return jnp.dot(h, w_ref[:, lo:hi], preferred_element_type=F32)

    u = proj(0, 256)
    n_chunks = us_ref.shape[1] // S5_CHUNK
    for c in range(SSM_WIDTH // LANES):
        us_ref[c] = u[:, c * LANES:(c + 1) * LANES]
        for j in range(S5_CHUNK):
            lo = j * SSM_WIDTH + c * LANES
            u2_ref[:, lo:lo + LANES] = us_ref[c, pl.ds(j, n_chunks, stride=S5_CHUNK), :]
    c_val = proj(256, 512)
    c_gate = proj(512, 768)
    hc_ref[...] = c_val * jax.nn.sigmoid(c_gate)
    q_ref[...] = (proj(768, 1280) * (ATTN_SCALE * LOG2E)).astype(BF16)
    return h, proj(1280, 1792), proj(1792, 2304)


def _in_proj_prompt_kernel(*refs, aliased):
    x_ref, g_ref, w_ref, wkvt_ref = refs[:4]
    u2_ref, hc_ref, q_ref, kb_ref, vt_ref, kt_ref, v_ref, us_ref = refs[6 if aliased else 4:]
    h, k, v = _mixer_inputs(x_ref, g_ref, w_ref, u2_ref, hc_ref, q_ref, us_ref)
    kb_ref[...] = k.astype(BF16)
    for head in range(N_HEADS):
        v_ref[pl.ds(head, v.shape[0], stride=N_HEADS), :] = v[:, head * V_DIM:(head + 1) * V_DIM]
    nt = (((1,), (1,)), ((), ()))
    kt_ref[...] = lax.dot_general(wkvt_ref[:ATTN_WIDTH, :], h, nt, preferred_element_type=F32)
    vt = lax.dot_general(wkvt_ref[ATTN_WIDTH:, :], h, nt, preferred_element_type=F32).astype(BF16)
    for head in range(N_HEADS):
        vt_ref[head * VT_ROWS:head * VT_ROWS + V_DIM, :] = vt[head * V_DIM:(head + 1) * V_DIM, :]
        vt_ref[head * VT_ROWS + V_DIM:(head + 1) * VT_ROWS, :] = jnp.ones((VT_ROWS - V_DIM, vt.shape[1]), BF16)


def _in_proj_prompt(x, g, w_bf16, wkvt_bf16, kv_bufs, layer, depth, n_seq, tm=512):
    t = x.shape[0]
    seq_len = t // n_seq
    per_seq = seq_len // tm
    row = lambda width: pl.BlockSpec((tm, width), lambda i: (i, 0))
    in_specs = [row(D_MODEL), pl.BlockSpec((1, D_MODEL), lambda i: (0, 0)),
                pl.BlockSpec((D_MODEL, IN_WIDTH), lambda i: (0, 0)),
                pl.BlockSpec((2 * ATTN_WIDTH, D_MODEL), lambda i: (0, 0))]
    args = [x, g.reshape(1, D_MODEL), w_bf16, wkvt_bf16]
    aliases = {}
    if kv_bufs is not None:
        in_specs += [pl.BlockSpec(memory_space=pl.ANY), pl.BlockSpec(memory_space=pl.ANY)]
        args += list(kv_bufs)
        aliases = {4: 5, 5: 6}
    return pl.pallas_call(
        functools.partial(_in_proj_prompt_kernel, aliased=kv_bufs is not None),
        grid=(t // tm,),
        in_specs=in_specs,
        out_specs=[pl.BlockSpec((tm // S5_CHUNK, S5_CHUNK * SSM_WIDTH), lambda i: (i, 0)),
                   row(256), row(512), row(512),
                   pl.BlockSpec((N_HEADS * VT_ROWS, tm), lambda i: (0, i)),
                   pl.BlockSpec((None, None, ATTN_WIDTH, tm), lambda i: (layer, i // per_seq, 0, i % per_seq)),
                   pl.BlockSpec((None, None, tm * N_HEADS, V_DIM), lambda i: (layer, i // per_seq, i % per_seq, 0))],
        out_shape=[jax.ShapeDtypeStruct((t // S5_CHUNK, S5_CHUNK * SSM_WIDTH), F32),
                   jax.ShapeDtypeStruct((t, 256), F32),
                   jax.ShapeDtypeStruct((t, 512), BF16), jax.ShapeDtypeStruct((t, 512), BF16),
                   jax.ShapeDtypeStruct((N_HEADS * VT_ROWS, t), BF16),
                   jax.ShapeDtypeStruct((depth, n_seq, ATTN_WIDTH, seq_len), F32),
                   jax.ShapeDtypeStruct((depth, n_seq, seq_len * N_HEADS, V_DIM), F32)],
        scratch_shapes=[pltpu.VMEM((SSM_WIDTH // LANES, tm, LANES), F32)],
        input_output_aliases=aliases,
        compiler_params=_cparams(("parallel",)),
        name="in_proj_prompt",
    )(*args)


def _in_proj_sample_kernel(x_ref, g_ref, w_ref, u2_ref, hc_ref, q_ref, kb_ref, vb_ref, k_ref, v_ref, us_ref):
    _, k, v = _mixer_inputs(x_ref, g_ref, w_ref, u2_ref, hc_ref, q_ref, us_ref)
    k_ref[...] = k
    kb_ref[...] = k.astype(BF16)
    v_ref[...] = v
    vb_ref[...] = v.astype(BF16)


def _in_proj_sample(x, g, w_bf16, tm=512):
    t = x.shape[0]
    tm = min(tm, t)
    row = lambda width: pl.BlockSpec((tm, width), lambda i: (i, 0))
    return pl.pallas_call(
        _in_proj_sample_kernel,
        grid=(t // tm,),
        in_specs=[row(D_MODEL), pl.BlockSpec((1, D_MODEL), lambda i: (0, 0)),
                  pl.BlockSpec((D_MODEL, IN_WIDTH), lambda i: (0, 0))],
        out_specs=[pl.BlockSpec((tm // S5_CHUNK, S5_CHUNK * SSM_WIDTH), lambda i: (i, 0)),
                   row(256), row(512), row(512), row(512), row(512), row(512)],
        out_shape=[jax.ShapeDtypeStruct((t // S5_CHUNK, S5_CHUNK * SSM_WIDTH), F32),
                   jax.ShapeDtypeStruct((t, 256), F32),
                   jax.ShapeDtypeStruct((t, 512), BF16), jax.ShapeDtypeStruct((t, 512), BF16),
                   jax.ShapeDtypeStruct((t, 512), BF16), jax.ShapeDtypeStruct((t, 512), F32),
                   jax.ShapeDtypeStruct((t, 512), F32)],
        scratch_shapes=[pltpu.VMEM((SSM_WIDTH // LANES, tm, LANES), F32)],
        compiler_params=_cparams(("parallel",)),
        name="in_proj_sample",
    )(x, g.reshape(1, D_MODEL), w_bf16)


def _mm_kernel(x_ref, w_ref, o_ref):
    o_ref[...] = jnp.dot(x_ref[...].astype(BF16), w_ref[...], preferred_element_type=F32)


def _mm(x, w_bf16, layer, tm, tn, name):
    m, k = x.shape
    n = w_bf16.shape[2]
    return pl.pallas_call(
        _mm_kernel,
        grid=(n // tn, m // tm),
        in_specs=[pl.BlockSpec((tm, k), lambda j, i: (i, 0)),
                  pl.BlockSpec((None, k, tn), lambda j, i: (layer, 0, j))],
        out_specs=pl.BlockSpec((tm, tn), lambda j, i: (i, j)),
        out_shape=jax.ShapeDtypeStruct((m, n), F32),
        compiler_params=_cparams(("parallel", "parallel")),
        name=name,
    )(x, w_bf16)


def _s5_scan_kernel(d_ref, s0_ref, a_ref, ss_ref, sf_ref, st_ref, *, rows):
    t = pl.program_id(1)

    @pl.when(t == 0)
    def _():
        st_ref[...] = s0_ref[...]

    a_re = a_ref[:, :STATE_W]
    a_im = a_ref[:, STATE_W:]

    def step(c, carry):
        s_re, s_im = carry
        ss_ref[pl.ds(c, 1), :STATE_W] = s_re
        ss_ref[pl.ds(c, 1), STATE_W:] = s_im
        d = d_ref[pl.ds(c, 1), :]
        n_re = a_re * s_re - a_im * s_im + d[:, :STATE_W]
        n_im = a_re * s_im + a_im * s_re + d[:, STATE_W:]
        return n_re, n_im

    s_re, s_im = lax.fori_loop(0, rows, step, (st_ref[:, :STATE_W], st_ref[:, STATE_W:]))
    st_ref[:, :STATE_W] = s_re
    st_ref[:, STATE_W:] = s_im
    sf_ref[...] = st_ref[...]


def _s5_scan(d, n_seq, n_chunks, s0, a_c, layer):
    rows = min(n_chunks, 256)
    nt = n_chunks // rows
    w = 2 * STATE_W
    return pl.pallas_call(
        functools.partial(_s5_scan_kernel, rows=rows),
        grid=(n_seq, nt),
        in_specs=[pl.BlockSpec((rows, w), lambda s, t: (s * nt + t, 0)),
                  pl.BlockSpec((None, 1, w), lambda s, t: (s, 0, 0)),
                  pl.BlockSpec((None, 1, w), lambda s, t: (layer, 0, 0))],
        out_specs=[pl.BlockSpec((rows, w), lambda s, t: (s * nt + t, 0)),
                   pl.BlockSpec((None, 1, w), lambda s, t: (s, 0, 0))],
        out_shape=[jax.ShapeDtypeStruct((n_seq * n_chunks, w), F32), jax.ShapeDtypeStruct((n_seq, 1, w), F32)],
        scratch_shapes=[pltpu.VMEM((1, w), F32)],
        compiler_params=_cparams(("parallel", "arbitrary")),
        name="s5_scan",
    )(d, s0, a_c)


def _s5_out_kernel(u_ref, ut_ref, ss_ref, wi_ref, ws_ref, dsk_ref, y_ref):
    y_ref[...] = (jnp.dot(u_ref[...].astype(BF16), wi_ref[...], preferred_element_type=F32)
                  + jnp.dot(ss_ref[...].astype(BF16), ws_ref[...], preferred_element_type=F32)
                  + dsk_ref[...] * ut_ref[...])


def _s5_out(u2, s_start, w_intra, w_inter, dsk, layer, tm=256, tn=512):
    m, n = u2.shape
    k = s_start.shape[1]
    return pl.pallas_call(
        _s5_out_kernel,
        grid=(n // tn, m // tm),
        in_specs=[pl.BlockSpec((tm, n), lambda j, i: (i, 0)), pl.BlockSpec((tm, tn), lambda j, i: (i, j)),
                  pl.BlockSpec((tm, k), lambda j, i: (i, 0)),
                  pl.BlockSpec((None, n, tn), lambda j, i: (layer, 0, j)),
                  pl.BlockSpec((None, k, tn), lambda j, i: (layer, 0, j)),
                  pl.BlockSpec((None, 1, tn), lambda j, i: (layer, 0, j))],
        out_specs=pl.BlockSpec((tm, tn), lambda j, i: (i, j)),
        out_shape=jax.ShapeDtypeStruct((m, n), F32),
        compiler_params=_cparams(("parallel", "parallel")),
        name="s5_out",
    )(u2, u2, s_start, w_intra, w_inter, dsk)


def _s5_operators(a_re, a_im, log_dt, b_re, b_im, c_re, c_im, d_skip, n_c):
    f32 = lambda x: x.astype(F32)
    a_re, a_im, b_re, b_im, c_re, c_im = map(f32, (a_re, a_im, b_re, b_im, c_re, c_im))
    n_l = a_re.shape[0]
    g_n = N_SSM_GROUPS
    dt = jnp.exp(f32(log_dt))[:, None, :, None]
    ks = jnp.arange(n_c + 1, dtype=F32)[None, :, None, None]
    mag = jnp.exp(a_re[:, None] * dt * ks)
    ang = a_im[:, None] * dt * ks
    pw_re, pw_im = mag * jnp.cos(ang), mag * jnp.sin(ang)
    den = a_re * a_re + a_im * a_im
    q_re = ((pw_re[:, 1] - 1.0) * a_re + pw_im[:, 1] * a_im) / den
    q_im = (pw_im[:, 1] * a_re - (pw_re[:, 1] - 1.0) * a_im) / den
    bb_re = q_re[..., None] * b_re - q_im[..., None] * b_im
    bb_im = q_re[..., None] * b_im + q_im[..., None] * b_re
    eye_g = jnp.eye(g_n, dtype=F32)

    def block_diag(x, rows, cols):
        return (x[:, :, :, None, :] * eye_g[None, :, None, :, None]).reshape(n_l, rows, cols)

    bt_re = block_diag(jnp.transpose(bb_re, (0, 1, 3, 2)), SSM_WIDTH, STATE_W)[:, None]
    bt_im = block_diag(jnp.transpose(bb_im, (0, 1, 3, 2)), SSM_WIDTH, STATE_W)[:, None]
    pj_re = pw_re[:, :n_c][:, ::-1].reshape(n_l, n_c, 1, STATE_W)
    pj_im = pw_im[:, :n_c][:, ::-1].reshape(n_l, n_c, 1, STATE_W)
    w_state = jnp.concatenate([(pj_re * bt_re - pj_im * bt_im).reshape(n_l, n_c * SSM_WIDTH, STATE_W),
                               (pj_re * bt_im + pj_im * bt_re).reshape(n_l, n_c * SSM_WIDTH, STATE_W)], axis=2)

    m_re = c_re[:, None] * pw_re[:, :n_c, :, None, :] - c_im[:, None] * pw_im[:, :n_c, :, None, :]
    m_im = c_re[:, None] * pw_im[:, :n_c, :, None, :] + c_im[:, None] * pw_re[:, :n_c, :, None, :]
    hi = lax.Precision.HIGHEST
    ker = (jnp.einsum('lkghp,lgpx->lkghx', m_re, bb_re, precision=hi)
           - jnp.einsum('lkghp,lgpx->lkghx', m_im, bb_im, precision=hi))
    blk = jnp.transpose(ker, (0, 1, 2, 4, 3))
    blk = (blk[:, :, :, :, None, :] * eye_g[None, None, :, None, :, None]).reshape(n_l, n_c, 256, 256).astype(BF16)
    zero = jnp.zeros((n_l, 256, 256), BF16)
    w_intra = jnp.concatenate(
        [jnp.concatenate([blk[:, t - j] if t >= j else zero for t in range(n_c)], axis=2) for j in range(n_c)],
        axis=1)

    ct_re = block_diag(jnp.transpose(c_re, (0, 1, 3, 2)), STATE_W, SSM_WIDTH)
    ct_im = block_diag(jnp.transpose(c_im, (0, 1, 3, 2)), STATE_W, SSM_WIDTH)
    pt_re = pw_re[:, 1:].reshape(n_l, n_c, STATE_W, 1)
    pt_im = pw_im[:, 1:].reshape(n_l, n_c, STATE_W, 1)
    top = jnp.concatenate([pt_re[:, t] * ct_re - pt_im[:, t] * ct_im for t in range(n_c)], axis=2)
    bot = jnp.concatenate([-(pt_re[:, t] * ct_im + pt_im[:, t] * ct_re) for t in range(n_c)], axis=2)
    w_inter = jnp.concatenate([top, bot], axis=1)

    a_c = jnp.concatenate([pw_re[:, n_c].reshape(n_l, 1, STATE_W), pw_im[:, n_c].reshape(n_l, 1, STATE_W)], axis=2)
    dsk = jnp.tile(f32(d_skip).reshape(n_l, 1, SSM_WIDTH), (1, 1, n_c))
    return w_state.astype(BF16), w_intra, w_inter.astype(BF16), a_c, dsk


def _s5_group(u2, n_seq, seq_len, s0, ops, layer):
    w_state, w_intra, w_inter, a_c, dsk = ops
    n_chunks = seq_len // S5_CHUNK
    d_state = _mm(u2, w_state, layer, 256, 1024, "s5_state")
    s_start, s_final = _s5_scan(d_state, n_seq, n_chunks, s0, a_c, layer)
    return _s5_out(u2, s_start, w_intra, w_inter, dsk, layer), s_final


_CONV_PAD = 32
_CONV_RB = 64


def _conv_kernel(h_ref, buf_ref, w_ref, g_ref, b_ref, y_ref, nb_ref, xp_ref, xs_ref, *, tl):
    t = pl.program_id(1)
    keep = CONV_KERNEL - 1
    lo = _CONV_PAD - keep

    @pl.when(t == 0)
    def _():
        xp_ref[lo:_CONV_PAD, :] = buf_ref[...]

    @pl.when(t > 0)
    def _():
        xp_ref[lo:_CONV_PAD, :] = xp_ref[tl + lo:tl + _CONV_PAD, :]

    xp_ref[_CONV_PAD:_CONV_PAD + tl, :] = h_ref[...]
    for phase in range(SUBLANES):
        span = tl + (keep - phase) // SUBLANES * SUBLANES
        xs_ref[phase, :span, :] = xp_ref[lo + phase:lo + phase + span, :]
    for r in range(tl // _CONV_RB):
        base = r * _CONV_RB
        acc = jnp.zeros((_CONV_RB, CONV_WIDTH), F32)
        for k in range(CONV_KERNEL):
            start = base + k - k % SUBLANES
            acc = acc + w_ref[k:k + 1, :] * xs_ref[k % SUBLANES, start:start + _CONV_RB, :]
        xc = acc - jnp.mean(acc, axis=-1, keepdims=True)
        var = jnp.mean(xc * xc, axis=-1, keepdims=True)
        y = xc * lax.rsqrt(var + EPS) * g_ref[...] + b_ref[...]
        y_ref[base:base + _CONV_RB, :] = y * jax.nn.sigmoid(y)
    nb_ref[...] = xp_ref[tl + lo:tl + _CONV_PAD, :]


def _conv(h, buf, w, ln_g, ln_b, tl):
    s, l, c = h.shape
    keep = CONV_KERNEL - 1
    return pl.pallas_call(
        functools.partial(_conv_kernel, tl=tl),
        grid=(s, l // tl),
        in_specs=[pl.BlockSpec((None, tl, c), lambda i, t: (i, t, 0)),
                  pl.BlockSpec((None, keep, c), lambda i, t: (i, 0, 0)),
                  pl.BlockSpec((CONV_KERNEL, c), lambda i, t: (0, 0)),
                  pl.BlockSpec((1, c), lambda i, t: (0, 0)), pl.BlockSpec((1, c), lambda i, t: (0, 0))],
        out_specs=[pl.BlockSpec((None, tl, c), lambda i, t: (i, t, 0)),
                   pl.BlockSpec((None, keep, c), lambda i, t: (i, 0, 0))],
        out_shape=[jax.ShapeDtypeStruct((s, l, c), F32), jax.ShapeDtypeStruct((s, keep, c), F32)],
        scratch_shapes=[pltpu.VMEM((tl + _CONV_PAD, c), F32),
                        pltpu.VMEM((SUBLANES, tl + (keep // SUBLANES) * SUBLANES, c), F32)],
        compiler_params=_cparams(("parallel", "arbitrary")),
        name="conv_module",
    )(h, buf, w, ln_g.reshape(1, c), ln_b.reshape(1, c))


def _stack_q(q_ref, qs_ref):
    tq = q_ref.shape[0]
    lane = lax.broadcasted_iota(jnp.int32, (tq, LANES), 1)
    for b in range(4):
        qb = q_ref[:, b * LANES:(b + 1) * LANES]
        qs_ref[b, :tq, :] = jnp.where(lane < HEAD_DIM, qb, jnp.zeros_like(qb))
        qs_ref[b, tq:, :] = jnp.where(lane >= HEAD_DIM, qb, jnp.zeros_like(qb))


def _attn_update(score, value, m_ref, l_ref, acc_ref, tq):
    def softmax(b, s):
        m_old = m_ref[b]
        m_new = jnp.maximum(m_old, jnp.max(s, axis=-1, keepdims=True))
        alpha = jnp.exp2(m_old - m_new)
        p = jnp.exp2(s - m_new)
        l_ref[b] = alpha * l_ref[b] + jnp.sum(p, axis=-1, keepdims=True)
        m_ref[b] = m_new
        return alpha, p.astype(BF16)

    def values(b, alpha, pb):
        h0 = 2 * (b % 2)
        pv0 = jnp.dot(pb[:tq], value(h0), preferred_element_type=F32)
        pv1 = jnp.dot(pb[tq:], value(h0 + 1), preferred_element_type=F32)
        acc_ref[b, :tq, :] = alpha[:tq] * acc_ref[b, :tq, :] + pv0
        acc_ref[b, tq:, :] = alpha[tq:] * acc_ref[b, tq:, :] + pv1

    s0 = score(0)
    s1 = score(1)
    s2 = score(2)
    a0, p0 = softmax(0, s0)
    s3 = score(3)
    a1, p1 = softmax(1, s1)
    values(0, a0, p0)
    a2, p2 = softmax(2, s2)
    values(1, a1, p1)
    a3, p3 = softmax(3, s3)
    values(2, a2, p2)
    values(3, a3, p3)


def _attn_finish(lp_ref, sg_ref, l_ref, acc_ref, o_ref, tq, lam_init):
    lam = _lambda(lp_ref, lam_init)
    for h in range(N_HEADS):
        b1, half = h // 2, h % 2
        rows = slice(half * tq, (half + 1) * tq)
        o1 = acc_ref[b1, rows, :] / l_ref[b1, rows, :]
        o2 = acc_ref[b1 + 2, rows, :] / l_ref[b1 + 2, rows, :]
        o = o1 - lam * o2
        o_ref[:, h * V_DIM:(h + 1) * V_DIM] = (_rms(o, sg_ref[...]) * (1.0 - lam_init)).astype(o_ref.dtype)


def _attn_init(m_ref, l_ref, acc_ref):
    m_ref[...] = jnp.full(m_ref.shape, NEG_INF, F32)
    l_ref[...] = jnp.zeros(l_ref.shape, F32)
    acc_ref[...] = jnp.zeros(acc_ref.shape, F32)


def _attn_prompt_update(qs_ref, k_ref, vt_ref, m_ref, acc_ref, tq, mask):
    def scores(c):
        b, r = divmod(c, 2)
        s = lax.dot_general(k_ref[:, b * LANES:(b + 1) * LANES], qs_ref[b, r * tq:(r + 1) * tq, :],
                            (((1,), (1,)), ((), ())), preferred_element_type=F32)
        return s if mask is None else jnp.where(mask, s, NEG_INF)

    def softmax(c, s):
        b, r = divmod(c, 2)
        cols = slice(r * tq, (r + 1) * tq)
        sb = s.astype(BF16)
        m_old = m_ref[b, :, cols]
        m_new = jnp.maximum(m_old, jnp.max(sb, axis=0, keepdims=True).astype(F32))
        m_ref[b, :, cols] = m_new
        return jnp.exp2(m_old - m_new), jnp.exp2(sb - m_new.astype(BF16))

    def values(c, alpha, pb):
        h = 2 * ((c // 2) % 2) + c % 2
        pv = jnp.dot(vt_ref[h * VT_ROWS:(h + 1) * VT_ROWS, :], pb, preferred_element_type=F32)
        acc_ref[c] = alpha * acc_ref[c] + pv

    n_chain, ahead = 8, 3
    pending = [scores(c) for c in range(ahead)]
    for c in range(n_chain):
        if c + ahead < n_chain:
            pending.append(scores(c + ahead))
        alpha, pb = softmax(c, pending.pop(0))
        values(c, alpha, pb)


def _lambda(lp_ref, lam_init):
    lp = lp_ref[...]
    return (jnp.exp(jnp.sum(lp[0:1] * lp[1:2], axis=-1, keepdims=True))
            - jnp.exp(jnp.sum(lp[2:3] * lp[3:4], axis=-1, keepdims=True)) + lam_init)


def _attn_prompt_finish(lp_ref, sgt_ref, acc_ref, o_ref, tq, lam_init):
    lam = _lambda(lp_ref, lam_init)
    for h in range(N_HEADS):
        i1 = 2 * (h // 2) + h % 2
        o1 = acc_ref[i1, :V_DIM, :] / acc_ref[i1, V_DIM:V_DIM + 1, :]
        o2 = acc_ref[i1 + 4, :V_DIM, :] / acc_ref[i1 + 4, V_DIM:V_DIM + 1, :]
        o = o1 - lam * o2
        on = o * lax.rsqrt(jnp.mean(o * o, axis=0, keepdims=True) + EPS) * sgt_ref[...] * (1.0 - lam_init)
        o_ref[:, h * V_DIM:(h + 1) * V_DIM] = on.T.astype(o_ref.dtype)


def _attn_prompt_kernel(it_ref, jt_ref, lp_ref, sgt_ref, q_ref, k_ref, vt_ref, o_ref, qs_ref, m_ref, acc_ref,
                        *, tq, lam_init):
    step = pl.program_id(1)
    i = it_ref[step]
    j = jt_ref[step]

    @pl.when(j == 0)
    def _():
        m_ref[...] = jnp.full(m_ref.shape, NEG_INF, F32)
        acc_ref[...] = jnp.zeros(acc_ref.shape, F32)
        _stack_q(q_ref, qs_ref)

    @pl.when(j < i)
    def _():
        _attn_prompt_update(qs_ref, k_ref, vt_ref, m_ref, acc_ref, tq, None)

    @pl.when(j == i)
    def _():
        tk = k_ref.shape[0]
        key_chunk = lax.broadcasted_iota(jnp.int32, (tk, tq), 0) // CHUNK
        qry_chunk = lax.broadcasted_iota(jnp.int32, (tk, tq), 1) // CHUNK
        _attn_prompt_update(qs_ref, k_ref, vt_ref, m_ref, acc_ref, tq, qry_chunk >= key_chunk)
        _attn_prompt_finish(lp_ref, sgt_ref, acc_ref, o_ref, tq, lam_init)


def _attn_prompt(q, kb, vt, lam_params, sub_g, n_seq, seq_len, lam_init, tq=512):
    nq = seq_len // tq
    pairs = [(i, j) for i in range(nq) for j in range(i + 1)]
    i_tab = jnp.asarray([p[0] for p in pairs], jnp.int32)
    j_tab = jnp.asarray([p[1] for p in pairs], jnp.int32)
    grid_spec = pltpu.PrefetchScalarGridSpec(
        num_scalar_prefetch=2,
        grid=(n_seq, len(pairs)),
        in_specs=[pl.BlockSpec((4, HEAD_DIM), lambda b, s, it, jt: (0, 0)),
                  pl.BlockSpec((V_DIM, 1), lambda b, s, it, jt: (0, 0)),
                  pl.BlockSpec((tq, 512), lambda b, s, it, jt: (b * nq + it[s], 0)),
                  pl.BlockSpec((tq, 512), lambda b, s, it, jt: (b * nq + jt[s], 0)),
                  pl.BlockSpec((N_HEADS * VT_ROWS, tq), lambda b, s, it, jt: (0, b * nq + jt[s]))],
        out_specs=pl.BlockSpec((tq, 512), lambda b, s, it, jt: (b * nq + it[s], 0)),
        scratch_shapes=[pltpu.VMEM((4, 2 * tq, LANES), BF16), pltpu.VMEM((4, 1, 2 * tq), F32),
                        pltpu.VMEM((8, VT_ROWS, tq), F32)],
    )
    return pl.pallas_call(
        functools.partial(_attn_prompt_kernel, tq=tq, lam_init=lam_init),
        grid_spec=grid_spec,
        out_shape=jax.ShapeDtypeStruct((n_seq * seq_len, 512), BF16),
        compiler_params=_cparams(("parallel", "arbitrary")),
        name="attn_prompt",
    )(i_tab, j_tab, lam_params, sub_g.reshape(V_DIM, 1), q, kb, vt)


def _attn_sample_kernel(lp_ref, sg_ref, q_ref, ckt_ref, cv_ref, kn_ref, vn_ref, o_ref, qs_ref, m_ref, l_ref, acc_ref,
                        *, tq, tk, lam_init):
    j = pl.program_id(1)

    @pl.when(j == 0)
    def _():
        _attn_init(m_ref, l_ref, acc_ref)
        _stack_q(q_ref, qs_ref)

    def cache_score(b):
        return jnp.dot(qs_ref[b], ckt_ref[b * LANES:(b + 1) * LANES, :].astype(BF16), preferred_element_type=F32)

    def cache_value(h):
        return cv_ref[pl.ds(h, tk, stride=N_HEADS), :].astype(BF16)

    _attn_update(cache_score, cache_value, m_ref, l_ref, acc_ref, tq)

    @pl.when(j == pl.num_programs(1) - 1)
    def _():
        def new_score(b):
            return lax.dot_general(qs_ref[b], kn_ref[:, b * LANES:(b + 1) * LANES], (((1,), (1,)), ((), ())),
                                   preferred_element_type=F32)

        def new_value(h):
            return vn_ref[:, h * V_DIM:(h + 1) * V_DIM]

        _attn_update(new_score, new_value, m_ref, l_ref, acc_ref, tq)
        _attn_finish(lp_ref, sg_ref, l_ref, acc_ref, o_ref, tq, lam_init)


def _attn_sample(q, kb, vb, cache_kt, cache_v, layer, lam_params, sub_g, lam_init, tk=2048):
    _, n_seq, _, past = cache_kt.shape
    tk = min(tk, past)
    assert past % tk == 0
    tq = q.shape[0] // n_seq
    new = pl.BlockSpec((tq, 512), lambda b, j: (b, 0))
    return pl.pallas_call(
        functools.partial(_attn_sample_kernel, tq=tq, tk=tk, lam_init=lam_init),
        grid=(n_seq, past // tk),
        in_specs=[pl.BlockSpec((4, HEAD_DIM), lambda b, j: (0, 0)), pl.BlockSpec((1, V_DIM), lambda b, j: (0, 0)),
                  new,
                  pl.BlockSpec((None, None, 512, tk), lambda b, j: (layer, b, 0, j)),
                  pl.BlockSpec((None, None, tk * N_HEADS, V_DIM), lambda b, j: (layer, b, j, 0)),
                  new, new],
        out_specs=new,
        out_shape=jax.ShapeDtypeStruct((n_seq * tq, 512), BF16),
        scratch_shapes=[pltpu.VMEM((4, 2 * tq, LANES), BF16), pltpu.VMEM((4, 2 * tq, 1), F32),
                        pltpu.VMEM((4, 2 * tq, 1), F32), pltpu.VMEM((4, 2 * tq, V_DIM), F32)],
        compiler_params=_cparams(("parallel", "arbitrary")),
        name="attn_sample",
    )(lam_params, sub_g.reshape(1, V_DIM), q, cache_kt, cache_v, kb, vb)


def _gelu_tanh(x):
    return 0.5 * x * (1.0 + jnp.tanh(math.sqrt(2.0 / math.pi) * (x + 0.044715 * (x * x * x))))


def _out_proj_kernel(x_ref, y2_ref, cv_ref, at_ref, wglu_ref, wo_ref, o_ref, ys_ref):
    n_chunks = y2_ref.shape[0]
    for c in range(SSM_WIDTH // LANES):
        for j in range(S5_CHUNK):
            lo = j * SSM_WIDTH + c * LANES
            ys_ref[c, pl.ds(j, n_chunks, stride=S5_CHUNK), :] = y2_ref[:, lo:lo + LANES]
    z = _gelu_tanh(jnp.concatenate([ys_ref[c] for c in range(SSM_WIDTH // LANES)], axis=1))
    gate = jax.nn.sigmoid(jnp.dot(z.astype(BF16), wglu_ref[...], preferred_element_type=F32))
    ssm = (z * gate).astype(BF16)
    acc = jnp.dot(ssm, wo_ref[0:256, :], preferred_element_type=F32)
    acc = acc + jnp.dot(cv_ref[...].astype(BF16), wo_ref[256:512, :], preferred_element_type=F32)
    acc = acc + jnp.dot(at_ref[...], wo_ref[512:1024, :], preferred_element_type=F32)
    o_ref[...] = x_ref[...] + acc


def _out_proj(x, y_ssm, conv_out, attn_out, w_glu, w_out, tm=512):
    t = x.shape[0]
    tm = min(tm, t)
    row = lambda width: pl.BlockSpec((tm, width), lambda i: (i, 0))
    return pl.pallas_call(
        _out_proj_kernel,
        grid=(t // tm,),
        in_specs=[row(D_MODEL), pl.BlockSpec((tm // S5_CHUNK, S5_CHUNK * SSM_WIDTH), lambda i: (i, 0)),
                  row(256), row(512),
                  pl.BlockSpec((256, 256), lambda i: (0, 0)), pl.BlockSpec((D_MODEL, D_MODEL), lambda i: (0, 0))],
        out_specs=row(D_MODEL),
        out_shape=jax.ShapeDtypeStruct((t, D_MODEL), F32),
        scratch_shapes=[pltpu.VMEM((SSM_WIDTH // LANES, tm, LANES), F32)],
        compiler_params=_cparams(("parallel",)),
        name="out_proj",
    )(x, y_ssm, conv_out, attn_out, w_glu, w_out)


def _swiglu_step(h, wg, wu, wd):
    g = jnp.dot(h, wg, preferred_element_type=F32)
    u = jnp.dot(h, wu, preferred_element_type=F32)
    a = (g * jax.nn.sigmoid(g) * u).astype(BF16)
    return jnp.dot(a, wd, preferred_element_type=F32)


_FFN_SUB = 512


def _ffn_kernel(x_ref, g_ref, wg_ref, wu_ref, wd_ref, o_ref, h_ref, a_ref, acc_ref):
    j = pl.program_id(1)
    tf = a_ref.shape[1]

    @pl.when(j == 0)
    def _():
        h_ref[...] = _rms(x_ref[...], g_ref[...]).astype(BF16)

    for lo in range(0, tf, _FFN_SUB):
        cols = slice(lo, min(lo + _FFN_SUB, tf))
        gate = jnp.dot(h_ref[...], wg_ref[:, cols], preferred_element_type=F32)
        up = jnp.dot(h_ref[...], wu_ref[:, cols], preferred_element_type=F32)
        a_ref[:, cols] = (gate * jax.nn.sigmoid(gate) * up).astype(BF16)
    y = jnp.dot(a_ref[...], wd_ref[...], preferred_element_type=F32)

    @pl.when(j == 0)
    def _():
        acc_ref[...] = y

    @pl.when(j > 0)
    def _():
        acc_ref[...] += y

    @pl.when(j == pl.num_programs(1) - 1)
    def _():
        o_ref[...] = x_ref[...] + acc_ref[...]


def _ffn(x, g, wg, wu, wd, tm=1024, tf=1408):
    t = x.shape[0]
    tm = min(tm, t)
    return pl.pallas_call(
        _ffn_kernel,
        grid=(t // tm, D_FF // tf),
        in_specs=[pl.BlockSpec((tm, D_MODEL), lambda i, j: (i, 0)), pl.BlockSpec((1, D_MODEL), lambda i, j: (0, 0)),
                  pl.BlockSpec((D_MODEL, tf), lambda i, j: (0, j)), pl.BlockSpec((D_MODEL, tf), lambda i, j: (0, j)),
                  pl.BlockSpec((tf, D_MODEL), lambda i, j: (j, 0))],
        out_specs=pl.BlockSpec((tm, D_MODEL), lambda i, j: (i, 0)),
        out_shape=jax.ShapeDtypeStruct((t, D_MODEL), F32),
        scratch_shapes=[pltpu.VMEM((tm, D_MODEL), BF16), pltpu.VMEM((tm, tf), BF16), pltpu.VMEM((tm, D_MODEL), F32)],
        compiler_params=_cparams(("parallel", "arbitrary")),
        name="ffn_dense",
    )(x, g.reshape(1, D_MODEL), wg, wu, wd)


_MOE_PREFIX = 256


def _moe_route(x_ref, g_ref, r_ref, o_ref, hb_ref, rank_ref, sel_ref, gate_ref, cnt_ref):
    n_blk = x_ref.shape[0] // _MOE_PREFIX

    for blk in range(n_blk):
        rows = slice(blk * _MOE_PREFIX, (blk + 1) * _MOE_PREFIX)
        x = x_ref[rows, :]
        h32 = _rms(x, g_ref[...])
        hb_ref[rows, :] = h32.astype(BF16)
        o_ref[rows, :] = x
        logits = jnp.dot(h32, r_ref[...], preferred_element_type=F32, precision=lax.Precision.HIGHEST)
        rank_ref[:, rows] = logits.T[:N_EXPERTS, :]
    lt = rank_ref[...]
    row = lax.broadcasted_iota(jnp.int32, lt.shape, 0)
    m1 = jnp.max(lt, axis=0, keepdims=True)
    i1 = jnp.min(jnp.where(lt == m1, row, N_EXPERTS), axis=0, keepdims=True)
    lt2 = jnp.where(row == i1, -jnp.inf, lt)
    m2 = jnp.max(lt2, axis=0, keepdims=True)
    i2 = jnp.min(jnp.where(lt2 == m2, row, N_EXPERTS), axis=0, keepdims=True)
    e2 = jnp.exp(m2 - m1)
    den = 1.0 + e2
    gate_ref[...] = jnp.where(row == i1, 1.0 / den, 0.0) + jnp.where(row == i2, e2 / den, 0.0)
    sel = jnp.where(row == i1, 1.0, 0.0) + jnp.where(row == i2, 1.0, 0.0)
    sel_ref[...] = sel
    before = (lax.broadcasted_iota(jnp.int32, (_MOE_PREFIX, _MOE_PREFIX), 0)
              < lax.broadcasted_iota(jnp.int32, (_MOE_PREFIX, _MOE_PREFIX), 1))
    tri = jnp.where(before, 1.0, 0.0).astype(BF16)
    carry = jnp.zeros((N_EXPERTS, 1), F32)
    for blk in range(lt.shape[1] // _MOE_PREFIX):
        cols = slice(blk * _MOE_PREFIX, (blk + 1) * _MOE_PREFIX)
        rank_ref[:, cols] = jnp.dot(sel[:, cols].astype(BF16), tri, preferred_element_type=F32) + carry
        carry = carry + jnp.sum(sel[:, cols], axis=1, keepdims=True)
    for k in range(N_EXPERTS):
        cnt_ref[k] = carry[k, 0].astype(jnp.int32)


def _moe_kernel(x_ref, g_ref, r_ref, gf_ref, wg_ref, wu_ref, wd_ref, o_ref, hb_ref, rank_ref, sel_ref, gate_ref,
                xe_ref, ye_ref, cnt_ref, *, tb, g_rows, f_rows, s_rows, final_norm):
    e = pl.program_id(1)
    f = pl.program_id(2)
    last_f = pl.num_programs(2) - 1

    @pl.when((e == 0) & (f == 0))
    def _():
        _moe_route(x_ref, g_ref, r_ref, o_ref, hb_ref, rank_ref, sel_ref, gate_ref, cnt_ref)

    cnt = cnt_ref[e]
    rank_e = rank_ref[pl.ds(e, 1), :]
    sel_e = sel_ref[pl.ds(e, 1), :]

    def groups(rows):
        return (cnt + rows - 1) // rows

    def onehot(first_row, rows):
        tgt = (lax.broadcasted_iota(jnp.int32, (rows, tb), 0) + first_row).astype(F32)
        return jnp.where((rank_e == tgt) & (sel_e > 0.0), 1.0, 0.0)

    @pl.when(f == 0)
    def _():
        def gather(s, c):
            off = pl.multiple_of(s * g_rows, g_rows)
            xe_ref[pl.ds(off, g_rows), :] = jnp.dot(onehot(off, g_rows).astype(BF16), hb_ref[...],
                                                    preferred_element_type=F32).astype(BF16)
            return c
        lax.fori_loop(0, groups(g_rows), gather, 0)

    full = cnt // s_rows
    rem = cnt - full * s_rows
    tail_off = pl.multiple_of(full * s_rows, s_rows)

    def ffn(first):
        def rows_at(off, rows):
            y = _swiglu_step(xe_ref[pl.ds(off, rows), :], wg_ref[...], wu_ref[...], wd_ref[...])
            if first:
                ye_ref[pl.ds(off, rows), :] = y
            else:
                ye_ref[pl.ds(off, rows), :] += y

        def body(s, c):
            rows_at(pl.multiple_of(s * s_rows, s_rows), s_rows)
            return c
        lax.fori_loop(0, full + (rem > f_rows).astype(jnp.int32), body, 0)

        @pl.when((rem > 0) & (rem <= f_rows))
        def _():
            rows_at(tail_off, f_rows)

    @pl.when(f == 0)
    def _():
        ffn(True)
        if s_rows > f_rows:
            @pl.when((rem > 0) & (rem <= f_rows))
            def _():
                rest = pl.ds(pl.multiple_of(tail_off + f_rows, f_rows), s_rows - f_rows)
                ye_ref[rest, :] = jnp.zeros((s_rows - f_rows, D_MODEL), F32)

    @pl.when(f > 0)
    def _():
        ffn(False)

    @pl.when(f == last_f)
    def _():
        gate_e = gate_ref[pl.ds(e, 1), :]

        def scatter(s, c):
            off = pl.multiple_of(s * s_rows, s_rows)
            p = onehot(off, s_rows)
            gate_rows = jnp.sum(p * gate_e, axis=1, keepdims=True)
            valid = (lax.broadcasted_iota(jnp.int32, (s_rows, 1), 0) + off) < cnt
            yg = jnp.where(valid, gate_rows * ye_ref[pl.ds(off, s_rows), :], 0.0).astype(BF16)
            o_ref[...] += jnp.dot(p.T.astype(BF16), yg, preferred_element_type=F32)
            return c
        lax.fori_loop(0, groups(s_rows), scatter, 0)

    if final_norm:
        @pl.when((e == pl.num_programs(1) - 1) & (f == last_f))
        def _():
            for lo in range(0, tb, _MOE_PREFIX):
                rows = slice(lo, lo + _MOE_PREFIX)
                o_ref[rows, :] = _rms(o_ref[rows, :], gf_ref[...])


def _moe(x, g, router_pad, wg, wu, wd, final_g=None, tb=2048, fc=1408, g_rows=256, f_rows=128, s_rows=256):
    t = x.shape[0]
    tb = min(tb, t)
    gf = jnp.ones((D_MODEL,), F32) if final_g is None else final_g
    assert s_rows in (f_rows, 2 * f_rows) and g_rows % f_rows == 0 and tb % g_rows == 0 and tb % s_rows == 0
    once = pl.Buffered(1)
    return pl.pallas_call(
        functools.partial(_moe_kernel, tb=tb, g_rows=g_rows, f_rows=f_rows, s_rows=s_rows,
                          final_norm=final_g is not None),
        grid=(t // tb, N_EXPERTS, D_FF // fc),
        in_specs=[pl.BlockSpec((tb, D_MODEL), lambda i, e, j: (i, 0), pipeline_mode=once),
                  pl.BlockSpec((1, D_MODEL), lambda i, e, j: (0, 0)),
                  pl.BlockSpec((D_MODEL, LANES), lambda i, e, j: (0, 0)),
                  pl.BlockSpec((1, D_MODEL), lambda i, e, j: (0, 0)),
                  pl.BlockSpec((None, D_MODEL, fc), lambda i, e, j: (e, 0, j)),
                  pl.BlockSpec((None, D_MODEL, fc), lambda i, e, j: (e, 0, j)),
                  pl.BlockSpec((None, fc, D_MODEL), lambda i, e, j: (e, j, 0))],
        out_specs=pl.BlockSpec((tb, D_MODEL), lambda i, e, j: (i, 0), pipeline_mode=once),
        out_shape=jax.ShapeDtypeStruct((t, D_MODEL), F32),
        scratch_shapes=[pltpu.VMEM((tb, D_MODEL), BF16), pltpu.VMEM((N_EXPERTS, tb), F32),
                        pltpu.VMEM((N_EXPERTS, tb), F32), pltpu.VMEM((N_EXPERTS, tb), F32),
                        pltpu.VMEM((tb, D_MODEL), BF16), pltpu.VMEM((tb, D_MODEL), F32),
                        pltpu.SMEM((N_EXPERTS,), jnp.int32)],
        compiler_params=pltpu.CompilerParams(dimension_semantics=("parallel", "arbitrary", "arbitrary"),
                                             vmem_limit_bytes=MOE_VMEM_LIMIT),
        name="ffn_moe",
    )(x, g.reshape(1, D_MODEL), router_pad, gf.reshape(1, D_MODEL), wg, wu, wd)


def _final_norm_kernel(x_ref, g_ref, o_ref):
    o_ref[...] = _rms(x_ref[...], g_ref[...])


def _final_norm(x, g, tm=1024):
    t = x.shape[0]
    tm = min(tm, t)
    return pl.pallas_call(
        _final_norm_kernel,
        grid=(t // tm,),
        in_specs=[pl.BlockSpec((tm, D_MODEL), lambda i: (i, 0)), pl.BlockSpec((1, D_MODEL), lambda i: (0, 0))],
        out_specs=pl.BlockSpec((tm, D_MODEL), lambda i: (i, 0)),
        out_shape=jax.ShapeDtypeStruct((t, D_MODEL), F32),
        compiler_params=_cparams(("parallel",)),
        name="final_norm",
    )(x, g.reshape(1, D_MODEL))


def kernel(x_prompt, x_sample, cache_k, cache_v, state_ssm_re, state_ssm_im, state_conv, norm_mix_g, w_in, ssm_a_re, ssm_a_im, ssm_log_dt, ssm_b_re, ssm_b_im, ssm_c_re, ssm_c_im, ssm_d, ssm_w_glu, conv_w, conv_ln_g, conv_ln_b, attn_lq1, attn_lk1, attn_lq2, attn_lk2, attn_sub_g, w_out, norm_ffn_g, ffn_w_gate, ffn_w_up, ffn_w_down, moe_router, moe_w_gate, moe_w_up, moe_w_down, final_norm_g):
    depth = w_in.shape[0]
    n_p, l_p, _ = x_prompt.shape
    n_s, l_s, _ = x_sample.shape
    past = cache_k.shape[2]
    keep = CONV_KERNEL - 1

    xp = x_prompt.reshape(n_p * l_p, D_MODEL)
    xs = x_sample.reshape(n_s * l_s, D_MODEL)
    cache_kt = jnp.transpose(cache_k, (0, 1, 3, 4, 5, 2)).reshape(depth, n_s, 512, past)
    cache_vr = cache_v.reshape(depth, n_s, past * N_HEADS, V_DIM)
    s0_p = jnp.zeros((n_p, 1, 2 * STATE_W), F32)
    buf_p = jnp.zeros((n_p, keep, CONV_WIDTH), F32)

    ops = _s5_operators(ssm_a_re, ssm_a_im, ssm_log_dt, ssm_b_re, ssm_b_im, ssm_c_re, ssm_c_im, ssm_d, S5_CHUNK)
    kv_bufs = None
    outs = {name: [] for name in ("srp", "sip", "cp", "ks", "vs", "srs", "sis", "cs")}
    for l in range(depth):
        lam_init = 0.8 - 0.6 * math.exp(-0.3 * l)
        w_in_b = w_in[l].astype(BF16)
        wkvt = w_in_b[:, IN_WIDTH - 2 * ATTN_WIDTH:].T
        lam_params = jnp.stack([attn_lq1[l], attn_lk1[l], attn_lq2[l], attn_lk2[l]]).astype(F32)
        w_glu_b = ssm_w_glu[l].astype(BF16)
        w_out_b = w_out[l].astype(BF16)
        conv_p = (conv_w[l], conv_ln_g[l], conv_ln_b[l])

        u2, hc, q, kb, vt, *kv_bufs = _in_proj_prompt(xp, norm_mix_g[l], w_in_b, wkvt, kv_bufs, l, depth, n_p)
        y_ssm, sf_p = _s5_group(u2, n_p, l_p, s0_p, ops, l)
        cv, cb_p = _conv(hc.reshape(n_p, l_p, CONV_WIDTH), buf_p, *conv_p, tl=256)
        at = _attn_prompt(q, kb, vt, lam_params, attn_sub_g[l], n_p, l_p, lam_init)
        xp = _out_proj(xp, y_ssm, cv.reshape(n_p * l_p, CONV_WIDTH), at, w_glu_b, w_out_b)

        s0_s = jnp.concatenate([state_ssm_re[l].reshape(n_s, 1, STATE_W), state_ssm_im[l].reshape(n_s, 1, STATE_W)],
                               axis=-1).astype(F32)
        u2, hc, q, kb, vb, k_s, v_s = _in_proj_sample(xs, norm_mix_g[l], w_in_b)
        y_ssm, sf_s = _s5_group(u2, n_s, l_s, s0_s, ops, l)
        cv, cb_s = _conv(hc.reshape(n_s, l_s, CONV_WIDTH), state_conv[l].astype(F32), *conv_p, tl=l_s)
        at = _attn_sample(q, kb, vb, cache_kt, cache_vr, l, lam_params, attn_sub_g[l], lam_init)
        xs = _out_proj(xs, y_ssm, cv.reshape(n_s * l_s, CONV_WIDTH), at, w_glu_b, w_out_b)

        j = l // 2
        if l % 2 == 0:
            wg, wu, wd = ffn_w_gate[j].astype(BF16), ffn_w_up[j].astype(BF16), ffn_w_down[j].astype(BF16)
            xp = _ffn(xp, norm_ffn_g[l], wg, wu, wd)
            xs = _ffn(xs, norm_ffn_g[l], wg, wu, wd)
        else:
            router_pad = jnp.pad(moe_router[j].astype(F32), ((0, 0), (0, LANES - N_EXPERTS)))
            wg, wu, wd = moe_w_gate[j].astype(BF16), moe_w_up[j].astype(BF16), moe_w_down[j].astype(BF16)
            final_g = final_norm_g if l == depth - 1 else None
            xp = _moe(xp, norm_ffn_g[l], router_pad, wg, wu, wd, final_g)
            xs = _moe(xs, norm_ffn_g[l], router_pad, wg, wu, wd, final_g)

        outs["srp"].append(sf_p[:, 0, :STATE_W].reshape(n_p, N_SSM_GROUPS, SSM_STATE))
        outs["sip"].append(sf_p[:, 0, STATE_W:].reshape(n_p, N_SSM_GROUPS, SSM_STATE))
        outs["cp"].append(cb_p)
        outs["ks"].append(k_s.reshape(n_s, l_s, 2, N_HEADS, HEAD_DIM))
        outs["vs"].append(v_s.reshape(n_s, l_s, N_HEADS, V_DIM))
        outs["srs"].append(sf_s[:, 0, :STATE_W].reshape(n_s, N_SSM_GROUPS, SSM_STATE))
        outs["sis"].append(sf_s[:, 0, STATE_W:].reshape(n_s, N_SSM_GROUPS, SSM_STATE))
        outs["cs"].append(cb_s)

    if depth % 2 == 0:
        yp, ys = xp, xs
    else:
        yp = _final_norm(xp, final_norm_g)
        ys = _final_norm(xs, final_norm_g)
    st = {name: jnp.stack(vals) for name, vals in outs.items()}
    kt_buf, v_buf = kv_bufs
    k_prompt = jnp.transpose(kt_buf.reshape(depth, n_p, 2, N_HEADS, HEAD_DIM, l_p), (0, 1, 5, 2, 3, 4))
    v_prompt = v_buf.reshape(depth, n_p, l_p, N_HEADS, V_DIM)
    return (yp.reshape(n_p, l_p, D_MODEL), ys.reshape(n_s, l_s, D_MODEL),
            k_prompt, v_prompt, st["srp"], st["sip"], st["cp"],
            st["ks"], st["vs"], st["srs"], st["sis"], st["cs"])
```

```python
import functools
import math

import jax
import jax.numpy as jnp
from jax import lax
from jax.experimental import pallas as pl
from jax.experimental.pallas import tpu as pltpu

F32 = jnp.float32
BF16 = jnp.bfloat16

D_MODEL = 1024
CHUNK = 64
SSM_GROUP = 16
N_SSM_GROUPS = 16
SSM_WIDTH = 256
SSM_STATE = 64
STATE_W = N_SSM_GROUPS * SSM_STATE
CONV_WIDTH = 256
CONV_KERNEL = 31
N_HEADS = 4
HEAD_DIM = 64
V_DIM = 128
VT_ROWS = V_DIM + 16
QK_WIDTH = 256
ATTN_WIDTH = 512
IN_WIDTH = 2304
ATTN_SCALE = 1.0 / math.sqrt(HEAD_DIM)
LOG2E = math.log2(math.e)
NEG_INF = -1e30
D_FF = 2816
N_EXPERTS = 8
EPS = 1e-6

S5_CHUNK = 8
LANES = 128
SUBLANES = 8
VMEM_LIMIT = 48 * 1024 * 1024
MOE_VMEM_LIMIT = 58 * 1024 * 1024


def _cparams(sem):
    return pltpu.CompilerParams(dimension_semantics=sem, vmem_limit_bytes=VMEM_LIMIT)


def _rms(x, g):
    return x * lax.rsqrt(jnp.mean(x * x, axis=-1, keepdims=True) + EPS) * g


def _mixer_inputs(x_ref, g_ref, w_ref, u2_ref, hc_ref, q_ref, us_ref):
    h = _rms(x_ref[...], g_ref[...]).astype(BF16)

    def proj(lo, hi):
        return jnp.dot(h, w_ref[:, lo:hi], preferred_element_type=F32)

    u = proj(0, 256)
    n_chunks = us_ref.shape[1] // S5_CHUNK
    for c in range(SSM_WIDTH // LANES):
        us_ref[c] = u[:, c * LANES:(c + 1) * LANES]
        for j in range(S5_CHUNK):
            lo = j * SSM_WIDTH + c * LANES
            u2_ref[:, lo:lo + LANES] = us_ref[c, pl.ds(j, n_chunks, stride=S5_CHUNK), :]
    c_val = proj(256, 512)
    c_gate = proj(512, 768)
    hc_ref[...] = c_val * jax.nn.sigmoid(c_gate)
    q_ref[...] = (proj(768, 1280) * (ATTN_SCALE * LOG2E)).astype(BF16)
    return h, proj(1280, 1792), proj(1792, 2304)


def _in_proj_prompt_kernel(*refs, aliased):
    x_ref, g_ref, w_ref, wkvt_ref = refs[:4]
    u2_ref, hc_ref, q_ref, kb_ref, vt_ref, kt_ref, v_ref, us_ref = refs[6 if aliased else 4:]
    h, k, v = _mixer_inputs(x_ref, g_ref, w_ref, u2_ref, hc_ref, q_ref, us_ref)
    kb_ref[...] = k.astype(BF16)
    for head in range(N_HEADS):
        v_ref[pl.ds(head, v.shape[0], stride=N_HEADS), :] = v[:, head * V_DIM:(head + 1) * V_DIM]
    nt = (((1,), (1,)), ((), ()))
    kt_ref[...] = lax.dot_general(wkvt_ref[:ATTN_WIDTH, :], h, nt, preferred_element_type=F32)
    vt = lax.dot_general(wkvt_ref[ATTN_WIDTH:, :], h, nt, preferred_element_type=F32).astype(BF16)
    for head in range(N_HEADS):
        vt_ref[head * VT_ROWS:head * VT_ROWS + V_DIM, :] = vt[head * V_DIM:(head + 1) * V_DIM, :]
        vt_ref[head * VT_ROWS + V_DIM:(head + 1) * VT_ROWS, :] = jnp.ones((VT_ROWS - V_DIM, vt.shape[1]), BF16)


def _in_proj_prompt(x, g, w_bf16, wkvt_bf16, kv_bufs, layer, depth, n_seq, tm=512):
    t = x.shape[0]
    seq_len = t // n_seq
    per_seq = seq_len // tm
    row = lambda width: pl.BlockSpec((tm, width), lambda i: (i, 0))
    in_specs = [row(D_MODEL), pl.BlockSpec((1, D_MODEL), lambda i: (0, 0)),
                pl.BlockSpec((D_MODEL, IN_WIDTH), lambda i: (0, 0)),
                pl.BlockSpec((2 * ATTN_WIDTH, D_MODEL), lambda i: (0, 0))]
    args = [x, g.reshape(1, D_MODEL), w_bf16, wkvt_bf16]
    aliases = {}
    if kv_bufs is not None:
        in_specs += [pl.BlockSpec(memory_space=pl.ANY), pl.BlockSpec(memory_space=pl.ANY)]
        args += list(kv_bufs)
        aliases = {4: 5, 5: 6}
    return pl.pallas_call(
        functools.partial(_in_proj_prompt_kernel, aliased=kv_bufs is not None),
        grid=(t // tm,),
        in_specs=in_specs,
        out_specs=[pl.BlockSpec((tm // S5_CHUNK, S5_CHUNK * SSM_WIDTH), lambda i: (i, 0)),
                   row(256), row(512), row(512),
                   pl.BlockSpec((N_HEADS * VT_ROWS, tm), lambda i: (0, i)),
                   pl.BlockSpec((None, None, ATTN_WIDTH, tm), lambda i: (layer, i // per_seq, 0, i % per_seq)),
                   pl.BlockSpec((None, None, tm * N_HEADS, V_DIM), lambda i: (layer, i // per_seq, i % per_seq, 0))],
        out_shape=[jax.ShapeDtypeStruct((t // S5_CHUNK, S5_CHUNK * SSM_WIDTH), F32),
                   jax.ShapeDtypeStruct((t, 256), F32),
                   jax.ShapeDtypeStruct((t, 512), BF16), jax.ShapeDtypeStruct((t, 512), BF16),
                   jax.ShapeDtypeStruct((N_HEADS * VT_ROWS, t), BF16),
                   jax.ShapeDtypeStruct((depth, n_seq, ATTN_WIDTH, seq_len), F32),
                   jax.ShapeDtypeStruct((depth, n_seq, seq_len * N_HEADS, V_DIM), F32)],
        scratch_shapes=[pltpu.VMEM((SSM_WIDTH // LANES, tm, LANES), F32)],
        input_output_aliases=aliases,
        compiler_params=_cparams(("parallel",)),
        name="in_proj_prompt",
    )(*args)


def _in_proj_sample_kernel(x_ref, g_ref, w_ref, u2_ref, hc_ref, q_ref, kb_ref, vb_ref, k_ref, v_ref, us_ref):
    _, k, v = _mixer_inputs(x_ref, g_ref, w_ref, u2_ref, hc_ref, q_ref, us_ref)
    k_ref[...] = k
    kb_ref[...] = k.astype(BF16)
    v_ref[...] = v
    vb_ref[...] = v.astype(BF16)


def _in_proj_sample(x, g, w_bf16, tm=512):
    t = x.shape[0]
    tm = min(tm, t)
    row = lambda width: pl.BlockSpec((tm, width), lambda i: (i, 0))
    return pl.pallas_call(
        _in_proj_sample_kernel,
        grid=(t // tm,),
        in_specs=[row(D_MODEL), pl.BlockSpec((1, D_MODEL), lambda i: (0, 0)),
                  pl.BlockSpec((D_MODEL, IN_WIDTH), lambda i: (0, 0))],
        out_specs=[pl.BlockSpec((tm // S5_CHUNK, S5_CHUNK * SSM_WIDTH), lambda i: (i, 0)),
                   row(256), row(512), row(512), row(512), row(512), row(512)],
        out_shape=[jax.ShapeDtypeStruct((t // S5_CHUNK, S5_CHUNK * SSM_WIDTH), F32),
                   jax.ShapeDtypeStruct((t, 256), F32),
                   jax.ShapeDtypeStruct((t, 512), BF16), jax.ShapeDtypeStruct((t, 512), BF16),
                   jax.ShapeDtypeStruct((t, 512), BF16), jax.ShapeDtypeStruct((t, 512), F32),
                   jax.ShapeDtypeStruct((t, 512), F32)],
        scratch_shapes=[pltpu.VMEM((SSM_WIDTH // LANES, tm, LANES), F32)],
        compiler_params=_cparams(("parallel",)),
        name="in_proj_sample",
    )(x, g.reshape(1, D_MODEL), w_bf16)


def _mm_kernel(x_ref, w_ref, o_ref):
    o_ref[...] = jnp.dot(x_ref[...].astype(BF16), w_ref[...], preferred_element_type=F32)


def _mm(x, w_bf16, layer, tm, tn, name):
    m, k = x.shape
    n = w_bf16.shape[2]
    return pl.pallas_call(
        _mm_kernel,
        grid=(n // tn, m // tm),
        in_specs=[pl.BlockSpec((tm, k), lambda j, i: (i, 0)),
                  pl.BlockSpec((None, k, tn), lambda j, i: (layer, 0, j))],
        out_specs=pl.BlockSpec((tm, tn), lambda j, i: (i, j)),
        out_shape=jax.ShapeDtypeStruct((m, n), F32),
        compiler_params=_cparams(("parallel", "parallel")),
        name=name,
    )(x, w_bf16)


def _s5_scan_kernel(d_ref, s0_ref, a_ref, ss_ref, sf_ref, st_ref, *, rows):
    t = pl.program_id(1)

    @pl.when(t == 0)
    def _():
        st_ref[...] = s0_ref[...]

    a_re = a_ref[:, :STATE_W]
    a_im = a_ref[:, STATE_W:]

    def step(c, carry):
        s_re, s_im = carry
        ss_ref[pl.ds(c, 1), :STATE_W] = s_re
        ss_ref[pl.ds(c, 1), STATE_W:] = s_im
        d = d_ref[pl.ds(c, 1), :]
        n_re = a_re * s_re - a_im * s_im + d[:, :STATE_W]
        n_im = a_re * s_im + a_im * s_re + d[:, STATE_W:]
        return n_re, n_im

    s_re, s_im = lax.fori_loop(0, rows, step, (st_ref[:, :STATE_W], st_ref[:, STATE_W:]))
    st_ref[:, :STATE_W] = s_re
    st_ref[:, STATE_W:] = s_im
    sf_ref[...] = st_ref[...]


def _s5_scan(d, n_seq, n_chunks, s0, a_c, layer):
    rows = min(n_chunks, 256)
    nt = n_chunks // rows
    w = 2 * STATE_W
    return pl.pallas_call(
        functools.partial(_s5_scan_kernel, rows=rows),
        grid=(n_seq, nt),
        in_specs=[pl.BlockSpec((rows, w), lambda s, t: (s * nt + t, 0)),
                  pl.BlockSpec((None, 1, w), lambda s, t: (s, 0, 0)),
                  pl.BlockSpec((None, 1, w), lambda s, t: (layer, 0, 0))],
        out_specs=[pl.BlockSpec((rows, w), lambda s, t: (s * nt + t, 0)),
                   pl.BlockSpec((None, 1, w), lambda s, t: (s, 0, 0))],
        out_shape=[jax.ShapeDtypeStruct((n_seq * n_chunks, w), F32), jax.ShapeDtypeStruct((n_seq, 1, w), F32)],
        scratch_shapes=[pltpu.VMEM((1, w), F32)],
        compiler_params=_cparams(("parallel", "arbitrary")),
        name="s5_scan",
    )(d, s0, a_c)


def _s5_out_kernel(u_ref, ut_ref, ss_ref, wi_ref, ws_ref, dsk_ref, y_ref):
    y_ref[...] = (jnp.dot(u_ref[...].astype(BF16), wi_ref[...], preferred_element_type=F32)
                  + jnp.dot(ss_ref[...].astype(BF16), ws_ref[...], preferred_element_type=F32)
                  + dsk_ref[...] * ut_ref[...])


def _s5_out(u2, s_start, w_intra, w_inter, dsk, layer, tm=256, tn=1024):
    m, n = u2.shape
    k = s_start.shape[1]
    return pl.pallas_call(
        _s5_out_kernel,
        grid=(n // tn, m // tm),
        in_specs=[pl.BlockSpec((tm, n), lambda j, i: (i, 0)), pl.BlockSpec((tm, tn), lambda j, i: (i, j)),
                  pl.BlockSpec((tm, k), lambda j, i: (i, 0)),
                  pl.BlockSpec((None, n, tn), lambda j, i: (layer, 0, j)),
                  pl.BlockSpec((None, k, tn), lambda j, i: (layer, 0, j)),
                  pl.BlockSpec((None, 1, tn), lambda j, i: (layer, 0, j))],
        out_specs=pl.BlockSpec((tm, tn), lambda j, i: (i, j)),
        out_shape=jax.ShapeDtypeStruct((m, n), F32),
        compiler_params=_cparams(("parallel", "parallel")),
        name="s5_out",
    )(u2, u2, s_start, w_intra, w_inter, dsk)


def _s5_operators(a_re, a_im, log_dt, b_re, b_im, c_re, c_im, d_skip, n_c):
    f32 = lambda x: x.astype(F32)
    a_re, a_im, b_re, b_im, c_re, c_im = map(f32, (a_re, a_im, b_re, b_im, c_re, c_im))
    n_l = a_re.shape[0]
    g_n = N_SSM_GROUPS
    dt = jnp.exp(f32(log_dt))[:, None, :, None]
    ks = jnp.arange(n_c + 1, dtype=F32)[None, :, None, None]
    mag = jnp.exp(a_re[:, None] * dt * ks)
    ang = a_im[:, None] * dt * ks
    pw_re, pw_im = mag * jnp.cos(ang), mag * jnp.sin(ang)
    den = a_re * a_re + a_im * a_im
    q_re = ((pw_re[:, 1] - 1.0) * a_re + pw_im[:, 1] * a_im) / den
    q_im = (pw_im[:, 1] * a_re - (pw_re[:, 1] - 1.0) * a_im) / den
    bb_re = q_re[..., None] * b_re - q_im[..., None] * b_im
    bb_im = q_re[..., None] * b_im + q_im[..., None] * b_re
    eye_g = jnp.eye(g_n, dtype=F32)

    def block_diag(x, rows, cols):
        return (x[:, :, :, None, :] * eye_g[None, :, None, :, None]).reshape(n_l, rows, cols)

    bt_re = block_diag(jnp.transpose(bb_re, (0, 1, 3, 2)), SSM_WIDTH, STATE_W)[:, None]
    bt_im = block_diag(jnp.transpose(bb_im, (0, 1, 3, 2)), SSM_WIDTH, STATE_W)[:, None]
    pj_re = pw_re[:, :n_c][:, ::-1].reshape(n_l, n_c, 1, STATE_W)
    pj_im = pw_im[:, :n_c][:, ::-1].reshape(n_l, n_c, 1, STATE_W)
    w_state = jnp.concatenate([(pj_re * bt_re - pj_im * bt_im).reshape(n_l, n_c * SSM_WIDTH, STATE_W),
                               (pj_re * bt_im + pj_im * bt_re).reshape(n_l, n_c * SSM_WIDTH, STATE_W)], axis=2)

    m_re = c_re[:, None] * pw_re[:, :n_c, :, None, :] - c_im[:, None] * pw_im[:, :n_c, :, None, :]
    m_im = c_re[:, None] * pw_im[:, :n_c, :, None, :] + c_im[:, None] * pw_re[:, :n_c, :, None, :]
    hi = lax.Precision.HIGHEST
    ker = (jnp.einsum('lkghp,lgpx->lkghx', m_re, bb_re, precision=hi)
           - jnp.einsum('lkghp,lgpx->lkghx', m_im, bb_im, precision=hi))
    blk = jnp.transpose(ker, (0, 1, 2, 4, 3))
    blk = (blk[:, :, :, :, None, :] * eye_g[None, None, :, None, :, None]).reshape(n_l, n_c, 256, 256).astype(BF16)
    zero = jnp.zeros((n_l, 256, 256), BF16)
    w_intra = jnp.concatenate(
        [jnp.concatenate([blk[:, t - j] if t >= j else zero for t in range(n_c)], axis=2) for j in range(n_c)],
        axis=1)

    ct_re = block_diag(jnp.transpose(c_re, (0, 1, 3, 2)), STATE_W, SSM_WIDTH)
    ct_im = block_diag(jnp.transpose(c_im, (0, 1, 3, 2)), STATE_W, SSM_WIDTH)
    pt_re = pw_re[:, 1:].reshape(n_l, n_c, STATE_W, 1)
    pt_im = pw_im[:, 1:].reshape(n_l, n_c, STATE_W, 1)
    top = jnp.concatenate([pt_re[:, t] * ct_re - pt_im[:, t] * ct_im for t in range(n_c)], axis=2)
    bot = jnp.concatenate([-(pt_re[:, t] * ct_im + pt_im[:, t] * ct_re) for t in range(n_c)], axis=2)
    w_inter = jnp.concatenate([top, bot], axis=1)

    a_c = jnp.concatenate([pw_re[:, n_c].reshape(n_l, 1, STATE_W), pw_im[:, n_c].reshape(n_l, 1, STATE_W)], axis=2)
    dsk = jnp.tile(f32(d_skip).reshape(n_l, 1, SSM_WIDTH), (1, 1, n_c))
    return w_state.astype(BF16), w_intra, w_inter.astype(BF16), a_c, dsk


def _s5_group(u2, n_seq, seq_len, s0, ops, layer):
    w_state, w_intra, w_inter, a_c, dsk = ops
    n_chunks = seq_len // S5_CHUNK
    d_state = _mm(u2, w_state, layer, 256, 1024, "s5_state")
    s_start, s_final = _s5_scan(d_state, n_seq, n_chunks, s0, a_c, layer)
    return _s5_out(u2, s_start, w_intra, w_inter, dsk, layer), s_final


_CONV_PAD = 32
_CONV_RB = 64


def _conv_kernel(h_ref, buf_ref, w_ref, g_ref, b_ref, y_ref, nb_ref, xp_ref, xs_ref, *, tl):
    t = pl.program_id(1)
    keep = CONV_KERNEL - 1
    lo = _CONV_PAD - keep

    @pl.when(t == 0)
    def _():
        xp_ref[lo:_CONV_PAD, :] = buf_ref[...]

    @pl.when(t > 0)
    def _():
        xp_ref[lo:_CONV_PAD, :] = xp_ref[tl + lo:tl + _CONV_PAD, :]

    xp_ref[_CONV_PAD:_CONV_PAD + tl, :] = h_ref[...]
    for phase in range(SUBLANES):
        span = tl + (keep - phase) // SUBLANES * SUBLANES
        xs_ref[phase, :span, :] = xp_ref[lo + phase:lo + phase + span, :]
    for r in range(tl // _CONV_RB):
        base = r * _CONV_RB
        acc = jnp.zeros((_CONV_RB, CONV_WIDTH), F32)
        for k in range(CONV_KERNEL):
            start = base + k - k % SUBLANES
            acc = acc + w_ref[k:k + 1, :] * xs_ref[k % SUBLANES, start:start + _CONV_RB, :]
        xc = acc - jnp.mean(acc, axis=-1, keepdims=True)
        var = jnp.mean(xc * xc, axis=-1, keepdims=True)
        y = xc * lax.rsqrt(var + EPS) * g_ref[...] + b_ref[...]
        y_ref[base:base + _CONV_RB, :] = y * jax.nn.sigmoid(y)
    nb_ref[...] = xp_ref[tl + lo:tl + _CONV_PAD, :]


def _conv(h, buf, w, ln_g, ln_b, tl):
    s, l, c = h.shape
    keep = CONV_KERNEL - 1
    return pl.pallas_call(
        functools.partial(_conv_kernel, tl=tl),
        grid=(s, l // tl),
        in_specs=[pl.BlockSpec((None, tl, c), lambda i, t: (i, t, 0)),
                  pl.BlockSpec((None, keep, c), lambda i, t: (i, 0, 0)),
                  pl.BlockSpec((CONV_KERNEL, c), lambda i, t: (0, 0)),
                  pl.BlockSpec((1, c), lambda i, t: (0, 0)), pl.BlockSpec((1, c), lambda i, t: (0, 0))],
        out_specs=[pl.BlockSpec((None, tl, c), lambda i, t: (i, t, 0)),
                   pl.BlockSpec((None, keep, c), lambda i, t: (i, 0, 0))],
        out_shape=[jax.ShapeDtypeStruct((s, l, c), F32), jax.ShapeDtypeStruct((s, keep, c), F32)],
        scratch_shapes=[pltpu.VMEM((tl + _CONV_PAD, c), F32),
                        pltpu.VMEM((SUBLANES, tl + (keep // SUBLANES) * SUBLANES, c), F32)],
        compiler_params=_cparams(("parallel", "arbitrary")),
        name="conv_module",
    )(h, buf, w, ln_g.reshape(1, c), ln_b.reshape(1, c))


def _stack_q(q_ref, qs_ref):
    tq = q_ref.shape[0]
    lane = lax.broadcasted_iota(jnp.int32, (tq, LANES), 1)
    for b in range(4):
        qb = q_ref[:, b * LANES:(b + 1) * LANES]
        qs_ref[b, :tq, :] = jnp.where(lane < HEAD_DIM, qb, jnp.zeros_like(qb))
        qs_ref[b, tq:, :] = jnp.where(lane >= HEAD_DIM, qb, jnp.zeros_like(qb))


def _attn_update(score, value, m_ref, l_ref, acc_ref, tq):
    def softmax(b, s):
        m_old = m_ref[b]
        m_new = jnp.maximum(m_old, jnp.max(s, axis=-1, keepdims=True))
        alpha = jnp.exp2(m_old - m_new)
        p = jnp.exp2(s - m_new)
        l_ref[b] = alpha * l_ref[b] + jnp.sum(p, axis=-1, keepdims=True)
        m_ref[b] = m_new
        return alpha, p.astype(BF16)

    def values(b, alpha, pb):
        h0 = 2 * (b % 2)
        pv0 = jnp.dot(pb[:tq], value(h0), preferred_element_type=F32)
        pv1 = jnp.dot(pb[tq:], value(h0 + 1), preferred_element_type=F32)
        acc_ref[b, :tq, :] = alpha[:tq] * acc_ref[b, :tq, :] + pv0
        acc_ref[b, tq:, :] = alpha[tq:] * acc_ref[b, tq:, :] + pv1

    s0 = score(0)
    s1 = score(1)
    s2 = score(2)
    a0, p0 = softmax(0, s0)
    s3 = score(3)
    a1, p1 = softmax(1, s1)
    values(0, a0, p0)
    a2, p2 = softmax(2, s2)
    values(1, a1, p1)
    a3, p3 = softmax(3, s3)
    values(2, a2, p2)
    values(3, a3, p3)


def _attn_finish(lp_ref, sg_ref, l_ref, acc_ref, o_ref, tq, lam_init):
    lam = _lambda(lp_ref, lam_init)
    for h in range(N_HEADS):
        b1, half = h // 2, h % 2
        rows = slice(half * tq, (half + 1) * tq)
        o1 = acc_ref[b1, rows, :] / l_ref[b1, rows, :]
        o2 = acc_ref[b1 + 2, rows, :] / l_ref[b1 + 2, rows, :]
        o = o1 - lam * o2
        o_ref[:, h * V_DIM:(h + 1) * V_DIM] = (_rms(o, sg_ref[...]) * (1.0 - lam_init)).astype(o_ref.dtype)


def _attn_init(m_ref, l_ref, acc_ref):
    m_ref[...] = jnp.full(m_ref.shape, NEG_INF, F32)
    l_ref[...] = jnp.zeros(l_ref.shape, F32)
    acc_ref[...] = jnp.zeros(acc_ref.shape, F32)


def _attn_prompt_update(qs_ref, k_ref, vt_ref, m_ref, acc_ref, tq, mask):
    def scores(c):
        b, r = divmod(c, 2)
        s = lax.dot_general(k_ref[:, b * LANES:(b + 1) * LANES], qs_ref[b, r * tq:(r + 1) * tq, :],
                            (((1,), (1,)), ((), ())), preferred_element_type=F32)
        return s if mask is None else jnp.where(mask, s, NEG_INF)

    def softmax(c, s):
        b, r = divmod(c, 2)
        cols = slice(r * tq, (r + 1) * tq)
        sb = s.astype(BF16)
        m_old = m_ref[b, :, cols]
        m_new = jnp.maximum(m_old, jnp.max(sb, axis=0, keepdims=True).astype(F32))
        m_ref[b, :, cols] = m_new
        return jnp.exp2(m_old - m_new), jnp.exp2(sb - m_new.astype(BF16))

    def values(c, alpha, pb):
        h = 2 * ((c // 2) % 2) + c % 2
        pv = jnp.dot(vt_ref[h * VT_ROWS:(h + 1) * VT_ROWS, :], pb, preferred_element_type=F32)
        acc_ref[c] = alpha * acc_ref[c] + pv

    n_chain, ahead = 8, 3
    pending = [scores(c) for c in range(ahead)]
    for c in range(n_chain):
        if c + ahead < n_chain:
            pending.append(scores(c + ahead))
        alpha, pb = softmax(c, pending.pop(0))
        values(c, alpha, pb)


def _lambda(lp_ref, lam_init):
    lp = lp_ref[...]
    return (jnp.exp(jnp.sum(lp[0:1] * lp[1:2], axis=-1, keepdims=True))
            - jnp.exp(jnp.sum(lp[2:3] * lp[3:4], axis=-1, keepdims=True)) + lam_init)


def _attn_prompt_finish(lp_ref, sgt_ref, acc_ref, o_ref, tq, lam_init):
    lam = _lambda(lp_ref, lam_init)
    for h in range(N_HEADS):
        i1 = 2 * (h // 2) + h % 2
        o1 = acc_ref[i1, :V_DIM, :] / acc_ref[i1, V_DIM:V_DIM + 1, :]
        o2 = acc_ref[i1 + 4, :V_DIM, :] / acc_ref[i1 + 4, V_DIM:V_DIM + 1, :]
        o = o1 - lam * o2
        on = o * lax.rsqrt(jnp.mean(o * o, axis=0, keepdims=True) + EPS) * sgt_ref[...] * (1.0 - lam_init)
        o_ref[:, h * V_DIM:(h + 1) * V_DIM] = on.T.astype(o_ref.dtype)


def _attn_prompt_kernel(it_ref, jt_ref, lp_ref, sgt_ref, q_ref, k_ref, vt_ref, o_ref, qs_ref, m_ref, acc_ref,
                        *, tq, lam_init):
    step = pl.program_id(1)
    i = it_ref[step]
    j = jt_ref[step]

    @pl.when(j == 0)
    def _():
        m_ref[...] = jnp.full(m_ref.shape, NEG_INF, F32)
        acc_ref[...] = jnp.zeros(acc_ref.shape, F32)
        _stack_q(q_ref, qs_ref)

    @pl.when(j < i)
    def _():
        _attn_prompt_update(qs_ref, k_ref, vt_ref, m_ref, acc_ref, tq, None)

    @pl.when(j == i)
    def _():
        tk = k_ref.shape[0]
        key_chunk = lax.broadcasted_iota(jnp.int32, (tk, tq), 0) // CHUNK
        qry_chunk = lax.broadcasted_iota(jnp.int32, (tk, tq), 1) // CHUNK
        _attn_prompt_update(qs_ref, k_ref, vt_ref, m_ref, acc_ref, tq, qry_chunk >= key_chunk)
        _attn_prompt_finish(lp_ref, sgt_ref, acc_ref, o_ref, tq, lam_init)


def _attn_prompt(q, kb, vt, lam_params, sub_g, n_seq, seq_len, lam_init, tq=512):
    nq = seq_len // tq
    pairs = [(i, j) for i in range(nq) for j in range(i + 1)]
    i_tab = jnp.asarray([p[0] for p in pairs], jnp.int32)
    j_tab = jnp.asarray([p[1] for p in pairs], jnp.int32)
    grid_spec = pltpu.PrefetchScalarGridSpec(
        num_scalar_prefetch=2,
        grid=(n_seq, len(pairs)),
        in_specs=[pl.BlockSpec((4, HEAD_DIM), lambda b, s, it, jt: (0, 0)),
                  pl.BlockSpec((V_DIM, 1), lambda b, s, it, jt: (0, 0)),
                  pl.BlockSpec((tq, 512), lambda b, s, it, jt: (b * nq + it[s], 0)),
                  pl.BlockSpec((tq, 512), lambda b, s, it, jt: (b * nq + jt[s], 0)),
                  pl.BlockSpec((N_HEADS * VT_ROWS, tq), lambda b, s, it, jt: (0, b * nq + jt[s]))],
        out_specs=pl.BlockSpec((tq, 512), lambda b, s, it, jt: (b * nq + it[s], 0)),
        scratch_shapes=[pltpu.VMEM((4, 2 * tq, LANES), BF16), pltpu.VMEM((4, 1, 2 * tq), F32),
                        pltpu.VMEM((8, VT_ROWS, tq), F32)],
    )
    return pl.pallas_call(
        functools.partial(_attn_prompt_kernel, tq=tq, lam_init=lam_init),
        grid_spec=grid_spec,
        out_shape=jax.ShapeDtypeStruct((n_seq * seq_len, 512), BF16),
        compiler_params=_cparams(("parallel", "arbitrary")),
        name="attn_prompt",
    )(i_tab, j_tab, lam_params, sub_g.reshape(V_DIM, 1), q, kb, vt)


def _attn_sample_kernel(lp_ref, sg_ref, q_ref, ckt_ref, cv_ref, kn_ref, vn_ref, o_ref, qs_ref, m_ref, l_ref, acc_ref,
                        *, tq, tk, lam_init):
    j = pl.program_id(1)

    @pl.when(j == 0)
    def _():
        _attn_init(m_ref, l_ref, acc_ref)
        _stack_q(q_ref, qs_ref)

    def cache_score(b):
        return jnp.dot(qs_ref[b], ckt_ref[b * LANES:(b + 1) * LANES, :].astype(BF16), preferred_element_type=F32)

    def cache_value(h):
        return cv_ref[pl.ds(h, tk, stride=N_HEADS), :].astype(BF16)

    _attn_update(cache_score, cache_value, m_ref, l_ref, acc_ref, tq)

    @pl.when(j == pl.num_programs(1) - 1)
    def _():
        def new_score(b):
            return lax.dot_general(qs_ref[b], kn_ref[:, b * LANES:(b + 1) * LANES], (((1,), (1,)), ((), ())),
                                   preferred_element_type=F32)

        def new_value(h):
            return vn_ref[:, h * V_DIM:(h + 1) * V_DIM]

        _attn_update(new_score, new_value, m_ref, l_ref, acc_ref, tq)
        _attn_finish(lp_ref, sg_ref, l_ref, acc_ref, o_ref, tq, lam_init)


def _attn_sample(q, kb, vb, cache_kt, cache_v, layer, lam_params, sub_g, lam_init, tk=2048):
    _, n_seq, _, past = cache_kt.shape
    tk = min(tk, past)
    assert past % tk == 0
    tq = q.shape[0] // n_seq
    new = pl.BlockSpec((tq, 512), lambda b, j: (b, 0))
    return pl.pallas_call(
        functools.partial(_attn_sample_kernel, tq=tq, tk=tk, lam_init=lam_init),
        grid=(n_seq, past // tk),
        in_specs=[pl.BlockSpec((4, HEAD_DIM), lambda b, j: (0, 0)), pl.BlockSpec((1, V_DIM), lambda b, j: (0, 0)),
                  new,
                  pl.BlockSpec((None, None, 512, tk), lambda b, j: (layer, b, 0, j)),
                  pl.BlockSpec((None, None, tk * N_HEADS, V_DIM), lambda b, j: (layer, b, j, 0)),
                  new, new],
        out_specs=new,
        out_shape=jax.ShapeDtypeStruct((n_seq * tq, 512), BF16),
        scratch_shapes=[pltpu.VMEM((4, 2 * tq, LANES), BF16), pltpu.VMEM((4, 2 * tq, 1), F32),
                        pltpu.VMEM((4, 2 * tq, 1), F32), pltpu.VMEM((4, 2 * tq, V_DIM), F32)],
        compiler_params=_cparams(("parallel", "arbitrary")),
        name="attn_sample",
    )(lam_params, sub_g.reshape(1, V_DIM), q, cache_kt, cache_v, kb, vb)


def _gelu_tanh(x):
    return 0.5 * x * (1.0 + jnp.tanh(math.sqrt(2.0 / math.pi) * (x + 0.044715 * (x * x * x))))


def _out_proj_kernel(x_ref, y2_ref, cv_ref, at_ref, wglu_ref, wo_ref, o_ref, ys_ref):
    n_chunks = y2_ref.shape[0]
    for c in range(SSM_WIDTH // LANES):
        for j in range(S5_CHUNK):
            lo = j * SSM_WIDTH + c * LANES
            ys_ref[c, pl.ds(j, n_chunks, stride=S5_CHUNK), :] = y2_ref[:, lo:lo + LANES]
    z = _gelu_tanh(jnp.concatenate([ys_ref[c] for c in range(SSM_WIDTH // LANES)], axis=1))
    gate = jax.nn.sigmoid(jnp.dot(z.astype(BF16), wglu_ref[...], preferred_element_type=F32))
    ssm = (z * gate).astype(BF16)
    acc = jnp.dot(ssm, wo_ref[0:256, :], preferred_element_type=F32)
    acc = acc + jnp.dot(cv_ref[...].astype(BF16), wo_ref[256:512, :], preferred_element_type=F32)
    acc = acc + jnp.dot(at_ref[...], wo_ref[512:1024, :], preferred_element_type=F32)
    o_ref[...] = x_ref[...] + acc


def _out_proj(x, y_ssm, conv_out, attn_out, w_glu, w_out, tm=512):
    t = x.shape[0]
    tm = min(tm, t)
    row = lambda width: pl.BlockSpec((tm, width), lambda i: (i, 0))
    return pl.pallas_call(
        _out_proj_kernel,
        grid=(t // tm,),
        in_specs=[row(D_MODEL), pl.BlockSpec((tm // S5_CHUNK, S5_CHUNK * SSM_WIDTH), lambda i: (i, 0)),
                  row(256), row(512),
                  pl.BlockSpec((256, 256), lambda i: (0, 0)), pl.BlockSpec((D_MODEL, D_MODEL), lambda i: (0, 0))],
        out_specs=row(D_MODEL),
        out_shape=jax.ShapeDtypeStruct((t, D_MODEL), F32),
        scratch_shapes=[pltpu.VMEM((SSM_WIDTH // LANES, tm, LANES), F32)],
        compiler_params=_cparams(("parallel",)),
        name="out_proj",
    )(x, y_ssm, conv_out, attn_out, w_glu, w_out)


def _swiglu_step(h, wg, wu, wd):
    g = jnp.dot(h, wg, preferred_element_type=F32)
    u = jnp.dot(h, wu, preferred_element_type=F32)
    a = (g * jax.nn.sigmoid(g) * u).astype(BF16)
    return jnp.dot(a, wd, preferred_element_type=F32)


_FFN_SUB = 512


def _ffn_kernel(x_ref, g_ref, wg_ref, wu_ref, wd_ref, o_ref, h_ref, a_ref, acc_ref):
    j = pl.program_id(1)
    tf = a_ref.shape[1]

    @pl.when(j == 0)
    def _():
        h_ref[...] = _rms(x_ref[...], g_ref[...]).astype(BF16)

    for lo in range(0, tf, _FFN_SUB):
        cols = slice(lo, min(lo + _FFN_SUB, tf))
        gate = jnp.dot(h_ref[...], wg_ref[:, cols], preferred_element_type=F32)
        up = jnp.dot(h_ref[...], wu_ref[:, cols], preferred_element_type=F32)
        a_ref[:, cols] = (gate * jax.nn.sigmoid(gate) * up).astype(BF16)
    y = jnp.dot(a_ref[...], wd_ref[...], preferred_element_type=F32)

    @pl.when(j == 0)
    def _():
        acc_ref[...] = y

    @pl.when(j > 0)
    def _():
        acc_ref[...] += y

    @pl.when(j == pl.num_programs(1) - 1)
    def _():
        o_ref[...] = x_ref[...] + acc_ref[...]


def _ffn(x, g, wg, wu, wd, tm=1024, tf=1408):
    t = x.shape[0]
    tm = min(tm, t)
    return pl.pallas_call(
        _ffn_kernel,
        grid=(t // tm, D_FF // tf),
        in_specs=[pl.BlockSpec((tm, D_MODEL), lambda i, j: (i, 0)), pl.BlockSpec((1, D_MODEL), lambda i, j: (0, 0)),
                  pl.BlockSpec((D_MODEL, tf), lambda i, j: (0, j)), pl.BlockSpec((D_MODEL, tf), lambda i, j: (0, j)),
                  pl.BlockSpec((tf, D_MODEL), lambda i, j: (j, 0))],
        out_specs=pl.BlockSpec((tm, D_MODEL), lambda i, j: (i, 0)),
        out_shape=jax.ShapeDtypeStruct((t, D_MODEL), F32),
        scratch_shapes=[pltpu.VMEM((tm, D_MODEL), BF16), pltpu.VMEM((tm, tf), BF16), pltpu.VMEM((tm, D_MODEL), F32)],
        compiler_params=_cparams(("parallel", "arbitrary")),
        name="ffn_dense",
    )(x, g.reshape(1, D_MODEL), wg, wu, wd)


_MOE_PREFIX = 256


def _moe_route(x_ref, g_ref, r_ref, o_ref, hb_ref, rank_ref, sel_ref, gate_ref, cnt_ref):
    n_blk = x_ref.shape[0] // _MOE_PREFIX

    for blk in range(n_blk):
        rows = slice(blk * _MOE_PREFIX, (blk + 1) * _MOE_PREFIX)
        x = x_ref[rows, :]
        h32 = _rms(x, g_ref[...])
        hb_ref[rows, :] = h32.astype(BF16)
        o_ref[rows, :] = x
        logits = jnp.dot(h32, r_ref[...], preferred_element_type=F32, precision=lax.Precision.HIGHEST)
        rank_ref[:, rows] = logits.T[:N_EXPERTS, :]
    lt = rank_ref[...]
    row = lax.broadcasted_iota(jnp.int32, lt.shape, 0)
    m1 = jnp.max(lt, axis=0, keepdims=True)
    i1 = jnp.min(jnp.where(lt == m1, row, N_EXPERTS), axis=0, keepdims=True)
    lt2 = jnp.where(row == i1, -jnp.inf, lt)
    m2 = jnp.max(lt2, axis=0, keepdims=True)
    i2 = jnp.min(jnp.where(lt2 == m2, row, N_EXPERTS), axis=0, keepdims=True)
    e2 = jnp.exp(m2 - m1)
    den = 1.0 + e2
    gate_ref[...] = jnp.where(row == i1, 1.0 / den, 0.0) + jnp.where(row == i2, e2 / den, 0.0)
    sel = jnp.where(row == i1, 1.0, 0.0) + jnp.where(row == i2, 1.0, 0.0)
    sel_ref[...] = sel
    before = (lax.broadcasted_iota(jnp.int32, (_MOE_PREFIX, _MOE_PREFIX), 0)
              < lax.broadcasted_iota(jnp.int32, (_MOE_PREFIX, _MOE_PREFIX), 1))
    tri = jnp.where(before, 1.0, 0.0).astype(BF16)
    carry = jnp.zeros((N_EXPERTS, 1), F32)
    for blk in range(lt.shape[1] // _MOE_PREFIX):
        cols = slice(blk * _MOE_PREFIX, (blk + 1) * _MOE_PREFIX)
        rank_ref[:, cols] = jnp.dot(sel[:, cols].astype(BF16), tri, preferred_element_type=F32) + carry
        carry = carry + jnp.sum(sel[:, cols], axis=1, keepdims=True)
    for k in range(N_EXPERTS):
        cnt_ref[k] = carry[k, 0].astype(jnp.int32)


def _moe_kernel(x_ref, g_ref, r_ref, gf_ref, wg_ref, wu_ref, wd_ref, o_ref, hb_ref, rank_ref, sel_ref, gate_ref,
                xe_ref, ye_ref, cnt_ref, *, tb, g_rows, f_rows, s_rows, final_norm):
    e = pl.program_id(1)
    f = pl.program_id(2)
    last_f = pl.num_programs(2) - 1

    @pl.when((e == 0) & (f == 0))
    def _():
        _moe_route(x_ref, g_ref, r_ref, o_ref, hb_ref, rank_ref, sel_ref, gate_ref, cnt_ref)

    cnt = cnt_ref[e]
    rank_e = rank_ref[pl.ds(e, 1), :]
    sel_e = sel_ref[pl.ds(e, 1), :]

    def groups(rows):
        return (cnt + rows - 1) // rows

    def onehot(first_row, rows):
        tgt = (lax.broadcasted_iota(jnp.int32, (rows, tb), 0) + first_row).astype(F32)
        return jnp.where((rank_e == tgt) & (sel_e > 0.0), 1.0, 0.0)

    @pl.when(f == 0)
    def _():
        def gather(s, c):
            off = pl.multiple_of(s * g_rows, g_rows)
            xe_ref[pl.ds(off, g_rows), :] = jnp.dot(onehot(off, g_rows).astype(BF16), hb_ref[...],
                                                    preferred_element_type=F32).astype(BF16)
            return c
        lax.fori_loop(0, groups(g_rows), gather, 0)

    full = cnt // s_rows
    rem = cnt - full * s_rows
    tail_off = pl.multiple_of(full * s_rows, s_rows)

    def ffn(first):
        def rows_at(off, rows):
            y = _swiglu_step(xe_ref[pl.ds(off, rows), :], wg_ref[...], wu_ref[...], wd_ref[...])
            if first:
                ye_ref[pl.ds(off, rows), :] = y
            else:
                ye_ref[pl.ds(off, rows), :] += y

        def body(s, c):
            rows_at(pl.multiple_of(s * s_rows, s_rows), s_rows)
            return c
        lax.fori_loop(0, full + (rem > f_rows).astype(jnp.int32), body, 0)

        @pl.when((rem > 0) & (rem <= f_rows))
        def _():
            rows_at(tail_off, f_rows)

    @pl.when(f == 0)
    def _():
        ffn(True)
        if s_rows > f_rows:
            @pl.when((rem > 0) & (rem <= f_rows))
            def _():
                rest = pl.ds(pl.multiple_of(tail_off + f_rows, f_rows), s_rows - f_rows)
                ye_ref[rest, :] = jnp.zeros((s_rows - f_rows, D_MODEL), F32)

    @pl.when(f > 0)
    def _():
        ffn(False)

    @pl.when(f == last_f)
    def _():
        gate_e = gate_ref[pl.ds(e, 1), :]

        def scatter(s, c):
            off = pl.multiple_of(s * s_rows, s_rows)
            p = onehot(off, s_rows)
            gate_rows = jnp.sum(p * gate_e, axis=1, keepdims=True)
            valid = (lax.broadcasted_iota(jnp.int32, (s_rows, 1), 0) + off) < cnt
            yg = jnp.where(valid, gate_rows * ye_ref[pl.ds(off, s_rows), :], 0.0).astype(BF16)
            o_ref[...] += jnp.dot(p.T.astype(BF16), yg, preferred_element_type=F32)
            return c
        lax.fori_loop(0, groups(s_rows), scatter, 0)

    if final_norm:
        @pl.when((e == pl.num_programs(1) - 1) & (f == last_f))
        def _():
            for lo in range(0, tb, _MOE_PREFIX):
                rows = slice(lo, lo + _MOE_PREFIX)
                o_ref[rows, :] = _rms(o_ref[rows, :], gf_ref[...])


def _moe(x, g, router_pad, wg, wu, wd, final_g=None, tb=2048, fc=1408, g_rows=256, f_rows=128, s_rows=256):
    t = x.shape[0]
    tb = min(tb, t)
    gf = jnp.ones((D_MODEL,), F32) if final_g is None else final_g
    assert s_rows in (f_rows, 2 * f_rows) and g_rows % f_rows == 0 and tb % g_rows == 0 and tb % s_rows == 0
    once = pl.Buffered(1)
    return pl.pallas_call(
        functools.partial(_moe_kernel, tb=tb, g_rows=g_rows, f_rows=f_rows, s_rows=s_rows,
                          final_norm=final_g is not None),
        grid=(t // tb, N_EXPERTS, D_FF // fc),
        in_specs=[pl.BlockSpec((tb, D_MODEL), lambda i, e, j: (i, 0), pipeline_mode=once),
                  pl.BlockSpec((1, D_MODEL), lambda i, e, j: (0, 0)),
                  pl.BlockSpec((D_MODEL, LANES), lambda i, e, j: (0, 0)),
                  pl.BlockSpec((1, D_MODEL), lambda i, e, j: (0, 0)),
                  pl.BlockSpec((None, D_MODEL, fc), lambda i, e, j: (e, 0, j)),
                  pl.BlockSpec((None, D_MODEL, fc), lambda i, e, j: (e, 0, j)),
                  pl.BlockSpec((None, fc, D_MODEL), lambda i, e, j: (e, j, 0))],
        out_specs=pl.BlockSpec((tb, D_MODEL), lambda i, e, j: (i, 0), pipeline_mode=once),
        out_shape=jax.ShapeDtypeStruct((t, D_MODEL), F32),
        scratch_shapes=[pltpu.VMEM((tb, D_MODEL), BF16), pltpu.VMEM((N_EXPERTS, tb), F32),
                        pltpu.VMEM((N_EXPERTS, tb), F32), pltpu.VMEM((N_EXPERTS, tb), F32),
                        pltpu.VMEM((tb, D_MODEL), BF16), pltpu.VMEM((tb, D_MODEL), F32),
                        pltpu.SMEM((N_EXPERTS,), jnp.int32)],
        compiler_params=pltpu.CompilerParams(dimension_semantics=("parallel", "arbitrary", "arbitrary"),
                                             vmem_limit_bytes=MOE_VMEM_LIMIT),
        name="ffn_moe",
    )(x, g.reshape(1, D_MODEL), router_pad, gf.reshape(1, D_MODEL), wg, wu, wd)


def _final_norm_kernel(x_ref, g_ref, o_ref):
    o_ref[...] = _rms(x_ref[...], g_ref[...])


def _final_norm(x, g, tm=1024):
    t = x.shape[0]
    tm = min(tm, t)
    return pl.pallas_call(
        _final_norm_kernel,
        grid=(t // tm,),
        in_specs=[pl.BlockSpec((tm, D_MODEL), lambda i: (i, 0)), pl.BlockSpec((1, D_MODEL), lambda i: (0, 0))],
        out_specs=pl.BlockSpec((tm, D_MODEL), lambda i: (i, 0)),
        out_shape=jax.ShapeDtypeStruct((t, D_MODEL), F32),
        compiler_params=_cparams(("parallel",)),
        name="final_norm",
    )(x, g.reshape(1, D_MODEL))


def kernel(x_prompt, x_sample, cache_k, cache_v, state_ssm_re, state_ssm_im, state_conv, norm_mix_g, w_in, ssm_a_re, ssm_a_im, ssm_log_dt, ssm_b_re, ssm_b_im, ssm_c_re, ssm_c_im, ssm_d, ssm_w_glu, conv_w, conv_ln_g, conv_ln_b, attn_lq1, attn_lk1, attn_lq2, attn_lk2, attn_sub_g, w_out, norm_ffn_g, ffn_w_gate, ffn_w_up, ffn_w_down, moe_router, moe_w_gate, moe_w_up, moe_w_down, final_norm_g):
    depth = w_in.shape[0]
    n_p, l_p, _ = x_prompt.shape
    n_s, l_s, _ = x_sample.shape
    past = cache_k.shape[2]
    keep = CONV_KERNEL - 1

    xp = x_prompt.reshape(n_p * l_p, D_MODEL)
    xs = x_sample.reshape(n_s * l_s, D_MODEL)
    cache_kt = jnp.transpose(cache_k, (0, 1, 3, 4, 5, 2)).reshape(depth, n_s, 512, past)
    cache_vr = cache_v.reshape(depth, n_s, past * N_HEADS, V_DIM)
    s0_p = jnp.zeros((n_p, 1, 2 * STATE_W), F32)
    buf_p = jnp.zeros((n_p, keep, CONV_WIDTH), F32)

    ops = _s5_operators(ssm_a_re, ssm_a_im, ssm_log_dt, ssm_b_re, ssm_b_im, ssm_c_re, ssm_c_im, ssm_d, S5_CHUNK)
    kv_bufs = None
    outs = {name: [] for name in ("srp", "sip", "cp", "ks", "vs", "srs", "sis", "cs")}
    for l in range(depth):
        lam_init = 0.8 - 0.6 * math.exp(-0.3 * l)
        w_in_b = w_in[l].astype(BF16)
        wkvt = w_in_b[:, IN_WIDTH - 2 * ATTN_WIDTH:].T
        lam_params = jnp.stack([attn_lq1[l], attn_lk1[l], attn_lq2[l], attn_lk2[l]]).astype(F32)
        w_glu_b = ssm_w_glu[l].astype(BF16)
        w_out_b = w_out[l].astype(BF16)
        conv_p = (conv_w[l], conv_ln_g[l], conv_ln_b[l])

        u2, hc, q, kb, vt, *kv_bufs = _in_proj_prompt(xp, norm_mix_g[l], w_in_b, wkvt, kv_bufs, l, depth, n_p)
        y_ssm, sf_p = _s5_group(u2, n_p, l_p, s0_p, ops, l)
        cv, cb_p = _conv(hc.reshape(n_p, l_p, CONV_WIDTH), buf_p, *conv_p, tl=min(512, l_p))
        at = _attn_prompt(q, kb, vt, lam_params, attn_sub_g[l], n_p, l_p, lam_init)
        xp = _out_proj(xp, y_ssm, cv.reshape(n_p * l_p, CONV_WIDTH), at, w_glu_b, w_out_b)

        s0_s = jnp.concatenate([state_ssm_re[l].reshape(n_s, 1, STATE_W), state_ssm_im[l].reshape(n_s, 1, STATE_W)],
                               axis=-1).astype(F32)
        u2, hc, q, kb, vb, k_s, v_s = _in_proj_sample(xs, norm_mix_g[l], w_in_b)
        y_ssm, sf_s = _s5_group(u2, n_s, l_s, s0_s, ops, l)
        cv, cb_s = _conv(hc.reshape(n_s, l_s, CONV_WIDTH), state_conv[l].astype(F32), *conv_p, tl=l_s)
        at = _attn_sample(q, kb, vb, cache_kt, cache_vr, l, lam_params, attn_sub_g[l], lam_init)
        xs = _out_proj(xs, y_ssm, cv.reshape(n_s * l_s, CONV_WIDTH), at, w_glu_b, w_out_b)

        j = l // 2
        if l % 2 == 0:
            wg, wu, wd = ffn_w_gate[j].astype(BF16), ffn_w_up[j].astype(BF16), ffn_w_down[j].astype(BF16)
            xp = _ffn(xp, norm_ffn_g[l], wg, wu, wd)
            xs = _ffn(xs, norm_ffn_g[l], wg, wu, wd)
        else:
            router_pad = jnp.pad(moe_router[j].astype(F32), ((0, 0), (0, LANES - N_EXPERTS)))
            wg, wu, wd = moe_w_gate[j].astype(BF16), moe_w_up[j].astype(BF16), moe_w_down[j].astype(BF16)
            final_g = final_norm_g if l == depth - 1 else None
            xp = _moe(xp, norm_ffn_g[l], router_pad, wg, wu, wd, final_g)
            xs = _moe(xs, norm_ffn_g[l], router_pad, wg, wu, wd, final_g)

        outs["srp"].append(sf_p[:, 0, :STATE_W].reshape(n_p, N_SSM_GROUPS, SSM_STATE))
        outs["sip"].append(sf_p[:, 0, STATE_W:].reshape(n_p, N_SSM_GROUPS, SSM_STATE))
        outs["cp"].append(cb_p)
        outs["ks"].append(k_s.reshape(n_s, l_s, 2, N_HEADS, HEAD_DIM))
        outs["vs"].append(v_s.reshape(n_s, l_s, N_HEADS, V_DIM))
        outs["srs"].append(sf_s[:, 0, :STATE_W].reshape(n_s, N_SSM_GROUPS, SSM_STATE))
        outs["sis"].append(sf_s[:, 0, STATE_W:].reshape(n_s, N_SSM_GROUPS, SSM_STATE))
        outs["cs"].append(cb_s)

    if depth % 2 == 0:
        yp, ys = xp, xs
    else:
        yp = _final_norm(xp, final_norm_g)
        ys = _final_norm(xs, final_norm_g)
    st = {name: jnp.stack(vals) for name, vals in outs.items()}
    kt_buf, v_buf = kv_bufs
    k_prompt = jnp.transpose(kt_buf.reshape(depth, n_p, 2, N_HEADS, HEAD_DIM, l_p), (0, 1, 5, 2, 3, 4))
    v_prompt = v_buf.reshape(depth, n_p, l_p, N_HEADS, V_DIM)
    return (yp.reshape(n_p, l_p, D_MODEL), ys.reshape(n_s, l_s, D_MODEL),
            k_prompt, v_prompt, st["srp"], st["sip"], st["cp"],
            st["ks"], st["vs"], st["srs"], st["sis"], st["cs"])
```

```python
import functools
import math

import jax
import jax.numpy as jnp
from jax import lax
from jax.experimental import pallas as pl
from jax.experimental.pallas import tpu as pltpu

F32 = jnp.float32
BF16 = jnp.bfloat16

D_MODEL = 1024
CHUNK = 64
SSM_GROUP = 16
N_SSM_GROUPS = 16
SSM_WIDTH = 256
SSM_STATE = 64
STATE_W = N_SSM_GROUPS * SSM_STATE
CONV_WIDTH = 256
CONV_KERNEL = 31
N_HEADS = 4
HEAD_DIM = 64
V_DIM = 128
VT_ROWS = V_DIM + 16
QK_WIDTH = 256
ATTN_WIDTH = 512
IN_WIDTH = 2304
ATTN_SCALE = 1.0 / math.sqrt(HEAD_DIM)
LOG2E = math.log2(math.e)
NEG_INF = -1e30
D_FF = 2816
N_EXPERTS = 8
EPS = 1e-6

S5_CHUNK = 8
LANES = 128
SUBLANES = 8
VMEM_LIMIT = 48 * 1024 * 1024
MOE_VMEM_LIMIT = 58 * 1024 * 1024


def _cparams(sem):
    return pltpu.CompilerParams(dimension_semantics=sem, vmem_limit_bytes=VMEM_LIMIT)


def _rms(x, g):
    return x * lax.rsqrt(jnp.mean(x * x, axis=-1, keepdims=True) + EPS) * g


def _mixer_inputs(x_ref, g_ref, w_ref, u2_ref, hc_ref, q_ref, us_ref):
    h = _rms(x_ref[...], g_ref[...]).astype(BF16)

    def proj(lo, hi):
        return jnp.dot(h, w_ref[:, lo:hi], preferred_element_type=F32)

    u = proj(0, 256)
    n_chunks = us_ref.shape[1] // S5_CHUNK
    for c in range(SSM_WIDTH // LANES):
        us_ref[c] = u[:, c * LANES:(c + 1) * LANES]
        for j in range(S5_CHUNK):
            lo = j * SSM_WIDTH + c * LANES
            u2_ref[:, lo:lo + LANES] = us_ref[c, pl.ds(j, n_chunks, stride=S5_CHUNK), :]
    c_val = proj(256, 512)
    c_gate = proj(512, 768)
    hc_ref[...] = c_val * jax.nn.sigmoid(c_gate)
    q_ref[...] = (proj(768, 1280) * (ATTN_SCALE * LOG2E)).astype(BF16)
    return h, proj(1280, 1792), proj(1792, 2304)


def _in_proj_prompt_kernel(*refs, aliased):
    x_ref, g_ref, w_ref = refs[:3]
    u2_ref, hc_ref, q_ref, kb_ref, vt_ref, kt_ref, v_ref, us_ref = refs[5 if aliased else 3:]
    h, k, v = _mixer_inputs(x_ref, g_ref, w_ref, u2_ref, hc_ref, q_ref, us_ref)
    kb_ref[...] = k.astype(BF16)
    for head in range(N_HEADS):
        v_ref[pl.ds(head, v.shape[0], stride=N_HEADS), :] = v[:, head * V_DIM:(head + 1) * V_DIM]
    kt_ref[...] = k.T
    vt = v.T.astype(BF16)
    for head in range(N_HEADS):
        vt_ref[head * VT_ROWS:head * VT_ROWS + V_DIM, :] = vt[head * V_DIM:(head + 1) * V_DIM, :]
        vt_ref[head * VT_ROWS + V_DIM:(head + 1) * VT_ROWS, :] = jnp.ones((VT_ROWS - V_DIM, vt.shape[1]), BF16)


def _in_proj_prompt(x, g, w_bf16, kv_bufs, layer, depth, n_seq, tm=512):
    t = x.shape[0]
    seq_len = t // n_seq
    per_seq = seq_len // tm
    row = lambda width: pl.BlockSpec((tm, width), lambda i: (i, 0))
    in_specs = [row(D_MODEL), pl.BlockSpec((1, D_MODEL), lambda i: (0, 0)),
                pl.BlockSpec((D_MODEL, IN_WIDTH), lambda i: (0, 0))]
    args = [x, g.reshape(1, D_MODEL), w_bf16]
    aliases = {}
    if kv_bufs is not None:
        in_specs += [pl.BlockSpec(memory_space=pl.ANY), pl.BlockSpec(memory_space=pl.ANY)]
        args += list(kv_bufs)
        aliases = {3: 5, 4: 6}
    return pl.pallas_call(
        functools.partial(_in_proj_prompt_kernel, aliased=kv_bufs is not None),
        grid=(t // tm,),
        in_specs=in_specs,
        out_specs=[pl.BlockSpec((tm // S5_CHUNK, S5_CHUNK * SSM_WIDTH), lambda i: (i, 0)),
                   row(256), row(512), row(512),
                   pl.BlockSpec((N_HEADS * VT_ROWS, tm), lambda i: (0, i)),
                   pl.BlockSpec((None, None, ATTN_WIDTH, tm), lambda i: (layer, i // per_seq, 0, i % per_seq)),
                   pl.BlockSpec((None, None, tm * N_HEADS, V_DIM), lambda i: (layer, i // per_seq, i % per_seq, 0))],
        out_shape=[jax.ShapeDtypeStruct((t // S5_CHUNK, S5_CHUNK * SSM_WIDTH), F32),
                   jax.ShapeDtypeStruct((t, 256), F32),
                   jax.ShapeDtypeStruct((t, 512), BF16), jax.ShapeDtypeStruct((t, 512), BF16),
                   jax.ShapeDtypeStruct((N_HEADS * VT_ROWS, t), BF16),
                   jax.ShapeDtypeStruct((depth, n_seq, ATTN_WIDTH, seq_len), F32),
                   jax.ShapeDtypeStruct((depth, n_seq, seq_len * N_HEADS, V_DIM), F32)],
        scratch_shapes=[pltpu.VMEM((SSM_WIDTH // LANES, tm, LANES), F32)],
        input_output_aliases=aliases,
        compiler_params=_cparams(("parallel",)),
        name="in_proj_prompt",
    )(*args)


def _in_proj_sample_kernel(x_ref, g_ref, w_ref, u2_ref, hc_ref, q_ref, kb_ref, vb_ref, k_ref, v_ref, us_ref):
    _, k, v = _mixer_inputs(x_ref, g_ref, w_ref, u2_ref, hc_ref, q_ref, us_ref)
    k_ref[...] = k
    kb_ref[...] = k.astype(BF16)
    v_ref[...] = v
    vb_ref[...] = v.astype(BF16)


def _in_proj_sample(x, g, w_bf16, tm=512):
    t = x.shape[0]
    tm = min(tm, t)
    row = lambda width: pl.BlockSpec((tm, width), lambda i: (i, 0))
    return pl.pallas_call(
        _in_proj_sample_kernel,
        grid=(t // tm,),
        in_specs=[row(D_MODEL), pl.BlockSpec((1, D_MODEL), lambda i: (0, 0)),
                  pl.BlockSpec((D_MODEL, IN_WIDTH), lambda i: (0, 0))],
        out_specs=[pl.BlockSpec((tm // S5_CHUNK, S5_CHUNK * SSM_WIDTH), lambda i: (i, 0)),
                   row(256), row(512), row(512), row(512), row(512), row(512)],
        out_shape=[jax.ShapeDtypeStruct((t // S5_CHUNK, S5_CHUNK * SSM_WIDTH), F32),
                   jax.ShapeDtypeStruct((t, 256), F32),
                   jax.ShapeDtypeStruct((t, 512), BF16), jax.ShapeDtypeStruct((t, 512), BF16),
                   jax.ShapeDtypeStruct((t, 512), BF16), jax.ShapeDtypeStruct((t, 512), F32),
                   jax.ShapeDtypeStruct((t, 512), F32)],
        scratch_shapes=[pltpu.VMEM((SSM_WIDTH // LANES, tm, LANES), F32)],
        compiler_params=_cparams(("parallel",)),
        name="in_proj_sample",
    )(x, g.reshape(1, D_MODEL), w_bf16)


def _mm_kernel(x_ref, w_ref, o_ref):
    o_ref[...] = jnp.dot(x_ref[...].astype(BF16), w_ref[...], preferred_element_type=F32)


def _mm(x, w_bf16, layer, tm, tn, name):
    m, k = x.shape
    n = w_bf16.shape[2]
    return pl.pallas_call(
        _mm_kernel,
        grid=(n // tn, m // tm),
        in_specs=[pl.BlockSpec((tm, k), lambda j, i: (i, 0)),
                  pl.BlockSpec((None, k, tn), lambda j, i: (layer, 0, j))],
        out_specs=pl.BlockSpec((tm, tn), lambda j, i: (i, j)),
        out_shape=jax.ShapeDtypeStruct((m, n), F32),
        compiler_params=_cparams(("parallel", "parallel")),
        name=name,
    )(x, w_bf16)


def _s5_scan_kernel(d_ref, s0_ref, a_ref, ss_ref, sf_ref, st_ref, *, rows):
    t = pl.program_id(1)

    @pl.when(t == 0)
    def _():
        st_ref[...] = s0_ref[...]

    a_re = a_ref[:, :STATE_W]
    a_im = a_ref[:, STATE_W:]

    def step(c, carry):
        s_re, s_im = carry
        ss_ref[pl.ds(c, 1), :STATE_W] = s_re
        ss_ref[pl.ds(c, 1), STATE_W:] = s_im
        d = d_ref[pl.ds(c, 1), :]
        n_re = a_re * s_re - a_im * s_im + d[:, :STATE_W]
        n_im = a_re * s_im + a_im * s_re + d[:, STATE_W:]
        return n_re, n_im

    s_re, s_im = lax.fori_loop(0, rows, step, (st_ref[:, :STATE_W], st_ref[:, STATE_W:]))
    st_ref[:, :STATE_W] = s_re
    st_ref[:, STATE_W:] = s_im
    sf_ref[...] = st_ref[...]


def _s5_scan(d, n_seq, n_chunks, s0, a_c, layer):
    rows = min(n_chunks, 256)
    nt = n_chunks // rows
    w = 2 * STATE_W
    return pl.pallas_call(
        functools.partial(_s5_scan_kernel, rows=rows),
        grid=(n_seq, nt),
        in_specs=[pl.BlockSpec((rows, w), lambda s, t: (s * nt + t, 0)),
                  pl.BlockSpec((None, 1, w), lambda s, t: (s, 0, 0)),
                  pl.BlockSpec((None, 1, w), lambda s, t: (layer, 0, 0))],
        out_specs=[pl.BlockSpec((rows, w), lambda s, t: (s * nt + t, 0)),
                   pl.BlockSpec((None, 1, w), lambda s, t: (s, 0, 0))],
        out_shape=[jax.ShapeDtypeStruct((n_seq * n_chunks, w), F32), jax.ShapeDtypeStruct((n_seq, 1, w), F32)],
        scratch_shapes=[pltpu.VMEM((1, w), F32)],
        compiler_params=_cparams(("parallel", "arbitrary")),
        name="s5_scan",
    )(d, s0, a_c)


def _s5_out_kernel(u_ref, ut_ref, ss_ref, wi_ref, ws_ref, dsk_ref, y_ref):
    y_ref[...] = (jnp.dot(u_ref[...].astype(BF16), wi_ref[...], preferred_element_type=F32)
                  + jnp.dot(ss_ref[...].astype(BF16), ws_ref[...], preferred_element_type=F32)
                  + dsk_ref[...] * ut_ref[...])


def _s5_out(u2, s_start, w_intra, w_inter, dsk, layer, tm=256, tn=1024):
    m, n = u2.shape
    k = s_start.shape[1]
    return pl.pallas_call(
        _s5_out_kernel,
        grid=(n // tn, m // tm),
        in_specs=[pl.BlockSpec((tm, n), lambda j, i: (i, 0)), pl.BlockSpec((tm, tn), lambda j, i: (i, j)),
                  pl.BlockSpec((tm, k), lambda j, i: (i, 0)),
                  pl.BlockSpec((None, n, tn), lambda j, i: (layer, 0, j)),
                  pl.BlockSpec((None, k, tn), lambda j, i: (layer, 0, j)),
                  pl.BlockSpec((None, 1, tn), lambda j, i: (layer, 0, j))],
        out_specs=pl.BlockSpec((tm, tn), lambda j, i: (i, j)),
        out_shape=jax.ShapeDtypeStruct((m, n), F32),
        compiler_params=_cparams(("parallel", "parallel")),
        name="s5_out",
    )(u2, u2, s_start, w_intra, w_inter, dsk)


def _s5_operators(a_re, a_im, log_dt, b_re, b_im, c_re, c_im, d_skip, n_c):
    f32 = lambda x: x.astype(F32)
    a_re, a_im, b_re, b_im, c_re, c_im = map(f32, (a_re, a_im, b_re, b_im, c_re, c_im))
    n_l = a_re.shape[0]
    g_n = N_SSM_GROUPS
    dt = jnp.exp(f32(log_dt))[:, None, :, None]
    ks = jnp.arange(n_c + 1, dtype=F32)[None, :, None, None]
    mag = jnp.exp(a_re[:, None] * dt * ks)
    ang = a_im[:, None] * dt * ks
    pw_re, pw_im = mag * jnp.cos(ang), mag * jnp.sin(ang)
    den = a_re * a_re + a_im * a_im
    q_re = ((pw_re[:, 1] - 1.0) * a_re + pw_im[:, 1] * a_im) / den
    q_im = (pw_im[:, 1] * a_re - (pw_re[:, 1] - 1.0) * a_im) / den
    bb_re = q_re[..., None] * b_re - q_im[..., None] * b_im
    bb_im = q_re[..., None] * b_im + q_im[..., None] * b_re
    eye_g = jnp.eye(g_n, dtype=F32)

    def block_diag(x, rows, cols):
        return (x[:, :, :, None, :] * eye_g[None, :, None, :, None]).reshape(n_l, rows, cols)

    bt_re = block_diag(jnp.transpose(bb_re, (0, 1, 3, 2)), SSM_WIDTH, STATE_W)[:, None]
    bt_im = block_diag(jnp.transpose(bb_im, (0, 1, 3, 2)), SSM_WIDTH, STATE_W)[:, None]
    pj_re = pw_re[:, :n_c][:, ::-1].reshape(n_l, n_c, 1, STATE_W)
    pj_im = pw_im[:, :n_c][:, ::-1].reshape(n_l, n_c, 1, STATE_W)
    w_state = jnp.concatenate([(pj_re * bt_re - pj_im * bt_im).reshape(n_l, n_c * SSM_WIDTH, STATE_W),
                               (pj_re * bt_im + pj_im * bt_re).reshape(n_l, n_c * SSM_WIDTH, STATE_W)], axis=2)

    m_re = c_re[:, None] * pw_re[:, :n_c, :, None, :] - c_im[:, None] * pw_im[:, :n_c, :, None, :]
    m_im = c_re[:, None] * pw_im[:, :n_c, :, None, :] + c_im[:, None] * pw_re[:, :n_c, :, None, :]
    hi = lax.Precision.HIGHEST
    ker = (jnp.einsum('lkghp,lgpx->lkghx', m_re, bb_re, precision=hi)
           - jnp.einsum('lkghp,lgpx->lkghx', m_im, bb_im, precision=hi))
    blk = jnp.transpose(ker, (0, 1, 2, 4, 3))
    blk = (blk[:, :, :, :, None, :] * eye_g[None, None, :, None, :, None]).reshape(n_l, n_c, 256, 256).astype(BF16)
    zero = jnp.zeros((n_l, 256, 256), BF16)
    w_intra = jnp.concatenate(
        [jnp.concatenate([blk[:, t - j] if t >= j else zero for t in range(n_c)], axis=2) for j in range(n_c)],
        axis=1)

    ct_re = block_diag(jnp.transpose(c_re, (0, 1, 3, 2)), STATE_W, SSM_WIDTH)
    ct_im = block_diag(jnp.transpose(c_im, (0, 1, 3, 2)), STATE_W, SSM_WIDTH)
    pt_re = pw_re[:, 1:].reshape(n_l, n_c, STATE_W, 1)
    pt_im = pw_im[:, 1:].reshape(n_l, n_c, STATE_W, 1)
    top = jnp.concatenate([pt_re[:, t] * ct_re - pt_im[:, t] * ct_im for t in range(n_c)], axis=2)
    bot = jnp.concatenate([-(pt_re[:, t] * ct_im + pt_im[:, t] * ct_re) for t in range(n_c)], axis=2)
    w_inter = jnp.concatenate([top, bot], axis=1)

    a_c = jnp.concatenate([pw_re[:, n_c].reshape(n_l, 1, STATE_W), pw_im[:, n_c].reshape(n_l, 1, STATE_W)], axis=2)
    dsk = jnp.tile(f32(d_skip).reshape(n_l, 1, SSM_WIDTH), (1, 1, n_c))
    return w_state.astype(BF16), w_intra, w_inter.astype(BF16), a_c, dsk


def _s5_group(u2, n_seq, seq_len, s0, ops, layer):
    w_state, w_intra, w_inter, a_c, dsk = ops
    n_chunks = seq_len // S5_CHUNK
    d_state = _mm(u2, w_state, layer, 256, 1024, "s5_state")
    s_start, s_final = _s5_scan(d_state, n_seq, n_chunks, s0, a_c, layer)
    return _s5_out(u2, s_start, w_intra, w_inter, dsk, layer), s_final


_CONV_PAD = 32
_CONV_RB = 64


def _conv_kernel(h_ref, buf_ref, w_ref, g_ref, b_ref, y_ref, nb_ref, xp_ref, xs_ref, *, tl):
    t = pl.program_id(1)
    keep = CONV_KERNEL - 1
    lo = _CONV_PAD - keep

    @pl.when(t == 0)
    def _():
        xp_ref[lo:_CONV_PAD, :] = buf_ref[...]

    @pl.when(t > 0)
    def _():
        xp_ref[lo:_CONV_PAD, :] = xp_ref[tl + lo:tl + _CONV_PAD, :]

    xp_ref[_CONV_PAD:_CONV_PAD + tl, :] = h_ref[...]
    for phase in range(SUBLANES):
        span = tl + (keep - phase) // SUBLANES * SUBLANES
        xs_ref[phase, :span, :] = xp_ref[lo + phase:lo + phase + span, :]
    for r in range(tl // _CONV_RB):
        base = r * _CONV_RB
        acc = jnp.zeros((_CONV_RB, CONV_WIDTH), F32)
        for k in range(CONV_KERNEL):
            start = base + k - k % SUBLANES
            acc = acc + w_ref[k:k + 1, :] * xs_ref[k % SUBLANES, start:start + _CONV_RB, :]
        xc = acc - jnp.mean(acc, axis=-1, keepdims=True)
        var = jnp.mean(xc * xc, axis=-1, keepdims=True)
        y = xc * lax.rsqrt(var + EPS) * g_ref[...] + b_ref[...]
        y_ref[base:base + _CONV_RB, :] = y * jax.nn.sigmoid(y)
    nb_ref[...] = xp_ref[tl + lo:tl + _CONV_PAD, :]


def _conv(h, buf, w, ln_g, ln_b, tl):
    s, l, c = h.shape
    keep = CONV_KERNEL - 1
    return pl.pallas_call(
        functools.partial(_conv_kernel, tl=tl),
        grid=(s, l // tl),
        in_specs=[pl.BlockSpec((None, tl, c), lambda i, t: (i, t, 0)),
                  pl.BlockSpec((None, keep, c), lambda i, t: (i, 0, 0)),
                  pl.BlockSpec((CONV_KERNEL, c), lambda i, t: (0, 0)),
                  pl.BlockSpec((1, c), lambda i, t: (0, 0)), pl.BlockSpec((1, c), lambda i, t: (0, 0))],
        out_specs=[pl.BlockSpec((None, tl, c), lambda i, t: (i, t, 0)),
                   pl.BlockSpec((None, keep, c), lambda i, t: (i, 0, 0))],
        out_shape=[jax.ShapeDtypeStruct((s, l, c), F32), jax.ShapeDtypeStruct((s, keep, c), F32)],
        scratch_shapes=[pltpu.VMEM((tl + _CONV_PAD, c), F32),
                        pltpu.VMEM((SUBLANES, tl + (keep // SUBLANES) * SUBLANES, c), F32)],
        compiler_params=_cparams(("parallel", "arbitrary")),
        name="conv_module",
    )(h, buf, w, ln_g.reshape(1, c), ln_b.reshape(1, c))


def _stack_q(q_ref, qs_ref):
    tq = q_ref.shape[0]
    lane = lax.broadcasted_iota(jnp.int32, (tq, LANES), 1)
    for b in range(4):
        qb = q_ref[:, b * LANES:(b + 1) * LANES]
        qs_ref[b, :tq, :] = jnp.where(lane < HEAD_DIM, qb, jnp.zeros_like(qb))
        qs_ref[b, tq:, :] = jnp.where(lane >= HEAD_DIM, qb, jnp.zeros_like(qb))


def _attn_update(score, value, m_ref, l_ref, acc_ref, tq):
    def softmax(b, s):
        m_old = m_ref[b]
        m_new = jnp.maximum(m_old, jnp.max(s, axis=-1, keepdims=True))
        alpha = jnp.exp2(m_old - m_new)
        p = jnp.exp2(s - m_new)
        l_ref[b] = alpha * l_ref[b] + jnp.sum(p, axis=-1, keepdims=True)
        m_ref[b] = m_new
        return alpha, p.astype(BF16)

    def values(b, alpha, pb):
        h0 = 2 * (b % 2)
        pv0 = jnp.dot(pb[:tq], value(h0), preferred_element_type=F32)
        pv1 = jnp.dot(pb[tq:], value(h0 + 1), preferred_element_type=F32)
        acc_ref[b, :tq, :] = alpha[:tq] * acc_ref[b, :tq, :] + pv0
        acc_ref[b, tq:, :] = alpha[tq:] * acc_ref[b, tq:, :] + pv1

    s0 = score(0)
    s1 = score(1)
    s2 = score(2)
    a0, p0 = softmax(0, s0)
    s3 = score(3)
    a1, p1 = softmax(1, s1)
    values(0, a0, p0)
    a2, p2 = softmax(2, s2)
    values(1, a1, p1)
    a3, p3 = softmax(3, s3)
    values(2, a2, p2)
    values(3, a3, p3)


def _attn_finish(lp_ref, sg_ref, l_ref, acc_ref, o_ref, tq, lam_init):
    lam = _lambda(lp_ref, lam_init)
    for h in range(N_HEADS):
        b1, half = h // 2, h % 2
        rows = slice(half * tq, (half + 1) * tq)
        o1 = acc_ref[b1, rows, :] / l_ref[b1, rows, :]
        o2 = acc_ref[b1 + 2, rows, :] / l_ref[b1 + 2, rows, :]
        o = o1 - lam * o2
        o_ref[:, h * V_DIM:(h + 1) * V_DIM] = (_rms(o, sg_ref[...]) * (1.0 - lam_init)).astype(o_ref.dtype)


def _attn_init(m_ref, l_ref, acc_ref):
    m_ref[...] = jnp.full(m_ref.shape, NEG_INF, F32)
    l_ref[...] = jnp.zeros(l_ref.shape, F32)
    acc_ref[...] = jnp.zeros(acc_ref.shape, F32)


def _attn_prompt_update(qs_ref, k_ref, vt_ref, m_ref, acc_ref, tq, mask):
    def scores(c):
        b, r = divmod(c, 2)
        s = lax.dot_general(k_ref[:, b * LANES:(b + 1) * LANES], qs_ref[b, r * tq:(r + 1) * tq, :],
                            (((1,), (1,)), ((), ())), preferred_element_type=F32)
        return s if mask is None else jnp.where(mask, s, NEG_INF)

    def softmax(c, s):
        b, r = divmod(c, 2)
        cols = slice(r * tq, (r + 1) * tq)
        sb = s.astype(BF16)
        m_old = m_ref[b, :, cols]
        m_new = jnp.maximum(m_old, jnp.max(sb, axis=0, keepdims=True).astype(F32))
        m_ref[b, :, cols] = m_new
        return jnp.exp2(m_old - m_new), jnp.exp2(sb - m_new.astype(BF16))

    def values(c, alpha, pb):
        h = 2 * ((c // 2) % 2) + c % 2
        pv = jnp.dot(vt_ref[h * VT_ROWS:(h + 1) * VT_ROWS, :], pb, preferred_element_type=F32)
        acc_ref[c] = alpha * acc_ref[c] + pv

    n_chain, ahead = 8, 3
    pending = [scores(c) for c in range(ahead)]
    for c in range(n_chain):
        if c + ahead < n_chain:
            pending.append(scores(c + ahead))
        alpha, pb = softmax(c, pending.pop(0))
        values(c, alpha, pb)


def _lambda(lp_ref, lam_init):
    lp = lp_ref[...]
    return (jnp.exp(jnp.sum(lp[0:1] * lp[1:2], axis=-1, keepdims=True))
            - jnp.exp(jnp.sum(lp[2:3] * lp[3:4], axis=-1, keepdims=True)) + lam_init)


def _attn_prompt_finish(lp_ref, sgt_ref, acc_ref, o_ref, tq, lam_init):
    lam = _lambda(lp_ref, lam_init)
    for h in range(N_HEADS):
        i1 = 2 * (h // 2) + h % 2
        o1 = acc_ref[i1, :V_DIM, :] / acc_ref[i1, V_DIM:V_DIM + 1, :]
        o2 = acc_ref[i1 + 4, :V_DIM, :] / acc_ref[i1 + 4, V_DIM:V_DIM + 1, :]
        o = o1 - lam * o2
        on = o * lax.rsqrt(jnp.mean(o * o, axis=0, keepdims=True) + EPS) * sgt_ref[...] * (1.0 - lam_init)
        o_ref[:, h * V_DIM:(h + 1) * V_DIM] = on.T.astype(o_ref.dtype)


def _attn_prompt_kernel(it_ref, jt_ref, lp_ref, sgt_ref, q_ref, k_ref, vt_ref, o_ref, qs_ref, m_ref, acc_ref,
                        *, tq, lam_init):
    step = pl.program_id(1)
    i = it_ref[step]
    j = jt_ref[step]

    @pl.when(j == 0)
    def _():
        m_ref[...] = jnp.full(m_ref.shape, NEG_INF, F32)
        acc_ref[...] = jnp.zeros(acc_ref.shape, F32)
        _stack_q(q_ref, qs_ref)

    @pl.when(j < i)
    def _():
        _attn_prompt_update(qs_ref, k_ref, vt_ref, m_ref, acc_ref, tq, None)

    @pl.when(j == i)
    def _():
        tk = k_ref.shape[0]
        key_chunk = lax.broadcasted_iota(jnp.int32, (tk, tq), 0) // CHUNK
        qry_chunk = lax.broadcasted_iota(jnp.int32, (tk, tq), 1) // CHUNK
        _attn_prompt_update(qs_ref, k_ref, vt_ref, m_ref, acc_ref, tq, qry_chunk >= key_chunk)
        _attn_prompt_finish(lp_ref, sgt_ref, acc_ref, o_ref, tq, lam_init)


def _attn_prompt(q, kb, vt, lam_params, sub_g, n_seq, seq_len, lam_init, tq=512):
    nq = seq_len // tq
    pairs = [(i, j) for i in range(nq) for j in range(i + 1)]
    i_tab = jnp.asarray([p[0] for p in pairs], jnp.int32)
    j_tab = jnp.asarray([p[1] for p in pairs], jnp.int32)
    grid_spec = pltpu.PrefetchScalarGridSpec(
        num_scalar_prefetch=2,
        grid=(n_seq, len(pairs)),
        in_specs=[pl.BlockSpec((4, HEAD_DIM), lambda b, s, it, jt: (0, 0)),
                  pl.BlockSpec((V_DIM, 1), lambda b, s, it, jt: (0, 0)),
                  pl.BlockSpec((tq, 512), lambda b, s, it, jt: (b * nq + it[s], 0)),
                  pl.BlockSpec((tq, 512), lambda b, s, it, jt: (b * nq + jt[s], 0)),
                  pl.BlockSpec((N_HEADS * VT_ROWS, tq), lambda b, s, it, jt: (0, b * nq + jt[s]))],
        out_specs=pl.BlockSpec((tq, 512), lambda b, s, it, jt: (b * nq + it[s], 0)),
        scratch_shapes=[pltpu.VMEM((4, 2 * tq, LANES), BF16), pltpu.VMEM((4, 1, 2 * tq), F32),
                        pltpu.VMEM((8, VT_ROWS, tq), F32)],
    )
    return pl.pallas_call(
        functools.partial(_attn_prompt_kernel, tq=tq, lam_init=lam_init),
        grid_spec=grid_spec,
        out_shape=jax.ShapeDtypeStruct((n_seq * seq_len, 512), BF16),
        compiler_params=_cparams(("parallel", "arbitrary")),
        name="attn_prompt",
    )(i_tab, j_tab, lam_params, sub_g.reshape(V_DIM, 1), q, kb, vt)


def _attn_sample_kernel(lp_ref, sg_ref, q_ref, ckt_ref, cv_ref, kn_ref, vn_ref, o_ref, qs_ref, m_ref, l_ref, acc_ref,
                        *, tq, tk, lam_init):
    j = pl.program_id(1)

    @pl.when(j == 0)
    def _():
        _attn_init(m_ref, l_ref, acc_ref)
        _stack_q(q_ref, qs_ref)

    def cache_score(b):
        return jnp.dot(qs_ref[b], ckt_ref[b * LANES:(b + 1) * LANES, :].astype(BF16), preferred_element_type=F32)

    def cache_value(h):
        return cv_ref[pl.ds(h, tk, stride=N_HEADS), :].astype(BF16)

    _attn_update(cache_score, cache_value, m_ref, l_ref, acc_ref, tq)

    @pl.when(j == pl.num_programs(1) - 1)
    def _():
        def new_score(b):
            return lax.dot_general(qs_ref[b], kn_ref[:, b * LANES:(b + 1) * LANES], (((1,), (1,)), ((), ())),
                                   preferred_element_type=F32)

        def new_value(h):
            return vn_ref[:, h * V_DIM:(h + 1) * V_DIM]

        _attn_update(new_score, new_value, m_ref, l_ref, acc_ref, tq)
        _attn_finish(lp_ref, sg_ref, l_ref, acc_ref, o_ref, tq, lam_init)


def _attn_sample(q, kb, vb, cache_kt, cache_v, layer, lam_params, sub_g, lam_init, tk=2048):
    _, n_seq, _, past = cache_kt.shape
    tk = min(tk, past)
    assert past % tk == 0
    tq = q.shape[0] // n_seq
    new = pl.BlockSpec((tq, 512), lambda b, j: (b, 0))
    return pl.pallas_call(
        functools.partial(_attn_sample_kernel, tq=tq, tk=tk, lam_init=lam_init),
        grid=(n_seq, past // tk),
        in_specs=[pl.BlockSpec((4, HEAD_DIM), lambda b, j: (0, 0)), pl.BlockSpec((1, V_DIM), lambda b, j: (0, 0)),
                  new,
                  pl.BlockSpec((None, None, 512, tk), lambda b, j: (layer, b, 0, j)),
                  pl.BlockSpec((None, None, tk * N_HEADS, V_DIM), lambda b, j: (layer, b, j, 0)),
                  new, new],
        out_specs=new,
        out_shape=jax.ShapeDtypeStruct((n_seq * tq, 512), BF16),
        scratch_shapes=[pltpu.VMEM((4, 2 * tq, LANES), BF16), pltpu.VMEM((4, 2 * tq, 1), F32),
                        pltpu.VMEM((4, 2 * tq, 1), F32), pltpu.VMEM((4, 2 * tq, V_DIM), F32)],
        compiler_params=_cparams(("parallel", "arbitrary")),
        name="attn_sample",
    )(lam_params, sub_g.reshape(1, V_DIM), q, cache_kt, cache_v, kb, vb)


def _gelu_tanh(x):
    return 0.5 * x * (1.0 + jnp.tanh(math.sqrt(2.0 / math.pi) * (x + 0.044715 * (x * x * x))))


def _out_proj_kernel(x_ref, y2_ref, cv_ref, at_ref, wglu_ref, wo_ref, o_ref, ys_ref):
    n_chunks = y2_ref.shape[0]
    for c in range(SSM_WIDTH // LANES):
        for j in range(S5_CHUNK):
            lo = j * SSM_WIDTH + c * LANES
            ys_ref[c, pl.ds(j, n_chunks, stride=S5_CHUNK), :] = y2_ref[:, lo:lo + LANES]
    z = _gelu_tanh(jnp.concatenate([ys_ref[c] for c in range(SSM_WIDTH // LANES)], axis=1))
    gate = jax.nn.sigmoid(jnp.dot(z.astype(BF16), wglu_ref[...], preferred_element_type=F32))
    ssm = (z * gate).astype(BF16)
    acc = jnp.dot(ssm, wo_ref[0:256, :], preferred_element_type=F32)
    acc = acc + jnp.dot(cv_ref[...].astype(BF16), wo_ref[256:512, :], preferred_element_type=F32)
    acc = acc + jnp.dot(at_ref[...], wo_ref[512:1024, :], preferred_element_type=F32)
    o_ref[...] = x_ref[...] + acc


def _out_proj(x, y_ssm, conv_out, attn_out, w_glu, w_out, tm=512):
    t = x.shape[0]
    tm = min(tm, t)
    row = lambda width: pl.BlockSpec((tm, width), lambda i: (i, 0))
    return pl.pallas_call(
        _out_proj_kernel,
        grid=(t // tm,),
        in_specs=[row(D_MODEL), pl.BlockSpec((tm // S5_CHUNK, S5_CHUNK * SSM_WIDTH), lambda i: (i, 0)),
                  row(256), row(512),
                  pl.BlockSpec((256, 256), lambda i: (0, 0)), pl.BlockSpec((D_MODEL, D_MODEL), lambda i: (0, 0))],
        out_specs=row(D_MODEL),
        out_shape=jax.ShapeDtypeStruct((t, D_MODEL), F32),
        scratch_shapes=[pltpu.VMEM((SSM_WIDTH // LANES, tm, LANES), F32)],
        compiler_params=_cparams(("parallel",)),
        name="out_proj",
    )(x, y_ssm, conv_out, attn_out, w_glu, w_out)


def _swiglu_step(h, wg, wu, wd):
    g = jnp.dot(h, wg, preferred_element_type=F32)
    u = jnp.dot(h, wu, preferred_element_type=F32)
    a = (g * jax.nn.sigmoid(g) * u).astype(BF16)
    return jnp.dot(a, wd, preferred_element_type=F32)


_FFN_SUB = 512


def _ffn_kernel(x_ref, g_ref, wg_ref, wu_ref, wd_ref, o_ref, h_ref, a_ref, acc_ref):
    j = pl.program_id(1)
    tf = a_ref.shape[1]

    @pl.when(j == 0)
    def _():
        h_ref[...] = _rms(x_ref[...], g_ref[...]).astype(BF16)

    for lo in range(0, tf, _FFN_SUB):
        cols = slice(lo, min(lo + _FFN_SUB, tf))
        gate = jnp.dot(h_ref[...], wg_ref[:, cols], preferred_element_type=F32)
        up = jnp.dot(h_ref[...], wu_ref[:, cols], preferred_element_type=F32)
        a_ref[:, cols] = (gate * jax.nn.sigmoid(gate) * up).astype(BF16)
    y = jnp.dot(a_ref[...], wd_ref[...], preferred_element_type=F32)

    @pl.when(j == 0)
    def _():
        acc_ref[...] = y

    @pl.when(j > 0)
    def _():
        acc_ref[...] += y

    @pl.when(j == pl.num_programs(1) - 1)
    def _():
        o_ref[...] = x_ref[...] + acc_ref[...]


def _ffn(x, g, wg, wu, wd, tm=1024, tf=1408):
    t = x.shape[0]
    tm = min(tm, t)
    return pl.pallas_call(
        _ffn_kernel,
        grid=(t // tm, D_FF // tf),
        in_specs=[pl.BlockSpec((tm, D_MODEL), lambda i, j: (i, 0)), pl.BlockSpec((1, D_MODEL), lambda i, j: (0, 0)),
                  pl.BlockSpec((D_MODEL, tf), lambda i, j: (0, j)), pl.BlockSpec((D_MODEL, tf), lambda i, j: (0, j)),
                  pl.BlockSpec((tf, D_MODEL), lambda i, j: (j, 0))],
        out_specs=pl.BlockSpec((tm, D_MODEL), lambda i, j: (i, 0)),
        out_shape=jax.ShapeDtypeStruct((t, D_MODEL), F32),
        scratch_shapes=[pltpu.VMEM((tm, D_MODEL), BF16), pltpu.VMEM((tm, tf), BF16), pltpu.VMEM((tm, D_MODEL), F32)],
        compiler_params=_cparams(("parallel", "arbitrary")),
        name="ffn_dense",
    )(x, g.reshape(1, D_MODEL), wg, wu, wd)


_MOE_PREFIX = 256


def _moe_route(x_ref, g_ref, r_ref, o_ref, hb_ref, rank_ref, sel_ref, gate_ref, cnt_ref):
    n_blk = x_ref.shape[0] // _MOE_PREFIX

    for blk in range(n_blk):
        rows = slice(blk * _MOE_PREFIX, (blk + 1) * _MOE_PREFIX)
        x = x_ref[rows, :]
        h32 = _rms(x, g_ref[...])
        hb_ref[rows, :] = h32.astype(BF16)
        o_ref[rows, :] = x
        logits = jnp.dot(h32, r_ref[...], preferred_element_type=F32, precision=lax.Precision.HIGHEST)
        rank_ref[:, rows] = logits.T[:N_EXPERTS, :]
    lt = rank_ref[...]
    row = lax.broadcasted_iota(jnp.int32, lt.shape, 0)
    m1 = jnp.max(lt, axis=0, keepdims=True)
    i1 = jnp.min(jnp.where(lt == m1, row, N_EXPERTS), axis=0, keepdims=True)
    lt2 = jnp.where(row == i1, -jnp.inf, lt)
    m2 = jnp.max(lt2, axis=0, keepdims=True)
    i2 = jnp.min(jnp.where(lt2 == m2, row, N_EXPERTS), axis=0, keepdims=True)
    e2 = jnp.exp(m2 - m1)
    den = 1.0 + e2
    gate_ref[...] = jnp.where(row == i1, 1.0 / den, 0.0) + jnp.where(row == i2, e2 / den, 0.0)
    sel = jnp.where(row == i1, 1.0, 0.0) + jnp.where(row == i2, 1.0, 0.0)
    sel_ref[...] = sel
    before = (lax.broadcasted_iota(jnp.int32, (_MOE_PREFIX, _MOE_PREFIX), 0)
              < lax.broadcasted_iota(jnp.int32, (_MOE_PREFIX, _MOE_PREFIX), 1))
    tri = jnp.where(before, 1.0, 0.0).astype(BF16)
    carry = jnp.zeros((N_EXPERTS, 1), F32)
    for blk in range(lt.shape[1] // _MOE_PREFIX):
        cols = slice(blk * _MOE_PREFIX, (blk + 1) * _MOE_PREFIX)
        rank_ref[:, cols] = jnp.dot(sel[:, cols].astype(BF16), tri, preferred_element_type=F32) + carry
        carry = carry + jnp.sum(sel[:, cols], axis=1, keepdims=True)
    for k in range(N_EXPERTS):
        cnt_ref[k] = carry[k, 0].astype(jnp.int32)


def _moe_kernel(x_ref, g_ref, r_ref, gf_ref, wg_ref, wu_ref, wd_ref, o_ref, hb_ref, rank_ref, sel_ref, gate_ref,
                xe_ref, ye_ref, cnt_ref, *, tb, g_rows, f_rows, s_rows, final_norm):
    e = pl.program_id(1)
    f = pl.program_id(2)
    last_f = pl.num_programs(2) - 1

    @pl.when((e == 0) & (f == 0))
    def _():
        _moe_route(x_ref, g_ref, r_ref, o_ref, hb_ref, rank_ref, sel_ref, gate_ref, cnt_ref)

    cnt = cnt_ref[e]
    rank_e = rank_ref[pl.ds(e, 1), :]
    sel_e = sel_ref[pl.ds(e, 1), :]

    def groups(rows):
        return (cnt + rows - 1) // rows

    def onehot(first_row, rows):
        tgt = (lax.broadcasted_iota(jnp.int32, (rows, tb), 0) + first_row).astype(F32)
        return jnp.where((rank_e == tgt) & (sel_e > 0.0), 1.0, 0.0)

    @pl.when(f == 0)
    def _():
        def gather(s, c):
            off = pl.multiple_of(s * g_rows, g_rows)
            xe_ref[pl.ds(off, g_rows), :] = jnp.dot(onehot(off, g_rows).astype(BF16), hb_ref[...],
                                                    preferred_element_type=F32).astype(BF16)
            return c
        lax.fori_loop(0, groups(g_rows), gather, 0)

    full = cnt // s_rows
    rem = cnt - full * s_rows
    tail_off = pl.multiple_of(full * s_rows, s_rows)

    def ffn(first):
        def rows_at(off, rows):
            y = _swiglu_step(xe_ref[pl.ds(off, rows), :], wg_ref[...], wu_ref[...], wd_ref[...])
            if first:
                ye_ref[pl.ds(off, rows), :] = y
            else:
                ye_ref[pl.ds(off, rows), :] += y

        def body(s, c):
            rows_at(pl.multiple_of(s * s_rows, s_rows), s_rows)
            return c
        lax.fori_loop(0, full + (rem > f_rows).astype(jnp.int32), body, 0)

        @pl.when((rem > 0) & (rem <= f_rows))
        def _():
            rows_at(tail_off, f_rows)

    @pl.when(f == 0)
    def _():
        ffn(True)
        if s_rows > f_rows:
            @pl.when((rem > 0) & (rem <= f_rows))
            def _():
                rest = pl.ds(pl.multiple_of(tail_off + f_rows, f_rows), s_rows - f_rows)
                ye_ref[rest, :] = jnp.zeros((s_rows - f_rows, D_MODEL), F32)

    @pl.when(f > 0)
    def _():
        ffn(False)

    @pl.when(f == last_f)
    def _():
        gate_e = gate_ref[pl.ds(e, 1), :]

        def scatter(s, c):
            off = pl.multiple_of(s * s_rows, s_rows)
            p = onehot(off, s_rows)
            gate_rows = jnp.sum(p * gate_e, axis=1, keepdims=True)
            valid = (lax.broadcasted_iota(jnp.int32, (s_rows, 1), 0) + off) < cnt
            yg = jnp.where(valid, gate_rows * ye_ref[pl.ds(off, s_rows), :], 0.0).astype(BF16)
            o_ref[...] += jnp.dot(p.T.astype(BF16), yg, preferred_element_type=F32)
            return c
        lax.fori_loop(0, groups(s_rows), scatter, 0)

    if final_norm:
        @pl.when((e == pl.num_programs(1) - 1) & (f == last_f))
        def _():
            for lo in range(0, tb, _MOE_PREFIX):
                rows = slice(lo, lo + _MOE_PREFIX)
                o_ref[rows, :] = _rms(o_ref[rows, :], gf_ref[...])


def _moe(x, g, router_pad, wg, wu, wd, final_g=None, tb=2048, fc=1408, g_rows=256, f_rows=128, s_rows=256):
    t = x.shape[0]
    tb = min(tb, t)
    gf = jnp.ones((D_MODEL,), F32) if final_g is None else final_g
    assert s_rows in (f_rows, 2 * f_rows) and g_rows % f_rows == 0 and tb % g_rows == 0 and tb % s_rows == 0
    once = pl.Buffered(1)
    return pl.pallas_call(
        functools.partial(_moe_kernel, tb=tb, g_rows=g_rows, f_rows=f_rows, s_rows=s_rows,
                          final_norm=final_g is not None),
        grid=(t // tb, N_EXPERTS, D_FF // fc),
        in_specs=[pl.BlockSpec((tb, D_MODEL), lambda i, e, j: (i, 0), pipeline_mode=once),
                  pl.BlockSpec((1, D_MODEL), lambda i, e, j: (0, 0)),
                  pl.BlockSpec((D_MODEL, LANES), lambda i, e, j: (0, 0)),
                  pl.BlockSpec((1, D_MODEL), lambda i, e, j: (0, 0)),
                  pl.BlockSpec((None, D_MODEL, fc), lambda i, e, j: (e, 0, j)),
                  pl.BlockSpec((None, D_MODEL, fc), lambda i, e, j: (e, 0, j)),
                  pl.BlockSpec((None, fc, D_MODEL), lambda i, e, j: (e, j, 0))],
        out_specs=pl.BlockSpec((tb, D_MODEL), lambda i, e, j: (i, 0), pipeline_mode=once),
        out_shape=jax.ShapeDtypeStruct((t, D_MODEL), F32),
        scratch_shapes=[pltpu.VMEM((tb, D_MODEL), BF16), pltpu.VMEM((N_EXPERTS, tb), F32),
                        pltpu.VMEM((N_EXPERTS, tb), F32), pltpu.VMEM((N_EXPERTS, tb), F32),
                        pltpu.VMEM((tb, D_MODEL), BF16), pltpu.VMEM((tb, D_MODEL), F32),
                        pltpu.SMEM((N_EXPERTS,), jnp.int32)],
        compiler_params=pltpu.CompilerParams(dimension_semantics=("parallel", "arbitrary", "arbitrary"),
                                             vmem_limit_bytes=MOE_VMEM_LIMIT),
        name="ffn_moe",
    )(x, g.reshape(1, D_MODEL), router_pad, gf.reshape(1, D_MODEL), wg, wu, wd)


def _final_norm_kernel(x_ref, g_ref, o_ref):
    o_ref[...] = _rms(x_ref[...], g_ref[...])


def _final_norm(x, g, tm=1024):
    t = x.shape[0]
    tm = min(tm, t)
    return pl.pallas_call(
        _final_norm_kernel,
        grid=(t // tm,),
        in_specs=[pl.BlockSpec((tm, D_MODEL), lambda i: (i, 0)), pl.BlockSpec((1, D_MODEL), lambda i: (0, 0))],
        out_specs=pl.BlockSpec((tm, D_MODEL), lambda i: (i, 0)),
        out_shape=jax.ShapeDtypeStruct((t, D_MODEL), F32),
        compiler_params=_cparams(("parallel",)),
        name="final_norm",
    )(x, g.reshape(1, D_MODEL))


def kernel(x_prompt, x_sample, cache_k, cache_v, state_ssm_re, state_ssm_im, state_conv, norm_mix_g, w_in, ssm_a_re, ssm_a_im, ssm_log_dt, ssm_b_re, ssm_b_im, ssm_c_re, ssm_c_im, ssm_d, ssm_w_glu, conv_w, conv_ln_g, conv_ln_b, attn_lq1, attn_lk1, attn_lq2, attn_lk2, attn_sub_g, w_out, norm_ffn_g, ffn_w_gate, ffn_w_up, ffn_w_down, moe_router, moe_w_gate, moe_w_up, moe_w_down, final_norm_g):
    depth = w_in.shape[0]
    n_p, l_p, _ = x_prompt.shape
    n_s, l_s, _ = x_sample.shape
    past = cache_k.shape[2]
    keep = CONV_KERNEL - 1

    xp = x_prompt.reshape(n_p * l_p, D_MODEL)
    xs = x_sample.reshape(n_s * l_s, D_MODEL)
    cache_kt = jnp.transpose(cache_k, (0, 1, 3, 4, 5, 2)).reshape(depth, n_s, 512, past)
    cache_vr = cache_v.reshape(depth, n_s, past * N_HEADS, V_DIM)
    s0_p = jnp.zeros((n_p, 1, 2 * STATE_W), F32)
    buf_p = jnp.zeros((n_p, keep, CONV_WIDTH), F32)

    ops = _s5_operators(ssm_a_re, ssm_a_im, ssm_log_dt, ssm_b_re, ssm_b_im, ssm_c_re, ssm_c_im, ssm_d, S5_CHUNK)
    kv_bufs = None
    outs = {name: [] for name in ("srp", "sip", "cp", "ks", "vs", "srs", "sis", "cs")}
    for l in range(depth):
        lam_init = 0.8 - 0.6 * math.exp(-0.3 * l)
        w_in_b = w_in[l].astype(BF16)
        lam_params = jnp.stack([attn_lq1[l], attn_lk1[l], attn_lq2[l], attn_lk2[l]]).astype(F32)
        w_glu_b = ssm_w_glu[l].astype(BF16)
        w_out_b = w_out[l].astype(BF16)
        conv_p = (conv_w[l], conv_ln_g[l], conv_ln_b[l])

        u2, hc, q, kb, vt, *kv_bufs = _in_proj_prompt(xp, norm_mix_g[l], w_in_b, kv_bufs, l, depth, n_p)
        y_ssm, sf_p = _s5_group(u2, n_p, l_p, s0_p, ops, l)
        cv, cb_p = _conv(hc.reshape(n_p, l_p, CONV_WIDTH), buf_p, *conv_p, tl=min(512, l_p))
        at = _attn_prompt(q, kb, vt, lam_params, attn_sub_g[l], n_p, l_p, lam_init)
        xp = _out_proj(xp, y_ssm, cv.reshape(n_p * l_p, CONV_WIDTH), at, w_glu_b, w_out_b)

        s0_s = jnp.concatenate([state_ssm_re[l].reshape(n_s, 1, STATE_W), state_ssm_im[l].reshape(n_s, 1, STATE_W)],
                               axis=-1).astype(F32)
        u2, hc, q, kb, vb, k_s, v_s = _in_proj_sample(xs, norm_mix_g[l], w_in_b)
        y_ssm, sf_s = _s5_group(u2, n_s, l_s, s0_s, ops, l)
        cv, cb_s = _conv(hc.reshape(n_s, l_s, CONV_WIDTH), state_conv[l].astype(F32), *conv_p, tl=l_s)
        at = _attn_sample(q, kb, vb, cache_kt, cache_vr, l, lam_params, attn_sub_g[l], lam_init)
        xs = _out_proj(xs, y_ssm, cv.reshape(n_s * l_s, CONV_WIDTH), at, w_glu_b, w_out_b)

        j = l // 2
        if l % 2 == 0:
            wg, wu, wd = ffn_w_gate[j].astype(BF16), ffn_w_up[j].astype(BF16), ffn_w_down[j].astype(BF16)
            xp = _ffn(xp, norm_ffn_g[l], wg, wu, wd)
            xs = _ffn(xs, norm_ffn_g[l], wg, wu, wd)
        else:
            router_pad = jnp.pad(moe_router[j].astype(F32), ((0, 0), (0, LANES - N_EXPERTS)))
            wg, wu, wd = moe_w_gate[j].astype(BF16), moe_w_up[j].astype(BF16), moe_w_down[j].astype(BF16)
            final_g = final_norm_g if l == depth - 1 else None
            xp = _moe(xp, norm_ffn_g[l], router_pad, wg, wu, wd, final_g)
            xs = _moe(xs, norm_ffn_g[l], router_pad, wg, wu, wd, final_g)

        outs["srp"].append(sf_p[:, 0, :STATE_W].reshape(n_p, N_SSM_GROUPS, SSM_STATE))
        outs["sip"].append(sf_p[:, 0, STATE_W:].reshape(n_p, N_SSM_GROUPS, SSM_STATE))
        outs["cp"].append(cb_p)
        outs["ks"].append(k_s.reshape(n_s, l_s, 2, N_HEADS, HEAD_DIM))
        outs["vs"].append(v_s.reshape(n_s, l_s, N_HEADS, V_DIM))
        outs["srs"].append(sf_s[:, 0, :STATE_W].reshape(n_s, N_SSM_GROUPS, SSM_STATE))
        outs["sis"].append(sf_s[:, 0, STATE_W:].reshape(n_s, N_SSM_GROUPS, SSM_STATE))
        outs["cs"].append(cb_s)

    if depth % 2 == 0:
        yp, ys = xp, xs
    else:
        yp = _final_norm(xp, final_norm_g)
        ys = _final_norm(xs, final_norm_g)
    st = {name: jnp.stack(vals) for name, vals in outs.items()}
    kt_buf, v_buf = kv_bufs
    k_prompt = jnp.transpose(kt_buf.reshape(depth, n_p, 2, N_HEADS, HEAD_DIM, l_p), (0, 1, 5, 2, 3, 4))
    v_prompt = v_buf.reshape(depth, n_p, l_p, N_HEADS, V_DIM)
    return (yp.reshape(n_p, l_p, D_MODEL), ys.reshape(n_s, l_s, D_MODEL),
            k_prompt, v_prompt, st["srp"], st["sip"], st["cp"],
            st["ks"], st["vs"], st["srs"], st["sis"], st["cs"])
```

```python
import functools
import math

import jax
import jax.numpy as jnp
from jax import lax
from jax.experimental import pallas as pl
from jax.experimental.pallas import tpu as pltpu

F32 = jnp.float32
BF16 = jnp.bfloat16

D_MODEL = 1024
CHUNK = 64
SSM_GROUP = 16
N_SSM_GROUPS = 16
SSM_WIDTH = 256
SSM_STATE = 64
STATE_W = N_SSM_GROUPS * SSM_STATE
CONV_WIDTH = 256
CONV_KERNEL = 31
N_HEADS = 4
HEAD_DIM = 64
V_DIM = 128
VT_ROWS = V_DIM + 16
QK_WIDTH = 256
ATTN_WIDTH = 512
IN_WIDTH = 2304
ATTN_SCALE = 1.0 / math.sqrt(HEAD_DIM)
LOG2E = math.log2(math.e)
NEG_INF = -1e30
D_FF = 2816
N_EXPERTS = 8
EPS = 1e-6

S5_CHUNK = 8
LANES = 128
SUBLANES = 8
VMEM_LIMIT = 48 * 1024 * 1024
MOE_VMEM_LIMIT = 58 * 1024 * 1024


def _cparams(sem):
    return pltpu.CompilerParams(dimension_semantics=sem, vmem_limit_bytes=VMEM_LIMIT)


def _rms(x, g):
    return x * lax.rsqrt(jnp.mean(x * x, axis=-1, keepdims=True) + EPS) * g


def _mixer_inputs(x_ref, g_ref, w_ref, u2_ref, hc_ref, q_ref, us_ref):
    h = _rms(x_ref[...], g_ref[...]).astype(BF16)

    def proj(lo, hi):
        return jnp.dot(h, w_ref[:, lo:hi], preferred_element_type=F32)

    u = proj(0, 256)
    n_chunks = us_ref.shape[1] // S5_CHUNK
    for c in range(SSM_WIDTH // LANES):
        us_ref[c] = u[:, c * LANES:(c + 1) * LANES]
        for j in range(S5_CHUNK):
            lo = j * SSM_WIDTH + c * LANES
            u2_ref[:, lo:lo + LANES] = us_ref[c, pl.ds(j, n_chunks, stride=S5_CHUNK), :]
    c_val = proj(256, 512)
    c_gate = proj(512, 768)
    hc_ref[...] = c_val * jax.nn.sigmoid(c_gate)
    q_ref[...] = (proj(768, 1280) * (ATTN_SCALE * LOG2E)).astype(BF16)
    return h, proj(1280, 1792), proj(1792, 2304)


def _in_proj_prompt_kernel(*refs, aliased):
    x_ref, g_ref, w_ref = refs[:3]
    u2_ref, hc_ref, q_ref, kb_ref, vt_ref, kt_ref, v_ref, us_ref = refs[5 if aliased else 3:]
    h, k, v = _mixer_inputs(x_ref, g_ref, w_ref, u2_ref, hc_ref, q_ref, us_ref)
    kb_ref[...] = k.astype(BF16)
    for head in range(N_HEADS):
        v_ref[pl.ds(head, v.shape[0], stride=N_HEADS), :] = v[:, head * V_DIM:(head + 1) * V_DIM]
    kt_ref[...] = k.T
    vt = v.T.astype(BF16)
    for head in range(N_HEADS):
        vt_ref[head * VT_ROWS:head * VT_ROWS + V_DIM, :] = vt[head * V_DIM:(head + 1) * V_DIM, :]
        vt_ref[head * VT_ROWS + V_DIM:(head + 1) * VT_ROWS, :] = jnp.ones((VT_ROWS - V_DIM, vt.shape[1]), BF16)


def _in_proj_prompt(x, g, w_bf16, kv_bufs, layer, depth, n_seq, tm=512):
    t = x.shape[0]
    seq_len = t // n_seq
    per_seq = seq_len // tm
    row = lambda width: pl.BlockSpec((tm, width), lambda i: (i, 0))
    in_specs = [row(D_MODEL), pl.BlockSpec((1, D_MODEL), lambda i: (0, 0)),
                pl.BlockSpec((D_MODEL, IN_WIDTH), lambda i: (0, 0))]
    args = [x, g.reshape(1, D_MODEL), w_bf16]
    aliases = {}
    if kv_bufs is not None:
        in_specs += [pl.BlockSpec(memory_space=pl.ANY), pl.BlockSpec(memory_space=pl.ANY)]
        args += list(kv_bufs)
        aliases = {3: 5, 4: 6}
    return pl.pallas_call(
        functools.partial(_in_proj_prompt_kernel, aliased=kv_bufs is not None),
        grid=(t // tm,),
        in_specs=in_specs,
        out_specs=[pl.BlockSpec((tm // S5_CHUNK, S5_CHUNK * SSM_WIDTH), lambda i: (i, 0)),
                   row(256), row(512), row(512),
                   pl.BlockSpec((N_HEADS * VT_ROWS, tm), lambda i: (0, i)),
                   pl.BlockSpec((None, None, ATTN_WIDTH, tm), lambda i: (layer, i // per_seq, 0, i % per_seq)),
                   pl.BlockSpec((None, None, tm * N_HEADS, V_DIM), lambda i: (layer, i // per_seq, i % per_seq, 0))],
        out_shape=[jax.ShapeDtypeStruct((t // S5_CHUNK, S5_CHUNK * SSM_WIDTH), F32),
                   jax.ShapeDtypeStruct((t, 256), F32),
                   jax.ShapeDtypeStruct((t, 512), BF16), jax.ShapeDtypeStruct((t, 512), BF16),
                   jax.ShapeDtypeStruct((N_HEADS * VT_ROWS, t), BF16),
                   jax.ShapeDtypeStruct((depth, n_seq, ATTN_WIDTH, seq_len), F32),
                   jax.ShapeDtypeStruct((depth, n_seq, seq_len * N_HEADS, V_DIM), F32)],
        scratch_shapes=[pltpu.VMEM((SSM_WIDTH // LANES, tm, LANES), F32)],
        input_output_aliases=aliases,
        compiler_params=_cparams(("parallel",)),
        name="in_proj_prompt",
    )(*args)


def _in_proj_sample_kernel(x_ref, g_ref, w_ref, u2_ref, hc_ref, q_ref, kb_ref, vb_ref, k_ref, v_ref, us_ref):
    _, k, v = _mixer_inputs(x_ref, g_ref, w_ref, u2_ref, hc_ref, q_ref, us_ref)
    k_ref[...] = k
    kb_ref[...] = k.astype(BF16)
    v_ref[...] = v
    vb_ref[...] = v.astype(BF16)


def _in_proj_sample(x, g, w_bf16, tm=512):
    t = x.shape[0]
    tm = min(tm, t)
    row = lambda width: pl.BlockSpec((tm, width), lambda i: (i, 0))
    return pl.pallas_call(
        _in_proj_sample_kernel,
        grid=(t // tm,),
        in_specs=[row(D_MODEL), pl.BlockSpec((1, D_MODEL), lambda i: (0, 0)),
                  pl.BlockSpec((D_MODEL, IN_WIDTH), lambda i: (0, 0))],
        out_specs=[pl.BlockSpec((tm // S5_CHUNK, S5_CHUNK * SSM_WIDTH), lambda i: (i, 0)),
                   row(256), row(512), row(512), row(512), row(512), row(512)],
        out_shape=[jax.ShapeDtypeStruct((t // S5_CHUNK, S5_CHUNK * SSM_WIDTH), F32),
                   jax.ShapeDtypeStruct((t, 256), F32),
                   jax.ShapeDtypeStruct((t, 512), BF16), jax.ShapeDtypeStruct((t, 512), BF16),
                   jax.ShapeDtypeStruct((t, 512), BF16), jax.ShapeDtypeStruct((t, 512), F32),
                   jax.ShapeDtypeStruct((t, 512), F32)],
        scratch_shapes=[pltpu.VMEM((SSM_WIDTH // LANES, tm, LANES), F32)],
        compiler_params=_cparams(("parallel",)),
        name="in_proj_sample",
    )(x, g.reshape(1, D_MODEL), w_bf16)


def _mm_kernel(x_ref, w_ref, o_ref):
    o_ref[...] = jnp.dot(x_ref[...].astype(BF16), w_ref[...], preferred_element_type=F32)


def _mm(x, w_bf16, layer, tm, tn, name):
    m, k = x.shape
    n = w_bf16.shape[2]
    return pl.pallas_call(
        _mm_kernel,
        grid=(n // tn, m // tm),
        in_specs=[pl.BlockSpec((tm, k), lambda j, i: (i, 0)),
                  pl.BlockSpec((None, k, tn), lambda j, i: (layer, 0, j))],
        out_specs=pl.BlockSpec((tm, tn), lambda j, i: (i, j)),
        out_shape=jax.ShapeDtypeStruct((m, n), F32),
        compiler_params=_cparams(("parallel", "parallel")),
        name=name,
    )(x, w_bf16)


def _s5_scan_kernel(d_ref, s0_ref, a_ref, ss_ref, sf_ref, st_ref, *, rows):
    t = pl.program_id(1)

    @pl.when(t == 0)
    def _():
        st_ref[...] = s0_ref[...]

    a_re = a_ref[:, :STATE_W]
    a_im = a_ref[:, STATE_W:]

    def step(c, carry):
        s_re, s_im = carry
        ss_ref[pl.ds(c, 1), :STATE_W] = s_re
        ss_ref[pl.ds(c, 1), STATE_W:] = s_im
        d = d_ref[pl.ds(c, 1), :]
        n_re = a_re * s_re - a_im * s_im + d[:, :STATE_W]
        n_im = a_re * s_im + a_im * s_re + d[:, STATE_W:]
        return n_re, n_im

    s_re, s_im = lax.fori_loop(0, rows, step, (st_ref[:, :STATE_W], st_ref[:, STATE_W:]))
    st_ref[:, :STATE_W] = s_re
    st_ref[:, STATE_W:] = s_im
    sf_ref[...] = st_ref[...]


def _s5_scan(d, n_seq, n_chunks, s0, a_c, layer):
    rows = min(n_chunks, 256)
    nt = n_chunks // rows
    w = 2 * STATE_W
    return pl.pallas_call(
        functools.partial(_s5_scan_kernel, rows=rows),
        grid=(n_seq, nt),
        in_specs=[pl.BlockSpec((rows, w), lambda s, t: (s * nt + t, 0)),
                  pl.BlockSpec((None, 1, w), lambda s, t: (s, 0, 0)),
                  pl.BlockSpec((None, 1, w), lambda s, t: (layer, 0, 0))],
        out_specs=[pl.BlockSpec((rows, w), lambda s, t: (s * nt + t, 0)),
                   pl.BlockSpec((None, 1, w), lambda s, t: (s, 0, 0))],
        out_shape=[jax.ShapeDtypeStruct((n_seq * n_chunks, w), F32), jax.ShapeDtypeStruct((n_seq, 1, w), F32)],
        scratch_shapes=[pltpu.VMEM((1, w), F32)],
        compiler_params=_cparams(("parallel", "arbitrary")),
        name="s5_scan",
    )(d, s0, a_c)


def _s5_out_kernel(u_ref, ut_ref, ss_ref, wi_ref, ws_ref, dsk_ref, y_ref):
    y_ref[...] = (jnp.dot(u_ref[...].astype(BF16), wi_ref[...], preferred_element_type=F32)
                  + jnp.dot(ss_ref[...].astype(BF16), ws_ref[...], preferred_element_type=F32)
                  + dsk_ref[...] * ut_ref[...])


def _s5_out(u2, s_start, w_intra, w_inter, dsk, layer, tm=256, tn=1024):
    m, n = u2.shape
    k = s_start.shape[1]
    return pl.pallas_call(
        _s5_out_kernel,
        grid=(n // tn, m // tm),
        in_specs=[pl.BlockSpec((tm, n), lambda j, i: (i, 0)), pl.BlockSpec((tm, tn), lambda j, i: (i, j)),
                  pl.BlockSpec((tm, k), lambda j, i: (i, 0)),
                  pl.BlockSpec((None, n, tn), lambda j, i: (layer, 0, j)),
                  pl.BlockSpec((None, k, tn), lambda j, i: (layer, 0, j)),
                  pl.BlockSpec((None, 1, tn), lambda j, i: (layer, 0, j))],
        out_specs=pl.BlockSpec((tm, tn), lambda j, i: (i, j)),
        out_shape=jax.ShapeDtypeStruct((m, n), F32),
        compiler_params=_cparams(("parallel", "parallel")),
        name="s5_out",
    )(u2, u2, s_start, w_intra, w_inter, dsk)


def _s5_operators(a_re, a_im, log_dt, b_re, b_im, c_re, c_im, d_skip, n_c):
    f32 = lambda x: x.astype(F32)
    a_re, a_im, b_re, b_im, c_re, c_im = map(f32, (a_re, a_im, b_re, b_im, c_re, c_im))
    n_l = a_re.shape[0]
    g_n = N_SSM_GROUPS
    dt = jnp.exp(f32(log_dt))[:, None, :, None]
    ks = jnp.arange(n_c + 1, dtype=F32)[None, :, None, None]
    mag = jnp.exp(a_re[:, None] * dt * ks)
    ang = a_im[:, None] * dt * ks
    pw_re, pw_im = mag * jnp.cos(ang), mag * jnp.sin(ang)
    den = a_re * a_re + a_im * a_im
    q_re = ((pw_re[:, 1] - 1.0) * a_re + pw_im[:, 1] * a_im) / den
    q_im = (pw_im[:, 1] * a_re - (pw_re[:, 1] - 1.0) * a_im) / den
    bb_re = q_re[..., None] * b_re - q_im[..., None] * b_im
    bb_im = q_re[..., None] * b_im + q_im[..., None] * b_re
    eye_g = jnp.eye(g_n, dtype=F32)

    def block_diag(x, rows, cols):
        return (x[:, :, :, None, :] * eye_g[None, :, None, :, None]).reshape(n_l, rows, cols)

    bt_re = block_diag(jnp.transpose(bb_re, (0, 1, 3, 2)), SSM_WIDTH, STATE_W)[:, None]
    bt_im = block_diag(jnp.transpose(bb_im, (0, 1, 3, 2)), SSM_WIDTH, STATE_W)[:, None]
    pj_re = pw_re[:, :n_c][:, ::-1].reshape(n_l, n_c, 1, STATE_W)
    pj_im = pw_im[:, :n_c][:, ::-1].reshape(n_l, n_c, 1, STATE_W)
    w_state = jnp.concatenate([(pj_re * bt_re - pj_im * bt_im).reshape(n_l, n_c * SSM_WIDTH, STATE_W),
                               (pj_re * bt_im + pj_im * bt_re).reshape(n_l, n_c * SSM_WIDTH, STATE_W)], axis=2)

    m_re = c_re[:, None] * pw_re[:, :n_c, :, None, :] - c_im[:, None] * pw_im[:, :n_c, :, None, :]
    m_im = c_re[:, None] * pw_im[:, :n_c, :, None, :] + c_im[:, None] * pw_re[:, :n_c, :, None, :]
    hi = lax.Precision.HIGHEST
    ker = (jnp.einsum('lkghp,lgpx->lkghx', m_re, bb_re, precision=hi)
           - jnp.einsum('lkghp,lgpx->lkghx', m_im, bb_im, precision=hi))
    blk = jnp.transpose(ker, (0, 1, 2, 4, 3))
    blk = (blk[:, :, :, :, None, :] * eye_g[None, None, :, None, :, None]).reshape(n_l, n_c, 256, 256).astype(BF16)
    zero = jnp.zeros((n_l, 256, 256), BF16)
    w_intra = jnp.concatenate(
        [jnp.concatenate([blk[:, t - j] if t >= j else zero for t in range(n_c)], axis=2) for j in range(n_c)],
        axis=1)

    ct_re = block_diag(jnp.transpose(c_re, (0, 1, 3, 2)), STATE_W, SSM_WIDTH)
    ct_im = block_diag(jnp.transpose(c_im, (0, 1, 3, 2)), STATE_W, SSM_WIDTH)
    pt_re = pw_re[:, 1:].reshape(n_l, n_c, STATE_W, 1)
    pt_im = pw_im[:, 1:].reshape(n_l, n_c, STATE_W, 1)
    top = jnp.concatenate([pt_re[:, t] * ct_re - pt_im[:, t] * ct_im for t in range(n_c)], axis=2)
    bot = jnp.concatenate([-(pt_re[:, t] * ct_im + pt_im[:, t] * ct_re) for t in range(n_c)], axis=2)
    w_inter = jnp.concatenate([top, bot], axis=1)

    a_c = jnp.concatenate([pw_re[:, n_c].reshape(n_l, 1, STATE_W), pw_im[:, n_c].reshape(n_l, 1, STATE_W)], axis=2)
    dsk = jnp.tile(f32(d_skip).reshape(n_l, 1, SSM_WIDTH), (1, 1, n_c))
    return w_state.astype(BF16), w_intra, w_inter.astype(BF16), a_c, dsk


def _s5_group(u2, n_seq, seq_len, s0, ops, layer):
    w_state, w_intra, w_inter, a_c, dsk = ops
    n_chunks = seq_len // S5_CHUNK
    d_state = _mm(u2, w_state, layer, 256, 2 * STATE_W, "s5_state")
    s_start, s_final = _s5_scan(d_state, n_seq, n_chunks, s0, a_c, layer)
    return _s5_out(u2, s_start, w_intra, w_inter, dsk, layer), s_final


_CONV_PAD = 32
_CONV_RB = 64


def _conv_kernel(h_ref, buf_ref, w_ref, g_ref, b_ref, y_ref, nb_ref, xp_ref, xs_ref, *, tl):
    t = pl.program_id(1)
    keep = CONV_KERNEL - 1
    lo = _CONV_PAD - keep

    @pl.when(t == 0)
    def _():
        xp_ref[lo:_CONV_PAD, :] = buf_ref[...]

    @pl.when(t > 0)
    def _():
        xp_ref[lo:_CONV_PAD, :] = xp_ref[tl + lo:tl + _CONV_PAD, :]

    xp_ref[_CONV_PAD:_CONV_PAD + tl, :] = h_ref[...]
    for phase in range(SUBLANES):
        span = tl + (keep - phase) // SUBLANES * SUBLANES
        xs_ref[phase, :span, :] = xp_ref[lo + phase:lo + phase + span, :]
    for r in range(tl // _CONV_RB):
        base = r * _CONV_RB
        acc = jnp.zeros((_CONV_RB, CONV_WIDTH), F32)
        for k in range(CONV_KERNEL):
            start = base + k - k % SUBLANES
            acc = acc + w_ref[k:k + 1, :] * xs_ref[k % SUBLANES, start:start + _CONV_RB, :]
        xc = acc - jnp.mean(acc, axis=-1, keepdims=True)
        var = jnp.mean(xc * xc, axis=-1, keepdims=True)
        y = xc * lax.rsqrt(var + EPS) * g_ref[...] + b_ref[...]
        y_ref[base:base + _CONV_RB, :] = y * jax.nn.sigmoid(y)
    nb_ref[...] = xp_ref[tl + lo:tl + _CONV_PAD, :]


def _conv(h, buf, w, ln_g, ln_b, tl):
    s, l, c = h.shape
    keep = CONV_KERNEL - 1
    return pl.pallas_call(
        functools.partial(_conv_kernel, tl=tl),
        grid=(s, l // tl),
        in_specs=[pl.BlockSpec((None, tl, c), lambda i, t: (i, t, 0)),
                  pl.BlockSpec((None, keep, c), lambda i, t: (i, 0, 0)),
                  pl.BlockSpec((CONV_KERNEL, c), lambda i, t: (0, 0)),
                  pl.BlockSpec((1, c), lambda i, t: (0, 0)), pl.BlockSpec((1, c), lambda i, t: (0, 0))],
        out_specs=[pl.BlockSpec((None, tl, c), lambda i, t: (i, t, 0)),
                   pl.BlockSpec((None, keep, c), lambda i, t: (i, 0, 0))],
        out_shape=[jax.ShapeDtypeStruct((s, l, c), F32), jax.ShapeDtypeStruct((s, keep, c), F32)],
        scratch_shapes=[pltpu.VMEM((tl + _CONV_PAD, c), F32),
                        pltpu.VMEM((SUBLANES, tl + (keep // SUBLANES) * SUBLANES, c), F32)],
        compiler_params=_cparams(("parallel", "arbitrary")),
        name="conv_module",
    )(h, buf, w, ln_g.reshape(1, c), ln_b.reshape(1, c))


def _stack_q(q_ref, qs_ref):
    tq = q_ref.shape[0]
    lane = lax.broadcasted_iota(jnp.int32, (tq, LANES), 1)
    for b in range(4):
        qb = q_ref[:, b * LANES:(b + 1) * LANES]
        qs_ref[b, :tq, :] = jnp.where(lane < HEAD_DIM, qb, jnp.zeros_like(qb))
        qs_ref[b, tq:, :] = jnp.where(lane >= HEAD_DIM, qb, jnp.zeros_like(qb))


def _attn_update(score, value, m_ref, l_ref, acc_ref, tq):
    def softmax(b, s):
        m_old = m_ref[b]
        m_new = jnp.maximum(m_old, jnp.max(s, axis=-1, keepdims=True))
        alpha = jnp.exp2(m_old - m_new)
        p = jnp.exp2(s - m_new)
        l_ref[b] = alpha * l_ref[b] + jnp.sum(p, axis=-1, keepdims=True)
        m_ref[b] = m_new
        return alpha, p.astype(BF16)

    def values(b, alpha, pb):
        h0 = 2 * (b % 2)
        pv0 = jnp.dot(pb[:tq], value(h0), preferred_element_type=F32)
        pv1 = jnp.dot(pb[tq:], value(h0 + 1), preferred_element_type=F32)
        acc_ref[b, :tq, :] = alpha[:tq] * acc_ref[b, :tq, :] + pv0
        acc_ref[b, tq:, :] = alpha[tq:] * acc_ref[b, tq:, :] + pv1

    s0 = score(0)
    s1 = score(1)
    s2 = score(2)
    a0, p0 = softmax(0, s0)
    s3 = score(3)
    a1, p1 = softmax(1, s1)
    values(0, a0, p0)
    a2, p2 = softmax(2, s2)
    values(1, a1, p1)
    a3, p3 = softmax(3, s3)
    values(2, a2, p2)
    values(3, a3, p3)


def _attn_finish(lp_ref, sg_ref, l_ref, acc_ref, o_ref, tq, lam_init):
    lam = _lambda(lp_ref, lam_init)
    for h in range(N_HEADS):
        b1, half = h // 2, h % 2
        rows = slice(half * tq, (half + 1) * tq)
        o1 = acc_ref[b1, rows, :] / l_ref[b1, rows, :]
        o2 = acc_ref[b1 + 2, rows, :] / l_ref[b1 + 2, rows, :]
        o = o1 - lam * o2
        o_ref[:, h * V_DIM:(h + 1) * V_DIM] = (_rms(o, sg_ref[...]) * (1.0 - lam_init)).astype(o_ref.dtype)


def _attn_init(m_ref, l_ref, acc_ref):
    m_ref[...] = jnp.full(m_ref.shape, NEG_INF, F32)
    l_ref[...] = jnp.zeros(l_ref.shape, F32)
    acc_ref[...] = jnp.zeros(acc_ref.shape, F32)


def _attn_prompt_update(qs_ref, k_ref, vt_ref, m_ref, acc_ref, tq, mask):
    def scores(c):
        b, r = divmod(c, 2)
        s = lax.dot_general(k_ref[:, b * LANES:(b + 1) * LANES], qs_ref[b, r * tq:(r + 1) * tq, :],
                            (((1,), (1,)), ((), ())), preferred_element_type=F32)
        return s if mask is None else jnp.where(mask, s, NEG_INF)

    def softmax(c, s):
        b, r = divmod(c, 2)
        cols = slice(r * tq, (r + 1) * tq)
        sb = s.astype(BF16)
        m_old = m_ref[b, :, cols]
        m_new = jnp.maximum(m_old, jnp.max(sb, axis=0, keepdims=True).astype(F32))
        m_ref[b, :, cols] = m_new
        return jnp.exp2(m_old - m_new), jnp.exp2(sb - m_new.astype(BF16))

    def values(c, alpha, pb):
        h = 2 * ((c // 2) % 2) + c % 2
        pv = jnp.dot(vt_ref[h * VT_ROWS:(h + 1) * VT_ROWS, :], pb, preferred_element_type=F32)
        acc_ref[c] = alpha * acc_ref[c] + pv

    n_chain, ahead = 8, 3
    pending = [scores(c) for c in range(ahead)]
    for c in range(n_chain):
        if c + ahead < n_chain:
            pending.append(scores(c + ahead))
        alpha, pb = softmax(c, pending.pop(0))
        values(c, alpha, pb)


def _lambda(lp_ref, lam_init):
    lp = lp_ref[...]
    return (jnp.exp(jnp.sum(lp[0:1] * lp[1:2], axis=-1, keepdims=True))
            - jnp.exp(jnp.sum(lp[2:3] * lp[3:4], axis=-1, keepdims=True)) + lam_init)


def _attn_prompt_finish(lp_ref, sgt_ref, acc_ref, o_ref, tq, lam_init):
    lam = _lambda(lp_ref, lam_init)
    for h in range(N_HEADS):
        i1 = 2 * (h // 2) + h % 2
        o1 = acc_ref[i1, :V_DIM, :] / acc_ref[i1, V_DIM:V_DIM + 1, :]
        o2 = acc_ref[i1 + 4, :V_DIM, :] / acc_ref[i1 + 4, V_DIM:V_DIM + 1, :]
        o = o1 - lam * o2
        on = o * lax.rsqrt(jnp.mean(o * o, axis=0, keepdims=True) + EPS) * sgt_ref[...] * (1.0 - lam_init)
        o_ref[:, h * V_DIM:(h + 1) * V_DIM] = on.T.astype(o_ref.dtype)


def _attn_prompt_kernel(it_ref, jt_ref, lp_ref, sgt_ref, q_ref, k_ref, vt_ref, o_ref, qs_ref, m_ref, acc_ref,
                        *, tq, lam_init):
    step = pl.program_id(1)
    i = it_ref[step]
    j = jt_ref[step]

    @pl.when(j == 0)
    def _():
        m_ref[...] = jnp.full(m_ref.shape, NEG_INF, F32)
        acc_ref[...] = jnp.zeros(acc_ref.shape, F32)
        _stack_q(q_ref, qs_ref)

    @pl.when(j < i)
    def _():
        _attn_prompt_update(qs_ref, k_ref, vt_ref, m_ref, acc_ref, tq, None)

    @pl.when(j == i)
    def _():
        tk = k_ref.shape[0]
        key_chunk = lax.broadcasted_iota(jnp.int32, (tk, tq), 0) // CHUNK
        qry_chunk = lax.broadcasted_iota(jnp.int32, (tk, tq), 1) // CHUNK
        _attn_prompt_update(qs_ref, k_ref, vt_ref, m_ref, acc_ref, tq, qry_chunk >= key_chunk)
        _attn_prompt_finish(lp_ref, sgt_ref, acc_ref, o_ref, tq, lam_init)


def _attn_prompt(q, kb, vt, lam_params, sub_g, n_seq, seq_len, lam_init, tq=512):
    nq = seq_len // tq
    pairs = [(i, j) for i in range(nq) for j in range(i + 1)]
    i_tab = jnp.asarray([p[0] for p in pairs], jnp.int32)
    j_tab = jnp.asarray([p[1] for p in pairs], jnp.int32)
    grid_spec = pltpu.PrefetchScalarGridSpec(
        num_scalar_prefetch=2,
        grid=(n_seq, len(pairs)),
        in_specs=[pl.BlockSpec((4, HEAD_DIM), lambda b, s, it, jt: (0, 0)),
                  pl.BlockSpec((V_DIM, 1), lambda b, s, it, jt: (0, 0)),
                  pl.BlockSpec((tq, 512), lambda b, s, it, jt: (b * nq + it[s], 0)),
                  pl.BlockSpec((tq, 512), lambda b, s, it, jt: (b * nq + jt[s], 0)),
                  pl.BlockSpec((N_HEADS * VT_ROWS, tq), lambda b, s, it, jt: (0, b * nq + jt[s]))],
        out_specs=pl.BlockSpec((tq, 512), lambda b, s, it, jt: (b * nq + it[s], 0)),
        scratch_shapes=[pltpu.VMEM((4, 2 * tq, LANES), BF16), pltpu.VMEM((4, 1, 2 * tq), F32),
                        pltpu.VMEM((8, VT_ROWS, tq), F32)],
    )
    return pl.pallas_call(
        functools.partial(_attn_prompt_kernel, tq=tq, lam_init=lam_init),
        grid_spec=grid_spec,
        out_shape=jax.ShapeDtypeStruct((n_seq * seq_len, 512), BF16),
        compiler_params=_cparams(("parallel", "arbitrary")),
        name="attn_prompt",
    )(i_tab, j_tab, lam_params, sub_g.reshape(V_DIM, 1), q, kb, vt)


def _attn_sample_kernel(lp_ref, sg_ref, q_ref, ckt_ref, cv_ref, kn_ref, vn_ref, o_ref, qs_ref, m_ref, l_ref, acc_ref,
                        *, tq, tk, lam_init):
    j = pl.program_id(1)

    @pl.when(j == 0)
    def _():
        _attn_init(m_ref, l_ref, acc_ref)
        _stack_q(q_ref, qs_ref)

    def cache_score(b):
        return jnp.dot(qs_ref[b], ckt_ref[b * LANES:(b + 1) * LANES, :].astype(BF16), preferred_element_type=F32)

    def cache_value(h):
        return cv_ref[pl.ds(h, tk, stride=N_HEADS), :].astype(BF16)

    _attn_update(cache_score, cache_value, m_ref, l_ref, acc_ref, tq)

    @pl.when(j == pl.num_programs(1) - 1)
    def _():
        def new_score(b):
            return lax.dot_general(qs_ref[b], kn_ref[:, b * LANES:(b + 1) * LANES], (((1,), (1,)), ((), ())),
                                   preferred_element_type=F32)

        def new_value(h):
            return vn_ref[:, h * V_DIM:(h + 1) * V_DIM]

        _attn_update(new_score, new_value, m_ref, l_ref, acc_ref, tq)
        _attn_finish(lp_ref, sg_ref, l_ref, acc_ref, o_ref, tq, lam_init)


def _attn_sample(q, kb, vb, cache_kt, cache_v, layer, lam_params, sub_g, lam_init, tk=2048):
    _, n_seq, _, past = cache_kt.shape
    tk = min(tk, past)
    assert past % tk == 0
    tq = q.shape[0] // n_seq
    new = pl.BlockSpec((tq, 512), lambda b, j: (b, 0))
    return pl.pallas_call(
        functools.partial(_attn_sample_kernel, tq=tq, tk=tk, lam_init=lam_init),
        grid=(n_seq, past // tk),
        in_specs=[pl.BlockSpec((4, HEAD_DIM), lambda b, j: (0, 0)), pl.BlockSpec((1, V_DIM), lambda b, j: (0, 0)),
                  new,
                  pl.BlockSpec((None, None, 512, tk), lambda b, j: (layer, b, 0, j)),
                  pl.BlockSpec((None, None, tk * N_HEADS, V_DIM), lambda b, j: (layer, b, j, 0)),
                  new, new],
        out_specs=new,
        out_shape=jax.ShapeDtypeStruct((n_seq * tq, 512), BF16),
        scratch_shapes=[pltpu.VMEM((4, 2 * tq, LANES), BF16), pltpu.VMEM((4, 2 * tq, 1), F32),
                        pltpu.VMEM((4, 2 * tq, 1), F32), pltpu.VMEM((4, 2 * tq, V_DIM), F32)],
        compiler_params=_cparams(("parallel", "arbitrary")),
        name="attn_sample",
    )(lam_params, sub_g.reshape(1, V_DIM), q, cache_kt, cache_v, kb, vb)


def _gelu_tanh(x):
    return 0.5 * x * (1.0 + jnp.tanh(math.sqrt(2.0 / math.pi) * (x + 0.044715 * (x * x * x))))


def _out_proj_kernel(x_ref, y2_ref, cv_ref, at_ref, wglu_ref, wo_ref, o_ref, ys_ref):
    n_chunks = y2_ref.shape[0]
    for c in range(SSM_WIDTH // LANES):
        for j in range(S5_CHUNK):
            lo = j * SSM_WIDTH + c * LANES
            ys_ref[c, pl.ds(j, n_chunks, stride=S5_CHUNK), :] = y2_ref[:, lo:lo + LANES]
    z = _gelu_tanh(jnp.concatenate([ys_ref[c] for c in range(SSM_WIDTH // LANES)], axis=1))
    gate = jax.nn.sigmoid(jnp.dot(z.astype(BF16), wglu_ref[...], preferred_element_type=F32))
    ssm = (z * gate).astype(BF16)
    acc = jnp.dot(ssm, wo_ref[0:256, :], preferred_element_type=F32)
    acc = acc + jnp.dot(cv_ref[...].astype(BF16), wo_ref[256:512, :], preferred_element_type=F32)
    acc = acc + jnp.dot(at_ref[...], wo_ref[512:1024, :], preferred_element_type=F32)
    o_ref[...] = x_ref[...] + acc


def _out_proj(x, y_ssm, conv_out, attn_out, w_glu, w_out, tm=512):
    t = x.shape[0]
    tm = min(tm, t)
    row = lambda width: pl.BlockSpec((tm, width), lambda i: (i, 0))
    return pl.pallas_call(
        _out_proj_kernel,
        grid=(t // tm,),
        in_specs=[row(D_MODEL), pl.BlockSpec((tm // S5_CHUNK, S5_CHUNK * SSM_WIDTH), lambda i: (i, 0)),
                  row(256), row(512),
                  pl.BlockSpec((256, 256), lambda i: (0, 0)), pl.BlockSpec((D_MODEL, D_MODEL), lambda i: (0, 0))],
        out_specs=row(D_MODEL),
        out_shape=jax.ShapeDtypeStruct((t, D_MODEL), F32),
        scratch_shapes=[pltpu.VMEM((SSM_WIDTH // LANES, tm, LANES), F32)],
        compiler_params=_cparams(("parallel",)),
        name="out_proj",
    )(x, y_ssm, conv_out, attn_out, w_glu, w_out)


def _swiglu_step(h, wg, wu, wd):
    g = jnp.dot(h, wg, preferred_element_type=F32)
    u = jnp.dot(h, wu, preferred_element_type=F32)
    a = (g * jax.nn.sigmoid(g) * u).astype(BF16)
    return jnp.dot(a, wd, preferred_element_type=F32)


_FFN_SUB = 512


def _ffn_kernel(x_ref, g_ref, wg_ref, wu_ref, wd_ref, o_ref, h_ref, a_ref, acc_ref):
    j = pl.program_id(1)
    tf = a_ref.shape[1]

    @pl.when(j == 0)
    def _():
        h_ref[...] = _rms(x_ref[...], g_ref[...]).astype(BF16)

    for lo in range(0, tf, _FFN_SUB):
        cols = slice(lo, min(lo + _FFN_SUB, tf))
        gate = jnp.dot(h_ref[...], wg_ref[:, cols], preferred_element_type=F32)
        up = jnp.dot(h_ref[...], wu_ref[:, cols], preferred_element_type=F32)
        a_ref[:, cols] = (gate * jax.nn.sigmoid(gate) * up).astype(BF16)
    y = jnp.dot(a_ref[...], wd_ref[...], preferred_element_type=F32)

    @pl.when(j == 0)
    def _():
        acc_ref[...] = y

    @pl.when(j > 0)
    def _():
        acc_ref[...] += y

    @pl.when(j == pl.num_programs(1) - 1)
    def _():
        o_ref[...] = x_ref[...] + acc_ref[...]


def _ffn(x, g, wg, wu, wd, tm=1024, tf=1408):
    t = x.shape[0]
    tm = min(tm, t)
    return pl.pallas_call(
        _ffn_kernel,
        grid=(t // tm, D_FF // tf),
        in_specs=[pl.BlockSpec((tm, D_MODEL), lambda i, j: (i, 0)), pl.BlockSpec((1, D_MODEL), lambda i, j: (0, 0)),
                  pl.BlockSpec((D_MODEL, tf), lambda i, j: (0, j)), pl.BlockSpec((D_MODEL, tf), lambda i, j: (0, j)),
                  pl.BlockSpec((tf, D_MODEL), lambda i, j: (j, 0))],
        out_specs=pl.BlockSpec((tm, D_MODEL), lambda i, j: (i, 0)),
        out_shape=jax.ShapeDtypeStruct((t, D_MODEL), F32),
        scratch_shapes=[pltpu.VMEM((tm, D_MODEL), BF16), pltpu.VMEM((tm, tf), BF16), pltpu.VMEM((tm, D_MODEL), F32)],
        compiler_params=_cparams(("parallel", "arbitrary")),
        name="ffn_dense",
    )(x, g.reshape(1, D_MODEL), wg, wu, wd)


_MOE_PREFIX = 256


def _moe_route(x_ref, g_ref, r_ref, o_ref, hb_ref, rank_ref, sel_ref, gate_ref, cnt_ref):
    n_blk = x_ref.shape[0] // _MOE_PREFIX
    router = r_ref[...]
    r_hi = router.astype(BF16)
    r_lo = (router - r_hi.astype(F32)).astype(BF16)

    for blk in range(n_blk):
        rows = slice(blk * _MOE_PREFIX, (blk + 1) * _MOE_PREFIX)
        x = x_ref[rows, :]
        h32 = _rms(x, g_ref[...])
        h_hi = h32.astype(BF16)
        h_lo = (h32 - h_hi.astype(F32)).astype(BF16)
        hb_ref[rows, :] = h_hi
        o_ref[rows, :] = x
        logits = (jnp.dot(h_hi, r_hi, preferred_element_type=F32)
                  + (jnp.dot(h_hi, r_lo, preferred_element_type=F32) + jnp.dot(h_lo, r_hi, preferred_element_type=F32)))
        rank_ref[:, rows] = logits.T[:N_EXPERTS, :]
    lt = rank_ref[...]
    row = lax.broadcasted_iota(jnp.int32, lt.shape, 0)
    m1 = jnp.max(lt, axis=0, keepdims=True)
    i1 = jnp.min(jnp.where(lt == m1, row, N_EXPERTS), axis=0, keepdims=True)
    lt2 = jnp.where(row == i1, -jnp.inf, lt)
    m2 = jnp.max(lt2, axis=0, keepdims=True)
    i2 = jnp.min(jnp.where(lt2 == m2, row, N_EXPERTS), axis=0, keepdims=True)
    e2 = jnp.exp(m2 - m1)
    den = 1.0 + e2
    gate_ref[...] = jnp.where(row == i1, 1.0 / den, 0.0) + jnp.where(row == i2, e2 / den, 0.0)
    sel = jnp.where(row == i1, 1.0, 0.0) + jnp.where(row == i2, 1.0, 0.0)
    sel_ref[...] = sel
    before = (lax.broadcasted_iota(jnp.int32, (_MOE_PREFIX, _MOE_PREFIX), 0)
              < lax.broadcasted_iota(jnp.int32, (_MOE_PREFIX, _MOE_PREFIX), 1))
    tri = jnp.where(before, 1.0, 0.0).astype(BF16)
    carry = jnp.zeros((N_EXPERTS, 1), F32)
    for blk in range(lt.shape[1] // _MOE_PREFIX):
        cols = slice(blk * _MOE_PREFIX, (blk + 1) * _MOE_PREFIX)
        rank_ref[:, cols] = jnp.dot(sel[:, cols].astype(BF16), tri, preferred_element_type=F32) + carry
        carry = carry + jnp.sum(sel[:, cols], axis=1, keepdims=True)
    for k in range(N_EXPERTS):
        cnt_ref[k] = carry[k, 0].astype(jnp.int32)


def _moe_kernel(x_ref, g_ref, r_ref, gf_ref, wg_ref, wu_ref, wd_ref, o_ref, hb_ref, rank_ref, sel_ref, gate_ref,
                xe_ref, ye_ref, cnt_ref, *, tb, g_rows, f_rows, s_rows, final_norm):
    e = pl.program_id(1)
    f = pl.program_id(2)
    last_f = pl.num_programs(2) - 1

    @pl.when((e == 0) & (f == 0))
    def _():
        _moe_route(x_ref, g_ref, r_ref, o_ref, hb_ref, rank_ref, sel_ref, gate_ref, cnt_ref)

    cnt = cnt_ref[e]
    rank_e = rank_ref[pl.ds(e, 1), :]
    sel_e = sel_ref[pl.ds(e, 1), :]

    def groups(rows):
        return (cnt + rows - 1) // rows

    def onehot(first_row, rows):
        tgt = (lax.broadcasted_iota(jnp.int32, (rows, tb), 0) + first_row).astype(F32)
        return jnp.where((rank_e == tgt) & (sel_e > 0.0), 1.0, 0.0)

    @pl.when(f == 0)
    def _():
        def gather(s, c):
            off = pl.multiple_of(s * g_rows, g_rows)
            xe_ref[pl.ds(off, g_rows), :] = jnp.dot(onehot(off, g_rows).astype(BF16), hb_ref[...],
                                                    preferred_element_type=F32).astype(BF16)
            return c
        lax.fori_loop(0, groups(g_rows), gather, 0)

    full = cnt // s_rows
    rem = cnt - full * s_rows
    tail_off = pl.multiple_of(full * s_rows, s_rows)

    def ffn(first):
        def rows_at(off, rows):
            y = _swiglu_step(xe_ref[pl.ds(off, rows), :], wg_ref[...], wu_ref[...], wd_ref[...])
            if first:
                ye_ref[pl.ds(off, rows), :] = y
            else:
                ye_ref[pl.ds(off, rows), :] += y

        def body(s, c):
            rows_at(pl.multiple_of(s * s_rows, s_rows), s_rows)
            return c
        lax.fori_loop(0, full + (rem > f_rows).astype(jnp.int32), body, 0)

        @pl.when((rem > 0) & (rem <= f_rows))
        def _():
            rows_at(tail_off, f_rows)

    @pl.when(f == 0)
    def _():
        ffn(True)
        if s_rows > f_rows:
            @pl.when((rem > 0) & (rem <= f_rows))
            def _():
                rest = pl.ds(pl.multiple_of(tail_off + f_rows, f_rows), s_rows - f_rows)
                ye_ref[rest, :] = jnp.zeros((s_rows - f_rows, D_MODEL), F32)

    @pl.when(f > 0)
    def _():
        ffn(False)

    @pl.when(f == last_f)
    def _():
        gate_e = gate_ref[pl.ds(e, 1), :]

        def scatter(s, c):
            off = pl.multiple_of(s * s_rows, s_rows)
            p = onehot(off, s_rows)
            gate_rows = jnp.sum(p * gate_e, axis=1, keepdims=True)
            valid = (lax.broadcasted_iota(jnp.int32, (s_rows, 1), 0) + off) < cnt
            yg = jnp.where(valid, gate_rows * ye_ref[pl.ds(off, s_rows), :], 0.0).astype(BF16)
            o_ref[...] += jnp.dot(p.T.astype(BF16), yg, preferred_element_type=F32)
            return c
        lax.fori_loop(0, groups(s_rows), scatter, 0)

    if final_norm:
        @pl.when((e == pl.num_programs(1) - 1) & (f == last_f))
        def _():
            for lo in range(0, tb, _MOE_PREFIX):
                rows = slice(lo, lo + _MOE_PREFIX)
                o_ref[rows, :] = _rms(o_ref[rows, :], gf_ref[...])


def _moe(x, g, router_pad, wg, wu, wd, final_g=None, tb=2048, fc=1408, g_rows=256, f_rows=128, s_rows=256):
    t = x.shape[0]
    tb = min(tb, t)
    gf = jnp.ones((D_MODEL,), F32) if final_g is None else final_g
    assert s_rows in (f_rows, 2 * f_rows) and g_rows % f_rows == 0 and tb % g_rows == 0 and tb % s_rows == 0
    once = pl.Buffered(1)
    return pl.pallas_call(
        functools.partial(_moe_kernel, tb=tb, g_rows=g_rows, f_rows=f_rows, s_rows=s_rows,
                          final_norm=final_g is not None),
        grid=(t // tb, N_EXPERTS, D_FF // fc),
        in_specs=[pl.BlockSpec((tb, D_MODEL), lambda i, e, j: (i, 0), pipeline_mode=once),
                  pl.BlockSpec((1, D_MODEL), lambda i, e, j: (0, 0)),
                  pl.BlockSpec((D_MODEL, LANES), lambda i, e, j: (0, 0)),
                  pl.BlockSpec((1, D_MODEL), lambda i, e, j: (0, 0)),
                  pl.BlockSpec((None, D_MODEL, fc), lambda i, e, j: (e, 0, j)),
                  pl.BlockSpec((None, D_MODEL, fc), lambda i, e, j: (e, 0, j)),
                  pl.BlockSpec((None, fc, D_MODEL), lambda i, e, j: (e, j, 0))],
        out_specs=pl.BlockSpec((tb, D_MODEL), lambda i, e, j: (i, 0), pipeline_mode=once),
        out_shape=jax.ShapeDtypeStruct((t, D_MODEL), F32),
        scratch_shapes=[pltpu.VMEM((tb, D_MODEL), BF16), pltpu.VMEM((N_EXPERTS, tb), F32),
                        pltpu.VMEM((N_EXPERTS, tb), F32), pltpu.VMEM((N_EXPERTS, tb), F32),
                        pltpu.VMEM((tb, D_MODEL), BF16), pltpu.VMEM((tb, D_MODEL), F32),
                        pltpu.SMEM((N_EXPERTS,), jnp.int32)],
        compiler_params=pltpu.CompilerParams(dimension_semantics=("parallel", "arbitrary", "arbitrary"),
                                             vmem_limit_bytes=MOE_VMEM_LIMIT),
        name="ffn_moe",
    )(x, g.reshape(1, D_MODEL), router_pad, gf.reshape(1, D_MODEL), wg, wu, wd)


def _final_norm_kernel(x_ref, g_ref, o_ref):
    o_ref[...] = _rms(x_ref[...], g_ref[...])


def _final_norm(x, g, tm=1024):
    t = x.shape[0]
    tm = min(tm, t)
    return pl.pallas_call(
        _final_norm_kernel,
        grid=(t // tm,),
        in_specs=[pl.BlockSpec((tm, D_MODEL), lambda i: (i, 0)), pl.BlockSpec((1, D_MODEL), lambda i: (0, 0))],
        out_specs=pl.BlockSpec((tm, D_MODEL), lambda i: (i, 0)),
        out_shape=jax.ShapeDtypeStruct((t, D_MODEL), F32),
        compiler_params=_cparams(("parallel",)),
        name="final_norm",
    )(x, g.reshape(1, D_MODEL))


def kernel(x_prompt, x_sample, cache_k, cache_v, state_ssm_re, state_ssm_im, state_conv, norm_mix_g, w_in, ssm_a_re, ssm_a_im, ssm_log_dt, ssm_b_re, ssm_b_im, ssm_c_re, ssm_c_im, ssm_d, ssm_w_glu, conv_w, conv_ln_g, conv_ln_b, attn_lq1, attn_lk1, attn_lq2, attn_lk2, attn_sub_g, w_out, norm_ffn_g, ffn_w_gate, ffn_w_up, ffn_w_down, moe_router, moe_w_gate, moe_w_up, moe_w_down, final_norm_g):
    depth = w_in.shape[0]
    n_p, l_p, _ = x_prompt.shape
    n_s, l_s, _ = x_sample.shape
    past = cache_k.shape[2]
    keep = CONV_KERNEL - 1

    xp = x_prompt.reshape(n_p * l_p, D_MODEL)
    xs = x_sample.reshape(n_s * l_s, D_MODEL)
    cache_kt = jnp.transpose(cache_k, (0, 1, 3, 4, 5, 2)).reshape(depth, n_s, 512, past)
    cache_vr = cache_v.reshape(depth, n_s, past * N_HEADS, V_DIM)
    s0_p = jnp.zeros((n_p, 1, 2 * STATE_W), F32)
    buf_p = jnp.zeros((n_p, keep, CONV_WIDTH), F32)

    ops = _s5_operators(ssm_a_re, ssm_a_im, ssm_log_dt, ssm_b_re, ssm_b_im, ssm_c_re, ssm_c_im, ssm_d, S5_CHUNK)
    kv_bufs = None
    outs = {name: [] for name in ("srp", "sip", "cp", "ks", "vs", "srs", "sis", "cs")}
    for l in range(depth):
        lam_init = 0.8 - 0.6 * math.exp(-0.3 * l)
        w_in_b = w_in[l].astype(BF16)
        lam_params = jnp.stack([attn_lq1[l], attn_lk1[l], attn_lq2[l], attn_lk2[l]]).astype(F32)
        w_glu_b = ssm_w_glu[l].astype(BF16)
        w_out_b = w_out[l].astype(BF16)
        conv_p = (conv_w[l], conv_ln_g[l], conv_ln_b[l])

        u2, hc, q, kb, vt, *kv_bufs = _in_proj_prompt(xp, norm_mix_g[l], w_in_b, kv_bufs, l, depth, n_p)
        y_ssm, sf_p = _s5_group(u2, n_p, l_p, s0_p, ops, l)
        cv, cb_p = _conv(hc.reshape(n_p, l_p, CONV_WIDTH), buf_p, *conv_p, tl=min(512, l_p))
        at = _attn_prompt(q, kb, vt, lam_params, attn_sub_g[l], n_p, l_p, lam_init)
        xp = _out_proj(xp, y_ssm, cv.reshape(n_p * l_p, CONV_WIDTH), at, w_glu_b, w_out_b)

        s0_s = jnp.concatenate([state_ssm_re[l].reshape(n_s, 1, STATE_W), state_ssm_im[l].reshape(n_s, 1, STATE_W)],
                               axis=-1).astype(F32)
        u2, hc, q, kb, vb, k_s, v_s = _in_proj_sample(xs, norm_mix_g[l], w_in_b)
        y_ssm, sf_s = _s5_group(u2, n_s, l_s, s0_s, ops, l)
        cv, cb_s = _conv(hc.reshape(n_s, l_s, CONV_WIDTH), state_conv[l].astype(F32), *conv_p, tl=l_s)
        at = _attn_sample(q, kb, vb, cache_kt, cache_vr, l, lam_params, attn_sub_g[l], lam_init)
        xs = _out_proj(xs, y_ssm, cv.reshape(n_s * l_s, CONV_WIDTH), at, w_glu_b, w_out_b)

        j = l // 2
        if l % 2 == 0:
            wg, wu, wd = ffn_w_gate[j].astype(BF16), ffn_w_up[j].astype(BF16), ffn_w_down[j].astype(BF16)
            xp = _ffn(xp, norm_ffn_g[l], wg, wu, wd)
            xs = _ffn(xs, norm_ffn_g[l], wg, wu, wd)
        else:
            router_pad = jnp.pad(moe_router[j].astype(F32), ((0, 0), (0, LANES - N_EXPERTS)))
            wg, wu, wd = moe_w_gate[j].astype(BF16), moe_w_up[j].astype(BF16), moe_w_down[j].astype(BF16)
            final_g = final_norm_g if l == depth - 1 else None
            xp = _moe(xp, norm_ffn_g[l], router_pad, wg, wu, wd, final_g)
            xs = _moe(xs, norm_ffn_g[l], router_pad, wg, wu, wd, final_g)

        outs["srp"].append(sf_p[:, 0, :STATE_W].reshape(n_p, N_SSM_GROUPS, SSM_STATE))
        outs["sip"].append(sf_p[:, 0, STATE_W:].reshape(n_p, N_SSM_GROUPS, SSM_STATE))
        outs["cp"].append(cb_p)
        outs["ks"].append(k_s.reshape(n_s, l_s, 2, N_HEADS, HEAD_DIM))
        outs["vs"].append(v_s.reshape(n_s, l_s, N_HEADS, V_DIM))
        outs["srs"].append(sf_s[:, 0, :STATE_W].reshape(n_s, N_SSM_GROUPS, SSM_STATE))
        outs["sis"].append(sf_s[:, 0, STATE_W:].reshape(n_s, N_SSM_GROUPS, SSM_STATE))
        outs["cs"].append(cb_s)

    if depth % 2 == 0:
        yp, ys = xp, xs
    else:
        yp = _final_norm(xp, final_norm_g)
        ys = _final_norm(xs, final_norm_g)
    st = {name: jnp.stack(vals) for name, vals in outs.items()}
    kt_buf, v_buf = kv_bufs
    k_prompt = jnp.transpose(kt_buf.reshape(depth, n_p, 2, N_HEADS, HEAD_DIM, l_p), (0, 1, 5, 2, 3, 4))
    v_prompt = v_buf.reshape(depth, n_p, l_p, N_HEADS, V_DIM)
    return (yp.reshape(n_p, l_p, D_MODEL), ys.reshape(n_s, l_s, D_MODEL),
            k_prompt, v_prompt, st["srp"], st["sip"], st["cp"],
            st["ks"], st["vs"], st["srs"], st["sis"], st["cs"])
```

```python
import functools
import math

import jax
import jax.numpy as jnp
from jax import lax
from jax.experimental import pallas as pl
from jax.experimental.pallas import tpu as pltpu

F32 = jnp.float32
BF16 = jnp.bfloat16

D_MODEL = 1024
CHUNK = 64
N_SSM_GROUPS = 16
SSM_WIDTH = 256
SSM_STATE = 64
STATE_W = N_SSM_GROUPS * SSM_STATE
CONV_WIDTH = 256
CONV_KERNEL = 31
N_HEADS = 4
HEAD_DIM = 64
V_DIM = 128
VT_ROWS = V_DIM + 16
ATTN_WIDTH = 512
IN_WIDTH = 2304
ATTN_SCALE = 1.0 / math.sqrt(HEAD_DIM)
LOG2E = math.log2(math.e)
NEG_INF = -1e30
D_FF = 2816
N_EXPERTS = 8
EPS = 1e-6

S5_CHUNK = 8
LANES = 128
SUBLANES = 8
VMEM_LIMIT = 48 * 1024 * 1024
MOE_VMEM_LIMIT = 58 * 1024 * 1024


def _cparams(sem):
    return pltpu.CompilerParams(dimension_semantics=sem, vmem_limit_bytes=VMEM_LIMIT)


def _rms(x, g):
    return x * lax.rsqrt(jnp.mean(x * x, axis=-1, keepdims=True) + EPS) * g


def _mixer_inputs(x_ref, g_ref, w_ref, u2_ref, hc_ref, q_ref, us_ref):
    h = _rms(x_ref[...], g_ref[...]).astype(BF16)

    def proj(lo, hi):
        return jnp.dot(h, w_ref[:, lo:hi], preferred_element_type=F32)

    u = proj(0, 256)
    n_chunks = us_ref.shape[1] // S5_CHUNK
    for c in range(SSM_WIDTH // LANES):
        us_ref[c] = u[:, c * LANES:(c + 1) * LANES]
        for j in range(S5_CHUNK):
            lo = j * SSM_WIDTH + c * LANES
            u2_ref[:, lo:lo + LANES] = us_ref[c, pl.ds(j, n_chunks, stride=S5_CHUNK), :]
    c_val = proj(256, 512)
    c_gate = proj(512, 768)
    hc_ref[...] = c_val * jax.nn.sigmoid(c_gate)
    q_ref[...] = (proj(768, 1280) * (ATTN_SCALE * LOG2E)).astype(BF16)
    return h, proj(1280, 1792), proj(1792, 2304)


def _in_proj_prompt_kernel(*refs, aliased):
    x_ref, g_ref, w_ref = refs[:3]
    u2_ref, hc_ref, q_ref, kb_ref, vt_ref, kt_ref, v_ref, us_ref = refs[5 if aliased else 3:]
    h, k, v = _mixer_inputs(x_ref, g_ref, w_ref, u2_ref, hc_ref, q_ref, us_ref)
    kb_ref[...] = k.astype(BF16)
    for head in range(N_HEADS):
        v_ref[pl.ds(head, v.shape[0], stride=N_HEADS), :] = v[:, head * V_DIM:(head + 1) * V_DIM]
    kt_ref[...] = k.T
    vt = v.T.astype(BF16)
    for head in range(N_HEADS):
        vt_ref[head * VT_ROWS:head * VT_ROWS + V_DIM, :] = vt[head * V_DIM:(head + 1) * V_DIM, :]
        vt_ref[head * VT_ROWS + V_DIM:(head + 1) * VT_ROWS, :] = jnp.ones((VT_ROWS - V_DIM, vt.shape[1]), BF16)


def _in_proj_prompt(x, g, w_bf16, kv_bufs, layer, depth, n_seq, tm=512):
    t = x.shape[0]
    seq_len = t // n_seq
    per_seq = seq_len // tm
    row = lambda width: pl.BlockSpec((tm, width), lambda i: (i, 0))
    in_specs = [row(D_MODEL), pl.BlockSpec((1, D_MODEL), lambda i: (0, 0)),
                pl.BlockSpec((D_MODEL, IN_WIDTH), lambda i: (0, 0))]
    args = [x, g.reshape(1, D_MODEL), w_bf16]
    aliases = {}
    if kv_bufs is not None:
        in_specs += [pl.BlockSpec(memory_space=pl.ANY), pl.BlockSpec(memory_space=pl.ANY)]
        args += list(kv_bufs)
        aliases = {3: 5, 4: 6}
    return pl.pallas_call(
        functools.partial(_in_proj_prompt_kernel, aliased=kv_bufs is not None),
        grid=(t // tm,),
        in_specs=in_specs,
        out_specs=[pl.BlockSpec((tm // S5_CHUNK, S5_CHUNK * SSM_WIDTH), lambda i: (i, 0)),
                   row(256), row(512), row(512),
                   pl.BlockSpec((N_HEADS * VT_ROWS, tm), lambda i: (0, i)),
                   pl.BlockSpec((None, None, ATTN_WIDTH, tm), lambda i: (layer, i // per_seq, 0, i % per_seq)),
                   pl.BlockSpec((None, None, tm * N_HEADS, V_DIM), lambda i: (layer, i // per_seq, i % per_seq, 0))],
        out_shape=[jax.ShapeDtypeStruct((t // S5_CHUNK, S5_CHUNK * SSM_WIDTH), F32),
                   jax.ShapeDtypeStruct((t, 256), F32),
                   jax.ShapeDtypeStruct((t, 512), BF16), jax.ShapeDtypeStruct((t, 512), BF16),
                   jax.ShapeDtypeStruct((N_HEADS * VT_ROWS, t), BF16),
                   jax.ShapeDtypeStruct((depth, n_seq, ATTN_WIDTH, seq_len), F32),
                   jax.ShapeDtypeStruct((depth, n_seq, seq_len * N_HEADS, V_DIM), F32)],
        scratch_shapes=[pltpu.VMEM((SSM_WIDTH // LANES, tm, LANES), F32)],
        input_output_aliases=aliases,
        compiler_params=_cparams(("parallel",)),
        name="in_proj_prompt",
    )(*args)


def _in_proj_sample_kernel(x_ref, g_ref, w_ref, u2_ref, hc_ref, q_ref, kb_ref, vb_ref, k_ref, v_ref, us_ref):
    _, k, v = _mixer_inputs(x_ref, g_ref, w_ref, u2_ref, hc_ref, q_ref, us_ref)
    k_ref[...] = k
    kb_ref[...] = k.astype(BF16)
    v_ref[...] = v
    vb_ref[...] = v.astype(BF16)


def _in_proj_sample(x, g, w_bf16, tm=512):
    t = x.shape[0]
    tm = min(tm, t)
    row = lambda width: pl.BlockSpec((tm, width), lambda i: (i, 0))
    return pl.pallas_call(
        _in_proj_sample_kernel,
        grid=(t // tm,),
        in_specs=[row(D_MODEL), pl.BlockSpec((1, D_MODEL), lambda i: (0, 0)),
                  pl.BlockSpec((D_MODEL, IN_WIDTH), lambda i: (0, 0))],
        out_specs=[pl.BlockSpec((tm // S5_CHUNK, S5_CHUNK * SSM_WIDTH), lambda i: (i, 0)),
                   row(256), row(512), row(512), row(512), row(512), row(512)],
        out_shape=[jax.ShapeDtypeStruct((t // S5_CHUNK, S5_CHUNK * SSM_WIDTH), F32),
                   jax.ShapeDtypeStruct((t, 256), F32),
                   jax.ShapeDtypeStruct((t, 512), BF16), jax.ShapeDtypeStruct((t, 512), BF16),
                   jax.ShapeDtypeStruct((t, 512), BF16), jax.ShapeDtypeStruct((t, 512), F32),
                   jax.ShapeDtypeStruct((t, 512), F32)],
        scratch_shapes=[pltpu.VMEM((SSM_WIDTH // LANES, tm, LANES), F32)],
        compiler_params=_cparams(("parallel",)),
        name="in_proj_sample",
    )(x, g.reshape(1, D_MODEL), w_bf16)


def _mm_kernel(x_ref, w_ref, o_ref):
    o_ref[...] = jnp.dot(x_ref[...].astype(BF16), w_ref[...], preferred_element_type=F32)


def _mm(x, w_bf16, layer, tm, tn, name):
    m, k = x.shape
    n = w_bf16.shape[2]
    return pl.pallas_call(
        _mm_kernel,
        grid=(n // tn, m // tm),
        in_specs=[pl.BlockSpec((tm, k), lambda j, i: (i, 0)),
                  pl.BlockSpec((None, k, tn), lambda j, i: (layer, 0, j))],
        out_specs=pl.BlockSpec((tm, tn), lambda j, i: (i, j)),
        out_shape=jax.ShapeDtypeStruct((m, n), F32),
        compiler_params=_cparams(("parallel", "parallel")),
        name=name,
    )(x, w_bf16)


def _s5_scan_kernel(d_ref, s0_ref, a_ref, ss_ref, sf_ref, st_ref, *, rows):
    t = pl.program_id(1)

    @pl.when(t == 0)
    def _():
        st_ref[...] = s0_ref[...]

    a_re = a_ref[:, :STATE_W]
    a_im = a_ref[:, STATE_W:]

    def step(c, carry):
        s_re, s_im = carry
        ss_ref[pl.ds(c, 1), :STATE_W] = s_re
        ss_ref[pl.ds(c, 1), STATE_W:] = s_im
        d = d_ref[pl.ds(c, 1), :]
        n_re = a_re * s_re - a_im * s_im + d[:, :STATE_W]
        n_im = a_re * s_im + a_im * s_re + d[:, STATE_W:]
        return n_re, n_im

    s_re, s_im = lax.fori_loop(0, rows, step, (st_ref[:, :STATE_W], st_ref[:, STATE_W:]))
    st_ref[:, :STATE_W] = s_re
    st_ref[:, STATE_W:] = s_im
    sf_ref[...] = st_ref[...]


def _s5_scan(d, n_seq, n_chunks, s0, a_c, layer):
    rows = min(n_chunks, 256)
    nt = n_chunks // rows
    w = 2 * STATE_W
    return pl.pallas_call(
        functools.partial(_s5_scan_kernel, rows=rows),
        grid=(n_seq, nt),
        in_specs=[pl.BlockSpec((rows, w), lambda s, t: (s * nt + t, 0)),
                  pl.BlockSpec((None, 1, w), lambda s, t: (s, 0, 0)),
                  pl.BlockSpec((None, 1, w), lambda s, t: (layer, 0, 0))],
        out_specs=[pl.BlockSpec((rows, w), lambda s, t: (s * nt + t, 0)),
                   pl.BlockSpec((None, 1, w), lambda s, t: (s, 0, 0))],
        out_shape=[jax.ShapeDtypeStruct((n_seq * n_chunks, w), F32), jax.ShapeDtypeStruct((n_seq, 1, w), F32)],
        scratch_shapes=[pltpu.VMEM((1, w), F32)],
        compiler_params=_cparams(("parallel", "arbitrary")),
        name="s5_scan",
    )(d, s0, a_c)


def _s5_out_kernel(u_ref, ut_ref, ss_ref, wi_ref, ws_ref, dsk_ref, y_ref):
    y_ref[...] = (jnp.dot(u_ref[...].astype(BF16), wi_ref[...], preferred_element_type=F32)
                  + jnp.dot(ss_ref[...].astype(BF16), ws_ref[...], preferred_element_type=F32)
                  + dsk_ref[...] * ut_ref[...])


def _s5_out(u2, s_start, w_intra, w_inter, dsk, layer, tm=256, tn=1024):
    m, n = u2.shape
    k = s_start.shape[1]
    return pl.pallas_call(
        _s5_out_kernel,
        grid=(n // tn, m // tm),
        in_specs=[pl.BlockSpec((tm, n), lambda j, i: (i, 0)), pl.BlockSpec((tm, tn), lambda j, i: (i, j)),
                  pl.BlockSpec((tm, k), lambda j, i: (i, 0)),
                  pl.BlockSpec((None, n, tn), lambda j, i: (layer, 0, j)),
                  pl.BlockSpec((None, k, tn), lambda j, i: (layer, 0, j)),
                  pl.BlockSpec((None, 1, tn), lambda j, i: (layer, 0, j))],
        out_specs=pl.BlockSpec((tm, tn), lambda j, i: (i, j)),
        out_shape=jax.ShapeDtypeStruct((m, n), F32),
        compiler_params=_cparams(("parallel", "parallel")),
        name="s5_out",
    )(u2, u2, s_start, w_intra, w_inter, dsk)


def _s5_operators(a_re, a_im, log_dt, b_re, b_im, c_re, c_im, d_skip, n_c):
    f32 = lambda x: x.astype(F32)
    a_re, a_im, b_re, b_im, c_re, c_im = map(f32, (a_re, a_im, b_re, b_im, c_re, c_im))
    n_l = a_re.shape[0]
    g_n = N_SSM_GROUPS
    dt = jnp.exp(f32(log_dt))[:, None, :, None]
    ks = jnp.arange(n_c + 1, dtype=F32)[None, :, None, None]
    mag = jnp.exp(a_re[:, None] * dt * ks)
    ang = a_im[:, None] * dt * ks
    pw_re, pw_im = mag * jnp.cos(ang), mag * jnp.sin(ang)
    den = a_re * a_re + a_im * a_im
    q_re = ((pw_re[:, 1] - 1.0) * a_re + pw_im[:, 1] * a_im) / den
    q_im = (pw_im[:, 1] * a_re - (pw_re[:, 1] - 1.0) * a_im) / den
    bb_re = q_re[..., None] * b_re - q_im[..., None] * b_im
    bb_im = q_re[..., None] * b_im + q_im[..., None] * b_re
    eye_g = jnp.eye(g_n, dtype=F32)

    def block_diag(x, rows, cols):
        return (x[:, :, :, None, :] * eye_g[None, :, None, :, None]).reshape(n_l, rows, cols)

    bt_re = block_diag(jnp.transpose(bb_re, (0, 1, 3, 2)), SSM_WIDTH, STATE_W)[:, None]
    bt_im = block_diag(jnp.transpose(bb_im, (0, 1, 3, 2)), SSM_WIDTH, STATE_W)[:, None]
    pj_re = pw_re[:, :n_c][:, ::-1].reshape(n_l, n_c, 1, STATE_W)
    pj_im = pw_im[:, :n_c][:, ::-1].reshape(n_l, n_c, 1, STATE_W)
    w_state = jnp.concatenate([(pj_re * bt_re - pj_im * bt_im).reshape(n_l, n_c * SSM_WIDTH, STATE_W),
                               (pj_re * bt_im + pj_im * bt_re).reshape(n_l, n_c * SSM_WIDTH, STATE_W)], axis=2)

    m_re = c_re[:, None] * pw_re[:, :n_c, :, None, :] - c_im[:, None] * pw_im[:, :n_c, :, None, :]
    m_im = c_re[:, None] * pw_im[:, :n_c, :, None, :] + c_im[:, None] * pw_re[:, :n_c, :, None, :]
    hi = lax.Precision.HIGHEST
    ker = (jnp.einsum('lkghp,lgpx->lkghx', m_re, bb_re, precision=hi)
           - jnp.einsum('lkghp,lgpx->lkghx', m_im, bb_im, precision=hi))
    blk = jnp.transpose(ker, (0, 1, 2, 4, 3))
    blk = (blk[:, :, :, :, None, :] * eye_g[None, None, :, None, :, None]).reshape(n_l, n_c, 256, 256).astype(BF16)
    zero = jnp.zeros((n_l, 256, 256), BF16)
    w_intra = jnp.concatenate(
        [jnp.concatenate([blk[:, t - j] if t >= j else zero for t in range(n_c)], axis=2) for j in range(n_c)],
        axis=1)

    ct_re = block_diag(jnp.transpose(c_re, (0, 1, 3, 2)), STATE_W, SSM_WIDTH)
    ct_im = block_diag(jnp.transpose(c_im, (0, 1, 3, 2)), STATE_W, SSM_WIDTH)
    pt_re = pw_re[:, 1:].reshape(n_l, n_c, STATE_W, 1)
    pt_im = pw_im[:, 1:].reshape(n_l, n_c, STATE_W, 1)
    top = jnp.concatenate([pt_re[:, t] * ct_re - pt_im[:, t] * ct_im for t in range(n_c)], axis=2)
    bot = jnp.concatenate([-(pt_re[:, t] * ct_im + pt_im[:, t] * ct_re) for t in range(n_c)], axis=2)
    w_inter = jnp.concatenate([top, bot], axis=1)

    a_c = jnp.concatenate([pw_re[:, n_c].reshape(n_l, 1, STATE_W), pw_im[:, n_c].reshape(n_l, 1, STATE_W)], axis=2)
    dsk = jnp.tile(f32(d_skip).reshape(n_l, 1, SSM_WIDTH), (1, 1, n_c))
    return w_state.astype(BF16), w_intra, w_inter.astype(BF16), a_c, dsk


def _s5_group(u2, n_seq, seq_len, s0, ops, layer):
    w_state, w_intra, w_inter, a_c, dsk = ops
    n_chunks = seq_len // S5_CHUNK
    d_state = _mm(u2, w_state, layer, 256, 2 * STATE_W, "s5_state")
    s_start, s_final = _s5_scan(d_state, n_seq, n_chunks, s0, a_c, layer)
    return _s5_out(u2, s_start, w_intra, w_inter, dsk, layer), s_final


_CONV_PAD = 32
_CONV_RB = 64


def _conv_kernel(h_ref, buf_ref, w_ref, g_ref, b_ref, y_ref, nb_ref, xp_ref, xs_ref, *, tl):
    t = pl.program_id(1)
    keep = CONV_KERNEL - 1
    lo = _CONV_PAD - keep

    @pl.when(t == 0)
    def _():
        xp_ref[lo:_CONV_PAD, :] = buf_ref[...]

    @pl.when(t > 0)
    def _():
        xp_ref[lo:_CONV_PAD, :] = xp_ref[tl + lo:tl + _CONV_PAD, :]

    xp_ref[_CONV_PAD:_CONV_PAD + tl, :] = h_ref[...]
    for phase in range(SUBLANES):
        span = tl + (keep - phase) // SUBLANES * SUBLANES
        xs_ref[phase, :span, :] = xp_ref[lo + phase:lo + phase + span, :]
    for r in range(tl // _CONV_RB):
        base = r * _CONV_RB
        acc = jnp.zeros((_CONV_RB, CONV_WIDTH), F32)
        for k in range(CONV_KERNEL):
            start = base + k - k % SUBLANES
            acc = acc + w_ref[k:k + 1, :] * xs_ref[k % SUBLANES, start:start + _CONV_RB, :]
        xc = acc - jnp.mean(acc, axis=-1, keepdims=True)
        var = jnp.mean(xc * xc, axis=-1, keepdims=True)
        y = xc * lax.rsqrt(var + EPS) * g_ref[...] + b_ref[...]
        y_ref[base:base + _CONV_RB, :] = y * jax.nn.sigmoid(y)
    nb_ref[...] = xp_ref[tl + lo:tl + _CONV_PAD, :]


def _conv(h, buf, w, ln_g, ln_b, tl):
    s, l, c = h.shape
    keep = CONV_KERNEL - 1
    return pl.pallas_call(
        functools.partial(_conv_kernel, tl=tl),
        grid=(s, l // tl),
        in_specs=[pl.BlockSpec((None, tl, c), lambda i, t: (i, t, 0)),
                  pl.BlockSpec((None, keep, c), lambda i, t: (i, 0, 0)),
                  pl.BlockSpec((CONV_KERNEL, c), lambda i, t: (0, 0)),
                  pl.BlockSpec((1, c), lambda i, t: (0, 0)), pl.BlockSpec((1, c), lambda i, t: (0, 0))],
        out_specs=[pl.BlockSpec((None, tl, c), lambda i, t: (i, t, 0)),
                   pl.BlockSpec((None, keep, c), lambda i, t: (i, 0, 0))],
        out_shape=[jax.ShapeDtypeStruct((s, l, c), F32), jax.ShapeDtypeStruct((s, keep, c), F32)],
        scratch_shapes=[pltpu.VMEM((tl + _CONV_PAD, c), F32),
                        pltpu.VMEM((SUBLANES, tl + (keep // SUBLANES) * SUBLANES, c), F32)],
        compiler_params=_cparams(("parallel", "arbitrary")),
        name="conv_module",
    )(h, buf, w, ln_g.reshape(1, c), ln_b.reshape(1, c))


def _stack_q(q_ref, qs_ref):
    tq = q_ref.shape[0]
    lane = lax.broadcasted_iota(jnp.int32, (tq, LANES), 1)
    for b in range(4):
        qb = q_ref[:, b * LANES:(b + 1) * LANES]
        qs_ref[b, :tq, :] = jnp.where(lane < HEAD_DIM, qb, jnp.zeros_like(qb))
        qs_ref[b, tq:, :] = jnp.where(lane >= HEAD_DIM, qb, jnp.zeros_like(qb))


def _attn_update(score, value, m_ref, l_ref, acc_ref, tq):
    def softmax(b, s):
        m_old = m_ref[b]
        m_new = jnp.maximum(m_old, jnp.max(s, axis=-1, keepdims=True))
        alpha = jnp.exp2(m_old - m_new)
        p = jnp.exp2(s - m_new)
        l_ref[b] = alpha * l_ref[b] + jnp.sum(p, axis=-1, keepdims=True)
        m_ref[b] = m_new
        return alpha, p.astype(BF16)

    def values(b, alpha, pb):
        h0 = 2 * (b % 2)
        pv0 = jnp.dot(pb[:tq], value(h0), preferred_element_type=F32)
        pv1 = jnp.dot(pb[tq:], value(h0 + 1), preferred_element_type=F32)
        acc_ref[b, :tq, :] = alpha[:tq] * acc_ref[b, :tq, :] + pv0
        acc_ref[b, tq:, :] = alpha[tq:] * acc_ref[b, tq:, :] + pv1

    s0 = score(0)
    s1 = score(1)
    s2 = score(2)
    a0, p0 = softmax(0, s0)
    s3 = score(3)
    a1, p1 = softmax(1, s1)
    values(0, a0, p0)
    a2, p2 = softmax(2, s2)
    values(1, a1, p1)
    a3, p3 = softmax(3, s3)
    values(2, a2, p2)
    values(3, a3, p3)


def _attn_finish(lp_ref, sg_ref, l_ref, acc_ref, o_ref, tq, lam_init):
    lam = _lambda(lp_ref, lam_init)
    for h in range(N_HEADS):
        b1, half = h // 2, h % 2
        rows = slice(half * tq, (half + 1) * tq)
        o1 = acc_ref[b1, rows, :] / l_ref[b1, rows, :]
        o2 = acc_ref[b1 + 2, rows, :] / l_ref[b1 + 2, rows, :]
        o = o1 - lam * o2
        o_ref[:, h * V_DIM:(h + 1) * V_DIM] = (_rms(o, sg_ref[...]) * (1.0 - lam_init)).astype(o_ref.dtype)


def _attn_init(m_ref, l_ref, acc_ref):
    m_ref[...] = jnp.full(m_ref.shape, NEG_INF, F32)
    l_ref[...] = jnp.zeros(l_ref.shape, F32)
    acc_ref[...] = jnp.zeros(acc_ref.shape, F32)


def _attn_prompt_update(qs_ref, k_ref, vt_ref, m_ref, acc_ref, tq, mask):
    def scores(c):
        b, r = divmod(c, 2)
        s = lax.dot_general(k_ref[:, b * LANES:(b + 1) * LANES], qs_ref[b, r * tq:(r + 1) * tq, :],
                            (((1,), (1,)), ((), ())), preferred_element_type=F32)
        return s if mask is None else jnp.where(mask, s, NEG_INF)

    def softmax(c, s):
        b, r = divmod(c, 2)
        cols = slice(r * tq, (r + 1) * tq)
        sb = s.astype(BF16)
        m_old = m_ref[b, :, cols]
        m_new = jnp.maximum(m_old, jnp.max(sb, axis=0, keepdims=True).astype(F32))
        m_ref[b, :, cols] = m_new
        return jnp.exp2(m_old - m_new), jnp.exp2(sb - m_new.astype(BF16))

    def values(c, alpha, pb):
        h = 2 * ((c // 2) % 2) + c % 2
        pv = jnp.dot(vt_ref[h * VT_ROWS:(h + 1) * VT_ROWS, :], pb, preferred_element_type=F32)
        acc_ref[c] = alpha * acc_ref[c] + pv

    n_chain, ahead = 8, 3
    pending = [scores(c) for c in range(ahead)]
    for c in range(n_chain):
        if c + ahead < n_chain:
            pending.append(scores(c + ahead))
        alpha, pb = softmax(c, pending.pop(0))
        values(c, alpha, pb)


def _lambda(lp_ref, lam_init):
    lp = lp_ref[...]
    return (jnp.exp(jnp.sum(lp[0:1] * lp[1:2], axis=-1, keepdims=True))
            - jnp.exp(jnp.sum(lp[2:3] * lp[3:4], axis=-1, keepdims=True)) + lam_init)


def _attn_prompt_finish(lp_ref, sgt_ref, acc_ref, o_ref, tq, lam_init):
    lam = _lambda(lp_ref, lam_init)
    for h in range(N_HEADS):
        i1 = 2 * (h // 2) + h % 2
        o1 = acc_ref[i1, :V_DIM, :] / acc_ref[i1, V_DIM:V_DIM + 1, :]
        o2 = acc_ref[i1 + 4, :V_DIM, :] / acc_ref[i1 + 4, V_DIM:V_DIM + 1, :]
        o = o1 - lam * o2
        on = o * lax.rsqrt(jnp.mean(o * o, axis=0, keepdims=True) + EPS) * sgt_ref[...] * (1.0 - lam_init)
        o_ref[:, h * V_DIM:(h + 1) * V_DIM] = on.T.astype(o_ref.dtype)


def _attn_prompt_kernel(it_ref, jt_ref, lp_ref, sgt_ref, q_ref, k_ref, vt_ref, o_ref, qs_ref, m_ref, acc_ref,
                        *, tq, lam_init):
    step = pl.program_id(1)
    i = it_ref[step]
    j = jt_ref[step]

    @pl.when(j == 0)
    def _():
        m_ref[...] = jnp.full(m_ref.shape, NEG_INF, F32)
        acc_ref[...] = jnp.zeros(acc_ref.shape, F32)
        _stack_q(q_ref, qs_ref)

    @pl.when(j < i)
    def _():
        _attn_prompt_update(qs_ref, k_ref, vt_ref, m_ref, acc_ref, tq, None)

    @pl.when(j == i)
    def _():
        tk = k_ref.shape[0]
        key_chunk = lax.broadcasted_iota(jnp.int32, (tk, tq), 0) // CHUNK
        qry_chunk = lax.broadcasted_iota(jnp.int32, (tk, tq), 1) // CHUNK
        _attn_prompt_update(qs_ref, k_ref, vt_ref, m_ref, acc_ref, tq, qry_chunk >= key_chunk)
        _attn_prompt_finish(lp_ref, sgt_ref, acc_ref, o_ref, tq, lam_init)


def _attn_prompt(q, kb, vt, lam_params, sub_g, n_seq, seq_len, lam_init, tq=512):
    nq = seq_len // tq
    pairs = [(i, j) for i in range(nq) for j in range(i + 1)]
    i_tab = jnp.asarray([p[0] for p in pairs], jnp.int32)
    j_tab = jnp.asarray([p[1] for p in pairs], jnp.int32)
    grid_spec = pltpu.PrefetchScalarGridSpec(
        num_scalar_prefetch=2,
        grid=(n_seq, len(pairs)),
        in_specs=[pl.BlockSpec((4, HEAD_DIM), lambda b, s, it, jt: (0, 0)),
                  pl.BlockSpec((V_DIM, 1), lambda b, s, it, jt: (0, 0)),
                  pl.BlockSpec((tq, 512), lambda b, s, it, jt: (b * nq + it[s], 0)),
                  pl.BlockSpec((tq, 512), lambda b, s, it, jt: (b * nq + jt[s], 0)),
                  pl.BlockSpec((N_HEADS * VT_ROWS, tq), lambda b, s, it, jt: (0, b * nq + jt[s]))],
        out_specs=pl.BlockSpec((tq, 512), lambda b, s, it, jt: (b * nq + it[s], 0)),
        scratch_shapes=[pltpu.VMEM((4, 2 * tq, LANES), BF16), pltpu.VMEM((4, 1, 2 * tq), F32),
                        pltpu.VMEM((8, VT_ROWS, tq), F32)],
    )
    return pl.pallas_call(
        functools.partial(_attn_prompt_kernel, tq=tq, lam_init=lam_init),
        grid_spec=grid_spec,
        out_shape=jax.ShapeDtypeStruct((n_seq * seq_len, 512), BF16),
        compiler_params=_cparams(("parallel", "arbitrary")),
        name="attn_prompt",
    )(i_tab, j_tab, lam_params, sub_g.reshape(V_DIM, 1), q, kb, vt)


def _attn_sample_kernel(lp_ref, sg_ref, q_ref, ckt_ref, cv_ref, kn_ref, vn_ref, o_ref, qs_ref, m_ref, l_ref, acc_ref,
                        *, tq, tk, lam_init):
    j = pl.program_id(1)

    @pl.when(j == 0)
    def _():
        _attn_init(m_ref, l_ref, acc_ref)
        _stack_q(q_ref, qs_ref)

    def cache_score(b):
        return jnp.dot(qs_ref[b], ckt_ref[b * LANES:(b + 1) * LANES, :].astype(BF16), preferred_element_type=F32)

    def cache_value(h):
        return cv_ref[pl.ds(h, tk, stride=N_HEADS), :].astype(BF16)

    _attn_update(cache_score, cache_value, m_ref, l_ref, acc_ref, tq)

    @pl.when(j == pl.num_programs(1) - 1)
    def _():
        def new_score(b):
            return lax.dot_general(qs_ref[b], kn_ref[:, b * LANES:(b + 1) * LANES], (((1,), (1,)), ((), ())),
                                   preferred_element_type=F32)

        def new_value(h):
            return vn_ref[:, h * V_DIM:(h + 1) * V_DIM]

        _attn_update(new_score, new_value, m_ref, l_ref, acc_ref, tq)
        _attn_finish(lp_ref, sg_ref, l_ref, acc_ref, o_ref, tq, lam_init)


def _attn_sample(q, kb, vb, cache_kt, cache_v, layer, lam_params, sub_g, lam_init, tk=2048):
    _, n_seq, _, past = cache_kt.shape
    tk = min(tk, past)
    assert past % tk == 0
    tq = q.shape[0] // n_seq
    new = pl.BlockSpec((tq, 512), lambda b, j: (b, 0))
    return pl.pallas_call(
        functools.partial(_attn_sample_kernel, tq=tq, tk=tk, lam_init=lam_init),
        grid=(n_seq, past // tk),
        in_specs=[pl.BlockSpec((4, HEAD_DIM), lambda b, j: (0, 0)), pl.BlockSpec((1, V_DIM), lambda b, j: (0, 0)),
                  new,
                  pl.BlockSpec((None, None, 512, tk), lambda b, j: (layer, b, 0, j)),
                  pl.BlockSpec((None, None, tk * N_HEADS, V_DIM), lambda b, j: (layer, b, j, 0)),
                  new, new],
        out_specs=new,
        out_shape=jax.ShapeDtypeStruct((n_seq * tq, 512), BF16),
        scratch_shapes=[pltpu.VMEM((4, 2 * tq, LANES), BF16), pltpu.VMEM((4, 2 * tq, 1), F32),
                        pltpu.VMEM((4, 2 * tq, 1), F32), pltpu.VMEM((4, 2 * tq, V_DIM), F32)],
        compiler_params=_cparams(("parallel", "arbitrary")),
        name="attn_sample",
    )(lam_params, sub_g.reshape(1, V_DIM), q, cache_kt, cache_v, kb, vb)


def _gelu_tanh(x):
    return 0.5 * x * (1.0 + jnp.tanh(math.sqrt(2.0 / math.pi) * (x + 0.044715 * (x * x * x))))


def _out_proj_kernel(x_ref, y2_ref, cv_ref, at_ref, wglu_ref, wo_ref, o_ref, ys_ref):
    n_chunks = y2_ref.shape[0]
    for c in range(SSM_WIDTH // LANES):
        for j in range(S5_CHUNK):
            lo = j * SSM_WIDTH + c * LANES
            ys_ref[c, pl.ds(j, n_chunks, stride=S5_CHUNK), :] = y2_ref[:, lo:lo + LANES]
    z = _gelu_tanh(jnp.concatenate([ys_ref[c] for c in range(SSM_WIDTH // LANES)], axis=1))
    gate = jax.nn.sigmoid(jnp.dot(z.astype(BF16), wglu_ref[...], preferred_element_type=F32))
    ssm = (z * gate).astype(BF16)
    acc = jnp.dot(ssm, wo_ref[0:256, :], preferred_element_type=F32)
    acc = acc + jnp.dot(cv_ref[...].astype(BF16), wo_ref[256:512, :], preferred_element_type=F32)
    acc = acc + jnp.dot(at_ref[...], wo_ref[512:1024, :], preferred_element_type=F32)
    o_ref[...] = x_ref[...] + acc


def _out_proj(x, y_ssm, conv_out, attn_out, w_glu, w_out, tm=512):
    t = x.shape[0]
    tm = min(tm, t)
    row = lambda width: pl.BlockSpec((tm, width), lambda i: (i, 0))
    return pl.pallas_call(
        _out_proj_kernel,
        grid=(t // tm,),
        in_specs=[row(D_MODEL), pl.BlockSpec((tm // S5_CHUNK, S5_CHUNK * SSM_WIDTH), lambda i: (i, 0)),
                  row(256), row(512),
                  pl.BlockSpec((256, 256), lambda i: (0, 0)), pl.BlockSpec((D_MODEL, D_MODEL), lambda i: (0, 0))],
        out_specs=row(D_MODEL),
        out_shape=jax.ShapeDtypeStruct((t, D_MODEL), F32),
        scratch_shapes=[pltpu.VMEM((SSM_WIDTH // LANES, tm, LANES), F32)],
        compiler_params=_cparams(("parallel",)),
        name="out_proj",
    )(x, y_ssm, conv_out, attn_out, w_glu, w_out)


def _swiglu_step(h, wg, wu, wd):
    g = jnp.dot(h, wg, preferred_element_type=F32)
    u = jnp.dot(h, wu, preferred_element_type=F32)
    a = (g * jax.nn.sigmoid(g) * u).astype(BF16)
    return jnp.dot(a, wd, preferred_element_type=F32)


_FFN_SUB = 512


def _ffn_kernel(x_ref, g_ref, wg_ref, wu_ref, wd_ref, o_ref, h_ref, a_ref, acc_ref):
    j = pl.program_id(1)
    tf = a_ref.shape[1]

    @pl.when(j == 0)
    def _():
        h_ref[...] = _rms(x_ref[...], g_ref[...]).astype(BF16)

    for lo in range(0, tf, _FFN_SUB):
        cols = slice(lo, min(lo + _FFN_SUB, tf))
        gate = jnp.dot(h_ref[...], wg_ref[:, cols], preferred_element_type=F32)
        up = jnp.dot(h_ref[...], wu_ref[:, cols], preferred_element_type=F32)
        a_ref[:, cols] = (gate * jax.nn.sigmoid(gate) * up).astype(BF16)
    y = jnp.dot(a_ref[...], wd_ref[...], preferred_element_type=F32)

    @pl.when(j == 0)
    def _():
        acc_ref[...] = y

    @pl.when(j > 0)
    def _():
        acc_ref[...] += y

    @pl.when(j == pl.num_programs(1) - 1)
    def _():
        o_ref[...] = x_ref[...] + acc_ref[...]


def _ffn(x, g, wg, wu, wd, tm=1024, tf=1408):
    t = x.shape[0]
    tm = min(tm, t)
    return pl.pallas_call(
        _ffn_kernel,
        grid=(t // tm, D_FF // tf),
        in_specs=[pl.BlockSpec((tm, D_MODEL), lambda i, j: (i, 0)), pl.BlockSpec((1, D_MODEL), lambda i, j: (0, 0)),
                  pl.BlockSpec((D_MODEL, tf), lambda i, j: (0, j)), pl.BlockSpec((D_MODEL, tf), lambda i, j: (0, j)),
                  pl.BlockSpec((tf, D_MODEL), lambda i, j: (j, 0))],
        out_specs=pl.BlockSpec((tm, D_MODEL), lambda i, j: (i, 0)),
        out_shape=jax.ShapeDtypeStruct((t, D_MODEL), F32),
        scratch_shapes=[pltpu.VMEM((tm, D_MODEL), BF16), pltpu.VMEM((tm, tf), BF16), pltpu.VMEM((tm, D_MODEL), F32)],
        compiler_params=_cparams(("parallel", "arbitrary")),
        name="ffn_dense",
    )(x, g.reshape(1, D_MODEL), wg, wu, wd)


_MOE_PREFIX = 256


def _moe_route(x_ref, g_ref, r_ref, o_ref, hb_ref, rank_ref, sel_ref, gate_ref, cnt_ref):
    n_blk = x_ref.shape[0] // _MOE_PREFIX
    router = r_ref[...]
    r_hi = router.astype(BF16)
    r_lo = (router - r_hi.astype(F32)).astype(BF16)

    for blk in range(n_blk):
        rows = slice(blk * _MOE_PREFIX, (blk + 1) * _MOE_PREFIX)
        x = x_ref[rows, :]
        h32 = _rms(x, g_ref[...])
        h_hi = h32.astype(BF16)
        h_lo = (h32 - h_hi.astype(F32)).astype(BF16)
        hb_ref[rows, :] = h_hi
        o_ref[rows, :] = x
        logits = (jnp.dot(h_hi, r_hi, preferred_element_type=F32)
                  + (jnp.dot(h_hi, r_lo, preferred_element_type=F32) + jnp.dot(h_lo, r_hi, preferred_element_type=F32)))
        rank_ref[:, rows] = logits.T[:N_EXPERTS, :]
    lt = rank_ref[...]
    row = lax.broadcasted_iota(jnp.int32, lt.shape, 0)
    m1 = jnp.max(lt, axis=0, keepdims=True)
    i1 = jnp.min(jnp.where(lt == m1, row, N_EXPERTS), axis=0, keepdims=True)
    lt2 = jnp.where(row == i1, -jnp.inf, lt)
    m2 = jnp.max(lt2, axis=0, keepdims=True)
    i2 = jnp.min(jnp.where(lt2 == m2, row, N_EXPERTS), axis=0, keepdims=True)
    e2 = jnp.exp(m2 - m1)
    den = 1.0 + e2
    gate_ref[...] = jnp.where(row == i1, 1.0 / den, 0.0) + jnp.where(row == i2, e2 / den, 0.0)
    sel = jnp.where(row == i1, 1.0, 0.0) + jnp.where(row == i2, 1.0, 0.0)
    sel_ref[...] = sel
    before = (lax.broadcasted_iota(jnp.int32, (_MOE_PREFIX, _MOE_PREFIX), 0)
              < lax.broadcasted_iota(jnp.int32, (_MOE_PREFIX, _MOE_PREFIX), 1))
    tri = jnp.where(before, 1.0, 0.0).astype(BF16)
    carry = jnp.zeros((N_EXPERTS, 1), F32)
    for blk in range(lt.shape[1] // _MOE_PREFIX):
        cols = slice(blk * _MOE_PREFIX, (blk + 1) * _MOE_PREFIX)
        rank_ref[:, cols] = jnp.dot(sel[:, cols].astype(BF16), tri, preferred_element_type=F32) + carry
        carry = carry + jnp.sum(sel[:, cols], axis=1, keepdims=True)
    for k in range(N_EXPERTS):
        cnt_ref[k] = carry[k, 0].astype(jnp.int32)


def _moe_kernel(x_ref, g_ref, r_ref, gf_ref, wg_ref, wu_ref, wd_ref, o_ref, hb_ref, rank_ref, sel_ref, gate_ref,
                xe_ref, ye_ref, cnt_ref, *, tb, g_rows, f_rows, s_rows, final_norm):
    e = pl.program_id(1)
    f = pl.program_id(2)
    last_f = pl.num_programs(2) - 1

    @pl.when((e == 0) & (f == 0))
    def _():
        _moe_route(x_ref, g_ref, r_ref, o_ref, hb_ref, rank_ref, sel_ref, gate_ref, cnt_ref)

    cnt = cnt_ref[e]
    rank_e = rank_ref[pl.ds(e, 1), :]
    sel_e = sel_ref[pl.ds(e, 1), :]

    def groups(rows):
        return (cnt + rows - 1) // rows

    def onehot(first_row, rows):
        tgt = (lax.broadcasted_iota(jnp.int32, (rows, tb), 0) + first_row).astype(F32)
        return jnp.where((rank_e == tgt) & (sel_e > 0.0), 1.0, 0.0)

    @pl.when(f == 0)
    def _():
        def gather(s, c):
            off = pl.multiple_of(s * g_rows, g_rows)
            xe_ref[pl.ds(off, g_rows), :] = jnp.dot(onehot(off, g_rows).astype(BF16), hb_ref[...],
                                                    preferred_element_type=F32).astype(BF16)
            return c
        lax.fori_loop(0, groups(g_rows), gather, 0)

    full = cnt // s_rows
    rem = cnt - full * s_rows
    tail_off = pl.multiple_of(full * s_rows, s_rows)

    def ffn(first):
        def rows_at(off, rows):
            y = _swiglu_step(xe_ref[pl.ds(off, rows), :], wg_ref[...], wu_ref[...], wd_ref[...])
            if first:
                ye_ref[pl.ds(off, rows), :] = y
            else:
                ye_ref[pl.ds(off, rows), :] += y

        def body(s, c):
            rows_at(pl.multiple_of(s * s_rows, s_rows), s_rows)
            return c
        lax.fori_loop(0, full + (rem > f_rows).astype(jnp.int32), body, 0)

        @pl.when((rem > 0) & (rem <= f_rows))
        def _():
            rows_at(tail_off, f_rows)

    @pl.when(f == 0)
    def _():
        ffn(True)
        if s_rows > f_rows:
            @pl.when((rem > 0) & (rem <= f_rows))
            def _():
                rest = pl.ds(pl.multiple_of(tail_off + f_rows, f_rows), s_rows - f_rows)
                ye_ref[rest, :] = jnp.zeros((s_rows - f_rows, D_MODEL), F32)

    @pl.when(f > 0)
    def _():
        ffn(False)

    @pl.when(f == last_f)
    def _():
        gate_e = gate_ref[pl.ds(e, 1), :]

        def scatter(s, c):
            off = pl.multiple_of(s * s_rows, s_rows)
            p = onehot(off, s_rows)
            gate_rows = jnp.sum(p * gate_e, axis=1, keepdims=True)
            valid = (lax.broadcasted_iota(jnp.int32, (s_rows, 1), 0) + off) < cnt
            yg = jnp.where(valid, gate_rows * ye_ref[pl.ds(off, s_rows), :], 0.0).astype(BF16)
            o_ref[...] += jnp.dot(p.T.astype(BF16), yg, preferred_element_type=F32)
            return c
        lax.fori_loop(0, groups(s_rows), scatter, 0)

    if final_norm:
        @pl.when((e == pl.num_programs(1) - 1) & (f == last_f))
        def _():
            for lo in range(0, tb, _MOE_PREFIX):
                rows = slice(lo, lo + _MOE_PREFIX)
                o_ref[rows, :] = _rms(o_ref[rows, :], gf_ref[...])


def _moe(x, g, router_pad, wg, wu, wd, final_g=None, tb=2048, fc=1408, g_rows=256, f_rows=128, s_rows=256):
    t = x.shape[0]
    tb = min(tb, t)
    gf = jnp.ones((D_MODEL,), F32) if final_g is None else final_g
    assert s_rows in (f_rows, 2 * f_rows) and g_rows % f_rows == 0 and tb % g_rows == 0 and tb % s_rows == 0
    once = pl.Buffered(1)
    return pl.pallas_call(
        functools.partial(_moe_kernel, tb=tb, g_rows=g_rows, f_rows=f_rows, s_rows=s_rows,
                          final_norm=final_g is not None),
        grid=(t // tb, N_EXPERTS, D_FF // fc),
        in_specs=[pl.BlockSpec((tb, D_MODEL), lambda i, e, j: (i, 0), pipeline_mode=once),
                  pl.BlockSpec((1, D_MODEL), lambda i, e, j: (0, 0)),
                  pl.BlockSpec((D_MODEL, LANES), lambda i, e, j: (0, 0)),
                  pl.BlockSpec((1, D_MODEL), lambda i, e, j: (0, 0)),
                  pl.BlockSpec((None, D_MODEL, fc), lambda i, e, j: (e, 0, j)),
                  pl.BlockSpec((None, D_MODEL, fc), lambda i, e, j: (e, 0, j)),
                  pl.BlockSpec((None, fc, D_MODEL), lambda i, e, j: (e, j, 0))],
        out_specs=pl.BlockSpec((tb, D_MODEL), lambda i, e, j: (i, 0), pipeline_mode=once),
        out_shape=jax.ShapeDtypeStruct((t, D_MODEL), F32),
        scratch_shapes=[pltpu.VMEM((tb, D_MODEL), BF16), pltpu.VMEM((N_EXPERTS, tb), F32),
                        pltpu.VMEM((N_EXPERTS, tb), F32), pltpu.VMEM((N_EXPERTS, tb), F32),
                        pltpu.VMEM((tb, D_MODEL), BF16), pltpu.VMEM((tb, D_MODEL), F32),
                        pltpu.SMEM((N_EXPERTS,), jnp.int32)],
        compiler_params=pltpu.CompilerParams(dimension_semantics=("parallel", "arbitrary", "arbitrary"),
                                             vmem_limit_bytes=MOE_VMEM_LIMIT),
        name="ffn_moe",
    )(x, g.reshape(1, D_MODEL), router_pad, gf.reshape(1, D_MODEL), wg, wu, wd)


def _final_norm_kernel(x_ref, g_ref, o_ref):
    o_ref[...] = _rms(x_ref[...], g_ref[...])


def _final_norm(x, g, tm=1024):
    t = x.shape[0]
    tm = min(tm, t)
    return pl.pallas_call(
        _final_norm_kernel,
        grid=(t // tm,),
        in_specs=[pl.BlockSpec((tm, D_MODEL), lambda i: (i, 0)), pl.BlockSpec((1, D_MODEL), lambda i: (0, 0))],
        out_specs=pl.BlockSpec((tm, D_MODEL), lambda i: (i, 0)),
        out_shape=jax.ShapeDtypeStruct((t, D_MODEL), F32),
        compiler_params=_cparams(("parallel",)),
        name="final_norm",
    )(x, g.reshape(1, D_MODEL))


def kernel(x_prompt, x_sample, cache_k, cache_v, state_ssm_re, state_ssm_im, state_conv, norm_mix_g, w_in, ssm_a_re, ssm_a_im, ssm_log_dt, ssm_b_re, ssm_b_im, ssm_c_re, ssm_c_im, ssm_d, ssm_w_glu, conv_w, conv_ln_g, conv_ln_b, attn_lq1, attn_lk1, attn_lq2, attn_lk2, attn_sub_g, w_out, norm_ffn_g, ffn_w_gate, ffn_w_up, ffn_w_down, moe_router, moe_w_gate, moe_w_up, moe_w_down, final_norm_g):
    depth = w_in.shape[0]
    n_p, l_p, _ = x_prompt.shape
    n_s, l_s, _ = x_sample.shape
    past = cache_k.shape[2]
    keep = CONV_KERNEL - 1

    xp = x_prompt.reshape(n_p * l_p, D_MODEL)
    xs = x_sample.reshape(n_s * l_s, D_MODEL)
    cache_kt = jnp.transpose(cache_k, (0, 1, 3, 4, 5, 2)).reshape(depth, n_s, 512, past)
    cache_vr = cache_v.reshape(depth, n_s, past * N_HEADS, V_DIM)
    s0_p = jnp.zeros((n_p, 1, 2 * STATE_W), F32)
    buf_p = jnp.zeros((n_p, keep, CONV_WIDTH), F32)

    ops = _s5_operators(ssm_a_re, ssm_a_im, ssm_log_dt, ssm_b_re, ssm_b_im, ssm_c_re, ssm_c_im, ssm_d, S5_CHUNK)
    kv_bufs = None
    outs = {name: [] for name in ("srp", "sip", "cp", "ks", "vs", "srs", "sis", "cs")}
    for l in range(depth):
        lam_init = 0.8 - 0.6 * math.exp(-0.3 * l)
        w_in_b = w_in[l].astype(BF16)
        lam_params = jnp.stack([attn_lq1[l], attn_lk1[l], attn_lq2[l], attn_lk2[l]]).astype(F32)
        w_glu_b = ssm_w_glu[l].astype(BF16)
        w_out_b = w_out[l].astype(BF16)
        conv_p = (conv_w[l], conv_ln_g[l], conv_ln_b[l])

        u2, hc, q, kb, vt, *kv_bufs = _in_proj_prompt(xp, norm_mix_g[l], w_in_b, kv_bufs, l, depth, n_p)
        y_ssm, sf_p = _s5_group(u2, n_p, l_p, s0_p, ops, l)
        cv, cb_p = _conv(hc.reshape(n_p, l_p, CONV_WIDTH), buf_p, *conv_p, tl=min(512, l_p))
        at = _attn_prompt(q, kb, vt, lam_params, attn_sub_g[l], n_p, l_p, lam_init)
        xp = _out_proj(xp, y_ssm, cv.reshape(n_p * l_p, CONV_WIDTH), at, w_glu_b, w_out_b)

        s0_s = jnp.concatenate([state_ssm_re[l].reshape(n_s, 1, STATE_W), state_ssm_im[l].reshape(n_s, 1, STATE_W)],
                               axis=-1).astype(F32)
        u2, hc, q, kb, vb, k_s, v_s = _in_proj_sample(xs, norm_mix_g[l], w_in_b)
        y_ssm, sf_s = _s5_group(u2, n_s, l_s, s0_s, ops, l)
        cv, cb_s = _conv(hc.reshape(n_s, l_s, CONV_WIDTH), state_conv[l].astype(F32), *conv_p, tl=l_s)
        at = _attn_sample(q, kb, vb, cache_kt, cache_vr, l, lam_params, attn_sub_g[l], lam_init)
        xs = _out_proj(xs, y_ssm, cv.reshape(n_s * l_s, CONV_WIDTH), at, w_glu_b, w_out_b)

        j = l // 2
        if l % 2 == 0:
            wg, wu, wd = ffn_w_gate[j].astype(BF16), ffn_w_up[j].astype(BF16), ffn_w_down[j].astype(BF16)
            xp = _ffn(xp, norm_ffn_g[l], wg, wu, wd)
            xs = _ffn(xs, norm_ffn_g[l], wg, wu, wd)
        else:
            router_pad = jnp.pad(moe_router[j].astype(F32), ((0, 0), (0, LANES - N_EXPERTS)))
            wg, wu, wd = moe_w_gate[j].astype(BF16), moe_w_up[j].astype(BF16), moe_w_down[j].astype(BF16)
            final_g = final_norm_g if l == depth - 1 else None
            xp = _moe(xp, norm_ffn_g[l], router_pad, wg, wu, wd, final_g)
            xs = _moe(xs, norm_ffn_g[l], router_pad, wg, wu, wd, final_g)

        outs["srp"].append(sf_p[:, 0, :STATE_W].reshape(n_p, N_SSM_GROUPS, SSM_STATE))
        outs["sip"].append(sf_p[:, 0, STATE_W:].reshape(n_p, N_SSM_GROUPS, SSM_STATE))
        outs["cp"].append(cb_p)
        outs["ks"].append(k_s.reshape(n_s, l_s, 2, N_HEADS, HEAD_DIM))
        outs["vs"].append(v_s.reshape(n_s, l_s, N_HEADS, V_DIM))
        outs["srs"].append(sf_s[:, 0, :STATE_W].reshape(n_s, N_SSM_GROUPS, SSM_STATE))
        outs["sis"].append(sf_s[:, 0, STATE_W:].reshape(n_s, N_SSM_GROUPS, SSM_STATE))
        outs["cs"].append(cb_s)

    if depth % 2 == 0:
        yp, ys = xp, xs
    else:
        yp = _final_norm(xp, final_norm_g)
        ys = _final_norm(xs, final_norm_g)
    st = {name: jnp.stack(vals) for name, vals in outs.items()}
    kt_buf, v_buf = kv_bufs
    k_prompt = jnp.transpose(kt_buf.reshape(depth, n_p, 2, N_HEADS, HEAD_DIM, l_p), (0, 1, 5, 2, 3, 4))
    v_prompt = v_buf.reshape(depth, n_p, l_p, N_HEADS, V_DIM)
    return (yp.reshape(n_p, l_p, D_MODEL), ys.reshape(n_s, l_s, D_MODEL),
            k_prompt, v_prompt, st["srp"], st["sip"], st["cp"],
            st["ks"], st["vs"], st["srs"], st["sis"], st["cs"])
```
